```python
import math
import jax, jax.numpy as jnp
from jax import lax
import numpy as np

D_MODEL = 1024
BATCH = 4
SEQ = 4096
DEPTH = 2
DEC_BATCH = 32
DEC_SEQ = 4
PAST_LEN = 8192
PAGE_SIZE = 128

N_PAIRS = DEPTH // 2
N_META = 16
H_A = 4
DK_A = 128
DV_A = 128
H_B = 4
DK_B = 128
DV_B = 128
H_C = 4
DC = 64
H_D = 4
DH_D = 128
H_I = 4
D_IDX = 64
TOPK_MAX = 256
D_FF = 2816
CONV_W = 3
ROPE_THETA = 500000.0
ROT_FRAC = 4
CHUNK = 64
QBLK = 128
EPS = 1e-6
REC_COLS = (H_A * DK_A, H_A * DK_A, H_A * DV_A, H_A * DV_A, H_B * DK_B, H_B * DK_B, H_B * DV_B, H_B * DV_B, 2 * H_B)
ATT_COLS = (H_C * 2 * DC, H_C * 2 * DC, H_C * 2 * DC, H_D * DH_D, DH_D, DH_D, H_I * D_IDX, D_IDX, H_I)
REC_OUT = H_A * DV_A + H_B * DV_B
ATT_OUT = H_C * 2 * DC + H_D * DH_D

kernel_name = 'hybrid_hgrn2_mlstm_diffattn_dsa_step'


def split_cols(a, sizes):
    return jnp.split(a, np.cumsum(sizes)[:-1].tolist(), axis=-1)


def rms_norm(x, g):
    xf = x.astype(jnp.float32)
    y = xf * lax.rsqrt(jnp.mean(xf * xf, axis=-1, keepdims=True) + EPS)
    return (y * g.astype(jnp.float32)).astype(x.dtype)


def partial_rotary(x, pos):
    d = x.shape[-1]
    r = d // ROT_FRAC
    half = r // 2
    inv = ROPE_THETA ** (-jnp.arange(half, dtype=jnp.float32) * 2.0 / r)
    ang = pos.astype(jnp.float32)[:, None] * inv[None, :]
    shape = (1, pos.shape[0]) + (1,) * (x.ndim - 3) + (half,)
    cos, sin = jnp.cos(ang).reshape(shape), jnp.sin(ang).reshape(shape)
    xf = x.astype(jnp.float32)
    x1, x2 = xf[..., :half], xf[..., half:r]
    out = jnp.concatenate([x1 * cos - x2 * sin, x1 * sin + x2 * cos, xf[..., r:]], axis=-1)
    return out.astype(x.dtype)


def conv_ffn(h, buf, w_up, cw, cb, w_down):
    T = h.shape[1]
    ug, uv = jnp.split(h @ w_up, 2, axis=-1)
    full = jnp.concatenate([buf.astype(ug.dtype), ug], axis=1)
    conv = cb
    for j in range(CONV_W):
        conv = conv + cw[j] * full[:, j:j + T]
    y = (jax.nn.silu(conv) * uv) @ w_down
    return y, full[:, full.shape[1] - (CONV_W - 1):]


def hgrn2_segment(q, k, v, logf, S, chunk):
    B_, T, H, _ = q.shape
    dv = v.shape[-1]
    nc = T // chunk
    causal = jnp.tril(jnp.ones((chunk, chunk), dtype=bool))

    def blocks(a):
        return a.reshape(B_, nc, chunk, H, a.shape[-1]).transpose(1, 0, 3, 2, 4)

    def step(S, inp):
        qc, kc, vc, gc = inp
        bc = jnp.cumsum(gc, axis=2)
        o = jnp.einsum('bhtd,bhde->bhte', qc * jnp.exp(bc), S)
        rel = jnp.where(causal[:, :, None], bc[:, :, :, None, :] - bc[:, :, None, :, :], -jnp.inf)
        a = jnp.einsum('bhtd,bhsd,bhtsd->bhts', qc, kc, jnp.exp(rel))
        o = o + jnp.einsum('bhts,bhse->bhte', a, vc)
        last = bc[:, :, -1:, :]
        S = jnp.exp(last[:, :, 0, :, None]) * S + jnp.einsum('bhsd,bhse->bhde', kc * jnp.exp(last - bc), vc)
        return S, o

    S, o = lax.scan(step, S, (blocks(q), blocks(k), blocks(v), blocks(logf)))
    return S, o.transpose(1, 0, 3, 2, 4).reshape(B_, T, H, dv)


def mlstm_segment(q, k, v, ig, lf, C, n, m, chunk):
    B_, T, H, _ = q.shape
    dv = v.shape[-1]
    nc = T // chunk
    causal = jnp.tril(jnp.ones((chunk, chunk), dtype=bool))

    def blocks(a):
        return a.reshape(B_, nc, chunk, H, a.shape[-1]).transpose(1, 0, 3, 2, 4)

    def gblocks(a):
        return a.reshape(B_, nc, chunk, H).transpose(1, 0, 3, 2)

    def step(carry, inp):
        C, n, m = carry
        qc, kc, vc, ic, fc = inp
        b = jnp.cumsum(fc, axis=-1)
        dmat = jnp.where(causal, b[..., :, None] - b[..., None, :] + ic[..., None, :], -jnp.inf)
        inter = b + m[..., None]
        mt = jnp.maximum(inter, dmat.max(axis=-1))
        w = jnp.exp(dmat - mt[..., None]) * jnp.einsum('bhtd,bhsd->bhts', qc, kc)
        wi = jnp.exp(inter - mt)
        num = wi[..., None] * jnp.einsum('bhtd,bhde->bhte', qc, C) + jnp.einsum('bhts,bhse->bhte', w, vc)
        den = wi * jnp.einsum('bhtd,bhd->bht', qc, n) + w.sum(axis=-1)
        hc = num / jnp.maximum(jnp.abs(den), jnp.exp(-mt))[..., None]
        m_new = mt[..., -1]
        decay = jnp.exp(b[..., -1] + m - m_new)
        kw = kc * jnp.exp(b[..., -1:] - b + ic - m_new[..., None])[..., None]
        C = decay[..., None, None] * C + jnp.einsum('bhsd,bhse->bhde', kw, vc)
        n = decay[..., None] * n + kw.sum(axis=2)
        return (C, n, m_new), hc

    (C, n, m), h = lax.scan(step, (C, n, m), (blocks(q), blocks(k), blocks(v), gblocks(ig), gblocks(lf)))
    return C, n, m, h.transpose(1, 0, 3, 2, 4).reshape(B_, T, H, dv)


def rec_mixer(h, segments, S_a, C_b, n_b, m_b, w_in, b_gates, lb, g_a, g_b):
    f32 = jnp.float32
    B_, T, _ = h.shape
    qa, fa, ia, ga, qb, kb, vb, ob, gt = split_cols(h @ w_in, REC_COLS)

    def heads(a, H):
        return a.reshape(B_, T, H, -1).astype(f32)

    lbh = lb.reshape(H_A, DK_A)
    f = lbh + (1.0 - lbh) * jax.nn.sigmoid(heads(fa, H_A))
    logf_a, k_a = jnp.log(f), 1.0 - f
    q_a = heads(qa, H_A) * DK_A ** -0.5
    v_a = heads(ia, H_A)
    q_b = heads(qb, H_B)
    k_b = heads(kb, H_B) * DK_B ** -0.5
    v_b = heads(vb, H_B)
    gt = gt.astype(f32).reshape(B_, T, 2, H_B) + b_gates.astype(f32)
    ig, lf = gt[:, :, 0], jax.nn.log_sigmoid(gt[:, :, 1])
    S, C, n, m = S_a.astype(f32), C_b.astype(f32), n_b.astype(f32), m_b.astype(f32)
    outs_a, outs_b = [], []
    t0 = 0
    for length, chunk in segments:
        sl = slice(t0, t0 + length)
        S, oa = hgrn2_segment(q_a[:, sl], k_a[:, sl], v_a[:, sl], logf_a[:, sl], S, chunk)
        C, n, m, obb = mlstm_segment(q_b[:, sl], k_b[:, sl], v_b[:, sl], ig[:, sl], lf[:, sl], C, n, m, chunk)
        outs_a.append(oa)
        outs_b.append(obb)
        t0 += length
    y_a = rms_norm(jnp.concatenate(outs_a, axis=1), g_a) * jax.nn.silu(heads(ga, H_A))
    y_b = rms_norm(jnp.concatenate(outs_b, axis=1), g_b) * jax.nn.sigmoid(heads(ob, H_B))
    y = jnp.concatenate([y_a.reshape(B_, T, -1), y_b.reshape(B_, T, -1)], axis=-1).astype(h.dtype)
    return y, S, C, n, m


def att_project(h, pos, w_in):
    B_, T, _ = h.shape
    qc, kc, vc, qd, kd, vd, qi, ki, wi = split_cols(h @ w_in, ATT_COLS)
    qc = partial_rotary(qc.reshape(B_, T, H_C, 2, DC), pos)
    kc = partial_rotary(kc.reshape(B_, T, H_C, 2, DC), pos).reshape(B_, T, H_C, 2 * DC)
    vc = vc.reshape(B_, T, H_C, 2 * DC)
    qd = partial_rotary(qd.reshape(B_, T, H_D, DH_D), pos)
    kd = partial_rotary(kd, pos)
    qi = partial_rotary(qi.reshape(B_, T, H_I, D_IDX), pos)
    ki = partial_rotary(ki, pos)
    return qc, qd, qi, wi, kc, vc, kd, vd, ki


def att_queries(qc, qd, qi, wi, qpos, kc, vc, kd, vd, ki, kpos, lam, lam_init, g_c, n_sel):
    f32 = jnp.float32
    B_, Tq = qc.shape[:2]
    mask = kpos[None, :] <= qpos[:, None]
    kc2 = kc.reshape(kc.shape[:3] + (2, DC))
    s = jnp.einsum('bqhcd,bshcd->bhcqs', qc, kc2).astype(f32) * DC ** -0.5
    p = jax.nn.softmax(jnp.where(mask, s, -jnp.inf), axis=-1)
    a = p[:, :, 0] - lam * p[:, :, 1]
    oc = jnp.einsum('bhqs,bshe->bqhe', a.astype(vc.dtype), vc)
    oc = (rms_norm(oc, g_c) * (1.0 - lam_init)).reshape(B_, Tq, H_C * 2 * DC)
    sc = jnp.einsum('bqhd,bsd->bqhs', qi, ki).astype(f32) * D_IDX ** -0.5
    score = jnp.einsum('bqh,bqhs->bqs', wi.astype(f32) * H_I ** -0.5, jax.nn.relu(sc))
    _, idx = lax.top_k(jnp.where(mask, score, -jnp.inf), n_sel)
    valid = kpos[idx] <= qpos[None, :, None]
    take = jax.vmap(lambda rows, ix: rows[ix])
    ks, vs = take(kd, idx), take(vd, idx)
    sd = jnp.einsum('bqhd,bqnd->bqhn', qd, ks).astype(f32) * DH_D ** -0.5
    pd = jax.nn.softmax(jnp.where(valid[:, :, None, :], sd, -jnp.inf), axis=-1)
    od = jnp.einsum('bqhn,bqnd->bqhd', pd.astype(vs.dtype), vs).reshape(B_, Tq, H_D * DH_D)
    return jnp.concatenate([oc, od.astype(oc.dtype)], axis=-1)


def diff_lambda_value(lam_p, lam_init):
    lq1, lk1, lq2, lk2 = lam_p.astype(jnp.float32)
    return jnp.exp(jnp.sum(lq1 * lk1)) - jnp.exp(jnp.sum(lq2 * lk2)) + lam_init


def prompt_attention(h, w_in, lam, lam_init, g_c, n_sel):
    B_, T, _ = h.shape
    pos = jnp.arange(T, dtype=jnp.int32)
    qc, qd, qi, wi, kc, vc, kd, vd, ki = att_project(h, pos, w_in)
    Tp = -(-T // QBLK) * QBLK
    nb = Tp // QBLK

    def qblocks(a):
        a = jnp.pad(a, [(0, 0), (0, Tp - T)] + [(0, 0)] * (a.ndim - 2))
        return a.reshape((B_, nb, QBLK) + a.shape[2:]).swapaxes(0, 1)

    qpos = jnp.arange(Tp, dtype=jnp.int32).reshape(nb, QBLK)

    def one(blk):
        qc_b, qd_b, qi_b, wi_b, qp = blk
        return att_queries(qc_b, qd_b, qi_b, wi_b, qp, kc, vc, kd, vd, ki, pos, lam, lam_init, g_c, n_sel)

    out = lax.map(one, (qblocks(qc), qblocks(qd), qblocks(qi), qblocks(wi), qpos))
    out = out.swapaxes(0, 1).reshape(B_, Tp, ATT_OUT)[:, :T]
    return out, (kc, vc, kd, vd, ki)


def gather_pages(cache, page_table):
    g = cache[page_table]
    return g.reshape((page_table.shape[0], -1) + cache.shape[2:])


def sample_attention(h, pos, page_table, ck, cv, dk, dv, ik, w_in, lam, lam_init, g_c, n_sel):
    qc, qd, qi, wi, kc, vc, kd, vd, ki = att_project(h, pos, w_in)

    def with_past(cache, new):
        return jnp.concatenate([gather_pages(cache, page_table).astype(new.dtype), new], axis=1)

    kpos = jnp.arange(PAST_LEN + h.shape[1], dtype=jnp.int32)
    out = att_queries(qc, qd, qi, wi, pos, with_past(ck, kc), with_past(cv, vc), with_past(dk, kd),
                      with_past(dv, vd), with_past(ik, ki), kpos, lam, lam_init, g_c, n_sel)
    return out, (kc, vc, kd, vd, ki)


def setup_inputs(seed: int = 0) -> dict:
    key = jax.random.key(seed)
    it = iter(list(jax.random.split(key, 40)))

    def nrm(shape, scale):
        return jax.random.normal(next(it), shape, jnp.float32) * scale

    n_pages = PAST_LEN // PAGE_SIZE
    n_used = DEC_BATCH * n_pages
    n_pool = n_used + max(1, n_used // 4)
    page_table = jax.random.permutation(next(it), n_pool)[:n_used].reshape(DEC_BATCH, n_pages).astype(jnp.int32)
    b_gates = jnp.stack([nrm((N_PAIRS, H_B), 0.1),
                         jnp.linspace(3.0, 6.0, H_B, dtype=jnp.float32)[None] + nrm((N_PAIRS, H_B), 0.01)], axis=1)
    return {
        'x_prompt': nrm((BATCH, SEQ, D_MODEL), 1.0),
        'x_sample': nrm((DEC_BATCH, DEC_SEQ, D_MODEL), 1.0),
        'state_hgrn': nrm((N_PAIRS, DEC_BATCH, H_A, DK_A, DV_A), 0.5),
        'state_mlstm_C': nrm((N_PAIRS, DEC_BATCH, H_B, DK_B, DV_B), 0.5),
        'state_mlstm_n': nrm((N_PAIRS, DEC_BATCH, H_B, DK_B), 0.5),
        'state_mlstm_m': nrm((N_PAIRS, DEC_BATCH, H_B), 1.0),
        'state_ffn_conv': nrm((DEPTH, DEC_BATCH, CONV_W - 1, D_FF), 1.0),
        'cache_diff_k': nrm((N_PAIRS, n_pool, PAGE_SIZE, H_C, 2 * DC), 1.0),
        'cache_diff_v': nrm((N_PAIRS, n_pool, PAGE_SIZE, H_C, 2 * DC), 1.0),
        'cache_dsa_k': nrm((N_PAIRS, n_pool, PAGE_SIZE, DH_D), 1.0),
        'cache_dsa_v': nrm((N_PAIRS, n_pool, PAGE_SIZE, DH_D), 1.0),
        'cache_idx_k': nrm((N_PAIRS, n_pool, PAGE_SIZE, D_IDX), 1.0),
        'page_table': page_table,
        'meta_tokens': nrm((N_META, D_MODEL), 1.0),
        'norm_gains': 1.0 + nrm((DEPTH, 4, D_MODEL), 0.01),
        'w_in_rec': nrm((N_PAIRS, D_MODEL, sum(REC_COLS)), D_MODEL ** -0.5),
        'b_gates_rec': b_gates,
        'lb_logits': nrm((N_PAIRS + 1, H_A * DK_A), 0.5),
        'g_norm_hgrn': 1.0 + nrm((N_PAIRS, DV_A), 0.01),
        'g_norm_mlstm': 1.0 + nrm((N_PAIRS, DV_B), 0.01),
        'w_out_rec': nrm((N_PAIRS, REC_OUT, D_MODEL), REC_OUT ** -0.5),
        'w_in_att': nrm((N_PAIRS, D_MODEL, sum(ATT_COLS)), D_MODEL ** -0.5),
        'diff_lambda': nrm((N_PAIRS, 4, DC), 0.1),
        'g_norm_diff': 1.0 + nrm((N_PAIRS, 2 * DC), 0.01),
        'w_out_att': nrm((N_PAIRS, ATT_OUT, D_MODEL), ATT_OUT ** -0.5),
        'w_ffn_up': nrm((DEPTH, D_MODEL, 2 * D_FF), D_MODEL ** -0.5),
        'ffn_conv_w': nrm((DEPTH, CONV_W, D_FF), CONV_W ** -0.5),
        'ffn_conv_b': nrm((DEPTH, D_FF), 0.01),
        'w_ffn_down': nrm((DEPTH, D_FF, D_MODEL), D_FF ** -0.5),
    }


def reference(x_prompt, x_sample, state_hgrn, state_mlstm_C, state_mlstm_n, state_mlstm_m, state_ffn_conv,
              cache_diff_k, cache_diff_v, cache_dsa_k, cache_dsa_v, cache_idx_k, page_table, meta_tokens,
              norm_gains, w_in_rec, b_gates_rec, lb_logits, g_norm_hgrn, g_norm_mlstm, w_out_rec, w_in_att,
              diff_lambda, g_norm_diff, w_out_att, w_ffn_up, ffn_conv_w, ffn_conv_b, w_ffn_down):
    f32 = jnp.float32
    B_p, T_in = x_prompt.shape[:2]
    T_s = x_sample.shape[1]
    meta = jnp.broadcast_to(meta_tokens.astype(x_prompt.dtype)[None], (B_p, N_META, D_MODEL))
    xp = jnp.concatenate([meta, x_prompt], axis=1)
    xs = x_sample
    seg_p = [(N_META, N_META), (T_in, min(CHUNK, T_in))]
    seg_s = [(T_s, T_s)]
    pos_s = PAST_LEN + jnp.arange(T_s, dtype=jnp.int32)
    sel_p = min(TOPK_MAX, T_in // 4)
    sel_s = min(TOPK_MAX, (PAST_LEN + T_s) // 4)
    lb_all = jnp.cumsum(jax.nn.softmax(lb_logits.astype(f32), axis=0), axis=0)
    rec_p = [[], [], [], []]
    rec_s = [[], [], [], []]
    conv_p, conv_s = [], []
    att_p = [[], [], [], [], []]
    att_s = [[], [], [], [], []]
    for l in range(DEPTH):
        p = l // 2
        g = norm_gains[l]
        hp, hs = rms_norm(xp, g[0]), rms_norm(xs, g[0])
        if l % 2 == 0:
            prm = (w_in_rec[p], b_gates_rec[p], lb_all[p], g_norm_hgrn[p], g_norm_mlstm[p])
            zp = (jnp.zeros((B_p, H_A, DK_A, DV_A), f32), jnp.zeros((B_p, H_B, DK_B, DV_B), f32),
                  jnp.zeros((B_p, H_B, DK_B), f32), jnp.zeros((B_p, H_B), f32))
            mp, *st_p = rec_mixer(hp, seg_p, *zp, *prm)
            ms, *st_s = rec_mixer(hs, seg_s, state_hgrn[p], state_mlstm_C[p], state_mlstm_n[p], state_mlstm_m[p], *prm)
            for j in range(4):
                rec_p[j].append(st_p[j])
                rec_s[j].append(st_s[j])
            mp, ms = mp @ w_out_rec[p], ms @ w_out_rec[p]
        else:
            lam_init = 0.8 - 0.6 * math.exp(-0.3 * l)
            lam = diff_lambda_value(diff_lambda[p], lam_init)
            mp, rows_p = prompt_attention(hp, w_in_att[p], lam, lam_init, g_norm_diff[p], sel_p)
            ms, rows_s = sample_attention(hs, pos_s, page_table, cache_diff_k[p], cache_diff_v[p], cache_dsa_k[p],
                                          cache_dsa_v[p], cache_idx_k[p], w_in_att[p], lam, lam_init, g_norm_diff[p], sel_s)
            for j in range(5):
                att_p[j].append(rows_p[j])
                att_s[j].append(rows_s[j])
            mp, ms = mp @ w_out_att[p], ms @ w_out_att[p]
        xp = xp + rms_norm(mp, g[1])
        xs = xs + rms_norm(ms, g[1])
        fp, bp = conv_ffn(rms_norm(xp, g[2]), jnp.zeros((B_p, CONV_W - 1, D_FF), xp.dtype),
                          w_ffn_up[l], ffn_conv_w[l], ffn_conv_b[l], w_ffn_down[l])
        fs, bs = conv_ffn(rms_norm(xs, g[2]), state_ffn_conv[l], w_ffn_up[l], ffn_conv_w[l], ffn_conv_b[l], w_ffn_down[l])
        conv_p.append(bp)
        conv_s.append(bs)
        xp = xp + rms_norm(fp, g[3])
        xs = xs + rms_norm(fs, g[3])
    return (xp[:, N_META:], xs,
            jnp.stack(rec_p[0]), jnp.stack(rec_s[0]),
            jnp.stack(rec_p[1]), jnp.stack(rec_s[1]),
            jnp.stack(rec_p[2]), jnp.stack(rec_s[2]),
            jnp.stack(rec_p[3]), jnp.stack(rec_s[3]),
            jnp.stack(conv_p), jnp.stack(conv_s),
            jnp.stack(att_p[0]), jnp.stack(att_s[0]),
            jnp.stack(att_p[1]), jnp.stack(att_s[1]),
            jnp.stack(att_p[2]), jnp.stack(att_s[2]),
            jnp.stack(att_p[3]), jnp.stack(att_s[3]),
            jnp.stack(att_p[4]), jnp.stack(att_s[4]))
```

```python
import functools
import math

import jax
import jax.numpy as jnp
import numpy as np
from jax import lax
from jax.experimental import pallas as pl
from jax.experimental.pallas import tpu as pltpu

F32 = jnp.float32
BF16 = jnp.bfloat16

D_MODEL = 1024
N_META = 16
H_A, DK_A, DV_A = 4, 128, 128
H_B, DK_B, DV_B = 4, 128, 128
H_C, DC = 4, 64
H_D, DH_D = 4, 128
H_I, D_IDX = 4, 64
TOPK_MAX = 256
D_FF = 2816
CONV_W = 3
ROPE_THETA = 500000.0
ROT_FRAC = 4
EPS = 1e-6
PAGE_SIZE = 128
LANES = 128
NEG = -1e30

REC_N = 8 * 512 + LANES
ATT_N = 2560 + LANES
INT_MIN = -2 ** 31
VMEM_LIMIT = 56 * 1024 * 1024


def _cparams(sem):
    return pltpu.CompilerParams(dimension_semantics=sem, vmem_limit_bytes=VMEM_LIMIT)


def _rms(x, g):
    return x * lax.rsqrt(jnp.mean(x * x, axis=-1, keepdims=True) + EPS) * g


def _dot(a, b):
    return jnp.dot(a, b, preferred_element_type=F32)


def _dot_nt(a, b):
    return lax.dot_general(a, b, (((1,), (1,)), ((), ())), preferred_element_type=F32)


def _dot_tn(a, b):
    return lax.dot_general(a, b, (((0,), (0,)), ((), ())), preferred_element_type=F32)


def _dot_exact_lhs(tri, x):
    hi = x.astype(BF16)
    r1 = x - hi.astype(F32)
    mid = r1.astype(BF16)
    lo = (r1 - mid.astype(F32)).astype(BF16)
    return _dot(tri, hi) + _dot(tri, mid) + _dot(tri, lo)


def _row_valid(i, tm, t_pad, front):
    r = lax.broadcasted_iota(jnp.int32, (tm, 1), 0)
    if t_pad % tm == 0:
        t = (i % (t_pad // tm)) * tm + r
    else:
        assert tm % t_pad == 0 and (t_pad & (t_pad - 1)) == 0
        t = r & (t_pad - 1)
    return t >= front


def _norm_matmul_kernel(x_ref, g_ref, w_ref, o_ref, h_ref):
    @pl.when(pl.program_id(1) == 0)
    def _():
        h_ref[...] = _rms(x_ref[...], g_ref[...]).astype(BF16)

    o_ref[...] = _dot(h_ref[...], w_ref[...])


def norm_matmul(x, g, w, *, tm, tn):
    m, d = x.shape
    n = w.shape[1]
    assert m % tm == 0 and n % tn == 0
    return pl.pallas_call(
        _norm_matmul_kernel,
        grid=(m // tm, n // tn),
        in_specs=[pl.BlockSpec((tm, d), lambda i, j: (i, 0)),
                  pl.BlockSpec((1, d), lambda i, j: (0, 0)),
                  pl.BlockSpec((d, tn), lambda i, j: (0, j))],
        out_specs=pl.BlockSpec((tm, tn), lambda i, j: (i, j)),
        out_shape=jax.ShapeDtypeStruct((m, n), F32),
        scratch_shapes=[pltpu.VMEM((tm, d), BF16)],
        compiler_params=_cparams(("parallel", "arbitrary")),
        name="norm_matmul",
    )(x, g.reshape(1, d), w)


def _matmul_norm_res_kernel(*refs, n_in, tm, t_pad, front):
    a_refs, w_refs = refs[:n_in], refs[n_in:2 * n_in]
    g_ref, x_ref, o_ref = refs[2 * n_in:]
    acc = _dot(a_refs[0][...], w_refs[0][...])
    for a, w in zip(a_refs[1:], w_refs[1:]):
        acc = acc + _dot(a[...], w[...])
    out = x_ref[...] + _rms(acc, g_ref[...])
    o_ref[...] = jnp.where(_row_valid(pl.program_id(0), tm, t_pad, front), out, 0.0)


def matmul_norm_res(a_list, w_list, g, x, *, tm, t_pad, front):
    m, d = x.shape
    assert m % tm == 0
    n_in = len(a_list)
    in_specs = ([pl.BlockSpec((tm, a.shape[1]), lambda i: (i, 0)) for a in a_list]
                + [pl.BlockSpec(w.shape, lambda i: (0, 0)) for w in w_list]
                + [pl.BlockSpec((1, d), lambda i: (0, 0)), pl.BlockSpec((tm, d), lambda i: (i, 0))])
    return pl.pallas_call(
        functools.partial(_matmul_norm_res_kernel, n_in=n_in, tm=tm, t_pad=t_pad, front=front),
        grid=(m // tm,),
        in_specs=in_specs,
        out_specs=pl.BlockSpec((tm, d), lambda i: (i, 0)),
        out_shape=jax.ShapeDtypeStruct((m, d), F32),
        compiler_params=_cparams(("parallel",)),
        name="matmul_norm_res",
    )(*a_list, *w_list, g.reshape(1, d), x)


def _ffn_down_kernel(ug_ref, uv_ref, halo_ref, cw_ref, cb_ref, w_ref, g_ref, x_ref, o_ref, *, tm, t_pad, front):
    ug = ug_ref[...]
    row = lax.broadcasted_iota(jnp.int32, (tm, 1), 0)
    prev1 = jnp.where(row == 0, halo_ref[7:8, :], pltpu.roll(ug, 1, axis=0))
    prev2 = jnp.where(row == 0, halo_ref[6:7, :],
                      jnp.where(row == 1, halo_ref[7:8, :], pltpu.roll(ug, 2, axis=0)))
    conv = cb_ref[...] + cw_ref[0:1, :] * prev2 + cw_ref[1:2, :] * prev1 + cw_ref[2:3, :] * ug
    act = (conv * jax.nn.sigmoid(conv) * uv_ref[...]).astype(BF16)
    out = x_ref[...] + _rms(_dot(act, w_ref[...]), g_ref[...])
    o_ref[...] = jnp.where(_row_valid(pl.program_id(0), tm, t_pad, front), out, 0.0)


def ffn_down(up, cw, cb, w_down, g, x, *, tm, t_pad, front):
    m, d = x.shape
    f = w_down.shape[0]
    assert m % tm == 0 and tm % 8 == 0 and up.shape == (m, 2 * f)
    hb = tm // 8
    return pl.pallas_call(
        functools.partial(_ffn_down_kernel, tm=tm, t_pad=t_pad, front=front),
        grid=(m // tm,),
        in_specs=[pl.BlockSpec((tm, f), lambda i: (i, 0)),
                  pl.BlockSpec((tm, f), lambda i: (i, 1)),
                  pl.BlockSpec((8, f), lambda i: (jnp.maximum(i * hb - 1, 0), 0)),
                  pl.BlockSpec((CONV_W, f), lambda i: (0, 0)),
                  pl.BlockSpec((1, f), lambda i: (0, 0)),
                  pl.BlockSpec((f, d), lambda i: (0, 0)),
                  pl.BlockSpec((1, d), lambda i: (0, 0)),
                  pl.BlockSpec((tm, d), lambda i: (i, 0))],
        out_specs=pl.BlockSpec((tm, d), lambda i: (i, 0)),
        out_shape=jax.ShapeDtypeStruct((m, d), F32),
        compiler_params=_cparams(("parallel",)),
        name="ffn_down",
    )(up, up, up, cw, cb.reshape(1, f), w_down, g.reshape(1, d), x)


def _log_sigmoid(x):
    return jnp.minimum(x, 0.0) - jnp.log1p(jnp.exp(-jnp.abs(x)))


def _rec_kernel(proj_ref, lb_ref, bg_ref, ga_ref, gb_ref, s0_ref, c0_ref, n0_ref, m0_ref,
                y_ref, s_ref, c_ref, n_ref, m_ref, st_ref, kpad, bpad, vpad, *, C, W, front):
    ci = pl.program_id(1)
    nci = pl.num_programs(1)

    @pl.when(ci == 0)
    def _():
        for h in range(H_A):
            st_ref[h] = s0_ref[0, h].T
        c_ref[...] = c0_ref[...]
        n_ref[...] = n0_ref[...]
        m_ref[...] = m0_ref[...]
        kpad[...] = jnp.zeros_like(kpad)
        bpad[...] = jnp.zeros_like(bpad)
        vpad[...] = jnp.zeros_like(vpad)

    row = ci * C + lax.broadcasted_iota(jnp.int32, (C, 1), 0)
    valid = row >= front
    r_i = lax.broadcasted_iota(jnp.int32, (C, C), 0)
    c_i = lax.broadcasted_iota(jnp.int32, (C, C), 1)
    causal = r_i >= c_i
    tri = jnp.where(causal, 1.0, 0.0).astype(BF16)

    gates = proj_ref[0, :, 8 * 512:8 * 512 + LANES] + bg_ref[...]
    lf_all = jnp.where(valid, _log_sigmoid(gates), 0.0)
    ig_all = jnp.where(valid, gates, NEG)
    b_all = _dot_exact_lhs(tri, lf_all)
    b_all_t = b_all.T
    ig_all_t = ig_all.T

    for h in range(H_B):
        q = proj_ref[0, :, 2048 + h * 128:2048 + (h + 1) * 128]
        k = proj_ref[0, :, 2560 + h * 128:2560 + (h + 1) * 128] * (DK_B ** -0.5)
        v = proj_ref[0, :, 3072 + h * 128:3072 + (h + 1) * 128]
        og = proj_ref[0, :, 3584 + h * 128:3584 + (h + 1) * 128]
        qb, kb, vb = q.astype(BF16), k.astype(BF16), v.astype(BF16)
        b_col = b_all[:, H_B + h:H_B + h + 1]
        b_row = b_all_t[H_B + h:H_B + h + 1, :]
        i_col = ig_all[:, h:h + 1]
        i_row = ig_all_t[h:h + 1, :]
        m_prev = m_ref[0, h:h + 1, 0:1]
        dmat = jnp.where(causal, b_col - b_row + i_row, NEG)
        inter = b_col + m_prev
        mt = jnp.maximum(inter, jnp.max(dmat, axis=1, keepdims=True))
        w = jnp.exp(dmat - mt) * _dot_nt(qb, kb)
        wi = jnp.exp(inter - mt)
        c_st = c_ref[0, h]
        n_st = n_ref[0, h:h + 1, :]
        num = wi * _dot(qb, c_st.astype(BF16)) + _dot(w.astype(BF16), vb)
        den = wi * jnp.sum(q * n_st, axis=1, keepdims=True) + jnp.sum(w, axis=1, keepdims=True)
        hc = num / jnp.maximum(jnp.abs(den), jnp.exp(-mt))
        m_new = mt[C - 1:C, :]
        b_last = b_col[C - 1:C, :]
        decay = jnp.exp(b_last + m_prev - m_new)
        kw = k * jnp.exp(b_last - b_col + i_col - m_new)
        c_ref[0, h] = decay * c_st + _dot_tn(kw.astype(BF16), vb)
        n_ref[0, h:h + 1, :] = decay * n_st + jnp.sum(kw, axis=0, keepdims=True)
        m_ref[0, h:h + 1, :] = jnp.broadcast_to(m_new, (1, LANES))
        yb = _rms(hc, gb_ref[...]) * jax.nn.sigmoid(og)
        y_ref[0, :, 512 + h * 128:512 + (h + 1) * 128] = jnp.where(valid, yb, 0.0).astype(y_ref.dtype)

    rw = lax.broadcasted_iota(jnp.int32, (W, 1), 0)
    tri_w = jnp.where(lax.broadcasted_iota(jnp.int32, (W, W), 0) >= lax.broadcasted_iota(jnp.int32, (W, W), 1),
                      1.0, 0.0).astype(BF16)
    for h in range(H_A):
        lb = lb_ref[:, h * 128:(h + 1) * 128]
        for j in range(C // W):
            rows = slice(j * W, (j + 1) * W)
            vld = (ci * C + j * W + rw) >= front
            f = lb + (1.0 - lb) * jax.nn.sigmoid(proj_ref[0, rows, 512 + h * 128:512 + (h + 1) * 128])
            logf = jnp.where(vld, jnp.log(f), 0.0)
            k = jnp.where(vld, 1.0 - f, 0.0)
            q = proj_ref[0, rows, h * 128:(h + 1) * 128] * (DK_A ** -0.5)
            v = proj_ref[0, rows, 1024 + h * 128:1024 + (h + 1) * 128]
            bc = _dot_exact_lhs(tri_w, logf)
            st = st_ref[h]
            o = _dot_nt((q * jnp.exp(bc)).astype(BF16), st.astype(BF16))
            kpad[W:2 * W, :] = k
            bpad[W:2 * W, :] = bc
            vpad[W:2 * W, :] = v
            for d in range(W):
                ok = rw >= d
                e = jnp.exp(jnp.where(ok, bc - bpad[W - d:2 * W - d, :], NEG))
                r = jnp.sum(q * kpad[W - d:2 * W - d, :] * e, axis=1, keepdims=True)
                o = o + r * vpad[W - d:2 * W - d, :]
            last = bc[W - 1:W, :]
            kdec = k * jnp.exp(last - bc)
            st_ref[h] = jnp.exp(last) * st + _dot_tn(v.astype(BF16), kdec.astype(BF16))
            ga = proj_ref[0, rows, 1536 + h * 128:1536 + (h + 1) * 128]
            ya = _rms(o, ga_ref[...]) * (ga * jax.nn.sigmoid(ga))
            y_ref[0, rows, h * 128:(h + 1) * 128] = jnp.where(vld, ya, 0.0).astype(y_ref.dtype)

    @pl.when(ci == nci - 1)
    def _():
        for h in range(H_A):
            s_ref[0, h] = st_ref[h].T


def rec_mixer(proj, lb, bg, g_a, g_b, s0, c0, n0, m0, *, C, W, front):
    b, t, _ = proj.shape
    assert t % C == 0 and C % W == 0
    m0b = jnp.broadcast_to(m0[:, :, None], (b, H_B, LANES))
    bgp = jnp.zeros((1, LANES), F32).at[0, :2 * H_B].set(bg.reshape(-1))
    st_spec = pl.BlockSpec((1, 4, 128, 128), lambda i, c: (i, 0, 0, 0))
    v_spec = pl.BlockSpec((1, 4, LANES), lambda i, c: (i, 0, 0))
    row_spec = lambda n: pl.BlockSpec((1, n), lambda i, c: (0, 0))
    y, s, cc, n, m = pl.pallas_call(
        functools.partial(_rec_kernel, C=C, W=W, front=front),
        grid=(b, t // C),
        in_specs=[pl.BlockSpec((1, C, REC_N), lambda i, c: (i, c, 0)),
                  row_spec(512), row_spec(LANES), row_spec(128), row_spec(128),
                  st_spec, st_spec, v_spec, v_spec],
        out_specs=[pl.BlockSpec((1, C, 1024), lambda i, c: (i, c, 0)), st_spec, st_spec, v_spec, v_spec],
        out_shape=[jax.ShapeDtypeStruct((b, t, 1024), BF16),
                   jax.ShapeDtypeStruct((b, 4, 128, 128), F32),
                   jax.ShapeDtypeStruct((b, 4, 128, 128), F32),
                   jax.ShapeDtypeStruct((b, 4, LANES), F32),
                   jax.ShapeDtypeStruct((b, 4, LANES), F32)],
        scratch_shapes=[pltpu.VMEM((4, 128, 128), F32),
                        pltpu.VMEM((2 * W, 128), F32), pltpu.VMEM((2 * W, 128), F32), pltpu.VMEM((2 * W, 128), F32)],
        compiler_params=_cparams(("parallel", "arbitrary")),
        name="rec_mixer",
    )(proj, lb.reshape(1, 512), bgp, g_a.reshape(1, 128), g_b.reshape(1, 128), s0, c0, n0, m0b)
    return y, s, cc, n, m[:, :, 0]


def rope_tables(pos, period, half):
    r = 2 * half
    inv = ROPE_THETA ** (-jnp.arange(half, dtype=F32) * 2.0 / r)
    ang = pos.astype(F32)[:, None] * inv[None, :]
    cos, sin = jnp.cos(ang), jnp.sin(ang)
    lane = np.arange(LANES) % period
    idx = np.where(lane < half, lane, np.where(lane < r, lane - half, 0))
    first, second = jnp.asarray(lane < half), jnp.asarray((lane >= half) & (lane < r))
    c = jnp.where(first | second, cos[:, idx], 1.0)
    sa = jnp.where(first, -sin[:, idx], 0.0)
    sb = jnp.where(second, sin[:, idx], 0.0)
    return c, sa, sb


def _att_prep_kernel(p_ref, c64, a64, b64, c128, a128, b128,
                     qc_o, kc_o, kcb_o, vc_o, vcb_o, qd_o, kd_o, kdb_o, vd_o, vdb_o, qi_o, ki_o, kib_o, wi_o):
    def rot(x, c, sa, sb, half):
        return x * c[...] + pltpu.roll(x, LANES - half, axis=1) * sa[...] + pltpu.roll(x, half, axis=1) * sb[...]

    h64 = D_IDX // ROT_FRAC // 2
    h128 = DH_D // ROT_FRAC // 2
    for t in range(4):
        sl = slice(t * LANES, (t + 1) * LANES)
        qc_o[:, sl] = (rot(p_ref[:, sl], c64, a64, b64, h64) * (DC ** -0.5)).astype(BF16)
        kc = rot(p_ref[:, 512 + t * LANES:512 + (t + 1) * LANES], c64, a64, b64, h64)
        kc_o[:, sl] = kc
        kcb_o[:, sl] = kc.astype(BF16)
        vc = p_ref[:, 1024 + t * LANES:1024 + (t + 1) * LANES]
        vc_o[:, sl] = vc
        vcb_o[:, sl] = vc.astype(BF16)
        qd_o[:, sl] = rot(p_ref[:, 1536 + t * LANES:1536 + (t + 1) * LANES], c128, a128, b128, h128).astype(BF16)
    kd = rot(p_ref[:, 2048:2176], c128, a128, b128, h128)
    kd_o[...] = kd
    kdb_o[...] = kd.astype(BF16)
    vd = p_ref[:, 2176:2304]
    vd_o[...] = vd
    vdb_o[...] = vd.astype(BF16)
    for t in range(2):
        qi = rot(p_ref[:, 2304 + t * LANES:2304 + (t + 1) * LANES], c64, a64, b64, h64) * (D_IDX ** -0.5)
        qi_o[:, (2 * t) * LANES:(2 * t + 1) * LANES] = qi.astype(BF16)
        qi_o[:, (2 * t + 1) * LANES:(2 * t + 2) * LANES] = pltpu.roll(qi, D_IDX, axis=1).astype(BF16)
    last = p_ref[:, 2560:2688]
    ki = rot(last, c64, a64, b64, h64)[:, :D_IDX]
    ki_o[...] = ki
    kib_o[...] = ki.astype(BF16)
    wi_o[...] = pltpu.roll(last, D_IDX, axis=1) * (H_I ** -0.5)


def att_prep(proj, tabs64, tabs128, *, tm):
    m = proj.shape[0]
    p = tabs64[0].shape[0]
    assert m % tm == 0 and p % tm == 0
    nper = p // tm
    tab_spec = pl.BlockSpec((tm, LANES), lambda i: (i % nper, 0))
    outs = [(512, BF16), (512, F32), (512, BF16), (512, F32), (512, BF16), (512, BF16),
            (128, F32), (128, BF16), (128, F32), (128, BF16), (512, BF16), (D_IDX, F32), (D_IDX, BF16), (128, F32)]
    return pl.pallas_call(
        _att_prep_kernel,
        grid=(m // tm,),
        in_specs=[pl.BlockSpec((tm, ATT_N), lambda i: (i, 0))] + [tab_spec] * 6,
        out_specs=[pl.BlockSpec((tm, w), lambda i: (i, 0)) for w, _ in outs],
        out_shape=[jax.ShapeDtypeStruct((m, w), dt) for w, dt in outs],
        compiler_params=_cparams(("parallel",)),
        name="att_prep",
    )(proj, *tabs64, *tabs128)


QB = 128


def _diff_lambda(lam_ref, lam_init):
    dl = lam_ref[...]
    s1 = jnp.sum(dl[0:1, :] * dl[1:2, :], axis=1, keepdims=True)
    s2 = jnp.sum(dl[2:3, :] * dl[3:4, :], axis=1, keepdims=True)
    return jnp.exp(s1) - jnp.exp(s2) + lam_init


def _diff_prompt_kernel(q_ref, k_ref, v_ref, lam_ref, g_ref, o_ref, s_ref, *, front, lam_init):
    i = pl.program_id(1)
    nkb = i + 1
    lam = _diff_lambda(lam_ref, lam_init)
    lane = lax.broadcasted_iota(jnp.int32, (QB, LANES), 1)
    qrow = i * QB + lax.broadcasted_iota(jnp.int32, (QB, 1), 0)
    kcol = lax.broadcasted_iota(jnp.int32, (1, QB), 1)
    for h in range(H_C):
        cs = slice(h * LANES, (h + 1) * LANES)
        qh = q_ref[:, cs]
        q0 = jnp.where(lane < DC, qh, jnp.zeros_like(qh))
        q1 = jnp.where(lane >= DC, qh, jnp.zeros_like(qh))

        def pass_a(kb, carry):
            m0, m1 = carry
            off = pl.multiple_of(kb * QB, QB)
            kblk = k_ref[pl.ds(off, QB), cs]
            kpos = off + kcol
            ok = (kpos <= qrow) & (kpos >= front)
            s0 = jnp.where(ok, _dot_nt(q0, kblk), NEG)
            s1 = jnp.where(ok, _dot_nt(q1, kblk), NEG)
            s_ref[0, kb] = s0
            s_ref[1, kb] = s1
            return jnp.maximum(m0, s0), jnp.maximum(m1, s1)

        neg = jnp.full((QB, QB), NEG, F32)
        m0, m1 = lax.fori_loop(0, nkb, pass_a, (neg, neg))
        m0 = jnp.max(m0, axis=1, keepdims=True)
        m1 = jnp.max(m1, axis=1, keepdims=True)

        def pass_b(kb, carry):
            l0, l1, a0, a1 = carry
            off = pl.multiple_of(kb * QB, QB)
            vblk = v_ref[pl.ds(off, QB), cs]
            p0 = jnp.exp(s_ref[0, kb] - m0)
            p1 = jnp.exp(s_ref[1, kb] - m1)
            return (l0 + p0, l1 + p1, a0 + _dot(p0.astype(BF16), vblk), a1 + _dot(p1.astype(BF16), vblk))

        z = jnp.zeros((QB, LANES), F32)
        l0, l1, a0, a1 = lax.fori_loop(0, nkb, pass_b, (z, z, z, z))
        l0 = jnp.sum(l0, axis=1, keepdims=True)
        l1 = jnp.sum(l1, axis=1, keepdims=True)
        o = a0 / l0 - lam * (a1 / l1)
        o_ref[:, cs] = (_rms(o, g_ref[...]) * (1.0 - lam_init)).astype(o_ref.dtype)


def diff_prompt(q, k, v, lam_p, g_c, *, b, t_pad, front, lam_init):
    nq = t_pad // QB
    return pl.pallas_call(
        functools.partial(_diff_prompt_kernel, front=front, lam_init=lam_init),
        grid=(b, nq),
        in_specs=[pl.BlockSpec((QB, 512), lambda bb, i: (bb * nq + i, 0)),
                  pl.BlockSpec((t_pad, 512), lambda bb, i: (bb, 0)),
                  pl.BlockSpec((t_pad, 512), lambda bb, i: (bb, 0)),
                  pl.BlockSpec((4, DC), lambda bb, i: (0, 0)),
                  pl.BlockSpec((1, 2 * DC), lambda bb, i: (0, 0))],
        out_specs=pl.BlockSpec((QB, 512), lambda bb, i: (bb * nq + i, 0)),
        out_shape=jax.ShapeDtypeStruct((b * t_pad, 512), BF16),
        scratch_shapes=[pltpu.VMEM((2, nq, QB, QB), F32)],
        compiler_params=_cparams(("parallel", "arbitrary")),
        name="diff_prompt",
    )(q, k, v, lam_p, g_c.reshape(1, 2 * DC))


NINF = float("-inf")


def _idx_scores(qi_ref, w_cols, kblk):
    score = None
    for h in range(H_I):
        sc = jnp.maximum(_dot_nt(qi_ref[:, h * LANES:h * LANES + D_IDX], kblk), 0.0) * w_cols[h]
        score = sc if score is None else score + sc
    return score


def _kth_threshold(count_ge, rows, n_sel):
    zero_i = jnp.zeros((rows, 1), jnp.int32)
    neg = jnp.where(count_ge(jnp.zeros((rows, 1), F32)) < n_sel, 1, 0)
    sign = jnp.where(neg == 1, jnp.int32(INT_MIN), 0)

    def bit_body(t, mag):
        cand = mag | lax.shift_left(jnp.int32(1), 30 - t)
        enough = jnp.where(count_ge(pltpu.bitcast(cand | sign, F32)) >= n_sel, 1, 0)
        return jnp.where(enough + neg == 1, cand, mag)

    mag = lax.fori_loop(0, 31, bit_body, zero_i)
    tau = pltpu.bitcast(jnp.where(neg == 1, (mag + 1) | sign, mag), F32)
    ninf = jnp.full((rows, 1), NINF, F32)
    return jnp.where(count_ge(ninf) >= n_sel, tau, ninf)


def _dsa_prompt_kernel(qi_ref, wi_ref, qd_ref, ki_ref, kd_ref, vd_ref, o_ref, sc_ref, s_ref, *, front, n_sel):
    i = pl.program_id(1)
    nkb = i + 1
    qrow = i * QB + lax.broadcasted_iota(jnp.int32, (QB, 1), 0)
    kcol = lax.broadcasted_iota(jnp.int32, (1, QB), 1)
    w_cols = [wi_ref[:, h:h + 1] for h in range(H_I)]

    def key_ok(kb):
        kpos = kb * QB + kcol
        return (kpos <= qrow) & (kpos >= front)

    def stage1(kb, c):
        off = pl.multiple_of(kb * QB, QB)
        score = _idx_scores(qi_ref, w_cols, ki_ref[pl.ds(off, QB), :])
        sc_ref[kb] = jnp.where(key_ok(kb), score, NINF)
        return c

    lax.fori_loop(0, nkb, stage1, 0)

    def count(pred):
        def body(kb, acc):
            return acc + jnp.where(pred(sc_ref[kb]), 1, 0)
        acc = lax.fori_loop(0, nkb, body, jnp.zeros((QB, QB), jnp.int32))
        return jnp.sum(acc, axis=1, keepdims=True)

    tau = _kth_threshold(lambda cand: count(lambda sc: sc >= cand), QB, n_sel)
    need = (n_sel - count(lambda sc: sc > tau)).astype(F32)

    strict_upper = jnp.where(lax.broadcasted_iota(jnp.int32, (QB, QB), 0) < lax.broadcasted_iota(jnp.int32, (QB, QB), 1),
                             1.0, 0.0).astype(BF16)
    qd_all = jnp.concatenate([qd_ref[:, h * LANES:(h + 1) * LANES] for h in range(H_D)], axis=0)

    def stage3(kb, carry):
        before, mx = carry
        off = pl.multiple_of(kb * QB, QB)
        sc = sc_ref[kb]
        eq = sc == tau
        eqf = jnp.where(eq, 1.0, 0.0)
        rank = _dot(eqf.astype(BF16), strict_upper) + before
        sel = ((sc > tau) | (eq & (rank < need))) & (sc > NINF)
        sd = _dot_nt(qd_all, kd_ref[pl.ds(off, QB), :]) * (DH_D ** -0.5)
        sd = jnp.where(jnp.concatenate([sel] * H_D, axis=0), sd, NEG)
        s_ref[kb] = sd
        return before + jnp.sum(eqf, axis=1, keepdims=True), jnp.maximum(mx, sd)

    _, mx = lax.fori_loop(0, nkb, stage3, (jnp.zeros((QB, 1), F32), jnp.full((H_D * QB, QB), NEG, F32)))
    mx = jnp.max(mx, axis=1, keepdims=True)

    def stage4(kb, carry):
        l, acc = carry
        off = pl.multiple_of(kb * QB, QB)
        p = jnp.exp(s_ref[kb] - mx)
        return l + p, acc + _dot(p.astype(BF16), vd_ref[pl.ds(off, QB), :])

    z = jnp.zeros((H_D * QB, LANES), F32)
    l, acc = lax.fori_loop(0, nkb, stage4, (z, z))
    o = acc / jnp.sum(l, axis=1, keepdims=True)
    for h in range(H_D):
        o_ref[:, h * LANES:(h + 1) * LANES] = o[h * QB:(h + 1) * QB].astype(o_ref.dtype)


def dsa_prompt(qi, wi, qd, ki, kd, vd, *, b, t_pad, front, n_sel):
    nq = t_pad // QB
    qspec = lambda w: pl.BlockSpec((QB, w), lambda bb, i: (bb * nq + i, 0))
    kspec = lambda w: pl.BlockSpec((t_pad, w), lambda bb, i: (bb, 0))
    return pl.pallas_call(
        functools.partial(_dsa_prompt_kernel, front=front, n_sel=n_sel),
        grid=(b, nq),
        in_specs=[qspec(512), qspec(128), qspec(512), kspec(D_IDX), kspec(128), kspec(128)],
        out_specs=qspec(512),
        out_shape=jax.ShapeDtypeStruct((b * t_pad, 512), BF16),
        scratch_shapes=[pltpu.VMEM((nq, QB, QB), F32), pltpu.VMEM((nq, H_D * QB, QB), F32)],
        compiler_params=_cparams(("parallel", "arbitrary")),
        name="dsa_prompt",
    )(qi, wi, qd, ki, kd, vd)


TS = 16


def _sample_a_kernel(pt_ref, qc_ref, qi_ref, wi_ref, ck_ref, cv_ref, cik_ref, kn_ref, vn_ref, kin_ref, lam_ref, g_ref,
                     o_ref, keys_ref, qbd_ref, m_ref, l_ref, acc_ref, *, n_pages, front, lam_init):
    j = pl.program_id(1)
    rows = H_C * 2 * TS
    lane = lax.broadcasted_iota(jnp.int32, (TS, LANES), 1)
    kr = lax.broadcasted_iota(jnp.int32, (1, PAGE_SIZE), 1)

    @pl.when(j == 0)
    def _():
        zero = jnp.zeros((2 * TS, LANES), BF16)
        for h in range(H_C):
            qh = qc_ref[:, h * LANES:(h + 1) * LANES]
            blk = jnp.concatenate([jnp.where(lane < DC, qh, jnp.zeros_like(qh)),
                                   jnp.where(lane >= DC, qh, jnp.zeros_like(qh))], axis=0)
            qbd_ref[h * 2 * TS:(h + 1) * 2 * TS, :] = jnp.concatenate(
                [blk if hh == h else zero for hh in range(H_C)], axis=1)
        m_ref[...] = jnp.full(m_ref.shape, NEG, F32)
        l_ref[...] = jnp.zeros(l_ref.shape, F32)
        acc_ref[...] = jnp.zeros(acc_ref.shape, F32)

    w_cols = [wi_ref[:, h:h + 1] for h in range(H_I)]

    def step(kpage, vpage, kip, ok_rows, ok_q):
        s = _dot_nt(qbd_ref[...], kpage)
        if ok_rows is not None:
            s = jnp.where(ok_rows, s, NEG)
        m_old = m_ref[...]
        m_new = jnp.maximum(m_old, jnp.max(s, axis=1, keepdims=True))
        alpha = jnp.exp(m_old - m_new)
        p = jnp.exp(s - m_new)
        l_ref[...] = alpha * l_ref[...] + jnp.sum(p, axis=1, keepdims=True)
        acc_ref[...] = alpha * acc_ref[...] + _dot(p.astype(BF16), vpage)
        m_ref[...] = m_new
        score = _idx_scores(qi_ref, w_cols, kip)
        keys_ref[0] = score if ok_q is None else jnp.where(ok_q, score, NINF)

    @pl.when(j < n_pages)
    def _():
        step(ck_ref[0].astype(BF16), cv_ref[0].astype(BF16), cik_ref[0].astype(BF16), None, None)

    @pl.when(j == n_pages)
    def _():
        q_of_row = lax.broadcasted_iota(jnp.int32, (rows, 1), 0) & (TS - 1)
        q16 = lax.broadcasted_iota(jnp.int32, (TS, 1), 0)
        real = (kr >= front) & (kr < TS)
        step(kn_ref[0], vn_ref[0], kin_ref[0], real & (kr <= q_of_row), real & (kr <= q16))
        lam = _diff_lambda(lam_ref, lam_init)
        for h in range(H_C):
            r0 = h * 2 * TS
            a = acc_ref[r0:r0 + 2 * TS, h * LANES:(h + 1) * LANES] / l_ref[r0:r0 + 2 * TS, :]
            o = a[0:TS] - lam * a[TS:2 * TS]
            o_ref[:, h * LANES:(h + 1) * LANES] = (_rms(o, g_ref[...]) * (1.0 - lam_init)).astype(o_ref.dtype)


def sample_diff_idx(pt, qc, qi, wi, ck, cv, cik, kn, vn, kin, lam_p, g_c, *, b, n_pages, front, lam_init):
    rows = H_C * 2 * TS
    page = lambda bb, j, pt: (pt[bb * n_pages + jnp.minimum(j, n_pages - 1)], 0, 0)
    qspec = lambda w: pl.BlockSpec((TS, w), lambda bb, j, pt: (bb, 0))
    nspec = lambda w: pl.BlockSpec((1, PAGE_SIZE, w), lambda bb, j, pt: (bb, 0, 0))
    return pl.pallas_call(
        functools.partial(_sample_a_kernel, n_pages=n_pages, front=front, lam_init=lam_init),
        grid_spec=pltpu.PrefetchScalarGridSpec(
            num_scalar_prefetch=1,
            grid=(b, n_pages + 1),
            in_specs=[qspec(512), qspec(512), qspec(128),
                      pl.BlockSpec((1, PAGE_SIZE, 512), page), pl.BlockSpec((1, PAGE_SIZE, 512), page),
                      pl.BlockSpec((1, PAGE_SIZE, D_IDX), page),
                      nspec(512), nspec(512), nspec(D_IDX),
                      pl.BlockSpec((4, DC), lambda bb, j, pt: (0, 0)),
                      pl.BlockSpec((1, 2 * DC), lambda bb, j, pt: (0, 0))],
            out_specs=[qspec(512), pl.BlockSpec((1, TS, PAGE_SIZE), lambda bb, j, pt: (bb, 0, j))],
            scratch_shapes=[pltpu.VMEM((rows, 512), BF16), pltpu.VMEM((rows, 1), F32), pltpu.VMEM((rows, 1), F32),
                            pltpu.VMEM((rows, 512), F32)]),
        out_shape=[jax.ShapeDtypeStruct((b * TS, 512), BF16),
                   jax.ShapeDtypeStruct((b, TS, (n_pages + 1) * PAGE_SIZE), F32)],
        compiler_params=_cparams(("parallel", "arbitrary")),
        name="sample_diff_idx",
    )(pt, qc, qi, wi, ck, cv, cik, kn, vn, kin, lam_p, g_c.reshape(1, 2 * DC))


def _sample_b_kernel(pt_ref, keys_all_ref, keys_ref, qd_ref, ck_ref, cv_ref, kn_ref, vn_ref, o_ref,
                     tau_ref, need_ref, before_ref, m_ref, l_ref, acc_ref, *, n_pages, n_sel):
    j = pl.program_id(1)

    @pl.when(j == 0)
    def _():
        keys = keys_all_ref[0]

        def count_ge(cand):
            return jnp.sum(jnp.where(keys >= cand, 1, 0), axis=1, keepdims=True)

        tau = _kth_threshold(count_ge, TS, n_sel)
        tau_ref[...] = tau
        need_ref[...] = (n_sel - jnp.sum(jnp.where(keys > tau, 1, 0), axis=1, keepdims=True)).astype(F32)
        before_ref[...] = jnp.zeros(before_ref.shape, F32)
        m_ref[...] = jnp.full(m_ref.shape, NEG, F32)
        l_ref[...] = jnp.zeros(l_ref.shape, F32)
        acc_ref[...] = jnp.zeros(acc_ref.shape, F32)

    strict_upper = jnp.where(lax.broadcasted_iota(jnp.int32, (PAGE_SIZE, PAGE_SIZE), 0)
                             < lax.broadcasted_iota(jnp.int32, (PAGE_SIZE, PAGE_SIZE), 1), 1.0, 0.0).astype(BF16)
    qd_all = jnp.concatenate([qd_ref[:, h * LANES:(h + 1) * LANES] for h in range(H_D)], axis=0)

    def step(kpage, vpage):
        key = keys_ref[0]
        tau = tau_ref[...]
        eq = key == tau
        eqf = jnp.where(eq, 1.0, 0.0)
        rank = _dot(eqf.astype(BF16), strict_upper) + before_ref[...]
        sel = ((key > tau) | (eq & (rank < need_ref[...]))) & (key > NINF)
        before_ref[...] = before_ref[...] + jnp.sum(eqf, axis=1, keepdims=True)
        s = _dot_nt(qd_all, kpage) * (DH_D ** -0.5)
        s = jnp.where(jnp.concatenate([sel] * H_D, axis=0), s, NEG)
        m_old = m_ref[...]
        m_new = jnp.maximum(m_old, jnp.max(s, axis=1, keepdims=True))
        alpha = jnp.exp(m_old - m_new)
        p = jnp.where(s > 0.5 * NEG, jnp.exp(s - m_new), 0.0)
        l_ref[...] = alpha * l_ref[...] + jnp.sum(p, axis=1, keepdims=True)
        acc_ref[...] = alpha * acc_ref[...] + _dot(p.astype(BF16), vpage)
        m_ref[...] = m_new

    @pl.when(j < n_pages)
    def _():
        step(ck_ref[0].astype(BF16), cv_ref[0].astype(BF16))

    @pl.when(j == n_pages)
    def _():
        step(kn_ref[0], vn_ref[0])
        o = acc_ref[...] / l_ref[...]
        for h in range(H_D):
            o_ref[:, h * LANES:(h + 1) * LANES] = o[h * TS:(h + 1) * TS].astype(o_ref.dtype)


def sample_dsa(pt, keys, qd, ck, cv, kn, vn, *, b, n_pages, n_sel):
    rows = H_D * TS
    nk = (n_pages + 1) * PAGE_SIZE
    page = lambda bb, j, pt: (pt[bb * n_pages + jnp.minimum(j, n_pages - 1)], 0, 0)
    nspec = pl.BlockSpec((1, PAGE_SIZE, 128), lambda bb, j, pt: (bb, 0, 0))
    return pl.pallas_call(
        functools.partial(_sample_b_kernel, n_pages=n_pages, n_sel=n_sel),
        grid_spec=pltpu.PrefetchScalarGridSpec(
            num_scalar_prefetch=1,
            grid=(b, n_pages + 1),
            in_specs=[pl.BlockSpec((1, TS, nk), lambda bb, j, pt: (bb, 0, 0)),
                      pl.BlockSpec((1, TS, PAGE_SIZE), lambda bb, j, pt: (bb, 0, j)),
                      pl.BlockSpec((TS, 512), lambda bb, j, pt: (bb, 0)),
                      pl.BlockSpec((1, PAGE_SIZE, 128), page), pl.BlockSpec((1, PAGE_SIZE, 128), page),
                      nspec, nspec],
            out_specs=pl.BlockSpec((TS, 512), lambda bb, j, pt: (bb, 0)),
            scratch_shapes=[pltpu.VMEM((TS, 1), F32), pltpu.VMEM((TS, 1), F32), pltpu.VMEM((TS, 1), F32),
                            pltpu.VMEM((rows, 1), F32), pltpu.VMEM((rows, 1), F32), pltpu.VMEM((rows, 128), F32)]),
        out_shape=jax.ShapeDtypeStruct((b * TS, 512), BF16),
        compiler_params=_cparams(("parallel", "arbitrary")),
        name="sample_dsa",
    )(pt, keys, keys, qd, ck, cv, kn, vn)


REC_CHUNK = 64
REC_SUB = 16
TM_PROJ = 512
TM_ROWS = 384


def _pad_cols(w, n):
    return jnp.pad(w, ((0, 0), (0, n - w.shape[1])))


def _tile_rows(m, pref):
    return pref if m % pref == 0 else m


def kernel(x_prompt, x_sample, state_hgrn, state_mlstm_C, state_mlstm_n, state_mlstm_m, state_ffn_conv, cache_diff_k, cache_diff_v, cache_dsa_k, cache_dsa_v, cache_idx_k, page_table, meta_tokens, norm_gains, w_in_rec, b_gates_rec, lb_logits, g_norm_hgrn, g_norm_mlstm, w_out_rec, w_in_att, diff_lambda, g_norm_diff, w_out_att, w_ffn_up, ffn_conv_w, ffn_conv_b, w_ffn_down):
    bp, t_in, d = x_prompt.shape
    bs, t_s, _ = x_sample.shape
    depth = norm_gains.shape[0]
    n_pages = page_table.shape[1]
    n_pool = cache_diff_k.shape[1]
    past_len = n_pages * PAGE_SIZE
    real_p = N_META + t_in
    tp = -(-real_p // QB) * QB
    front_p = tp - real_p
    front_s = TS - t_s
    assert tp % REC_CHUNK == 0 and tp % TM_ROWS == 0 and front_p >= CONV_W - 1 and front_s >= CONV_W - 1
    mp, ms = bp * tp, bs * TS

    meta = jnp.broadcast_to(meta_tokens.astype(x_prompt.dtype)[None], (bp, N_META, d))
    xp = jnp.concatenate([jnp.zeros((bp, front_p, d), x_prompt.dtype), meta, x_prompt], axis=1).reshape(mp, d)
    xs = jnp.concatenate([jnp.zeros((bs, front_s, d), x_sample.dtype), x_sample], axis=1).reshape(ms, d)
    lb_all = jnp.cumsum(jax.nn.softmax(lb_logits.astype(F32), axis=0), axis=0)
    pt_flat = page_table.reshape(-1).astype(jnp.int32)
    sel_p = min(TOPK_MAX, t_in // 4)
    sel_s = min(TOPK_MAX, (past_len + t_s) // 4)
    tmp_p, tmp_s = _tile_rows(mp, TM_PROJ), _tile_rows(ms, TM_PROJ)
    tmr_s = _tile_rows(ms, TM_ROWS)

    pos_p = jnp.arange(tp, dtype=jnp.int32) - front_p
    pos_s = jnp.tile(past_len + jnp.arange(TS, dtype=jnp.int32) - front_s, ms // TS)
    tabs_p = (rope_tables(pos_p, DC, DC // ROT_FRAC // 2), rope_tables(pos_p, DH_D, DH_D // ROT_FRAC // 2))
    tabs_s = (rope_tables(pos_s, DC, DC // ROT_FRAC // 2), rope_tables(pos_s, DH_D, DH_D // ROT_FRAC // 2))

    rec_p, rec_s = [[], [], [], []], [[], [], [], []]
    att_p, att_s = [[], [], [], [], []], [[], [], [], [], []]
    conv_p, conv_s = [], []
    for l in range(depth):
        p = l // 2
        g = norm_gains[l].astype(F32)
        if l % 2 == 0:
            w_in = _pad_cols(w_in_rec[p], REC_N).astype(BF16)
            w_out = w_out_rec[p].astype(BF16)
            prm = (lb_all[p], b_gates_rec[p].astype(F32), g_norm_hgrn[p].astype(F32), g_norm_mlstm[p].astype(F32))
            proj = norm_matmul(xp, g[0], w_in, tm=tmp_p, tn=384).reshape(bp, tp, REC_N)
            zs = jnp.zeros((bp, 4, 128, 128), F32)
            y, *st = rec_mixer(proj, *prm, zs, zs, jnp.zeros((bp, 4, 128), F32), jnp.zeros((bp, 4), F32),
                               C=REC_CHUNK, W=REC_SUB, front=front_p)
            xp = matmul_norm_res([y.reshape(mp, -1)], [w_out], g[1], xp, tm=TM_ROWS, t_pad=tp, front=front_p)
            for j in range(4):
                rec_p[j].append(st[j])
            proj = norm_matmul(xs, g[0], w_in, tm=tmp_s, tn=384).reshape(bs, TS, REC_N)
            y, *st = rec_mixer(proj, *prm, state_hgrn[p].astype(F32), state_mlstm_C[p].astype(F32),
                               state_mlstm_n[p].astype(F32), state_mlstm_m[p].astype(F32), C=TS, W=TS, front=front_s)
            xs = matmul_norm_res([y.reshape(ms, -1)], [w_out], g[1], xs, tm=tmr_s, t_pad=TS, front=front_s)
            for j in range(4):
                rec_s[j].append(st[j])
        else:
            lam_init = 0.8 - 0.6 * math.exp(-0.3 * l)
            w_in = _pad_cols(w_in_att[p], ATT_N).astype(BF16)
            w_out = w_out_att[p].astype(BF16)
            dl, gc = diff_lambda[p].astype(F32), g_norm_diff[p].astype(F32)
            proj = norm_matmul(xp, g[0], w_in, tm=tmp_p, tn=384)
            (qc, kc, kcb, vc, vcb, qd, kd, kdb, vd, vdb, qi, ki, kib, wi) = att_prep(proj, *tabs_p, tm=TM_ROWS)
            oc = diff_prompt(qc, kcb, vcb, dl, gc, b=bp, t_pad=tp, front=front_p, lam_init=lam_init)
            od = dsa_prompt(qi, wi, qd, kib, kdb, vdb, b=bp, t_pad=tp, front=front_p, n_sel=sel_p)
            xp = matmul_norm_res([oc, od], [w_out[:512], w_out[512:]], g[1], xp, tm=TM_ROWS, t_pad=tp, front=front_p)
            for j, (a, shp) in enumerate([(kc, (H_C, 2 * DC)), (vc, (H_C, 2 * DC)), (kd, (DH_D,)), (vd, (DH_D,)), (ki, (D_IDX,))]):
                att_p[j].append(a.reshape((bp, tp) + shp)[:, front_p:])
            proj = norm_matmul(xs, g[0], w_in, tm=tmp_s, tn=384)
            (qc, kc, kcb, vc, vcb, qd, kd, kdb, vd, vdb, qi, ki, kib, wi) = att_prep(proj, *tabs_s, tm=ms)
            as_page = lambda a: jnp.pad(a.reshape(bs, TS, -1), ((0, 0), (0, PAGE_SIZE - TS), (0, 0)))
            oc, keys = sample_diff_idx(pt_flat, qc, qi, wi,
                                       cache_diff_k[p].reshape(n_pool, PAGE_SIZE, H_C * 2 * DC),
                                       cache_diff_v[p].reshape(n_pool, PAGE_SIZE, H_C * 2 * DC), cache_idx_k[p],
                                       as_page(kcb), as_page(vcb), as_page(kib), dl, gc,
                                       b=bs, n_pages=n_pages, front=front_s, lam_init=lam_init)
            od = sample_dsa(pt_flat, keys, qd, cache_dsa_k[p], cache_dsa_v[p], as_page(kdb), as_page(vdb),
                            b=bs, n_pages=n_pages, n_sel=sel_s)
            xs = matmul_norm_res([oc, od], [w_out[:512], w_out[512:]], g[1], xs, tm=tmr_s, t_pad=TS, front=front_s)
            for j, (a, shp) in enumerate([(kc, (H_C, 2 * DC)), (vc, (H_C, 2 * DC)), (kd, (DH_D,)), (vd, (DH_D,)), (ki, (D_IDX,))]):
                att_s[j].append(a.reshape((bs, TS) + shp)[:, front_s:])
        w_up, w_down = w_ffn_up[l].astype(BF16), w_ffn_down[l].astype(BF16)
        cw, cb = ffn_conv_w[l].astype(F32), ffn_conv_b[l].astype(F32)
        up = norm_matmul(xp, g[2], w_up, tm=tmp_p, tn=512)
        conv_p.append(up[:, :D_FF].reshape(bp, tp, D_FF)[:, tp - (CONV_W - 1):])
        xp = ffn_down(up, cw, cb, w_down, g[3], xp, tm=TM_ROWS, t_pad=tp, front=front_p)
        up = norm_matmul(xs, g[2], w_up, tm=tmp_s, tn=512).reshape(bs, TS, 2 * D_FF)
        up = up.at[:, front_s - (CONV_W - 1):front_s, :D_FF].set(state_ffn_conv[l].astype(F32))
        conv_s.append(up[:, TS - (CONV_W - 1):, :D_FF])
        xs = ffn_down(up.reshape(ms, 2 * D_FF), cw, cb, w_down, g[3], xs, tm=tmr_s, t_pad=TS, front=front_s)

    y_p = xp.reshape(bp, tp, d)[:, front_p + N_META:]
    y_s = xs.reshape(bs, TS, d)[:, front_s:]
    return (y_p, y_s,
            jnp.stack(rec_p[0]), jnp.stack(rec_s[0]), jnp.stack(rec_p[1]), jnp.stack(rec_s[1]),
            jnp.stack(rec_p[2]), jnp.stack(rec_s[2]), jnp.stack(rec_p[3]), jnp.stack(rec_s[3]),
            jnp.stack(conv_p), jnp.stack(conv_s),
            jnp.stack(att_p[0]), jnp.stack(att_s[0]), jnp.stack(att_p[1]), jnp.stack(att_s[1]),
            jnp.stack(att_p[2]), jnp.stack(att_s[2]), jnp.stack(att_p[3]), jnp.stack(att_s[3]),
            jnp.stack(att_p[4]), jnp.stack(att_s[4]))
```

```python
import functools
import math

import jax
import jax.numpy as jnp
import numpy as np
from jax import lax
from jax.experimental import pallas as pl
from jax.experimental.pallas import tpu as pltpu

F32 = jnp.float32
BF16 = jnp.bfloat16

D_MODEL = 1024
N_META = 16
H_A, DK_A, DV_A = 4, 128, 128
H_B, DK_B, DV_B = 4, 128, 128
H_C, DC = 4, 64
H_D, DH_D = 4, 128
H_I, D_IDX = 4, 64
TOPK_MAX = 256
D_FF = 2816
CONV_W = 3
ROPE_THETA = 500000.0
ROT_FRAC = 4
EPS = 1e-6
PAGE_SIZE = 128
LANES = 128
NEG = -1e30

REC_N = 8 * 512 + LANES
ATT_N = 2560 + LANES
INT_MIN = -2 ** 31
VMEM_LIMIT = 56 * 1024 * 1024


def _cparams(sem):
    return pltpu.CompilerParams(dimension_semantics=sem, vmem_limit_bytes=VMEM_LIMIT)


def _rms(x, g):
    return x * lax.rsqrt(jnp.mean(x * x, axis=-1, keepdims=True) + EPS) * g


def _dot(a, b):
    return jnp.dot(a, b, preferred_element_type=F32)


def _dot_nt(a, b):
    return lax.dot_general(a, b, (((1,), (1,)), ((), ())), preferred_element_type=F32)


def _dot_tn(a, b):
    return lax.dot_general(a, b, (((0,), (0,)), ((), ())), preferred_element_type=F32)


def _dot_exact_lhs(tri, x):
    hi = x.astype(BF16)
    r1 = x - hi.astype(F32)
    mid = r1.astype(BF16)
    lo = (r1 - mid.astype(F32)).astype(BF16)
    return _dot(tri, hi) + _dot(tri, mid) + _dot(tri, lo)


def _row_valid(i, tm, t_pad, front):
    r = lax.broadcasted_iota(jnp.int32, (tm, 1), 0)
    if t_pad % tm == 0:
        t = (i % (t_pad // tm)) * tm + r
    else:
        assert tm % t_pad == 0 and (t_pad & (t_pad - 1)) == 0
        t = r & (t_pad - 1)
    return t >= front


def _norm_matmul_kernel(x_ref, g_ref, w_ref, o_ref, h_ref):
    @pl.when(pl.program_id(1) == 0)
    def _():
        h_ref[...] = _rms(x_ref[...], g_ref[...]).astype(BF16)

    o_ref[...] = _dot(h_ref[...], w_ref[...])


def norm_matmul(x, g, w, *, tm, tn):
    m, d = x.shape
    n = w.shape[1]
    assert m % tm == 0 and n % tn == 0
    return pl.pallas_call(
        _norm_matmul_kernel,
        grid=(m // tm, n // tn),
        in_specs=[pl.BlockSpec((tm, d), lambda i, j: (i, 0)),
                  pl.BlockSpec((1, d), lambda i, j: (0, 0)),
                  pl.BlockSpec((d, tn), lambda i, j: (0, j))],
        out_specs=pl.BlockSpec((tm, tn), lambda i, j: (i, j)),
        out_shape=jax.ShapeDtypeStruct((m, n), F32),
        scratch_shapes=[pltpu.VMEM((tm, d), BF16)],
        compiler_params=_cparams(("parallel", "arbitrary")),
        name="norm_matmul",
    )(x, g.reshape(1, d), w)


def _matmul_norm_res_kernel(*refs, n_in, tm, t_pad, front):
    a_refs, w_refs = refs[:n_in], refs[n_in:2 * n_in]
    g_ref, x_ref, o_ref = refs[2 * n_in:]
    acc = _dot(a_refs[0][...], w_refs[0][...])
    for a, w in zip(a_refs[1:], w_refs[1:]):
        acc = acc + _dot(a[...], w[...])
    out = x_ref[...] + _rms(acc, g_ref[...])
    o_ref[...] = jnp.where(_row_valid(pl.program_id(0), tm, t_pad, front), out, 0.0)


def matmul_norm_res(a_list, w_list, g, x, *, tm, t_pad, front):
    m, d = x.shape
    assert m % tm == 0
    n_in = len(a_list)
    in_specs = ([pl.BlockSpec((tm, a.shape[1]), lambda i: (i, 0)) for a in a_list]
                + [pl.BlockSpec(w.shape, lambda i: (0, 0)) for w in w_list]
                + [pl.BlockSpec((1, d), lambda i: (0, 0)), pl.BlockSpec((tm, d), lambda i: (i, 0))])
    return pl.pallas_call(
        functools.partial(_matmul_norm_res_kernel, n_in=n_in, tm=tm, t_pad=t_pad, front=front),
        grid=(m // tm,),
        in_specs=in_specs,
        out_specs=pl.BlockSpec((tm, d), lambda i: (i, 0)),
        out_shape=jax.ShapeDtypeStruct((m, d), F32),
        compiler_params=_cparams(("parallel",)),
        name="matmul_norm_res",
    )(*a_list, *w_list, g.reshape(1, d), x)


def _ffn_down_kernel(ug_ref, uv_ref, halo_ref, cw_ref, cb_ref, w_ref, g_ref, x_ref, o_ref, *, tm, t_pad, front):
    ug = ug_ref[...]
    row = lax.broadcasted_iota(jnp.int32, (tm, 1), 0)
    prev1 = jnp.where(row == 0, halo_ref[7:8, :], pltpu.roll(ug, 1, axis=0))
    prev2 = jnp.where(row == 0, halo_ref[6:7, :],
                      jnp.where(row == 1, halo_ref[7:8, :], pltpu.roll(ug, 2, axis=0)))
    conv = cb_ref[...] + cw_ref[0:1, :] * prev2 + cw_ref[1:2, :] * prev1 + cw_ref[2:3, :] * ug
    act = (conv * jax.nn.sigmoid(conv) * uv_ref[...]).astype(BF16)
    out = x_ref[...] + _rms(_dot(act, w_ref[...]), g_ref[...])
    o_ref[...] = jnp.where(_row_valid(pl.program_id(0), tm, t_pad, front), out, 0.0)


def ffn_down(up, cw, cb, w_down, g, x, *, tm, t_pad, front):
    m, d = x.shape
    f = w_down.shape[0]
    assert m % tm == 0 and tm % 8 == 0 and up.shape == (m, 2 * f)
    hb = tm // 8
    return pl.pallas_call(
        functools.partial(_ffn_down_kernel, tm=tm, t_pad=t_pad, front=front),
        grid=(m // tm,),
        in_specs=[pl.BlockSpec((tm, f), lambda i: (i, 0)),
                  pl.BlockSpec((tm, f), lambda i: (i, 1)),
                  pl.BlockSpec((8, f), lambda i: (jnp.maximum(i * hb - 1, 0), 0)),
                  pl.BlockSpec((CONV_W, f), lambda i: (0, 0)),
                  pl.BlockSpec((1, f), lambda i: (0, 0)),
                  pl.BlockSpec((f, d), lambda i: (0, 0)),
                  pl.BlockSpec((1, d), lambda i: (0, 0)),
                  pl.BlockSpec((tm, d), lambda i: (i, 0))],
        out_specs=pl.BlockSpec((tm, d), lambda i: (i, 0)),
        out_shape=jax.ShapeDtypeStruct((m, d), F32),
        compiler_params=_cparams(("parallel",)),
        name="ffn_down",
    )(up, up, up, cw, cb.reshape(1, f), w_down, g.reshape(1, d), x)


def _log_sigmoid(x):
    return jnp.minimum(x, 0.0) - jnp.log1p(jnp.exp(-jnp.abs(x)))


def _rec_kernel(proj_ref, lb_ref, bg_ref, ga_ref, gb_ref, s0_ref, c0_ref, n0_ref, m0_ref,
                y_ref, s_ref, c_ref, n_ref, m_ref, st_ref, kpad, bpad, vpad, *, C, W, front):
    ci = pl.program_id(1)
    nci = pl.num_programs(1)

    @pl.when(ci == 0)
    def _():
        for h in range(H_A):
            st_ref[h] = s0_ref[0, h].T
        c_ref[...] = c0_ref[...]
        n_ref[...] = n0_ref[...]
        m_ref[...] = m0_ref[...]
        kpad[...] = jnp.zeros_like(kpad)
        bpad[...] = jnp.zeros_like(bpad)
        vpad[...] = jnp.zeros_like(vpad)

    row = ci * C + lax.broadcasted_iota(jnp.int32, (C, 1), 0)
    valid = row >= front
    r_i = lax.broadcasted_iota(jnp.int32, (C, C), 0)
    c_i = lax.broadcasted_iota(jnp.int32, (C, C), 1)
    causal = r_i >= c_i
    tri = jnp.where(causal, 1.0, 0.0).astype(BF16)

    gates = proj_ref[0, :, 8 * 512:8 * 512 + LANES] + bg_ref[...]
    lf_all = jnp.where(valid, _log_sigmoid(gates), 0.0)
    ig_all = jnp.where(valid, gates, NEG)
    b_all = _dot_exact_lhs(tri, lf_all)
    b_all_t = b_all.T
    ig_all_t = ig_all.T

    for h in range(H_B):
        q = proj_ref[0, :, 2048 + h * 128:2048 + (h + 1) * 128]
        k = proj_ref[0, :, 2560 + h * 128:2560 + (h + 1) * 128] * (DK_B ** -0.5)
        v = proj_ref[0, :, 3072 + h * 128:3072 + (h + 1) * 128]
        og = proj_ref[0, :, 3584 + h * 128:3584 + (h + 1) * 128]
        qb, kb, vb = q.astype(BF16), k.astype(BF16), v.astype(BF16)
        b_col = b_all[:, H_B + h:H_B + h + 1]
        b_row = b_all_t[H_B + h:H_B + h + 1, :]
        i_col = ig_all[:, h:h + 1]
        i_row = ig_all_t[h:h + 1, :]
        m_prev = m_ref[0, h:h + 1, 0:1]
        dmat = jnp.where(causal, b_col - b_row + i_row, NEG)
        inter = b_col + m_prev
        mt = jnp.maximum(inter, jnp.max(dmat, axis=1, keepdims=True))
        w = jnp.exp(dmat - mt) * _dot_nt(qb, kb)
        wi = jnp.exp(inter - mt)
        c_st = c_ref[0, h]
        n_st = n_ref[0, h:h + 1, :]
        num = wi * _dot(qb, c_st.astype(BF16)) + _dot(w.astype(BF16), vb)
        den = wi * jnp.sum(q * n_st, axis=1, keepdims=True) + jnp.sum(w, axis=1, keepdims=True)
        hc = num / jnp.maximum(jnp.abs(den), jnp.exp(-mt))
        m_new = mt[C - 1:C, :]
        b_last = b_col[C - 1:C, :]
        decay = jnp.exp(b_last + m_prev - m_new)
        kw = k * jnp.exp(b_last - b_col + i_col - m_new)
        c_ref[0, h] = decay * c_st + _dot_tn(kw.astype(BF16), vb)
        n_ref[0, h:h + 1, :] = decay * n_st + jnp.sum(kw, axis=0, keepdims=True)
        m_ref[0, h:h + 1, :] = jnp.broadcast_to(m_new, (1, LANES))
        yb = _rms(hc, gb_ref[...]) * jax.nn.sigmoid(og)
        y_ref[0, :, 512 + h * 128:512 + (h + 1) * 128] = jnp.where(valid, yb, 0.0).astype(y_ref.dtype)

    rw = lax.broadcasted_iota(jnp.int32, (W, 1), 0)
    tri_w = jnp.where(lax.broadcasted_iota(jnp.int32, (W, W), 0) >= lax.broadcasted_iota(jnp.int32, (W, W), 1),
                      1.0, 0.0).astype(BF16)
    for h in range(H_A):
        lb = lb_ref[:, h * 128:(h + 1) * 128]
        for j in range(C // W):
            rows = slice(j * W, (j + 1) * W)
            vld = (ci * C + j * W + rw) >= front
            f = lb + (1.0 - lb) * jax.nn.sigmoid(proj_ref[0, rows, 512 + h * 128:512 + (h + 1) * 128])
            logf = jnp.where(vld, jnp.log(f), 0.0)
            k = jnp.where(vld, 1.0 - f, 0.0)
            q = proj_ref[0, rows, h * 128:(h + 1) * 128] * (DK_A ** -0.5)
            v = proj_ref[0, rows, 1024 + h * 128:1024 + (h + 1) * 128]
            bc = _dot_exact_lhs(tri_w, logf)
            st = st_ref[h]
            o = _dot_nt((q * jnp.exp(bc)).astype(BF16), st.astype(BF16))
            kpad[W:2 * W, :] = k
            bpad[W:2 * W, :] = bc
            vpad[W:2 * W, :] = v
            for d in range(W):
                ok = rw >= d
                e = jnp.exp(jnp.where(ok, bc - bpad[W - d:2 * W - d, :], NEG))
                r = jnp.sum(q * kpad[W - d:2 * W - d, :] * e, axis=1, keepdims=True)
                o = o + r * vpad[W - d:2 * W - d, :]
            last = bc[W - 1:W, :]
            kdec = k * jnp.exp(last - bc)
            st_ref[h] = jnp.exp(last) * st + _dot_tn(v.astype(BF16), kdec.astype(BF16))
            ga = proj_ref[0, rows, 1536 + h * 128:1536 + (h + 1) * 128]
            ya = _rms(o, ga_ref[...]) * (ga * jax.nn.sigmoid(ga))
            y_ref[0, rows, h * 128:(h + 1) * 128] = jnp.where(vld, ya, 0.0).astype(y_ref.dtype)

    @pl.when(ci == nci - 1)
    def _():
        for h in range(H_A):
            s_ref[0, h] = st_ref[h].T


def rec_mixer(proj, lb, bg, g_a, g_b, s0, c0, n0, m0, *, C, W, front):
    b, t, _ = proj.shape
    assert t % C == 0 and C % W == 0
    m0b = jnp.broadcast_to(m0[:, :, None], (b, H_B, LANES))
    bgp = jnp.zeros((1, LANES), F32).at[0, :2 * H_B].set(bg.reshape(-1))
    st_spec = pl.BlockSpec((1, 4, 128, 128), lambda i, c: (i, 0, 0, 0))
    v_spec = pl.BlockSpec((1, 4, LANES), lambda i, c: (i, 0, 0))
    row_spec = lambda n: pl.BlockSpec((1, n), lambda i, c: (0, 0))
    y, s, cc, n, m = pl.pallas_call(
        functools.partial(_rec_kernel, C=C, W=W, front=front),
        grid=(b, t // C),
        in_specs=[pl.BlockSpec((1, C, REC_N), lambda i, c: (i, c, 0)),
                  row_spec(512), row_spec(LANES), row_spec(128), row_spec(128),
                  st_spec, st_spec, v_spec, v_spec],
        out_specs=[pl.BlockSpec((1, C, 1024), lambda i, c: (i, c, 0)), st_spec, st_spec, v_spec, v_spec],
        out_shape=[jax.ShapeDtypeStruct((b, t, 1024), BF16),
                   jax.ShapeDtypeStruct((b, 4, 128, 128), F32),
                   jax.ShapeDtypeStruct((b, 4, 128, 128), F32),
                   jax.ShapeDtypeStruct((b, 4, LANES), F32),
                   jax.ShapeDtypeStruct((b, 4, LANES), F32)],
        scratch_shapes=[pltpu.VMEM((4, 128, 128), F32),
                        pltpu.VMEM((2 * W, 128), F32), pltpu.VMEM((2 * W, 128), F32), pltpu.VMEM((2 * W, 128), F32)],
        compiler_params=_cparams(("parallel", "arbitrary")),
        name="rec_mixer",
    )(proj, lb.reshape(1, 512), bgp, g_a.reshape(1, 128), g_b.reshape(1, 128), s0, c0, n0, m0b)
    return y, s, cc, n, m[:, :, 0]


def rope_tables(pos, period, half):
    r = 2 * half
    inv = ROPE_THETA ** (-jnp.arange(half, dtype=F32) * 2.0 / r)
    ang = pos.astype(F32)[:, None] * inv[None, :]
    cos, sin = jnp.cos(ang), jnp.sin(ang)
    lane = np.arange(LANES) % period
    idx = np.where(lane < half, lane, np.where(lane < r, lane - half, 0))
    first, second = jnp.asarray(lane < half), jnp.asarray((lane >= half) & (lane < r))
    c = jnp.where(first | second, cos[:, idx], 1.0)
    sa = jnp.where(first, -sin[:, idx], 0.0)
    sb = jnp.where(second, sin[:, idx], 0.0)
    return c, sa, sb


def _att_prep_kernel(p_ref, c64, a64, b64, c128, a128, b128,
                     qc_o, kc_o, kcb_o, vc_o, vcb_o, qd_o, kd_o, kdb_o, vd_o, vdb_o, qi_o, ki_o, kib_o, wi_o,
                     vct_o, vdt_o):
    def rot(x, c, sa, sb, half):
        return x * c[...] + pltpu.roll(x, LANES - half, axis=1) * sa[...] + pltpu.roll(x, half, axis=1) * sb[...]

    h64 = D_IDX // ROT_FRAC // 2
    h128 = DH_D // ROT_FRAC // 2
    for t in range(4):
        sl = slice(t * LANES, (t + 1) * LANES)
        qc_o[:, sl] = (rot(p_ref[:, sl], c64, a64, b64, h64) * (DC ** -0.5)).astype(BF16)
        kc = rot(p_ref[:, 512 + t * LANES:512 + (t + 1) * LANES], c64, a64, b64, h64)
        kc_o[:, sl] = kc
        kcb_o[:, sl] = kc.astype(BF16)
        vc = p_ref[:, 1024 + t * LANES:1024 + (t + 1) * LANES]
        vc_o[:, sl] = vc
        vcb_o[:, sl] = vc.astype(BF16)
        vct_o[0, sl, :] = vc.T.astype(BF16)
        qd_o[:, sl] = rot(p_ref[:, 1536 + t * LANES:1536 + (t + 1) * LANES], c128, a128, b128, h128).astype(BF16)
    kd = rot(p_ref[:, 2048:2176], c128, a128, b128, h128)
    kd_o[...] = kd
    kdb_o[...] = kd.astype(BF16)
    vd = p_ref[:, 2176:2304]
    vd_o[...] = vd
    vdb_o[...] = vd.astype(BF16)
    vdt_o[0] = vd.T.astype(BF16)
    for t in range(2):
        qi = rot(p_ref[:, 2304 + t * LANES:2304 + (t + 1) * LANES], c64, a64, b64, h64) * (D_IDX ** -0.5)
        qi_o[:, (2 * t) * LANES:(2 * t + 1) * LANES] = qi.astype(BF16)
        qi_o[:, (2 * t + 1) * LANES:(2 * t + 2) * LANES] = pltpu.roll(qi, D_IDX, axis=1).astype(BF16)
    last = p_ref[:, 2560:2688]
    ki = rot(last, c64, a64, b64, h64)[:, :D_IDX]
    ki_o[...] = ki
    kib_o[...] = ki.astype(BF16)
    wi_o[...] = pltpu.roll(last, D_IDX, axis=1) * (H_I ** -0.5)


def att_prep(proj, tabs64, tabs128, *, tm):
    m = proj.shape[0]
    p = tabs64[0].shape[0]
    assert m % tm == 0 and p % tm == 0
    nper = p // tm
    tab_spec = pl.BlockSpec((tm, LANES), lambda i: (i % nper, 0))
    outs = [(512, BF16), (512, F32), (512, BF16), (512, F32), (512, BF16), (512, BF16),
            (128, F32), (128, BF16), (128, F32), (128, BF16), (512, BF16), (D_IDX, F32), (D_IDX, BF16), (128, F32)]
    outs_t = [512, 128]
    return pl.pallas_call(
        _att_prep_kernel,
        grid=(m // tm,),
        in_specs=[pl.BlockSpec((tm, ATT_N), lambda i: (i, 0))] + [tab_spec] * 6,
        out_specs=([pl.BlockSpec((tm, w), lambda i: (i, 0)) for w, _ in outs]
                   + [pl.BlockSpec((1, w, tm), lambda i: (i, 0, 0)) for w in outs_t]),
        out_shape=([jax.ShapeDtypeStruct((m, w), dt) for w, dt in outs]
                   + [jax.ShapeDtypeStruct((m // tm, w, tm), BF16) for w in outs_t]),
        compiler_params=_cparams(("parallel",)),
        name="att_prep",
    )(proj, *tabs64, *tabs128)


QB = 128


def _diff_lambda(lam_ref, lam_init):
    dl = lam_ref[...]
    s1 = jnp.sum(dl[0:1, :] * dl[1:2, :], axis=1, keepdims=True)
    s2 = jnp.sum(dl[2:3, :] * dl[3:4, :], axis=1, keepdims=True)
    return jnp.exp(s1) - jnp.exp(s2) + lam_init


KB = 384
SUB = 8


def _group_max(x):
    return jnp.max(x.reshape(x.shape[0] // SUB, SUB, x.shape[1]), axis=0)


def _group_sum(x):
    return jnp.sum(x.reshape(x.shape[0] // SUB, SUB, x.shape[1]), axis=0)


def _n_key_blocks(i):
    return (i * QB + QB + KB - 1) // KB


def _key_visible(i, off, n_q_cols, front):
    krow = lax.broadcasted_iota(jnp.int32, (KB, 1), 0)
    qcol = i * QB + (lax.broadcasted_iota(jnp.int32, (1, n_q_cols), 1) & (QB - 1))
    return ((qcol - krow) >= off) & (krow >= front - off)


def _diff_prompt_kernel(q_ref, k_ref, vt_ref, lam_ref, g_ref, o_ref, s_ref, *, front, lam_init):
    i = pl.program_id(1)
    nkb = _n_key_blocks(i)
    lam = _diff_lambda(lam_ref, lam_init)
    lane = lax.broadcasted_iota(jnp.int32, (QB, LANES), 1)
    for h in range(H_C):
        cs = slice(h * LANES, (h + 1) * LANES)
        qh = q_ref[:, cs]
        qstack = jnp.concatenate([jnp.where(lane < DC, qh, jnp.zeros_like(qh)),
                                  jnp.where(lane >= DC, qh, jnp.zeros_like(qh))], axis=0)

        def pass_a(kb, mx):
            off = pl.multiple_of(kb * KB, KB)
            st = _dot_nt(k_ref[pl.ds(off, KB), cs], qstack)
            st = jnp.where(_key_visible(i, off, 2 * QB, front), st, NEG)
            s_ref[kb] = st
            return jnp.maximum(mx, _group_max(st))

        mx = lax.fori_loop(0, nkb, pass_a, jnp.full((SUB, 2 * QB), NEG, F32))
        m = jnp.max(mx, axis=0, keepdims=True)

        def pass_b(kb, carry):
            l8, acc = carry
            p = jnp.exp(s_ref[kb] - m)
            return l8 + _group_sum(p), acc + _dot(vt_ref[kb, cs, :], p.astype(BF16))

        l8, acc = lax.fori_loop(0, nkb, pass_b, (jnp.zeros((SUB, 2 * QB), F32), jnp.zeros((LANES, 2 * QB), F32)))
        a = acc / jnp.sum(l8, axis=0, keepdims=True)
        ot = a[:, :QB] - lam * a[:, QB:]
        ot = ot * lax.rsqrt(jnp.mean(ot * ot, axis=0, keepdims=True) + EPS) * g_ref[...] * (1.0 - lam_init)
        o_ref[:, cs] = ot.T.astype(o_ref.dtype)


def diff_prompt(q, k, vt, lam_p, g_c, *, b, t_pad, front, lam_init):
    nq, nkb = t_pad // QB, t_pad // KB
    return pl.pallas_call(
        functools.partial(_diff_prompt_kernel, front=front, lam_init=lam_init),
        grid=(b, nq),
        in_specs=[pl.BlockSpec((QB, 512), lambda bb, i: (bb * nq + i, 0)),
                  pl.BlockSpec((t_pad, 512), lambda bb, i: (bb, 0)),
                  pl.BlockSpec((nkb, 512, KB), lambda bb, i: (bb, 0, 0)),
                  pl.BlockSpec((4, DC), lambda bb, i: (0, 0)),
                  pl.BlockSpec((2 * DC, 1), lambda bb, i: (0, 0))],
        out_specs=pl.BlockSpec((QB, 512), lambda bb, i: (bb * nq + i, 0)),
        out_shape=jax.ShapeDtypeStruct((b * t_pad, 512), BF16),
        scratch_shapes=[pltpu.VMEM((nkb, KB, 2 * QB), F32)],
        compiler_params=_cparams(("parallel", "arbitrary")),
        name="diff_prompt",
    )(q, k, vt, lam_p, g_c.reshape(2 * DC, 1))


NINF = float("-inf")


def _kth_threshold(count_ge, shape, n_sel):
    zero_i = jnp.zeros(shape, jnp.int32)
    neg = jnp.where(count_ge(jnp.zeros(shape, F32)) < n_sel, 1, 0)
    sign = jnp.where(neg == 1, jnp.int32(INT_MIN), 0)

    def bit_body(t, mag):
        cand = mag | lax.shift_left(jnp.int32(1), 30 - t)
        enough = jnp.where(count_ge(pltpu.bitcast(cand | sign, F32)) >= n_sel, 1, 0)
        return jnp.where(enough + neg == 1, cand, mag)

    mag = lax.fori_loop(0, 31, bit_body, zero_i)
    tau = pltpu.bitcast(jnp.where(neg == 1, (mag + 1) | sign, mag), F32)
    ninf = jnp.full(shape, NINF, F32)
    return jnp.where(count_ge(ninf) >= n_sel, tau, ninf)


def _dsa_prompt_kernel(qi_ref, wi_ref, qd_ref, ki_ref, kd_ref, vdt_ref, o_ref, sc_ref, s_ref, *, front, n_sel):
    i = pl.program_id(1)
    nkb = _n_key_blocks(i)
    qi_all = jnp.concatenate([qi_ref[:, h * LANES:h * LANES + D_IDX] for h in range(H_I)], axis=0)
    qd_all = jnp.concatenate([qd_ref[:, h * LANES:(h + 1) * LANES] for h in range(H_D)], axis=0)
    wt = wi_ref[...].T
    w_row = jnp.concatenate([wt[h:h + 1, :] for h in range(H_I)], axis=1)

    def stage1(kb, c):
        off = pl.multiple_of(kb * KB, KB)
        sct = jnp.maximum(_dot_nt(ki_ref[pl.ds(off, KB), :], qi_all), 0.0) * w_row
        score = sct[:, 0:QB] + sct[:, QB:2 * QB] + sct[:, 2 * QB:3 * QB] + sct[:, 3 * QB:4 * QB]
        sc_ref[kb] = jnp.where(_key_visible(i, off, QB, front), score, NINF)
        return c

    lax.fori_loop(0, nkb, stage1, 0)

    def count(pred):
        def body(kb, acc):
            return acc + _group_sum(jnp.where(pred(sc_ref[kb]), 1, 0))
        return jnp.sum(lax.fori_loop(0, nkb, body, jnp.zeros((SUB, QB), jnp.int32)), axis=0, keepdims=True)

    tau = _kth_threshold(lambda cand: count(lambda sc: sc >= cand), (1, QB), n_sel)
    need = (n_sel - count(lambda sc: sc > tau)).astype(F32)

    strict_lower = jnp.where(lax.broadcasted_iota(jnp.int32, (KB, KB), 1) < lax.broadcasted_iota(jnp.int32, (KB, KB), 0),
                             1.0, 0.0).astype(BF16)

    def stage3(kb, carry):
        before, mx = carry
        off = pl.multiple_of(kb * KB, KB)
        sc = sc_ref[kb]
        eq = sc == tau
        eqf = jnp.where(eq, 1.0, 0.0)
        rank = _dot(strict_lower, eqf.astype(BF16)) + before
        sel = ((sc > tau) | (eq & (rank < need))) & (sc > NINF)
        sdt = _dot_nt(kd_ref[pl.ds(off, KB), :], qd_all) * (DH_D ** -0.5)
        sdt = jnp.where(jnp.concatenate([sel] * H_D, axis=1), sdt, NEG)
        s_ref[kb] = sdt
        return before + jnp.sum(eqf, axis=0, keepdims=True), jnp.maximum(mx, _group_max(sdt))

    _, mx = lax.fori_loop(0, nkb, stage3, (jnp.zeros((1, QB), F32), jnp.full((SUB, H_D * QB), NEG, F32)))
    m = jnp.max(mx, axis=0, keepdims=True)

    def stage4(kb, carry):
        l8, acc = carry
        p = jnp.exp(s_ref[kb] - m)
        return l8 + _group_sum(p), acc + _dot(vdt_ref[kb], p.astype(BF16))

    l8, acc = lax.fori_loop(0, nkb, stage4, (jnp.zeros((SUB, H_D * QB), F32), jnp.zeros((DH_D, H_D * QB), F32)))
    ot = acc / jnp.sum(l8, axis=0, keepdims=True)
    for h in range(H_D):
        o_ref[:, h * LANES:(h + 1) * LANES] = ot[:, h * QB:(h + 1) * QB].T.astype(o_ref.dtype)


def dsa_prompt(qi, wi, qd, ki, kd, vdt, *, b, t_pad, front, n_sel):
    nq, nkb = t_pad // QB, t_pad // KB
    qspec = lambda w: pl.BlockSpec((QB, w), lambda bb, i: (bb * nq + i, 0))
    kspec = lambda w: pl.BlockSpec((t_pad, w), lambda bb, i: (bb, 0))
    return pl.pallas_call(
        functools.partial(_dsa_prompt_kernel, front=front, n_sel=n_sel),
        grid=(b, nq),
        in_specs=[qspec(512), qspec(128), qspec(512), kspec(D_IDX), kspec(DH_D),
                  pl.BlockSpec((nkb, DH_D, KB), lambda bb, i: (bb, 0, 0))],
        out_specs=qspec(512),
        out_shape=jax.ShapeDtypeStruct((b * t_pad, 512), BF16),
        scratch_shapes=[pltpu.VMEM((nkb, KB, QB), F32), pltpu.VMEM((nkb, KB, H_D * QB), F32)],
        compiler_params=_cparams(("parallel", "arbitrary")),
        name="dsa_prompt",
    )(qi, wi, qd, ki, kd, vdt)


TS = 16


PG = 4


def _page_map(g, n_pages, nd):
    def index(bb, j, pt):
        return (pt[bb * n_pages + jnp.minimum(j * PG + g, n_pages - 1)],) + (0,) * nd
    return index


def _online_softmax_update(rows, s, v, m_ref, l_ref, acc_ref):
    m_old = m_ref[rows, :]
    m_new = jnp.maximum(m_old, jnp.max(s, axis=1, keepdims=True))
    alpha = jnp.exp(m_old - m_new)
    p = jnp.where(s > 0.5 * NEG, jnp.exp(s - m_new), 0.0)
    l_ref[rows, :] = alpha * l_ref[rows, :] + jnp.sum(p, axis=1, keepdims=True)
    acc_ref[rows, :] = alpha * acc_ref[rows, :] + _dot(p.astype(BF16), v)
    m_ref[rows, :] = m_new


def _sample_a_kernel(pt_ref, qc_ref, qi_ref, wi_ref, *refs, n_steps, front, lam_init):
    ck_refs, cv_refs, cik_refs = refs[:PG], refs[PG:2 * PG], refs[2 * PG:3 * PG]
    kn_ref, vn_ref, kin_ref, lam_ref, g_ref, o_ref, keys_ref, m_ref, l_ref, acc_ref = refs[3 * PG:]
    j = pl.program_id(1)
    lane = lax.broadcasted_iota(jnp.int32, (TS, LANES), 1)

    @pl.when(j == 0)
    def _():
        m_ref[...] = jnp.full(m_ref.shape, NEG, F32)
        l_ref[...] = jnp.zeros(l_ref.shape, F32)
        acc_ref[...] = jnp.zeros(acc_ref.shape, F32)

    def qstack(h):
        qh = qc_ref[:, h * LANES:(h + 1) * LANES]
        return jnp.concatenate([jnp.where(lane < DC, qh, jnp.zeros_like(qh)),
                                jnp.where(lane >= DC, qh, jnp.zeros_like(qh))], axis=0)

    def head_rows(h):
        return slice(h * 2 * TS, (h + 1) * 2 * TS)

    qi_all = jnp.concatenate([qi_ref[:, h * LANES:h * LANES + D_IDX] for h in range(H_I)], axis=0)
    w_col = jnp.concatenate([wi_ref[:, h:h + 1] for h in range(H_I)], axis=0)

    def idx_scores(kip):
        sc = jnp.maximum(_dot_nt(qi_all, kip), 0.0) * w_col
        return sc[0:TS] + sc[TS:2 * TS] + sc[2 * TS:3 * TS] + sc[3 * TS:4 * TS]

    @pl.when(j < n_steps)
    def _():
        for h in range(H_C):
            rows_h = pl.ds(h, PAGE_SIZE, stride=H_C)
            k = jnp.concatenate([r[0, rows_h, :].astype(BF16) for r in ck_refs], axis=0)
            v = jnp.concatenate([r[0, rows_h, :].astype(BF16) for r in cv_refs], axis=0)
            _online_softmax_update(head_rows(h), _dot_nt(qstack(h), k), v, m_ref, l_ref, acc_ref)
        keys_ref[0] = idx_scores(jnp.concatenate([r[0].astype(BF16) for r in cik_refs], axis=0))

    @pl.when(j == n_steps)
    def _():
        kr = lax.broadcasted_iota(jnp.int32, (1, TS), 1)
        q_of_row = lax.broadcasted_iota(jnp.int32, (2 * TS, 1), 0) & (TS - 1)
        ok = (kr >= front) & (kr <= q_of_row)
        for h in range(H_C):
            cs = slice(h * LANES, (h + 1) * LANES)
            s = jnp.where(ok, _dot_nt(qstack(h), kn_ref[:, cs]), NEG)
            _online_softmax_update(head_rows(h), s, vn_ref[:, cs], m_ref, l_ref, acc_ref)
        krp = lax.broadcasted_iota(jnp.int32, (1, PAGE_SIZE), 1)
        okq = (krp >= front) & (krp < TS) & (krp <= lax.broadcasted_iota(jnp.int32, (TS, 1), 0))
        keys_ref[0] = jnp.concatenate([jnp.where(okq, idx_scores(kin_ref[0]), NINF),
                                       jnp.full((TS, (PG - 1) * PAGE_SIZE), NINF, F32)], axis=1)
        lam = _diff_lambda(lam_ref, lam_init)
        for h in range(H_C):
            a = acc_ref[head_rows(h), :] / l_ref[head_rows(h), :]
            o = a[0:TS] - lam * a[TS:2 * TS]
            o_ref[:, h * LANES:(h + 1) * LANES] = (_rms(o, g_ref[...]) * (1.0 - lam_init)).astype(o_ref.dtype)


def sample_diff_idx(pt, qc, qi, wi, ck, cv, cik, kn, vn, kin, lam_p, g_c, *, b, n_pages, front, lam_init):
    assert n_pages % PG == 0
    n_steps = n_pages // PG
    rows = H_C * 2 * TS
    qspec = lambda w: pl.BlockSpec((TS, w), lambda bb, j, pt: (bb, 0))
    pages4 = [pl.BlockSpec((1, PAGE_SIZE * H_C, 2 * DC), _page_map(g, n_pages, 2)) for g in range(PG)]
    pages_i = [pl.BlockSpec((1, PAGE_SIZE, D_IDX), _page_map(g, n_pages, 2)) for g in range(PG)]
    return pl.pallas_call(
        functools.partial(_sample_a_kernel, n_steps=n_steps, front=front, lam_init=lam_init),
        grid_spec=pltpu.PrefetchScalarGridSpec(
            num_scalar_prefetch=1,
            grid=(b, n_steps + 1),
            in_specs=[qspec(512), qspec(512), qspec(128)] + pages4 + pages4 + pages_i
                     + [qspec(512), qspec(512), pl.BlockSpec((1, PAGE_SIZE, D_IDX), lambda bb, j, pt: (bb, 0, 0)),
                        pl.BlockSpec((4, DC), lambda bb, j, pt: (0, 0)),
                        pl.BlockSpec((1, 2 * DC), lambda bb, j, pt: (0, 0))],
            out_specs=[qspec(512), pl.BlockSpec((1, TS, PG * PAGE_SIZE), lambda bb, j, pt: (bb, 0, j))],
            scratch_shapes=[pltpu.VMEM((rows, 1), F32), pltpu.VMEM((rows, 1), F32), pltpu.VMEM((rows, 2 * DC), F32)]),
        out_shape=[jax.ShapeDtypeStruct((b * TS, 512), BF16),
                   jax.ShapeDtypeStruct((b, TS, (n_steps + 1) * PG * PAGE_SIZE), F32)],
        compiler_params=_cparams(("parallel", "arbitrary")),
        name="sample_diff_idx",
    )(pt, qc, qi, wi, *([ck] * PG), *([cv] * PG), *([cik] * PG), kn, vn, kin, lam_p, g_c.reshape(1, 2 * DC))


def _sample_b_kernel(pt_ref, keys_all_ref, keys_ref, qd_ref, *refs, n_steps, n_sel):
    ck_refs, cv_refs = refs[:PG], refs[PG:2 * PG]
    kn_ref, vn_ref, o_ref, tau_ref, need_ref, before_ref, m_ref, l_ref, acc_ref = refs[2 * PG:]
    j = pl.program_id(1)

    @pl.when(j == 0)
    def _():
        keys = keys_all_ref[0]

        def count_ge(cand):
            return jnp.sum(jnp.where(keys >= cand, 1, 0), axis=1, keepdims=True)

        tau = _kth_threshold(count_ge, (TS, 1), n_sel)
        tau_ref[...] = tau
        need_ref[...] = (n_sel - jnp.sum(jnp.where(keys > tau, 1, 0), axis=1, keepdims=True)).astype(F32)
        before_ref[...] = jnp.zeros(before_ref.shape, F32)
        m_ref[...] = jnp.full(m_ref.shape, NEG, F32)
        l_ref[...] = jnp.zeros(l_ref.shape, F32)
        acc_ref[...] = jnp.zeros(acc_ref.shape, F32)

    strict_upper = jnp.where(lax.broadcasted_iota(jnp.int32, (PAGE_SIZE, PAGE_SIZE), 0)
                             < lax.broadcasted_iota(jnp.int32, (PAGE_SIZE, PAGE_SIZE), 1), 1.0, 0.0).astype(BF16)
    qd_all = jnp.concatenate([qd_ref[:, h * LANES:(h + 1) * LANES] for h in range(H_D)], axis=0)

    def select(key, before):
        tau = tau_ref[...]
        eq = key == tau
        eqf = jnp.where(eq, 1.0, 0.0)
        rank = _dot(eqf.astype(BF16), strict_upper) + before
        sel = ((key > tau) | (eq & (rank < need_ref[...]))) & (key > NINF)
        return sel, before + jnp.sum(eqf, axis=1, keepdims=True)

    def attend(s, sel, v):
        s = jnp.where(jnp.concatenate([sel] * H_D, axis=0), s * (DH_D ** -0.5), NEG)
        _online_softmax_update(slice(None), s, v, m_ref, l_ref, acc_ref)

    @pl.when(j < n_steps)
    def _():
        key = keys_ref[0]
        before = before_ref[...]
        sels = []
        for g in range(PG):
            sel, before = select(key[:, g * PAGE_SIZE:(g + 1) * PAGE_SIZE], before)
            sels.append(sel)
        before_ref[...] = before
        k = jnp.concatenate([r[0].astype(BF16) for r in ck_refs], axis=0)
        v = jnp.concatenate([r[0].astype(BF16) for r in cv_refs], axis=0)
        attend(_dot_nt(qd_all, k), jnp.concatenate(sels, axis=1), v)

    @pl.when(j == n_steps)
    def _():
        sel, _ = select(keys_ref[0][:, :PAGE_SIZE], before_ref[...])
        attend(_dot_nt(qd_all, kn_ref[0]), sel, vn_ref[0])
        o = acc_ref[...] / l_ref[...]
        for h in range(H_D):
            o_ref[:, h * LANES:(h + 1) * LANES] = o[h * TS:(h + 1) * TS].astype(o_ref.dtype)


def sample_dsa(pt, keys, qd, ck, cv, kn, vn, *, b, n_pages, n_sel):
    assert n_pages % PG == 0
    n_steps = n_pages // PG
    rows = H_D * TS
    nk = keys.shape[2]
    pages = [pl.BlockSpec((1, PAGE_SIZE, DH_D), _page_map(g, n_pages, 2)) for g in range(PG)]
    nspec = pl.BlockSpec((1, PAGE_SIZE, DH_D), lambda bb, j, pt: (bb, 0, 0))
    return pl.pallas_call(
        functools.partial(_sample_b_kernel, n_steps=n_steps, n_sel=n_sel),
        grid_spec=pltpu.PrefetchScalarGridSpec(
            num_scalar_prefetch=1,
            grid=(b, n_steps + 1),
            in_specs=[pl.BlockSpec((1, TS, nk), lambda bb, j, pt: (bb, 0, 0)),
                      pl.BlockSpec((1, TS, PG * PAGE_SIZE), lambda bb, j, pt: (bb, 0, j)),
                      pl.BlockSpec((TS, 512), lambda bb, j, pt: (bb, 0))] + pages + pages + [nspec, nspec],
            out_specs=pl.BlockSpec((TS, 512), lambda bb, j, pt: (bb, 0)),
            scratch_shapes=[pltpu.VMEM((TS, 1), F32), pltpu.VMEM((TS, 1), F32), pltpu.VMEM((TS, 1), F32),
                            pltpu.VMEM((rows, 1), F32), pltpu.VMEM((rows, 1), F32), pltpu.VMEM((rows, DH_D), F32)]),
        out_shape=jax.ShapeDtypeStruct((b * TS, 512), BF16),
        compiler_params=_cparams(("parallel", "arbitrary")),
        name="sample_dsa",
    )(pt, keys, keys, qd, *([ck] * PG), *([cv] * PG), kn, vn)


REC_CHUNK = 64
REC_SUB = 16
TM_PROJ = 512
TM_ROWS = 384


def _pad_cols(w, n):
    return jnp.pad(w, ((0, 0), (0, n - w.shape[1])))


def _tile_rows(m, pref):
    return pref if m % pref == 0 else m


def kernel(x_prompt, x_sample, state_hgrn, state_mlstm_C, state_mlstm_n, state_mlstm_m, state_ffn_conv, cache_diff_k, cache_diff_v, cache_dsa_k, cache_dsa_v, cache_idx_k, page_table, meta_tokens, norm_gains, w_in_rec, b_gates_rec, lb_logits, g_norm_hgrn, g_norm_mlstm, w_out_rec, w_in_att, diff_lambda, g_norm_diff, w_out_att, w_ffn_up, ffn_conv_w, ffn_conv_b, w_ffn_down):
    bp, t_in, d = x_prompt.shape
    bs, t_s, _ = x_sample.shape
    depth = norm_gains.shape[0]
    n_pages = page_table.shape[1]
    past_len = n_pages * PAGE_SIZE
    real_p = N_META + t_in
    tp = -(-real_p // QB) * QB
    front_p = tp - real_p
    front_s = TS - t_s
    assert tp % REC_CHUNK == 0 and tp % TM_ROWS == 0 and front_p >= CONV_W - 1 and front_s >= CONV_W - 1
    mp, ms = bp * tp, bs * TS

    meta = jnp.broadcast_to(meta_tokens.astype(x_prompt.dtype)[None], (bp, N_META, d))
    xp = jnp.concatenate([jnp.zeros((bp, front_p, d), x_prompt.dtype), meta, x_prompt], axis=1).reshape(mp, d)
    xs = jnp.concatenate([jnp.zeros((bs, front_s, d), x_sample.dtype), x_sample], axis=1).reshape(ms, d)
    lb_all = jnp.cumsum(jax.nn.softmax(lb_logits.astype(F32), axis=0), axis=0)
    pt_flat = page_table.reshape(-1).astype(jnp.int32)
    sel_p = min(TOPK_MAX, t_in // 4)
    sel_s = min(TOPK_MAX, (past_len + t_s) // 4)
    tmp_p, tmp_s = _tile_rows(mp, TM_PROJ), _tile_rows(ms, TM_PROJ)
    tmr_s = _tile_rows(ms, TM_ROWS)

    pos_p = jnp.arange(tp, dtype=jnp.int32) - front_p
    pos_s = jnp.tile(past_len + jnp.arange(TS, dtype=jnp.int32) - front_s, ms // TS)
    tabs_p = (rope_tables(pos_p, DC, DC // ROT_FRAC // 2), rope_tables(pos_p, DH_D, DH_D // ROT_FRAC // 2))
    tabs_s = (rope_tables(pos_s, DC, DC // ROT_FRAC // 2), rope_tables(pos_s, DH_D, DH_D // ROT_FRAC // 2))

    rec_p, rec_s = [[], [], [], []], [[], [], [], []]
    att_p, att_s = [[], [], [], [], []], [[], [], [], [], []]
    conv_p, conv_s = [], []
    for l in range(depth):
        p = l // 2
        g = norm_gains[l].astype(F32)
        if l % 2 == 0:
            w_in = _pad_cols(w_in_rec[p], REC_N).astype(BF16)
            w_out = w_out_rec[p].astype(BF16)
            prm = (lb_all[p], b_gates_rec[p].astype(F32), g_norm_hgrn[p].astype(F32), g_norm_mlstm[p].astype(F32))
            proj = norm_matmul(xp, g[0], w_in, tm=tmp_p, tn=384).reshape(bp, tp, REC_N)
            zs = jnp.zeros((bp, 4, 128, 128), F32)
            y, *st = rec_mixer(proj, *prm, zs, zs, jnp.zeros((bp, 4, 128), F32), jnp.zeros((bp, 4), F32),
                               C=REC_CHUNK, W=REC_SUB, front=front_p)
            xp = matmul_norm_res([y.reshape(mp, -1)], [w_out], g[1], xp, tm=TM_ROWS, t_pad=tp, front=front_p)
            for j in range(4):
                rec_p[j].append(st[j])
            proj = norm_matmul(xs, g[0], w_in, tm=tmp_s, tn=384).reshape(bs, TS, REC_N)
            y, *st = rec_mixer(proj, *prm, state_hgrn[p].astype(F32), state_mlstm_C[p].astype(F32),
                               state_mlstm_n[p].astype(F32), state_mlstm_m[p].astype(F32), C=TS, W=TS, front=front_s)
            xs = matmul_norm_res([y.reshape(ms, -1)], [w_out], g[1], xs, tm=tmr_s, t_pad=TS, front=front_s)
            for j in range(4):
                rec_s[j].append(st[j])
        else:
            lam_init = 0.8 - 0.6 * math.exp(-0.3 * l)
            w_in = _pad_cols(w_in_att[p], ATT_N).astype(BF16)
            w_out = w_out_att[p].astype(BF16)
            dl, gc = diff_lambda[p].astype(F32), g_norm_diff[p].astype(F32)
            proj = norm_matmul(xp, g[0], w_in, tm=tmp_p, tn=384)
            (qc, kc, kcb, vc, vcb, qd, kd, kdb, vd, vdb, qi, ki, kib, wi, vct, vdt) = att_prep(proj, *tabs_p, tm=KB)
            oc = diff_prompt(qc, kcb, vct, dl, gc, b=bp, t_pad=tp, front=front_p, lam_init=lam_init)
            od = dsa_prompt(qi, wi, qd, kib, kdb, vdt, b=bp, t_pad=tp, front=front_p, n_sel=sel_p)
            xp = matmul_norm_res([oc, od], [w_out[:512], w_out[512:]], g[1], xp, tm=TM_ROWS, t_pad=tp, front=front_p)
            for j, (a, shp) in enumerate([(kc, (H_C, 2 * DC)), (vc, (H_C, 2 * DC)), (kd, (DH_D,)), (vd, (DH_D,)), (ki, (D_IDX,))]):
                att_p[j].append(a.reshape((bp, tp) + shp)[:, front_p:])
            proj = norm_matmul(xs, g[0], w_in, tm=tmp_s, tn=384)
            (qc, kc, kcb, vc, vcb, qd, kd, kdb, vd, vdb, qi, ki, kib, wi, _, _) = att_prep(proj, *tabs_s, tm=ms)
            as_page = lambda a: jnp.pad(a.reshape(bs, TS, -1), ((0, 0), (0, PAGE_SIZE - TS), (0, 0)))
            rows_kh = lambda c: c.reshape(c.shape[0], PAGE_SIZE * H_C, 2 * DC)
            oc, keys = sample_diff_idx(pt_flat, qc, qi, wi, rows_kh(cache_diff_k[p]), rows_kh(cache_diff_v[p]), cache_idx_k[p],
                                       kcb, vcb, as_page(kib), dl, gc,
                                       b=bs, n_pages=n_pages, front=front_s, lam_init=lam_init)
            od = sample_dsa(pt_flat, keys, qd, cache_dsa_k[p], cache_dsa_v[p], as_page(kdb), as_page(vdb),
                            b=bs, n_pages=n_pages, n_sel=sel_s)
            xs = matmul_norm_res([oc, od], [w_out[:512], w_out[512:]], g[1], xs, tm=tmr_s, t_pad=TS, front=front_s)
            for j, (a, shp) in enumerate([(kc, (H_C, 2 * DC)), (vc, (H_C, 2 * DC)), (kd, (DH_D,)), (vd, (DH_D,)), (ki, (D_IDX,))]):
                att_s[j].append(a.reshape((bs, TS) + shp)[:, front_s:])
        w_up, w_down = w_ffn_up[l].astype(BF16), w_ffn_down[l].astype(BF16)
        cw, cb = ffn_conv_w[l].astype(F32), ffn_conv_b[l].astype(F32)
        up = norm_matmul(xp, g[2], w_up, tm=tmp_p, tn=512)
        conv_p.append(up[:, :D_FF].reshape(bp, tp, D_FF)[:, tp - (CONV_W - 1):])
        xp = ffn_down(up, cw, cb, w_down, g[3], xp, tm=TM_ROWS, t_pad=tp, front=front_p)
        up = norm_matmul(xs, g[2], w_up, tm=tmp_s, tn=512).reshape(bs, TS, 2 * D_FF)
        up = up.at[:, front_s - (CONV_W - 1):front_s, :D_FF].set(state_ffn_conv[l].astype(F32))
        conv_s.append(up[:, TS - (CONV_W - 1):, :D_FF])
        xs = ffn_down(up.reshape(ms, 2 * D_FF), cw, cb, w_down, g[3], xs, tm=tmr_s, t_pad=TS, front=front_s)

    y_p = xp.reshape(bp, tp, d)[:, front_p + N_META:]
    y_s = xs.reshape(bs, TS, d)[:, front_s:]
    return (y_p, y_s,
            jnp.stack(rec_p[0]), jnp.stack(rec_s[0]), jnp.stack(rec_p[1]), jnp.stack(rec_s[1]),
            jnp.stack(rec_p[2]), jnp.stack(rec_s[2]), jnp.stack(rec_p[3]), jnp.stack(rec_s[3]),
            jnp.stack(conv_p), jnp.stack(conv_s),
            jnp.stack(att_p[0]), jnp.stack(att_s[0]), jnp.stack(att_p[1]), jnp.stack(att_s[1]),
            jnp.stack(att_p[2]), jnp.stack(att_s[2]), jnp.stack(att_p[3]), jnp.stack(att_s[3]),
            jnp.stack(att_p[4]), jnp.stack(att_s[4]))
```

```python
import functools
import math

import jax
import jax.numpy as jnp
import numpy as np
from jax import lax
from jax.experimental import pallas as pl
from jax.experimental.pallas import tpu as pltpu

F32 = jnp.float32
BF16 = jnp.bfloat16

D_MODEL = 1024
N_META = 16
H_A, DK_A, DV_A = 4, 128, 128
H_B, DK_B, DV_B = 4, 128, 128
H_C, DC = 4, 64
H_D, DH_D = 4, 128
H_I, D_IDX = 4, 64
TOPK_MAX = 256
D_FF = 2816
CONV_W = 3
ROPE_THETA = 500000.0
ROT_FRAC = 4
EPS = 1e-6
PAGE_SIZE = 128
LANES = 128
NEG = -1e30

REC_N = 8 * 512 + LANES
ATT_N = 2560 + LANES
INT_MIN = -2 ** 31
VMEM_LIMIT = 56 * 1024 * 1024


def _cparams(sem):
    return pltpu.CompilerParams(dimension_semantics=sem, vmem_limit_bytes=VMEM_LIMIT)


def _rms(x, g):
    return x * lax.rsqrt(jnp.mean(x * x, axis=-1, keepdims=True) + EPS) * g


def _dot(a, b):
    return jnp.dot(a, b, preferred_element_type=F32)


def _dot_nt(a, b):
    return lax.dot_general(a, b, (((1,), (1,)), ((), ())), preferred_element_type=F32)


def _dot_tn(a, b):
    return lax.dot_general(a, b, (((0,), (0,)), ((), ())), preferred_element_type=F32)


def _dot_exact_lhs(tri, x):
    hi = x.astype(BF16)
    r1 = x - hi.astype(F32)
    mid = r1.astype(BF16)
    lo = (r1 - mid.astype(F32)).astype(BF16)
    return _dot(tri, hi) + _dot(tri, mid) + _dot(tri, lo)


def _row_valid(i, tm, t_pad, front):
    r = lax.broadcasted_iota(jnp.int32, (tm, 1), 0)
    if t_pad % tm == 0:
        t = (i % (t_pad // tm)) * tm + r
    else:
        assert tm % t_pad == 0 and (t_pad & (t_pad - 1)) == 0
        t = r & (t_pad - 1)
    return t >= front


def _norm_matmul_kernel(x_ref, g_ref, w_ref, o_ref, h_ref):
    @pl.when(pl.program_id(1) == 0)
    def _():
        h_ref[...] = _rms(x_ref[...], g_ref[...]).astype(BF16)

    o_ref[...] = _dot(h_ref[...], w_ref[...])


def norm_matmul(x, g, w, *, tm, tn):
    m, d = x.shape
    n = w.shape[1]
    assert m % tm == 0 and n % tn == 0
    return pl.pallas_call(
        _norm_matmul_kernel,
        grid=(m // tm, n // tn),
        in_specs=[pl.BlockSpec((tm, d), lambda i, j: (i, 0)),
                  pl.BlockSpec((1, d), lambda i, j: (0, 0)),
                  pl.BlockSpec((d, tn), lambda i, j: (0, j))],
        out_specs=pl.BlockSpec((tm, tn), lambda i, j: (i, j)),
        out_shape=jax.ShapeDtypeStruct((m, n), F32),
        scratch_shapes=[pltpu.VMEM((tm, d), BF16)],
        compiler_params=_cparams(("parallel", "arbitrary")),
        name="norm_matmul",
    )(x, g.reshape(1, d), w)


def _matmul_norm_res_kernel(*refs, n_in, tm, t_pad, front):
    a_refs, w_refs = refs[:n_in], refs[n_in:2 * n_in]
    g_ref, x_ref, o_ref = refs[2 * n_in:]
    acc = _dot(a_refs[0][...], w_refs[0][...])
    for a, w in zip(a_refs[1:], w_refs[1:]):
        acc = acc + _dot(a[...], w[...])
    out = x_ref[...] + _rms(acc, g_ref[...])
    o_ref[...] = jnp.where(_row_valid(pl.program_id(0), tm, t_pad, front), out, 0.0)


def matmul_norm_res(a_list, w_list, g, x, *, tm, t_pad, front):
    m, d = x.shape
    assert m % tm == 0
    n_in = len(a_list)
    in_specs = ([pl.BlockSpec((tm, a.shape[1]), lambda i: (i, 0)) for a in a_list]
                + [pl.BlockSpec(w.shape, lambda i: (0, 0)) for w in w_list]
                + [pl.BlockSpec((1, d), lambda i: (0, 0)), pl.BlockSpec((tm, d), lambda i: (i, 0))])
    return pl.pallas_call(
        functools.partial(_matmul_norm_res_kernel, n_in=n_in, tm=tm, t_pad=t_pad, front=front),
        grid=(m // tm,),
        in_specs=in_specs,
        out_specs=pl.BlockSpec((tm, d), lambda i: (i, 0)),
        out_shape=jax.ShapeDtypeStruct((m, d), F32),
        compiler_params=_cparams(("parallel",)),
        name="matmul_norm_res",
    )(*a_list, *w_list, g.reshape(1, d), x)


def _ffn_down_kernel(ug_ref, uv_ref, halo_ref, cw_ref, cb_ref, w_ref, g_ref, x_ref, o_ref, *, tm, t_pad, front):
    ug = ug_ref[...]
    row = lax.broadcasted_iota(jnp.int32, (tm, 1), 0)
    prev1 = jnp.where(row == 0, halo_ref[7:8, :], pltpu.roll(ug, 1, axis=0))
    prev2 = jnp.where(row == 0, halo_ref[6:7, :],
                      jnp.where(row == 1, halo_ref[7:8, :], pltpu.roll(ug, 2, axis=0)))
    conv = cb_ref[...] + cw_ref[0:1, :] * prev2 + cw_ref[1:2, :] * prev1 + cw_ref[2:3, :] * ug
    act = (conv * jax.nn.sigmoid(conv) * uv_ref[...]).astype(BF16)
    out = x_ref[...] + _rms(_dot(act, w_ref[...]), g_ref[...])
    o_ref[...] = jnp.where(_row_valid(pl.program_id(0), tm, t_pad, front), out, 0.0)


def ffn_down(up, cw, cb, w_down, g, x, *, tm, t_pad, front):
    m, d = x.shape
    f = w_down.shape[0]
    assert m % tm == 0 and tm % 8 == 0 and up.shape == (m, 2 * f)
    hb = tm // 8
    return pl.pallas_call(
        functools.partial(_ffn_down_kernel, tm=tm, t_pad=t_pad, front=front),
        grid=(m // tm,),
        in_specs=[pl.BlockSpec((tm, f), lambda i: (i, 0)),
                  pl.BlockSpec((tm, f), lambda i: (i, 1)),
                  pl.BlockSpec((8, f), lambda i: (jnp.maximum(i * hb - 1, 0), 0)),
                  pl.BlockSpec((CONV_W, f), lambda i: (0, 0)),
                  pl.BlockSpec((1, f), lambda i: (0, 0)),
                  pl.BlockSpec((f, d), lambda i: (0, 0)),
                  pl.BlockSpec((1, d), lambda i: (0, 0)),
                  pl.BlockSpec((tm, d), lambda i: (i, 0))],
        out_specs=pl.BlockSpec((tm, d), lambda i: (i, 0)),
        out_shape=jax.ShapeDtypeStruct((m, d), F32),
        compiler_params=_cparams(("parallel",)),
        name="ffn_down",
    )(up, up, up, cw, cb.reshape(1, f), w_down, g.reshape(1, d), x)


FF_CW = 256
FF_HALO = 16


def _ffn_fused_kernel(x_ref, halo_ref, gin_ref, wup_ref, cw_ref, cb_ref, wdn_ref, gout_ref, o_ref, conv_ref, acc_ref,
                      *, tm, t_pad, front):
    i = pl.program_id(0)
    x = x_ref[...]
    h = jnp.concatenate([_rms(halo_ref[...], gin_ref[...]), _rms(x, gin_ref[...])], axis=0).astype(BF16)
    for c in range(D_FF // FF_CW):
        cs = slice(c * FF_CW, (c + 1) * FF_CW)
        u = _dot(h, wup_ref[c])
        ug = u[:, :FF_CW]
        prev1 = pltpu.roll(ug, 1, axis=0)[FF_HALO:]
        prev2 = pltpu.roll(ug, 2, axis=0)[FF_HALO:]
        conv = cb_ref[:, cs] + cw_ref[0:1, cs] * prev2 + cw_ref[1:2, cs] * prev1 + cw_ref[2:3, cs] * ug[FF_HALO:]
        act = (conv * jax.nn.sigmoid(conv) * u[FF_HALO:, FF_CW:]).astype(BF16)
        part = _dot(act, wdn_ref[c])
        if c == 0:
            acc_ref[...] = part
        else:
            acc_ref[...] += part
        conv_ref[0, :, cs] = ug[FF_HALO + tm - SUB:, :]
    out = x + _rms(acc_ref[...], gout_ref[...])
    o_ref[...] = jnp.where(_row_valid(i, tm, t_pad, front), out, 0.0)


def ffn_fused(x, g_in, w_up, cw, cb, w_down, g_out, *, tm, t_pad, front):
    m, d = x.shape
    nc = D_FF // FF_CW
    assert m % tm == 0 and t_pad % tm == 0 and tm % FF_HALO == 0 and D_FF % FF_CW == 0
    per_seq = t_pad // tm
    wup = jnp.concatenate([w_up[:, :D_FF].reshape(d, nc, FF_CW), w_up[:, D_FF:].reshape(d, nc, FF_CW)], axis=2)
    wup = wup.transpose(1, 0, 2)
    wdn = w_down.reshape(nc, FF_CW, d)
    hb = tm // FF_HALO
    const = lambda shape: pl.BlockSpec(shape, lambda i: (0,) * len(shape), pipeline_mode=pl.Buffered(1))
    return pl.pallas_call(
        functools.partial(_ffn_fused_kernel, tm=tm, t_pad=t_pad, front=front),
        grid=(m // tm,),
        in_specs=[pl.BlockSpec((tm, d), lambda i: (i, 0)),
                  pl.BlockSpec((FF_HALO, d), lambda i: (jnp.maximum(i * hb - 1, 0), 0)),
                  const((1, d)), const((nc, d, 2 * FF_CW)), const((CONV_W, D_FF)), const((1, D_FF)),
                  const((nc, FF_CW, d)), const((1, d))],
        out_specs=[pl.BlockSpec((tm, d), lambda i: (i, 0)),
                   pl.BlockSpec((1, SUB, D_FF), lambda i: (i // per_seq, 0, 0))],
        out_shape=[jax.ShapeDtypeStruct((m, d), F32), jax.ShapeDtypeStruct((m // t_pad, SUB, D_FF), F32)],
        scratch_shapes=[pltpu.VMEM((tm, d), F32)],
        compiler_params=_cparams(("arbitrary",)),
        name="ffn_fused",
    )(x, x, g_in.reshape(1, d), wup, cw, cb.reshape(1, D_FF), wdn, g_out.reshape(1, d))


def _log_sigmoid(x):
    return jnp.minimum(x, 0.0) - jnp.log1p(jnp.exp(-jnp.abs(x)))


def _rec_kernel(proj_ref, lb_ref, bg_ref, ga_ref, gb_ref, s0_ref, c0_ref, n0_ref, m0_ref,
                y_ref, s_ref, c_ref, n_ref, m_ref, st_ref, kpad, bpad, vpad, *, C, W, front):
    ci = pl.program_id(1)
    nci = pl.num_programs(1)

    @pl.when(ci == 0)
    def _():
        for h in range(H_A):
            st_ref[h] = s0_ref[0, h].T
        c_ref[...] = c0_ref[...]
        n_ref[...] = n0_ref[...]
        m_ref[...] = m0_ref[...]
        kpad[...] = jnp.zeros_like(kpad)
        bpad[...] = jnp.zeros_like(bpad)
        vpad[...] = jnp.zeros_like(vpad)

    row = ci * C + lax.broadcasted_iota(jnp.int32, (C, 1), 0)
    valid = row >= front
    r_i = lax.broadcasted_iota(jnp.int32, (C, C), 0)
    c_i = lax.broadcasted_iota(jnp.int32, (C, C), 1)
    causal = r_i >= c_i
    tri = jnp.where(causal, 1.0, 0.0).astype(BF16)

    gates = proj_ref[0, :, 8 * 512:8 * 512 + LANES] + bg_ref[...]
    lf_all = jnp.where(valid, _log_sigmoid(gates), 0.0)
    ig_all = jnp.where(valid, gates, NEG)
    b_all = _dot_exact_lhs(tri, lf_all)
    b_all_t = b_all.T
    ig_all_t = ig_all.T

    for h in range(H_B):
        q = proj_ref[0, :, 2048 + h * 128:2048 + (h + 1) * 128]
        k = proj_ref[0, :, 2560 + h * 128:2560 + (h + 1) * 128] * (DK_B ** -0.5)
        v = proj_ref[0, :, 3072 + h * 128:3072 + (h + 1) * 128]
        og = proj_ref[0, :, 3584 + h * 128:3584 + (h + 1) * 128]
        qb, kb, vb = q.astype(BF16), k.astype(BF16), v.astype(BF16)
        b_col = b_all[:, H_B + h:H_B + h + 1]
        b_row = b_all_t[H_B + h:H_B + h + 1, :]
        i_col = ig_all[:, h:h + 1]
        i_row = ig_all_t[h:h + 1, :]
        m_prev = m_ref[0, h:h + 1, 0:1]
        dmat = jnp.where(causal, b_col - b_row + i_row, NEG)
        inter = b_col + m_prev
        mt = jnp.maximum(inter, jnp.max(dmat, axis=1, keepdims=True))
        w = jnp.exp(dmat - mt) * _dot_nt(qb, kb)
        wi = jnp.exp(inter - mt)
        c_st = c_ref[0, h]
        n_st = n_ref[0, h:h + 1, :]
        num = wi * _dot(qb, c_st.astype(BF16)) + _dot(w.astype(BF16), vb)
        den = wi * jnp.sum(q * n_st, axis=1, keepdims=True) + jnp.sum(w, axis=1, keepdims=True)
        hc = num / jnp.maximum(jnp.abs(den), jnp.exp(-mt))
        m_new = mt[C - 1:C, :]
        b_last = b_col[C - 1:C, :]
        decay = jnp.exp(b_last + m_prev - m_new)
        kw = k * jnp.exp(b_last - b_col + i_col - m_new)
        c_ref[0, h] = decay * c_st + _dot_tn(kw.astype(BF16), vb)
        n_ref[0, h:h + 1, :] = decay * n_st + jnp.sum(kw, axis=0, keepdims=True)
        m_ref[0, h:h + 1, :] = jnp.broadcast_to(m_new, (1, LANES))
        yb = _rms(hc, gb_ref[...]) * jax.nn.sigmoid(og)
        y_ref[0, :, 512 + h * 128:512 + (h + 1) * 128] = jnp.where(valid, yb, 0.0).astype(y_ref.dtype)

    rw = lax.broadcasted_iota(jnp.int32, (W, 1), 0)
    tri_w = jnp.where(lax.broadcasted_iota(jnp.int32, (W, W), 0) >= lax.broadcasted_iota(jnp.int32, (W, W), 1),
                      1.0, 0.0).astype(BF16)
    for h in range(H_A):
        lb = lb_ref[:, h * 128:(h + 1) * 128]
        st = st_ref[h]
        for j in range(C // W):
            u = h * (C // W) + j
            rows = slice(j * W, (j + 1) * W)
            vld = (ci * C + j * W + rw) >= front
            f = lb + (1.0 - lb) * jax.nn.sigmoid(proj_ref[0, rows, 512 + h * 128:512 + (h + 1) * 128])
            logf = jnp.where(vld, jnp.log(f), 0.0)
            k = jnp.where(vld, 1.0 - f, 0.0)
            q = proj_ref[0, rows, h * 128:(h + 1) * 128] * (DK_A ** -0.5)
            v = proj_ref[0, rows, 1024 + h * 128:1024 + (h + 1) * 128]
            bc = _dot_exact_lhs(tri_w, logf)
            o = _dot_nt((q * jnp.exp(bc)).astype(BF16), st.astype(BF16))
            kpad[u, W:2 * W, :] = k
            bpad[u, W:2 * W, :] = bc
            vpad[u, W:2 * W, :] = v
            for d in range(W):
                ok = rw >= d
                e = jnp.exp(jnp.where(ok, bc - bpad[u, W - d:2 * W - d, :], NEG))
                r = jnp.sum(q * kpad[u, W - d:2 * W - d, :] * e, axis=1, keepdims=True)
                o = o + r * vpad[u, W - d:2 * W - d, :]
            last = bc[W - 1:W, :]
            kdec = k * jnp.exp(last - bc)
            st = jnp.exp(last) * st + _dot_tn(v.astype(BF16), kdec.astype(BF16))
            ga = proj_ref[0, rows, 1536 + h * 128:1536 + (h + 1) * 128]
            ya = _rms(o, ga_ref[...]) * (ga * jax.nn.sigmoid(ga))
            y_ref[0, rows, h * 128:(h + 1) * 128] = jnp.where(vld, ya, 0.0).astype(y_ref.dtype)
        st_ref[h] = st

    @pl.when(ci == nci - 1)
    def _():
        for h in range(H_A):
            s_ref[0, h] = st_ref[h].T


def rec_mixer(proj, lb, bg, g_a, g_b, s0, c0, n0, m0, *, C, W, front):
    b, t, _ = proj.shape
    assert t % C == 0 and C % W == 0
    m0b = jnp.broadcast_to(m0[:, :, None], (b, H_B, LANES))
    bgp = jnp.zeros((1, LANES), F32).at[0, :2 * H_B].set(bg.reshape(-1))
    st_spec = pl.BlockSpec((1, 4, 128, 128), lambda i, c: (i, 0, 0, 0))
    v_spec = pl.BlockSpec((1, 4, LANES), lambda i, c: (i, 0, 0))
    row_spec = lambda n: pl.BlockSpec((1, n), lambda i, c: (0, 0))
    y, s, cc, n, m = pl.pallas_call(
        functools.partial(_rec_kernel, C=C, W=W, front=front),
        grid=(b, t // C),
        in_specs=[pl.BlockSpec((1, C, REC_N), lambda i, c: (i, c, 0)),
                  row_spec(512), row_spec(LANES), row_spec(128), row_spec(128),
                  st_spec, st_spec, v_spec, v_spec],
        out_specs=[pl.BlockSpec((1, C, 1024), lambda i, c: (i, c, 0)), st_spec, st_spec, v_spec, v_spec],
        out_shape=[jax.ShapeDtypeStruct((b, t, 1024), BF16),
                   jax.ShapeDtypeStruct((b, 4, 128, 128), F32),
                   jax.ShapeDtypeStruct((b, 4, 128, 128), F32),
                   jax.ShapeDtypeStruct((b, 4, LANES), F32),
                   jax.ShapeDtypeStruct((b, 4, LANES), F32)],
        scratch_shapes=[pltpu.VMEM((4, 128, 128), F32)] + [pltpu.VMEM((H_A * (C // W), 2 * W, 128), F32)] * 3,
        compiler_params=_cparams(("parallel", "arbitrary")),
        name="rec_mixer",
    )(proj, lb.reshape(1, 512), bgp, g_a.reshape(1, 128), g_b.reshape(1, 128), s0, c0, n0, m0b)
    return y, s, cc, n, m[:, :, 0]


def rope_tables(pos, period, half):
    r = 2 * half
    inv = ROPE_THETA ** (-jnp.arange(half, dtype=F32) * 2.0 / r)
    ang = pos.astype(F32)[:, None] * inv[None, :]
    cos, sin = jnp.cos(ang), jnp.sin(ang)
    lane = np.arange(LANES) % period
    idx = np.where(lane < half, lane, np.where(lane < r, lane - half, 0))
    first, second = jnp.asarray(lane < half), jnp.asarray((lane >= half) & (lane < r))
    c = jnp.where(first | second, cos[:, idx], 1.0)
    sa = jnp.where(first, -sin[:, idx], 0.0)
    sb = jnp.where(second, sin[:, idx], 0.0)
    return c, sa, sb


def _att_prep_kernel(p_ref, c64, a64, b64, c128, a128, b128,
                     qc_o, kc_o, kcb_o, vc_o, vcb_o, qd_o, kd_o, kdb_o, vd_o, vdb_o, qi_o, ki_o, kib_o, wi_o,
                     vct_o, vdt_o):
    def rot(x, c, sa, sb, half):
        return x * c[...] + pltpu.roll(x, LANES - half, axis=1) * sa[...] + pltpu.roll(x, half, axis=1) * sb[...]

    h64 = D_IDX // ROT_FRAC // 2
    h128 = DH_D // ROT_FRAC // 2
    for t in range(4):
        sl = slice(t * LANES, (t + 1) * LANES)
        qc_o[:, sl] = (rot(p_ref[:, sl], c64, a64, b64, h64) * (DC ** -0.5)).astype(BF16)
        kc = rot(p_ref[:, 512 + t * LANES:512 + (t + 1) * LANES], c64, a64, b64, h64)
        kc_o[:, sl] = kc
        kcb_o[:, sl] = kc.astype(BF16)
        vc = p_ref[:, 1024 + t * LANES:1024 + (t + 1) * LANES]
        vc_o[:, sl] = vc
        vcb_o[:, sl] = vc.astype(BF16)
        vct_o[0, sl, :] = vc.T.astype(BF16)
        qd_o[:, sl] = rot(p_ref[:, 1536 + t * LANES:1536 + (t + 1) * LANES], c128, a128, b128, h128).astype(BF16)
    kd = rot(p_ref[:, 2048:2176], c128, a128, b128, h128)
    kd_o[...] = kd
    kdb_o[...] = kd.astype(BF16)
    vd = p_ref[:, 2176:2304]
    vd_o[...] = vd
    vdb_o[...] = vd.astype(BF16)
    vdt_o[0] = vd.T.astype(BF16)
    for t in range(2):
        qi = rot(p_ref[:, 2304 + t * LANES:2304 + (t + 1) * LANES], c64, a64, b64, h64) * (D_IDX ** -0.5)
        qi_o[:, (2 * t) * LANES:(2 * t + 1) * LANES] = qi.astype(BF16)
        qi_o[:, (2 * t + 1) * LANES:(2 * t + 2) * LANES] = pltpu.roll(qi, D_IDX, axis=1).astype(BF16)
    last = p_ref[:, 2560:2688]
    ki = rot(last, c64, a64, b64, h64)[:, :D_IDX]
    ki_o[...] = ki
    kib_o[...] = ki.astype(BF16)
    wi_o[...] = pltpu.roll(last, D_IDX, axis=1) * (H_I ** -0.5)


def att_prep(proj, tabs64, tabs128, *, tm):
    m = proj.shape[0]
    p = tabs64[0].shape[0]
    assert m % tm == 0 and p % tm == 0
    nper = p // tm
    tab_spec = pl.BlockSpec((tm, LANES), lambda i: (i % nper, 0))
    outs = [(512, BF16), (512, F32), (512, BF16), (512, F32), (512, BF16), (512, BF16),
            (128, F32), (128, BF16), (128, F32), (128, BF16), (512, BF16), (D_IDX, F32), (D_IDX, BF16), (128, F32)]
    outs_t = [512, 128]
    return pl.pallas_call(
        _att_prep_kernel,
        grid=(m // tm,),
        in_specs=[pl.BlockSpec((tm, ATT_N), lambda i: (i, 0))] + [tab_spec] * 6,
        out_specs=([pl.BlockSpec((tm, w), lambda i: (i, 0)) for w, _ in outs]
                   + [pl.BlockSpec((1, w, tm), lambda i: (i, 0, 0)) for w in outs_t]),
        out_shape=([jax.ShapeDtypeStruct((m, w), dt) for w, dt in outs]
                   + [jax.ShapeDtypeStruct((m // tm, w, tm), BF16) for w in outs_t]),
        compiler_params=_cparams(("parallel",)),
        name="att_prep",
    )(proj, *tabs64, *tabs128)


QB = 128


def _diff_lambda(lam_ref, lam_init):
    dl = lam_ref[...]
    s1 = jnp.sum(dl[0:1, :] * dl[1:2, :], axis=1, keepdims=True)
    s2 = jnp.sum(dl[2:3, :] * dl[3:4, :], axis=1, keepdims=True)
    return jnp.exp(s1) - jnp.exp(s2) + lam_init


KB = 384
SUB = 8


def _group_max(x):
    return jnp.max(x.reshape(x.shape[0] // SUB, SUB, x.shape[1]), axis=0)


def _group_sum(x):
    return jnp.sum(x.reshape(x.shape[0] // SUB, SUB, x.shape[1]), axis=0)


def _n_key_blocks(i):
    return (i * QB + QB + KB - 1) // KB


def _key_visible(i, off, n_q_cols, front):
    krow = lax.broadcasted_iota(jnp.int32, (KB, 1), 0)
    qcol = i * QB + (lax.broadcasted_iota(jnp.int32, (1, n_q_cols), 1) & (QB - 1))
    return ((qcol - krow) >= off) & (krow >= front - off)


def _diff_prompt_kernel(q_ref, k_ref, vt_ref, lam_ref, g_ref, o_ref, s_ref, *, front, lam_init):
    i = pl.program_id(1)
    nkb = _n_key_blocks(i)
    lam = _diff_lambda(lam_ref, lam_init)
    lane = lax.broadcasted_iota(jnp.int32, (QB, LANES), 1)
    for h in range(H_C):
        cs = slice(h * LANES, (h + 1) * LANES)
        qh = q_ref[:, cs]
        qstack = jnp.concatenate([jnp.where(lane < DC, qh, jnp.zeros_like(qh)),
                                  jnp.where(lane >= DC, qh, jnp.zeros_like(qh))], axis=0)

        def pass_a(kb, mx):
            off = pl.multiple_of(kb * KB, KB)
            st = _dot_nt(k_ref[pl.ds(off, KB), cs], qstack)
            st = jnp.where(_key_visible(i, off, 2 * QB, front), st, NEG)
            s_ref[kb] = st
            return jnp.maximum(mx, _group_max(st))

        mx = lax.fori_loop(0, nkb, pass_a, jnp.full((SUB, 2 * QB), NEG, F32))
        m = jnp.max(mx, axis=0, keepdims=True)

        def pass_b(kb, carry):
            l8, acc = carry
            p = jnp.exp(s_ref[kb] - m)
            return l8 + _group_sum(p), acc + _dot(vt_ref[kb, cs, :], p.astype(BF16))

        l8, acc = lax.fori_loop(0, nkb, pass_b, (jnp.zeros((SUB, 2 * QB), F32), jnp.zeros((LANES, 2 * QB), F32)))
        a = acc / jnp.sum(l8, axis=0, keepdims=True)
        ot = a[:, :QB] - lam * a[:, QB:]
        ot = ot * lax.rsqrt(jnp.mean(ot * ot, axis=0, keepdims=True) + EPS) * g_ref[...] * (1.0 - lam_init)
        o_ref[:, cs] = ot.T.astype(o_ref.dtype)


def diff_prompt(q, k, vt, lam_p, g_c, *, b, t_pad, front, lam_init):
    nq, nkb = t_pad // QB, t_pad // KB
    return pl.pallas_call(
        functools.partial(_diff_prompt_kernel, front=front, lam_init=lam_init),
        grid=(b, nq),
        in_specs=[pl.BlockSpec((QB, 512), lambda bb, i: (bb * nq + i, 0)),
                  pl.BlockSpec((t_pad, 512), lambda bb, i: (bb, 0)),
                  pl.BlockSpec((nkb, 512, KB), lambda bb, i: (bb, 0, 0)),
                  pl.BlockSpec((4, DC), lambda bb, i: (0, 0)),
                  pl.BlockSpec((2 * DC, 1), lambda bb, i: (0, 0))],
        out_specs=pl.BlockSpec((QB, 512), lambda bb, i: (bb * nq + i, 0)),
        out_shape=jax.ShapeDtypeStruct((b * t_pad, 512), BF16),
        scratch_shapes=[pltpu.VMEM((nkb, KB, 2 * QB), F32)],
        compiler_params=_cparams(("parallel", "arbitrary")),
        name="diff_prompt",
    )(q, k, vt, lam_p, g_c.reshape(2 * DC, 1))


NINF = float("-inf")


def _kth_threshold(count_ge, shape, n_sel):
    zero_i = jnp.zeros(shape, jnp.int32)
    neg = jnp.where(count_ge(jnp.zeros(shape, F32)) < n_sel, 1, 0)
    sign = jnp.where(neg == 1, jnp.int32(INT_MIN), 0)

    def bit_body(t, mag):
        cand = mag | lax.shift_left(jnp.int32(1), 30 - t)
        enough = jnp.where(count_ge(pltpu.bitcast(cand | sign, F32)) >= n_sel, 1, 0)
        return jnp.where(enough + neg == 1, cand, mag)

    mag = lax.fori_loop(0, 31, bit_body, zero_i)
    tau = pltpu.bitcast(jnp.where(neg == 1, (mag + 1) | sign, mag), F32)
    ninf = jnp.full(shape, NINF, F32)
    return jnp.where(count_ge(ninf) >= n_sel, tau, ninf)


def _dsa_prompt_kernel(qi_ref, wi_ref, qd_ref, ki_ref, kd_ref, vdt_ref, o_ref, sc_ref, s_ref, *, front, n_sel):
    i = pl.program_id(1)
    nkb = _n_key_blocks(i)
    qi_all = jnp.concatenate([qi_ref[:, h * LANES:h * LANES + D_IDX] for h in range(H_I)], axis=0)
    qd_all = jnp.concatenate([qd_ref[:, h * LANES:(h + 1) * LANES] for h in range(H_D)], axis=0)
    wt = wi_ref[...].T
    w_row = jnp.concatenate([wt[h:h + 1, :] for h in range(H_I)], axis=1)

    def stage1(kb, c):
        off = pl.multiple_of(kb * KB, KB)
        sct = jnp.maximum(_dot_nt(ki_ref[pl.ds(off, KB), :], qi_all), 0.0) * w_row
        score = sct[:, 0:QB] + sct[:, QB:2 * QB] + sct[:, 2 * QB:3 * QB] + sct[:, 3 * QB:4 * QB]
        sc_ref[kb] = jnp.where(_key_visible(i, off, QB, front), score, NINF)
        return c

    lax.fori_loop(0, nkb, stage1, 0)

    def count(pred):
        def body(kb, acc):
            return acc + _group_sum(jnp.where(pred(sc_ref[kb]), 1, 0))
        return jnp.sum(lax.fori_loop(0, nkb, body, jnp.zeros((SUB, QB), jnp.int32)), axis=0, keepdims=True)

    tau = _kth_threshold(lambda cand: count(lambda sc: sc >= cand), (1, QB), n_sel)
    need = (n_sel - count(lambda sc: sc > tau)).astype(F32)

    strict_lower = jnp.where(lax.broadcasted_iota(jnp.int32, (KB, KB), 1) < lax.broadcasted_iota(jnp.int32, (KB, KB), 0),
                             1.0, 0.0).astype(BF16)

    def stage3(kb, carry):
        before, mx = carry
        off = pl.multiple_of(kb * KB, KB)
        sc = sc_ref[kb]
        eq = sc == tau
        eqf = jnp.where(eq, 1.0, 0.0)
        rank = _dot(strict_lower, eqf.astype(BF16)) + before
        sel = ((sc > tau) | (eq & (rank < need))) & (sc > NINF)
        sdt = _dot_nt(kd_ref[pl.ds(off, KB), :], qd_all) * (DH_D ** -0.5)
        sdt = jnp.where(jnp.concatenate([sel] * H_D, axis=1), sdt, NEG)
        s_ref[kb] = sdt
        return before + jnp.sum(eqf, axis=0, keepdims=True), jnp.maximum(mx, _group_max(sdt))

    _, mx = lax.fori_loop(0, nkb, stage3, (jnp.zeros((1, QB), F32), jnp.full((SUB, H_D * QB), NEG, F32)))
    m = jnp.max(mx, axis=0, keepdims=True)

    def stage4(kb, carry):
        l8, acc = carry
        p = jnp.exp(s_ref[kb] - m)
        return l8 + _group_sum(p), acc + _dot(vdt_ref[kb], p.astype(BF16))

    l8, acc = lax.fori_loop(0, nkb, stage4, (jnp.zeros((SUB, H_D * QB), F32), jnp.zeros((DH_D, H_D * QB), F32)))
    ot = acc / jnp.sum(l8, axis=0, keepdims=True)
    for h in range(H_D):
        o_ref[:, h * LANES:(h + 1) * LANES] = ot[:, h * QB:(h + 1) * QB].T.astype(o_ref.dtype)


def dsa_prompt(qi, wi, qd, ki, kd, vdt, *, b, t_pad, front, n_sel):
    nq, nkb = t_pad // QB, t_pad // KB
    qspec = lambda w: pl.BlockSpec((QB, w), lambda bb, i: (bb * nq + i, 0))
    kspec = lambda w: pl.BlockSpec((t_pad, w), lambda bb, i: (bb, 0))
    return pl.pallas_call(
        functools.partial(_dsa_prompt_kernel, front=front, n_sel=n_sel),
        grid=(b, nq),
        in_specs=[qspec(512), qspec(128), qspec(512), kspec(D_IDX), kspec(DH_D),
                  pl.BlockSpec((nkb, DH_D, KB), lambda bb, i: (bb, 0, 0))],
        out_specs=qspec(512),
        out_shape=jax.ShapeDtypeStruct((b * t_pad, 512), BF16),
        scratch_shapes=[pltpu.VMEM((nkb, KB, QB), F32), pltpu.VMEM((nkb, KB, H_D * QB), F32)],
        compiler_params=_cparams(("parallel", "arbitrary")),
        name="dsa_prompt",
    )(qi, wi, qd, ki, kd, vdt)


TS = 16


PG = 4


def _page_map(g, n_pages, nd):
    def index(bb, j, pt):
        return (pt[bb * n_pages + jnp.minimum(j * PG + g, n_pages - 1)],) + (0,) * nd
    return index


def _online_softmax_update(rows, s, v, m_ref, l_ref, acc_ref):
    m_old = m_ref[rows, :]
    m_new = jnp.maximum(m_old, jnp.max(s, axis=1, keepdims=True))
    alpha = jnp.exp(m_old - m_new)
    p = jnp.where(s > 0.5 * NEG, jnp.exp(s - m_new), 0.0)
    l_ref[rows, :] = alpha * l_ref[rows, :] + jnp.sum(p, axis=1, keepdims=True)
    acc_ref[rows, :] = alpha * acc_ref[rows, :] + _dot(p.astype(BF16), v)
    m_ref[rows, :] = m_new


def _sample_a_kernel(pt_ref, qc_ref, qi_ref, wi_ref, *refs, n_steps, front, lam_init):
    ck_refs, cv_refs, cik_refs = refs[:PG], refs[PG:2 * PG], refs[2 * PG:3 * PG]
    kn_ref, vn_ref, kin_ref, lam_ref, g_ref, o_ref, keys_ref, m_ref, l_ref, acc_ref = refs[3 * PG:]
    j = pl.program_id(1)
    lane = lax.broadcasted_iota(jnp.int32, (TS, LANES), 1)

    @pl.when(j == 0)
    def _():
        m_ref[...] = jnp.full(m_ref.shape, NEG, F32)
        l_ref[...] = jnp.zeros(l_ref.shape, F32)
        acc_ref[...] = jnp.zeros(acc_ref.shape, F32)

    def qstack(h):
        qh = qc_ref[:, h * LANES:(h + 1) * LANES]
        return jnp.concatenate([jnp.where(lane < DC, qh, jnp.zeros_like(qh)),
                                jnp.where(lane >= DC, qh, jnp.zeros_like(qh))], axis=0)

    def head_rows(h):
        return slice(h * 2 * TS, (h + 1) * 2 * TS)

    qi_all = jnp.concatenate([qi_ref[:, h * LANES:h * LANES + D_IDX] for h in range(H_I)], axis=0)
    w_col = jnp.concatenate([wi_ref[:, h:h + 1] for h in range(H_I)], axis=0)

    def idx_scores(kip):
        sc = jnp.maximum(_dot_nt(qi_all, kip), 0.0) * w_col
        return sc[0:TS] + sc[TS:2 * TS] + sc[2 * TS:3 * TS] + sc[3 * TS:4 * TS]

    @pl.when(j < n_steps)
    def _():
        for h in range(H_C):
            rows_h = pl.ds(h, PAGE_SIZE, stride=H_C)
            k = jnp.concatenate([r[0, rows_h, :].astype(BF16) for r in ck_refs], axis=0)
            v = jnp.concatenate([r[0, rows_h, :].astype(BF16) for r in cv_refs], axis=0)
            _online_softmax_update(head_rows(h), _dot_nt(qstack(h), k), v, m_ref, l_ref, acc_ref)
        keys_ref[0] = idx_scores(jnp.concatenate([r[0].astype(BF16) for r in cik_refs], axis=0))

    @pl.when(j == n_steps)
    def _():
        kr = lax.broadcasted_iota(jnp.int32, (1, TS), 1)
        q_of_row = lax.broadcasted_iota(jnp.int32, (2 * TS, 1), 0) & (TS - 1)
        ok = (kr >= front) & (kr <= q_of_row)
        for h in range(H_C):
            cs = slice(h * LANES, (h + 1) * LANES)
            s = jnp.where(ok, _dot_nt(qstack(h), kn_ref[:, cs]), NEG)
            _online_softmax_update(head_rows(h), s, vn_ref[:, cs], m_ref, l_ref, acc_ref)
        krp = lax.broadcasted_iota(jnp.int32, (1, PAGE_SIZE), 1)
        okq = (krp >= front) & (krp < TS) & (krp <= lax.broadcasted_iota(jnp.int32, (TS, 1), 0))
        keys_ref[0] = jnp.concatenate([jnp.where(okq, idx_scores(kin_ref[0]), NINF),
                                       jnp.full((TS, (PG - 1) * PAGE_SIZE), NINF, F32)], axis=1)
        lam = _diff_lambda(lam_ref, lam_init)
        for h in range(H_C):
            a = acc_ref[head_rows(h), :] / l_ref[head_rows(h), :]
            o = a[0:TS] - lam * a[TS:2 * TS]
            o_ref[:, h * LANES:(h + 1) * LANES] = (_rms(o, g_ref[...]) * (1.0 - lam_init)).astype(o_ref.dtype)


def sample_diff_idx(pt, qc, qi, wi, ck, cv, cik, kn, vn, kin, lam_p, g_c, *, b, n_pages, front, lam_init):
    assert n_pages % PG == 0
    n_steps = n_pages // PG
    rows = H_C * 2 * TS
    qspec = lambda w: pl.BlockSpec((TS, w), lambda bb, j, pt: (bb, 0))
    pages4 = [pl.BlockSpec((1, PAGE_SIZE * H_C, 2 * DC), _page_map(g, n_pages, 2)) for g in range(PG)]
    pages_i = [pl.BlockSpec((1, PAGE_SIZE, D_IDX), _page_map(g, n_pages, 2)) for g in range(PG)]
    return pl.pallas_call(
        functools.partial(_sample_a_kernel, n_steps=n_steps, front=front, lam_init=lam_init),
        grid_spec=pltpu.PrefetchScalarGridSpec(
            num_scalar_prefetch=1,
            grid=(b, n_steps + 1),
            in_specs=[qspec(512), qspec(512), qspec(128)] + pages4 + pages4 + pages_i
                     + [qspec(512), qspec(512), pl.BlockSpec((1, PAGE_SIZE, D_IDX), lambda bb, j, pt: (bb, 0, 0)),
                        pl.BlockSpec((4, DC), lambda bb, j, pt: (0, 0)),
                        pl.BlockSpec((1, 2 * DC), lambda bb, j, pt: (0, 0))],
            out_specs=[qspec(512), pl.BlockSpec((1, TS, PG * PAGE_SIZE), lambda bb, j, pt: (bb, 0, j))],
            scratch_shapes=[pltpu.VMEM((rows, 1), F32), pltpu.VMEM((rows, 1), F32), pltpu.VMEM((rows, 2 * DC), F32)]),
        out_shape=[jax.ShapeDtypeStruct((b * TS, 512), BF16),
                   jax.ShapeDtypeStruct((b, TS, (n_steps + 1) * PG * PAGE_SIZE), F32)],
        compiler_params=_cparams(("parallel", "arbitrary")),
        name="sample_diff_idx",
    )(pt, qc, qi, wi, *([ck] * PG), *([cv] * PG), *([cik] * PG), kn, vn, kin, lam_p, g_c.reshape(1, 2 * DC))


def _sample_b_kernel(pt_ref, keys_all_ref, keys_ref, qd_ref, *refs, n_steps, n_sel):
    ck_refs, cv_refs = refs[:PG], refs[PG:2 * PG]
    kn_ref, vn_ref, o_ref, tau_ref, need_ref, before_ref, m_ref, l_ref, acc_ref = refs[2 * PG:]
    j = pl.program_id(1)

    @pl.when(j == 0)
    def _():
        keys = keys_all_ref[0]

        def count_ge(cand):
            return jnp.sum(jnp.where(keys >= cand, 1, 0), axis=1, keepdims=True)

        tau = _kth_threshold(count_ge, (TS, 1), n_sel)
        tau_ref[...] = tau
        need_ref[...] = (n_sel - jnp.sum(jnp.where(keys > tau, 1, 0), axis=1, keepdims=True)).astype(F32)
        before_ref[...] = jnp.zeros(before_ref.shape, F32)
        m_ref[...] = jnp.full(m_ref.shape, NEG, F32)
        l_ref[...] = jnp.zeros(l_ref.shape, F32)
        acc_ref[...] = jnp.zeros(acc_ref.shape, F32)

    strict_upper = jnp.where(lax.broadcasted_iota(jnp.int32, (PAGE_SIZE, PAGE_SIZE), 0)
                             < lax.broadcasted_iota(jnp.int32, (PAGE_SIZE, PAGE_SIZE), 1), 1.0, 0.0).astype(BF16)
    qd_all = jnp.concatenate([qd_ref[:, h * LANES:(h + 1) * LANES] for h in range(H_D)], axis=0)

    def select(key, before):
        tau = tau_ref[...]
        eq = key == tau
        eqf = jnp.where(eq, 1.0, 0.0)
        rank = _dot(eqf.astype(BF16), strict_upper) + before
        sel = ((key > tau) | (eq & (rank < need_ref[...]))) & (key > NINF)
        return sel, before + jnp.sum(eqf, axis=1, keepdims=True)

    def attend(s, sel, v):
        s = jnp.where(jnp.concatenate([sel] * H_D, axis=0), s * (DH_D ** -0.5), NEG)
        _online_softmax_update(slice(None), s, v, m_ref, l_ref, acc_ref)

    @pl.when(j < n_steps)
    def _():
        key = keys_ref[0]
        before = before_ref[...]
        sels = []
        for g in range(PG):
            sel, before = select(key[:, g * PAGE_SIZE:(g + 1) * PAGE_SIZE], before)
            sels.append(sel)
        before_ref[...] = before
        k = jnp.concatenate([r[0].astype(BF16) for r in ck_refs], axis=0)
        v = jnp.concatenate([r[0].astype(BF16) for r in cv_refs], axis=0)
        attend(_dot_nt(qd_all, k), jnp.concatenate(sels, axis=1), v)

    @pl.when(j == n_steps)
    def _():
        sel, _ = select(keys_ref[0][:, :PAGE_SIZE], before_ref[...])
        attend(_dot_nt(qd_all, kn_ref[0]), sel, vn_ref[0])
        o = acc_ref[...] / l_ref[...]
        for h in range(H_D):
            o_ref[:, h * LANES:(h + 1) * LANES] = o[h * TS:(h + 1) * TS].astype(o_ref.dtype)


def sample_dsa(pt, keys, qd, ck, cv, kn, vn, *, b, n_pages, n_sel):
    assert n_pages % PG == 0
    n_steps = n_pages // PG
    rows = H_D * TS
    nk = keys.shape[2]
    pages = [pl.BlockSpec((1, PAGE_SIZE, DH_D), _page_map(g, n_pages, 2)) for g in range(PG)]
    nspec = pl.BlockSpec((1, PAGE_SIZE, DH_D), lambda bb, j, pt: (bb, 0, 0))
    return pl.pallas_call(
        functools.partial(_sample_b_kernel, n_steps=n_steps, n_sel=n_sel),
        grid_spec=pltpu.PrefetchScalarGridSpec(
            num_scalar_prefetch=1,
            grid=(b, n_steps + 1),
            in_specs=[pl.BlockSpec((1, TS, nk), lambda bb, j, pt: (bb, 0, 0)),
                      pl.BlockSpec((1, TS, PG * PAGE_SIZE), lambda bb, j, pt: (bb, 0, j)),
                      pl.BlockSpec((TS, 512), lambda bb, j, pt: (bb, 0))] + pages + pages + [nspec, nspec],
            out_specs=pl.BlockSpec((TS, 512), lambda bb, j, pt: (bb, 0)),
            scratch_shapes=[pltpu.VMEM((TS, 1), F32), pltpu.VMEM((TS, 1), F32), pltpu.VMEM((TS, 1), F32),
                            pltpu.VMEM((rows, 1), F32), pltpu.VMEM((rows, 1), F32), pltpu.VMEM((rows, DH_D), F32)]),
        out_shape=jax.ShapeDtypeStruct((b * TS, 512), BF16),
        compiler_params=_cparams(("parallel", "arbitrary")),
        name="sample_dsa",
    )(pt, keys, keys, qd, *([ck] * PG), *([cv] * PG), kn, vn)


REC_CHUNK = 64
REC_SUB = 16
TM_PROJ = 512
TM_REC, TN_REC = 1408, 1408
TM_ATT = 704
TM_ROWS = 384
TM_FFN = 528


def _pad_cols(w, n):
    return jnp.pad(w, ((0, 0), (0, n - w.shape[1])))


def _tile_rows(m, pref):
    return pref if m % pref == 0 else m


def kernel(x_prompt, x_sample, state_hgrn, state_mlstm_C, state_mlstm_n, state_mlstm_m, state_ffn_conv, cache_diff_k, cache_diff_v, cache_dsa_k, cache_dsa_v, cache_idx_k, page_table, meta_tokens, norm_gains, w_in_rec, b_gates_rec, lb_logits, g_norm_hgrn, g_norm_mlstm, w_out_rec, w_in_att, diff_lambda, g_norm_diff, w_out_att, w_ffn_up, ffn_conv_w, ffn_conv_b, w_ffn_down):
    bp, t_in, d = x_prompt.shape
    bs, t_s, _ = x_sample.shape
    depth = norm_gains.shape[0]
    n_pages = page_table.shape[1]
    past_len = n_pages * PAGE_SIZE
    real_p = N_META + t_in
    tp = -(-real_p // QB) * QB
    front_p = tp - real_p
    front_s = TS - t_s
    assert tp % REC_CHUNK == 0 and tp % TM_ROWS == 0 and front_p >= CONV_W - 1 and front_s >= CONV_W - 1
    mp, ms = bp * tp, bs * TS

    meta = jnp.broadcast_to(meta_tokens.astype(x_prompt.dtype)[None], (bp, N_META, d))
    xp = jnp.concatenate([jnp.zeros((bp, front_p, d), x_prompt.dtype), meta, x_prompt], axis=1).reshape(mp, d)
    xs = jnp.concatenate([jnp.zeros((bs, front_s, d), x_sample.dtype), x_sample], axis=1).reshape(ms, d)
    lb_all = jnp.cumsum(jax.nn.softmax(lb_logits.astype(F32), axis=0), axis=0)
    pt_flat = page_table.reshape(-1).astype(jnp.int32)
    sel_p = min(TOPK_MAX, t_in // 4)
    sel_s = min(TOPK_MAX, (past_len + t_s) // 4)
    tmp_s = _tile_rows(ms, TM_PROJ)
    tmr_s = _tile_rows(ms, TM_ROWS)

    pos_p = jnp.arange(tp, dtype=jnp.int32) - front_p
    pos_s = jnp.tile(past_len + jnp.arange(TS, dtype=jnp.int32) - front_s, ms // TS)
    tabs_p = (rope_tables(pos_p, DC, DC // ROT_FRAC // 2), rope_tables(pos_p, DH_D, DH_D // ROT_FRAC // 2))
    tabs_s = (rope_tables(pos_s, DC, DC // ROT_FRAC // 2), rope_tables(pos_s, DH_D, DH_D // ROT_FRAC // 2))

    rec_p, rec_s = [[], [], [], []], [[], [], [], []]
    att_p, att_s = [[], [], [], [], []], [[], [], [], [], []]
    conv_p, conv_s = [], []
    for l in range(depth):
        p = l // 2
        g = norm_gains[l].astype(F32)
        if l % 2 == 0:
            w_in = _pad_cols(w_in_rec[p], REC_N).astype(BF16)
            w_out = w_out_rec[p].astype(BF16)
            prm = (lb_all[p], b_gates_rec[p].astype(F32), g_norm_hgrn[p].astype(F32), g_norm_mlstm[p].astype(F32))
            proj = norm_matmul(xp, g[0], w_in, tm=_tile_rows(mp, TM_REC), tn=TN_REC).reshape(bp, tp, REC_N)
            zs = jnp.zeros((bp, 4, 128, 128), F32)
            y, *st = rec_mixer(proj, *prm, zs, zs, jnp.zeros((bp, 4, 128), F32), jnp.zeros((bp, 4), F32),
                               C=REC_CHUNK, W=REC_SUB, front=front_p)
            xp = matmul_norm_res([y.reshape(mp, -1)], [w_out], g[1], xp, tm=TM_ROWS, t_pad=tp, front=front_p)
            for j in range(4):
                rec_p[j].append(st[j])
            proj = norm_matmul(xs, g[0], w_in, tm=tmp_s, tn=384).reshape(bs, TS, REC_N)
            y, *st = rec_mixer(proj, *prm, state_hgrn[p].astype(F32), state_mlstm_C[p].astype(F32),
                               state_mlstm_n[p].astype(F32), state_mlstm_m[p].astype(F32), C=TS, W=TS, front=front_s)
            xs = matmul_norm_res([y.reshape(ms, -1)], [w_out], g[1], xs, tm=tmr_s, t_pad=TS, front=front_s)
            for j in range(4):
                rec_s[j].append(st[j])
        else:
            lam_init = 0.8 - 0.6 * math.exp(-0.3 * l)
            w_in = _pad_cols(w_in_att[p], ATT_N).astype(BF16)
            w_out = w_out_att[p].astype(BF16)
            dl, gc = diff_lambda[p].astype(F32), g_norm_diff[p].astype(F32)
            proj = norm_matmul(xp, g[0], w_in, tm=_tile_rows(mp, TM_ATT), tn=ATT_N)
            (qc, kc, kcb, vc, vcb, qd, kd, kdb, vd, vdb, qi, ki, kib, wi, vct, vdt) = att_prep(proj, *tabs_p, tm=KB)
            oc = diff_prompt(qc, kcb, vct, dl, gc, b=bp, t_pad=tp, front=front_p, lam_init=lam_init)
            od = dsa_prompt(qi, wi, qd, kib, kdb, vdt, b=bp, t_pad=tp, front=front_p, n_sel=sel_p)
            xp = matmul_norm_res([oc, od], [w_out[:512], w_out[512:]], g[1], xp, tm=TM_ROWS, t_pad=tp, front=front_p)
            for j, (a, shp) in enumerate([(kc, (H_C, 2 * DC)), (vc, (H_C, 2 * DC)), (kd, (DH_D,)), (vd, (DH_D,)), (ki, (D_IDX,))]):
                att_p[j].append(a.reshape((bp, tp) + shp)[:, front_p:])
            proj = norm_matmul(xs, g[0], w_in, tm=tmp_s, tn=384)
            (qc, kc, kcb, vc, vcb, qd, kd, kdb, vd, vdb, qi, ki, kib, wi, _, _) = att_prep(proj, *tabs_s, tm=ms)
            as_page = lambda a: jnp.pad(a.reshape(bs, TS, -1), ((0, 0), (0, PAGE_SIZE - TS), (0, 0)))
            rows_kh = lambda c: c.reshape(c.shape[0], PAGE_SIZE * H_C, 2 * DC)
            oc, keys = sample_diff_idx(pt_flat, qc, qi, wi, rows_kh(cache_diff_k[p]), rows_kh(cache_diff_v[p]), cache_idx_k[p],
                                       kcb, vcb, as_page(kib), dl, gc,
                                       b=bs, n_pages=n_pages, front=front_s, lam_init=lam_init)
            od = sample_dsa(pt_flat, keys, qd, cache_dsa_k[p], cache_dsa_v[p], as_page(kdb), as_page(vdb),
                            b=bs, n_pages=n_pages, n_sel=sel_s)
            xs = matmul_norm_res([oc, od], [w_out[:512], w_out[512:]], g[1], xs, tm=tmr_s, t_pad=TS, front=front_s)
            for j, (a, shp) in enumerate([(kc, (H_C, 2 * DC)), (vc, (H_C, 2 * DC)), (kd, (DH_D,)), (vd, (DH_D,)), (ki, (D_IDX,))]):
                att_s[j].append(a.reshape((bs, TS) + shp)[:, front_s:])
        w_up, w_down = w_ffn_up[l].astype(BF16), w_ffn_down[l].astype(BF16)
        cw, cb = ffn_conv_w[l].astype(F32), ffn_conv_b[l].astype(F32)
        xp, tail = ffn_fused(xp, g[2], w_up, cw, cb, w_down, g[3], tm=TM_FFN, t_pad=tp, front=front_p)
        conv_p.append(tail[:, SUB - (CONV_W - 1):])
        up = norm_matmul(xs, g[2], w_up, tm=tmp_s, tn=512).reshape(bs, TS, 2 * D_FF)
        up = up.at[:, front_s - (CONV_W - 1):front_s, :D_FF].set(state_ffn_conv[l].astype(F32))
        conv_s.append(up[:, TS - (CONV_W - 1):, :D_FF])
        xs = ffn_down(up.reshape(ms, 2 * D_FF), cw, cb, w_down, g[3], xs, tm=tmr_s, t_pad=TS, front=front_s)

    y_p = xp.reshape(bp, tp, d)[:, front_p + N_META:]
    y_s = xs.reshape(bs, TS, d)[:, front_s:]
    return (y_p, y_s,
            jnp.stack(rec_p[0]), jnp.stack(rec_s[0]), jnp.stack(rec_p[1]), jnp.stack(rec_s[1]),
            jnp.stack(rec_p[2]), jnp.stack(rec_s[2]), jnp.stack(rec_p[3]), jnp.stack(rec_s[3]),
            jnp.stack(conv_p), jnp.stack(conv_s),
            jnp.stack(att_p[0]), jnp.stack(att_s[0]), jnp.stack(att_p[1]), jnp.stack(att_s[1]),
            jnp.stack(att_p[2]), jnp.stack(att_s[2]), jnp.stack(att_p[3]), jnp.stack(att_s[3]),
            jnp.stack(att_p[4]), jnp.stack(att_s[4]))
```

```python
import functools
import math

import jax
import jax.numpy as jnp
import numpy as np
from jax import lax
from jax.experimental import pallas as pl
from jax.experimental.pallas import tpu as pltpu

F32 = jnp.float32
BF16 = jnp.bfloat16

D_MODEL = 1024
N_META = 16
H_A, DK_A, DV_A = 4, 128, 128
H_B, DK_B, DV_B = 4, 128, 128
H_C, DC = 4, 64
H_D, DH_D = 4, 128
H_I, D_IDX = 4, 64
TOPK_MAX = 256
D_FF = 2816
CONV_W = 3
ROPE_THETA = 500000.0
ROT_FRAC = 4
EPS = 1e-6
PAGE_SIZE = 128
LANES = 128
NEG = -1e30

REC_N = 8 * 512 + LANES
ATT_N = 2560 + LANES
INT_MIN = -2 ** 31
VMEM_LIMIT = 56 * 1024 * 1024


def _cparams(sem):
    return pltpu.CompilerParams(dimension_semantics=sem, vmem_limit_bytes=VMEM_LIMIT)


def _rms(x, g):
    return x * lax.rsqrt(jnp.mean(x * x, axis=-1, keepdims=True) + EPS) * g


def _dot(a, b):
    return jnp.dot(a, b, preferred_element_type=F32)


def _dot_nt(a, b):
    return lax.dot_general(a, b, (((1,), (1,)), ((), ())), preferred_element_type=F32)


def _dot_tn(a, b):
    return lax.dot_general(a, b, (((0,), (0,)), ((), ())), preferred_element_type=F32)


def _dot_exact_lhs(tri, x):
    hi = x.astype(BF16)
    r1 = x - hi.astype(F32)
    mid = r1.astype(BF16)
    lo = (r1 - mid.astype(F32)).astype(BF16)
    return _dot(tri, hi) + _dot(tri, mid) + _dot(tri, lo)


def _row_valid(i, tm, t_pad, front):
    r = lax.broadcasted_iota(jnp.int32, (tm, 1), 0)
    if t_pad % tm == 0:
        t = (i % (t_pad // tm)) * tm + r
    else:
        assert tm % t_pad == 0 and (t_pad & (t_pad - 1)) == 0
        t = r & (t_pad - 1)
    return t >= front


def _norm_matmul_kernel(x_ref, g_ref, w_ref, o_ref, h_ref):
    @pl.when(pl.program_id(1) == 0)
    def _():
        h_ref[...] = _rms(x_ref[...], g_ref[...]).astype(BF16)

    o_ref[...] = _dot(h_ref[...], w_ref[...])


def norm_matmul(x, g, w, *, tm, tn):
    m, d = x.shape
    n = w.shape[1]
    assert m % tm == 0 and n % tn == 0
    return pl.pallas_call(
        _norm_matmul_kernel,
        grid=(m // tm, n // tn),
        in_specs=[pl.BlockSpec((tm, d), lambda i, j: (i, 0)),
                  pl.BlockSpec((1, d), lambda i, j: (0, 0)),
                  pl.BlockSpec((d, tn), lambda i, j: (0, j))],
        out_specs=pl.BlockSpec((tm, tn), lambda i, j: (i, j)),
        out_shape=jax.ShapeDtypeStruct((m, n), F32),
        scratch_shapes=[pltpu.VMEM((tm, d), BF16)],
        compiler_params=_cparams(("parallel", "arbitrary")),
        name="norm_matmul",
    )(x, g.reshape(1, d), w)


def _matmul_norm_res_kernel(*refs, n_in, tm, t_pad, front):
    a_refs, w_refs = refs[:n_in], refs[n_in:2 * n_in]
    g_ref, x_ref, o_ref = refs[2 * n_in:]
    acc = _dot(a_refs[0][...], w_refs[0][...])
    for a, w in zip(a_refs[1:], w_refs[1:]):
        acc = acc + _dot(a[...], w[...])
    out = x_ref[...] + _rms(acc, g_ref[...])
    o_ref[...] = jnp.where(_row_valid(pl.program_id(0), tm, t_pad, front), out, 0.0)


def matmul_norm_res(a_list, w_list, g, x, *, tm, t_pad, front):
    m, d = x.shape
    assert m % tm == 0
    n_in = len(a_list)
    in_specs = ([pl.BlockSpec((tm, a.shape[1]), lambda i: (i, 0)) for a in a_list]
                + [pl.BlockSpec(w.shape, lambda i: (0, 0)) for w in w_list]
                + [pl.BlockSpec((1, d), lambda i: (0, 0)), pl.BlockSpec((tm, d), lambda i: (i, 0))])
    return pl.pallas_call(
        functools.partial(_matmul_norm_res_kernel, n_in=n_in, tm=tm, t_pad=t_pad, front=front),
        grid=(m // tm,),
        in_specs=in_specs,
        out_specs=pl.BlockSpec((tm, d), lambda i: (i, 0)),
        out_shape=jax.ShapeDtypeStruct((m, d), F32),
        compiler_params=_cparams(("parallel",)),
        name="matmul_norm_res",
    )(*a_list, *w_list, g.reshape(1, d), x)


def _ffn_down_kernel(ug_ref, uv_ref, halo_ref, cw_ref, cb_ref, w_ref, g_ref, x_ref, o_ref, *, tm, t_pad, front):
    ug = ug_ref[...]
    row = lax.broadcasted_iota(jnp.int32, (tm, 1), 0)
    prev1 = jnp.where(row == 0, halo_ref[7:8, :], pltpu.roll(ug, 1, axis=0))
    prev2 = jnp.where(row == 0, halo_ref[6:7, :],
                      jnp.where(row == 1, halo_ref[7:8, :], pltpu.roll(ug, 2, axis=0)))
    conv = cb_ref[...] + cw_ref[0:1, :] * prev2 + cw_ref[1:2, :] * prev1 + cw_ref[2:3, :] * ug
    act = (conv * jax.nn.sigmoid(conv) * uv_ref[...]).astype(BF16)
    out = x_ref[...] + _rms(_dot(act, w_ref[...]), g_ref[...])
    o_ref[...] = jnp.where(_row_valid(pl.program_id(0), tm, t_pad, front), out, 0.0)


def ffn_down(up, cw, cb, w_down, g, x, *, tm, t_pad, front):
    m, d = x.shape
    f = w_down.shape[0]
    assert m % tm == 0 and tm % 8 == 0 and up.shape == (m, 2 * f)
    hb = tm // 8
    return pl.pallas_call(
        functools.partial(_ffn_down_kernel, tm=tm, t_pad=t_pad, front=front),
        grid=(m // tm,),
        in_specs=[pl.BlockSpec((tm, f), lambda i: (i, 0)),
                  pl.BlockSpec((tm, f), lambda i: (i, 1)),
                  pl.BlockSpec((8, f), lambda i: (jnp.maximum(i * hb - 1, 0), 0)),
                  pl.BlockSpec((CONV_W, f), lambda i: (0, 0)),
                  pl.BlockSpec((1, f), lambda i: (0, 0)),
                  pl.BlockSpec((f, d), lambda i: (0, 0)),
                  pl.BlockSpec((1, d), lambda i: (0, 0)),
                  pl.BlockSpec((tm, d), lambda i: (i, 0))],
        out_specs=pl.BlockSpec((tm, d), lambda i: (i, 0)),
        out_shape=jax.ShapeDtypeStruct((m, d), F32),
        compiler_params=_cparams(("parallel",)),
        name="ffn_down",
    )(up, up, up, cw, cb.reshape(1, f), w_down, g.reshape(1, d), x)


FF_CW = 256
FF_HALO = 16


def _ffn_fused_kernel(x_ref, halo_ref, gin_ref, wup_ref, cw_ref, cb_ref, wdn_ref, gout_ref, o_ref, conv_ref, acc_ref,
                      *, tm, t_pad, front):
    i = pl.program_id(0)
    x = x_ref[...]
    h = jnp.concatenate([_rms(halo_ref[...], gin_ref[...]), _rms(x, gin_ref[...])], axis=0).astype(BF16)
    for c in range(D_FF // FF_CW):
        cs = slice(c * FF_CW, (c + 1) * FF_CW)
        u = _dot(h, wup_ref[c])
        ug = u[:, :FF_CW]
        prev1 = pltpu.roll(ug, 1, axis=0)[FF_HALO:]
        prev2 = pltpu.roll(ug, 2, axis=0)[FF_HALO:]
        conv = cb_ref[:, cs] + cw_ref[0:1, cs] * prev2 + cw_ref[1:2, cs] * prev1 + cw_ref[2:3, cs] * ug[FF_HALO:]
        act = (conv * jax.nn.sigmoid(conv) * u[FF_HALO:, FF_CW:]).astype(BF16)
        part = _dot(act, wdn_ref[c])
        if c == 0:
            acc_ref[...] = part
        else:
            acc_ref[...] += part
        conv_ref[0, :, cs] = ug[FF_HALO + tm - SUB:, :]
    out = x + _rms(acc_ref[...], gout_ref[...])
    o_ref[...] = jnp.where(_row_valid(i, tm, t_pad, front), out, 0.0)


def ffn_fused(x, g_in, w_up, cw, cb, w_down, g_out, *, tm, t_pad, front):
    m, d = x.shape
    nc = D_FF // FF_CW
    assert m % tm == 0 and t_pad % tm == 0 and tm % FF_HALO == 0 and D_FF % FF_CW == 0
    per_seq = t_pad // tm
    wup = jnp.concatenate([w_up[:, :D_FF].reshape(d, nc, FF_CW), w_up[:, D_FF:].reshape(d, nc, FF_CW)], axis=2)
    wup = wup.transpose(1, 0, 2)
    wdn = w_down.reshape(nc, FF_CW, d)
    hb = tm // FF_HALO
    const = lambda shape: pl.BlockSpec(shape, lambda i: (0,) * len(shape), pipeline_mode=pl.Buffered(1))
    return pl.pallas_call(
        functools.partial(_ffn_fused_kernel, tm=tm, t_pad=t_pad, front=front),
        grid=(m // tm,),
        in_specs=[pl.BlockSpec((tm, d), lambda i: (i, 0)),
                  pl.BlockSpec((FF_HALO, d), lambda i: (jnp.maximum(i * hb - 1, 0), 0)),
                  const((1, d)), const((nc, d, 2 * FF_CW)), const((CONV_W, D_FF)), const((1, D_FF)),
                  const((nc, FF_CW, d)), const((1, d))],
        out_specs=[pl.BlockSpec((tm, d), lambda i: (i, 0)),
                   pl.BlockSpec((1, SUB, D_FF), lambda i: (i // per_seq, 0, 0))],
        out_shape=[jax.ShapeDtypeStruct((m, d), F32), jax.ShapeDtypeStruct((m // t_pad, SUB, D_FF), F32)],
        scratch_shapes=[pltpu.VMEM((tm, d), F32)],
        compiler_params=_cparams(("arbitrary",)),
        name="ffn_fused",
    )(x, x, g_in.reshape(1, d), wup, cw, cb.reshape(1, D_FF), wdn, g_out.reshape(1, d))


def _log_sigmoid(x):
    return jnp.minimum(x, 0.0) - jnp.log1p(jnp.exp(-jnp.abs(x)))


def _rec_kernel(proj_ref, lb_ref, bg_ref, ga_ref, gb_ref, s0_ref, c0_ref, n0_ref, m0_ref,
                y_ref, s_ref, c_ref, n_ref, m_ref, st_ref, kpad, bpad, vpad, *, C, W, front):
    ci = pl.program_id(1)
    nci = pl.num_programs(1)

    @pl.when(ci == 0)
    def _():
        for h in range(H_A):
            st_ref[h] = s0_ref[0, h].T
        c_ref[...] = c0_ref[...]
        n_ref[...] = n0_ref[...]
        m_ref[...] = m0_ref[...]
        kpad[...] = jnp.zeros_like(kpad)
        bpad[...] = jnp.zeros_like(bpad)
        vpad[...] = jnp.zeros_like(vpad)

    row = ci * C + lax.broadcasted_iota(jnp.int32, (C, 1), 0)
    valid = row >= front
    r_i = lax.broadcasted_iota(jnp.int32, (C, C), 0)
    c_i = lax.broadcasted_iota(jnp.int32, (C, C), 1)
    causal = r_i >= c_i
    tri = jnp.where(causal, 1.0, 0.0).astype(BF16)

    gates = proj_ref[0, :, 8 * 512:8 * 512 + LANES] + bg_ref[...]
    lf_all = jnp.where(valid, _log_sigmoid(gates), 0.0)
    ig_all = jnp.where(valid, gates, NEG)
    b_all = _dot_exact_lhs(tri, lf_all)
    b_all_t = b_all.T
    ig_all_t = ig_all.T

    for h in range(H_B):
        q = proj_ref[0, :, 2048 + h * 128:2048 + (h + 1) * 128]
        k = proj_ref[0, :, 2560 + h * 128:2560 + (h + 1) * 128] * (DK_B ** -0.5)
        v = proj_ref[0, :, 3072 + h * 128:3072 + (h + 1) * 128]
        og = proj_ref[0, :, 3584 + h * 128:3584 + (h + 1) * 128]
        qb, kb, vb = q.astype(BF16), k.astype(BF16), v.astype(BF16)
        b_col = b_all[:, H_B + h:H_B + h + 1]
        b_row = b_all_t[H_B + h:H_B + h + 1, :]
        i_col = ig_all[:, h:h + 1]
        i_row = ig_all_t[h:h + 1, :]
        m_prev = m_ref[0, h:h + 1, 0:1]
        dmat = jnp.where(causal, b_col - b_row + i_row, NEG)
        inter = b_col + m_prev
        mt = jnp.maximum(inter, jnp.max(dmat, axis=1, keepdims=True))
        w = jnp.exp(dmat - mt) * _dot_nt(qb, kb)
        wi = jnp.exp(inter - mt)
        c_st = c_ref[0, h]
        n_st = n_ref[0, h:h + 1, :]
        num = wi * _dot(qb, c_st.astype(BF16)) + _dot(w.astype(BF16), vb)
        den = wi * jnp.sum(q * n_st, axis=1, keepdims=True) + jnp.sum(w, axis=1, keepdims=True)
        hc = num / jnp.maximum(jnp.abs(den), jnp.exp(-mt))
        m_new = mt[C - 1:C, :]
        b_last = b_col[C - 1:C, :]
        decay = jnp.exp(b_last + m_prev - m_new)
        kw = k * jnp.exp(b_last - b_col + i_col - m_new)
        c_ref[0, h] = decay * c_st + _dot_tn(kw.astype(BF16), vb)
        n_ref[0, h:h + 1, :] = decay * n_st + jnp.sum(kw, axis=0, keepdims=True)
        m_ref[0, h:h + 1, :] = jnp.broadcast_to(m_new, (1, LANES))
        yb = _rms(hc, gb_ref[...]) * jax.nn.sigmoid(og)
        y_ref[0, :, 512 + h * 128:512 + (h + 1) * 128] = jnp.where(valid, yb, 0.0).astype(y_ref.dtype)

    rw = lax.broadcasted_iota(jnp.int32, (W, 1), 0)
    tri_w = jnp.where(lax.broadcasted_iota(jnp.int32, (W, W), 0) >= lax.broadcasted_iota(jnp.int32, (W, W), 1),
                      1.0, 0.0).astype(BF16)
    for h in range(H_A):
        lb = lb_ref[:, h * 128:(h + 1) * 128]
        st = st_ref[h]
        for j in range(C // W):
            u = h * (C // W) + j
            rows = slice(j * W, (j + 1) * W)
            vld = (ci * C + j * W + rw) >= front
            f = lb + (1.0 - lb) * jax.nn.sigmoid(proj_ref[0, rows, 512 + h * 128:512 + (h + 1) * 128])
            logf = jnp.where(vld, jnp.log(f), 0.0)
            k = jnp.where(vld, 1.0 - f, 0.0)
            q = proj_ref[0, rows, h * 128:(h + 1) * 128] * (DK_A ** -0.5)
            v = proj_ref[0, rows, 1024 + h * 128:1024 + (h + 1) * 128]
            bc = _dot_exact_lhs(tri_w, logf)
            o = _dot_nt((q * jnp.exp(bc)).astype(BF16), st.astype(BF16))
            kpad[u, W:2 * W, :] = k
            bpad[u, W:2 * W, :] = bc
            vpad[u, W:2 * W, :] = v
            for d in range(W):
                ok = rw >= d
                e = jnp.exp(jnp.where(ok, bc - bpad[u, W - d:2 * W - d, :], NEG))
                r = jnp.sum(q * kpad[u, W - d:2 * W - d, :] * e, axis=1, keepdims=True)
                o = o + r * vpad[u, W - d:2 * W - d, :]
            last = bc[W - 1:W, :]
            kdec = k * jnp.exp(last - bc)
            st = jnp.exp(last) * st + _dot_tn(v.astype(BF16), kdec.astype(BF16))
            ga = proj_ref[0, rows, 1536 + h * 128:1536 + (h + 1) * 128]
            ya = _rms(o, ga_ref[...]) * (ga * jax.nn.sigmoid(ga))
            y_ref[0, rows, h * 128:(h + 1) * 128] = jnp.where(vld, ya, 0.0).astype(y_ref.dtype)
        st_ref[h] = st

    @pl.when(ci == nci - 1)
    def _():
        for h in range(H_A):
            s_ref[0, h] = st_ref[h].T


def rec_mixer(proj, lb, bg, g_a, g_b, s0, c0, n0, m0, *, C, W, front):
    b, t, _ = proj.shape
    assert t % C == 0 and C % W == 0
    m0b = jnp.broadcast_to(m0[:, :, None], (b, H_B, LANES))
    bgp = jnp.zeros((1, LANES), F32).at[0, :2 * H_B].set(bg.reshape(-1))
    st_spec = pl.BlockSpec((1, 4, 128, 128), lambda i, c: (i, 0, 0, 0))
    v_spec = pl.BlockSpec((1, 4, LANES), lambda i, c: (i, 0, 0))
    row_spec = lambda n: pl.BlockSpec((1, n), lambda i, c: (0, 0))
    y, s, cc, n, m = pl.pallas_call(
        functools.partial(_rec_kernel, C=C, W=W, front=front),
        grid=(b, t // C),
        in_specs=[pl.BlockSpec((1, C, REC_N), lambda i, c: (i, c, 0)),
                  row_spec(512), row_spec(LANES), row_spec(128), row_spec(128),
                  st_spec, st_spec, v_spec, v_spec],
        out_specs=[pl.BlockSpec((1, C, 1024), lambda i, c: (i, c, 0)), st_spec, st_spec, v_spec, v_spec],
        out_shape=[jax.ShapeDtypeStruct((b, t, 1024), BF16),
                   jax.ShapeDtypeStruct((b, 4, 128, 128), F32),
                   jax.ShapeDtypeStruct((b, 4, 128, 128), F32),
                   jax.ShapeDtypeStruct((b, 4, LANES), F32),
                   jax.ShapeDtypeStruct((b, 4, LANES), F32)],
        scratch_shapes=[pltpu.VMEM((4, 128, 128), F32)] + [pltpu.VMEM((H_A * (C // W), 2 * W, 128), F32)] * 3,
        compiler_params=_cparams(("parallel", "arbitrary")),
        name="rec_mixer",
    )(proj, lb.reshape(1, 512), bgp, g_a.reshape(1, 128), g_b.reshape(1, 128), s0, c0, n0, m0b)
    return y, s, cc, n, m[:, :, 0]


def rope_tables(pos, period, half):
    r = 2 * half
    inv = ROPE_THETA ** (-jnp.arange(half, dtype=F32) * 2.0 / r)
    ang = pos.astype(F32)[:, None] * inv[None, :]
    cos, sin = jnp.cos(ang), jnp.sin(ang)
    lane = np.arange(LANES) % period
    idx = np.where(lane < half, lane, np.where(lane < r, lane - half, 0))
    first, second = jnp.asarray(lane < half), jnp.asarray((lane >= half) & (lane < r))
    c = jnp.where(first | second, cos[:, idx], 1.0)
    sa = jnp.where(first, -sin[:, idx], 0.0)
    sb = jnp.where(second, sin[:, idx], 0.0)
    return c, sa, sb


def _att_prep_kernel(p_ref, c64, a64, b64, c128, a128, b128,
                     qc_o, kc_o, kcb_o, vc_o, vcb_o, qd_o, kd_o, kdb_o, vd_o, vdb_o, qi_o, ki_o, kib_o, wi_o,
                     vct_o, vdt_o):
    def rot(x, c, sa, sb, half):
        return x * c[...] + pltpu.roll(x, LANES - half, axis=1) * sa[...] + pltpu.roll(x, half, axis=1) * sb[...]

    h64 = D_IDX // ROT_FRAC // 2
    h128 = DH_D // ROT_FRAC // 2
    for t in range(4):
        sl = slice(t * LANES, (t + 1) * LANES)
        qc_o[:, sl] = (rot(p_ref[:, sl], c64, a64, b64, h64) * (DC ** -0.5)).astype(BF16)
        kc = rot(p_ref[:, 512 + t * LANES:512 + (t + 1) * LANES], c64, a64, b64, h64)
        kc_o[:, sl] = kc
        kcb_o[:, sl] = kc.astype(BF16)
        vc = p_ref[:, 1024 + t * LANES:1024 + (t + 1) * LANES]
        vc_o[:, sl] = vc
        vcb_o[:, sl] = vc.astype(BF16)
        vct_o[0, sl, :] = vc.T.astype(BF16)
        qd_o[:, sl] = rot(p_ref[:, 1536 + t * LANES:1536 + (t + 1) * LANES], c128, a128, b128, h128).astype(BF16)
    kd = rot(p_ref[:, 2048:2176], c128, a128, b128, h128)
    kd_o[...] = kd
    kdb_o[...] = kd.astype(BF16)
    vd = p_ref[:, 2176:2304]
    vd_o[...] = vd
    vdb_o[...] = vd.astype(BF16)
    vdt_o[0] = vd.T.astype(BF16)
    for t in range(2):
        qi = rot(p_ref[:, 2304 + t * LANES:2304 + (t + 1) * LANES], c64, a64, b64, h64) * (D_IDX ** -0.5)
        qi_o[:, (2 * t) * LANES:(2 * t + 1) * LANES] = qi.astype(BF16)
        qi_o[:, (2 * t + 1) * LANES:(2 * t + 2) * LANES] = pltpu.roll(qi, D_IDX, axis=1).astype(BF16)
    last = p_ref[:, 2560:2688]
    ki = rot(last, c64, a64, b64, h64)[:, :D_IDX]
    ki_o[...] = ki
    kib_o[...] = ki.astype(BF16)
    wi_o[...] = pltpu.roll(last, D_IDX, axis=1) * (H_I ** -0.5)


def att_prep(proj, tabs64, tabs128, *, tm):
    m = proj.shape[0]
    p = tabs64[0].shape[0]
    assert m % tm == 0 and p % tm == 0
    nper = p // tm
    tab_spec = pl.BlockSpec((tm, LANES), lambda i: (i % nper, 0))
    outs = [(512, BF16), (512, F32), (512, BF16), (512, F32), (512, BF16), (512, BF16),
            (128, F32), (128, BF16), (128, F32), (128, BF16), (512, BF16), (D_IDX, F32), (D_IDX, BF16), (128, F32)]
    outs_t = [512, 128]
    return pl.pallas_call(
        _att_prep_kernel,
        grid=(m // tm,),
        in_specs=[pl.BlockSpec((tm, ATT_N), lambda i: (i, 0))] + [tab_spec] * 6,
        out_specs=([pl.BlockSpec((tm, w), lambda i: (i, 0)) for w, _ in outs]
                   + [pl.BlockSpec((1, w, tm), lambda i: (i, 0, 0)) for w in outs_t]),
        out_shape=([jax.ShapeDtypeStruct((m, w), dt) for w, dt in outs]
                   + [jax.ShapeDtypeStruct((m // tm, w, tm), BF16) for w in outs_t]),
        compiler_params=_cparams(("parallel",)),
        name="att_prep",
    )(proj, *tabs64, *tabs128)


QB = 128


def _diff_lambda(lam_ref, lam_init):
    dl = lam_ref[...]
    s1 = jnp.sum(dl[0:1, :] * dl[1:2, :], axis=1, keepdims=True)
    s2 = jnp.sum(dl[2:3, :] * dl[3:4, :], axis=1, keepdims=True)
    return jnp.exp(s1) - jnp.exp(s2) + lam_init


KB = 384
SUB = 8


def _group_max(x):
    return jnp.max(x.reshape(x.shape[0] // SUB, SUB, x.shape[1]), axis=0)


def _group_sum(x):
    return jnp.sum(x.reshape(x.shape[0] // SUB, SUB, x.shape[1]), axis=0)


def _key_visible(i, off, n_rep, front):
    krow = lax.broadcasted_iota(jnp.int32, (KB, 1), 0)
    lane = lax.broadcasted_iota(jnp.int32, (1, n_rep * KB), 1)
    q = lane
    for r in range(1, n_rep):
        q = jnp.where(lane >= r * KB, lane - r * KB, q)
    return ((i * KB + q - krow) >= off) & (krow >= front - off)


def _edge_then_middle(i, edge_body, middle_body, init):
    carry = lax.fori_loop(0, jnp.minimum(i, 1) + 1, lambda t, c: edge_body(t * i, c), init)
    return lax.fori_loop(1, i, middle_body, carry)


def _diff_prompt_kernel(q_ref, k_ref, vt_ref, lam_ref, g_ref, o_ref, s_ref, *, front, lam_init):
    i = pl.program_id(1)
    lam = _diff_lambda(lam_ref, lam_init)
    lane = lax.broadcasted_iota(jnp.int32, (KB, LANES), 1)
    for h in range(H_C):
        cs = slice(h * LANES, (h + 1) * LANES)
        qh = q_ref[:, cs]
        qstack = jnp.concatenate([jnp.where(lane < DC, qh, jnp.zeros_like(qh)),
                                  jnp.where(lane >= DC, qh, jnp.zeros_like(qh))], axis=0)

        def scores(kb, masked):
            off = pl.multiple_of(kb * KB, KB)
            st = _dot_nt(k_ref[pl.ds(off, KB), cs], qstack)
            return jnp.where(_key_visible(i, off, 2, front), st, NEG) if masked else st

        def pass_a(kb, mx, masked):
            st = scores(kb, masked)
            s_ref[kb] = st
            return jnp.maximum(mx, _group_max(st))

        mx = _edge_then_middle(i, functools.partial(pass_a, masked=True), functools.partial(pass_a, masked=False),
                               jnp.full((SUB, 2 * KB), NEG, F32))
        m = jnp.max(mx, axis=0, keepdims=True)

        def pass_b(kb, carry):
            l8, acc = carry
            p = jnp.exp(s_ref[kb] - m)
            return l8 + _group_sum(p), acc + _dot(vt_ref[kb, cs, :], p.astype(BF16))

        l8, acc = lax.fori_loop(0, i + 1, pass_b, (jnp.zeros((SUB, 2 * KB), F32), jnp.zeros((LANES, 2 * KB), F32)))
        a = acc / jnp.sum(l8, axis=0, keepdims=True)
        ot = a[:, :KB] - lam * a[:, KB:]
        ot = ot * lax.rsqrt(jnp.mean(ot * ot, axis=0, keepdims=True) + EPS) * g_ref[...] * (1.0 - lam_init)
        o_ref[:, cs] = ot.T.astype(o_ref.dtype)


def diff_prompt(q, k, vt, lam_p, g_c, *, b, t_pad, front, lam_init):
    nkb = t_pad // KB
    return pl.pallas_call(
        functools.partial(_diff_prompt_kernel, front=front, lam_init=lam_init),
        grid=(b, nkb),
        in_specs=[pl.BlockSpec((KB, 512), lambda bb, i: (bb * nkb + i, 0)),
                  pl.BlockSpec((t_pad, 512), lambda bb, i: (bb, 0)),
                  pl.BlockSpec((nkb, 512, KB), lambda bb, i: (bb, 0, 0)),
                  pl.BlockSpec((4, DC), lambda bb, i: (0, 0)),
                  pl.BlockSpec((2 * DC, 1), lambda bb, i: (0, 0))],
        out_specs=pl.BlockSpec((KB, 512), lambda bb, i: (bb * nkb + i, 0)),
        out_shape=jax.ShapeDtypeStruct((b * t_pad, 512), BF16),
        scratch_shapes=[pltpu.VMEM((nkb, KB, 2 * KB), F32)],
        compiler_params=_cparams(("parallel", "arbitrary")),
        name="diff_prompt",
    )(q, k, vt, lam_p, g_c.reshape(2 * DC, 1))


NINF = float("-inf")


def _kth_threshold(count_ge, shape, n_sel):
    zero_i = jnp.zeros(shape, jnp.int32)
    neg = jnp.where(count_ge(jnp.zeros(shape, F32)) < n_sel, 1, 0)
    sign = jnp.where(neg == 1, jnp.int32(INT_MIN), 0)

    def bit_body(t, mag):
        cand = mag | lax.shift_left(jnp.int32(1), 30 - t)
        enough = jnp.where(count_ge(pltpu.bitcast(cand | sign, F32)) >= n_sel, 1, 0)
        return jnp.where(enough + neg == 1, cand, mag)

    mag = lax.fori_loop(0, 31, bit_body, zero_i)
    tau = pltpu.bitcast(jnp.where(neg == 1, (mag + 1) | sign, mag), F32)
    ninf = jnp.full(shape, NINF, F32)
    return jnp.where(count_ge(ninf) >= n_sel, tau, ninf)


def _dsa_prompt_kernel(qi_ref, wi_ref, qd_ref, ki_ref, kd_ref, vdt_ref, o_ref, sc_ref, sel_ref, s_ref, *, front, n_sel):
    i = pl.program_id(1)
    nkb = i + 1
    qi_all = jnp.concatenate([qi_ref[:, h * LANES:h * LANES + D_IDX] for h in range(H_I)], axis=0)
    wt = wi_ref[...].T
    w_row = jnp.concatenate([wt[h:h + 1, :] for h in range(H_I)], axis=1)

    def stage1(kb, c, masked):
        off = pl.multiple_of(kb * KB, KB)
        sct = jnp.maximum(_dot_nt(ki_ref[pl.ds(off, KB), :], qi_all), 0.0) * w_row
        score = sct[:, 0:KB] + sct[:, KB:2 * KB] + sct[:, 2 * KB:3 * KB] + sct[:, 3 * KB:4 * KB]
        sc_ref[kb] = jnp.where(_key_visible(i, off, 1, front), score, NINF) if masked else score
        return c

    _edge_then_middle(i, functools.partial(stage1, masked=True), functools.partial(stage1, masked=False), 0)

    def count(pred):
        def body(kb, acc):
            return acc + _group_sum(jnp.where(pred(sc_ref[kb]), 1, 0))
        return jnp.sum(lax.fori_loop(0, nkb, body, jnp.zeros((SUB, KB), jnp.int32)), axis=0, keepdims=True)

    tau = _kth_threshold(lambda cand: count(lambda sc: sc >= cand), (1, KB), n_sel)
    need = (n_sel - count(lambda sc: sc > tau)).astype(F32)

    strict_lower = jnp.where(lax.broadcasted_iota(jnp.int32, (KB, KB), 1) < lax.broadcasted_iota(jnp.int32, (KB, KB), 0),
                             1.0, 0.0).astype(BF16)

    def select(kb, before):
        sc = sc_ref[kb]
        eq = sc == tau
        eqf = jnp.where(eq, 1.0, 0.0)
        rank = _dot(strict_lower, eqf.astype(BF16)) + before
        sel_ref[kb] = jnp.where(((sc > tau) | (eq & (rank < need))) & (sc > NINF), 0.0, NEG)
        return before + jnp.sum(eqf, axis=0, keepdims=True)

    lax.fori_loop(0, nkb, select, jnp.zeros((1, KB), F32))

    for h in range(H_D):
        cs = slice(h * LANES, (h + 1) * LANES)
        qd = qd_ref[:, cs]

        def stage3(kb, mx):
            off = pl.multiple_of(kb * KB, KB)
            sdt = _dot_nt(kd_ref[pl.ds(off, KB), :], qd) * (DH_D ** -0.5) + sel_ref[kb]
            s_ref[kb] = sdt
            return jnp.maximum(mx, _group_max(sdt))

        m = jnp.max(lax.fori_loop(0, nkb, stage3, jnp.full((SUB, KB), NEG, F32)), axis=0, keepdims=True)

        def stage4(kb, carry):
            l8, acc = carry
            p = jnp.exp(s_ref[kb] - m)
            return l8 + _group_sum(p), acc + _dot(vdt_ref[kb], p.astype(BF16))

        l8, acc = lax.fori_loop(0, nkb, stage4, (jnp.zeros((SUB, KB), F32), jnp.zeros((DH_D, KB), F32)))
        o_ref[:, cs] = (acc / jnp.sum(l8, axis=0, keepdims=True)).T.astype(o_ref.dtype)


def dsa_prompt(qi, wi, qd, ki, kd, vdt, *, b, t_pad, front, n_sel):
    nkb = t_pad // KB
    qspec = lambda w: pl.BlockSpec((KB, w), lambda bb, i: (bb * nkb + i, 0))
    kspec = lambda w: pl.BlockSpec((t_pad, w), lambda bb, i: (bb, 0))
    return pl.pallas_call(
        functools.partial(_dsa_prompt_kernel, front=front, n_sel=n_sel),
        grid=(b, nkb),
        in_specs=[qspec(512), qspec(128), qspec(512), kspec(D_IDX), kspec(DH_D),
                  pl.BlockSpec((nkb, DH_D, KB), lambda bb, i: (bb, 0, 0))],
        out_specs=qspec(512),
        out_shape=jax.ShapeDtypeStruct((b * t_pad, 512), BF16),
        scratch_shapes=[pltpu.VMEM((nkb, KB, KB), F32)] * 3,
        compiler_params=_cparams(("parallel", "arbitrary")),
        name="dsa_prompt",
    )(qi, wi, qd, ki, kd, vdt)


TS = 16


PG = 4


def _page_map(g, n_pages, nd):
    def index(bb, j, pt):
        return (pt[bb * n_pages + jnp.minimum(j * PG + g, n_pages - 1)],) + (0,) * nd
    return index


def _online_softmax_update(s, vs, m_ref, l_ref, acc_ref):
    m_old = m_ref[...]
    m_new = jnp.maximum(m_old, jnp.max(s, axis=1, keepdims=True))
    alpha = jnp.exp(m_old - m_new)
    p = jnp.where(s > 0.5 * NEG, jnp.exp(s - m_new), 0.0)
    l_ref[...] = alpha * l_ref[...] + jnp.sum(p, axis=1, keepdims=True)
    r = s.shape[0] // len(vs)
    pv = jnp.concatenate([_dot(p[g * r:(g + 1) * r].astype(BF16), v) for g, v in enumerate(vs)], axis=0)
    acc_ref[...] = alpha * acc_ref[...] + pv
    m_ref[...] = m_new


def _sample_a_kernel(pt_ref, qc_ref, qi_ref, wi_ref, *refs, n_steps, front, lam_init):
    ck_refs, cv_refs, cik_refs = refs[:PG], refs[PG:2 * PG], refs[2 * PG:3 * PG]
    kn_ref, vn_ref, kin_ref, lam_ref, g_ref, o_ref, keys_ref, m_ref, l_ref, acc_ref = refs[3 * PG:]
    j = pl.program_id(1)
    lane = lax.broadcasted_iota(jnp.int32, (TS, LANES), 1)

    @pl.when(j == 0)
    def _():
        m_ref[...] = jnp.full(m_ref.shape, NEG, F32)
        l_ref[...] = jnp.zeros(l_ref.shape, F32)
        acc_ref[...] = jnp.zeros(acc_ref.shape, F32)

    def qstack(h):
        qh = qc_ref[:, h * LANES:(h + 1) * LANES]
        return jnp.concatenate([jnp.where(lane < DC, qh, jnp.zeros_like(qh)),
                                jnp.where(lane >= DC, qh, jnp.zeros_like(qh))], axis=0)

    def head_rows(h):
        return slice(h * 2 * TS, (h + 1) * 2 * TS)

    qi_all = jnp.concatenate([qi_ref[:, h * LANES:h * LANES + D_IDX] for h in range(H_I)], axis=0)
    w_col = jnp.concatenate([wi_ref[:, h:h + 1] for h in range(H_I)], axis=0)

    def idx_scores(kip):
        sc = jnp.maximum(_dot_nt(qi_all, kip), 0.0) * w_col
        return sc[0:TS] + sc[TS:2 * TS] + sc[2 * TS:3 * TS] + sc[3 * TS:4 * TS]

    @pl.when(j < n_steps)
    def _():
        ss, vs = [], []
        for h in range(H_C):
            rows_h = pl.ds(h, PAGE_SIZE, stride=H_C)
            k = jnp.concatenate([r[0, rows_h, :].astype(BF16) for r in ck_refs], axis=0)
            vs.append(jnp.concatenate([r[0, rows_h, :].astype(BF16) for r in cv_refs], axis=0))
            ss.append(_dot_nt(qstack(h), k))
        _online_softmax_update(jnp.concatenate(ss, axis=0), vs, m_ref, l_ref, acc_ref)
        keys_ref[0] = idx_scores(jnp.concatenate([r[0].astype(BF16) for r in cik_refs], axis=0))

    @pl.when(j == n_steps)
    def _():
        kr = lax.broadcasted_iota(jnp.int32, (1, TS), 1)
        q_of_row = lax.broadcasted_iota(jnp.int32, (2 * TS, 1), 0) & (TS - 1)
        ok = (kr >= front) & (kr <= q_of_row)
        ss = [jnp.where(ok, _dot_nt(qstack(h), kn_ref[:, h * LANES:(h + 1) * LANES]), NEG) for h in range(H_C)]
        vs = [vn_ref[:, h * LANES:(h + 1) * LANES] for h in range(H_C)]
        _online_softmax_update(jnp.concatenate(ss, axis=0), vs, m_ref, l_ref, acc_ref)
        krp = lax.broadcasted_iota(jnp.int32, (1, PAGE_SIZE), 1)
        okq = (krp >= front) & (krp < TS) & (krp <= lax.broadcasted_iota(jnp.int32, (TS, 1), 0))
        keys_ref[0] = jnp.concatenate([jnp.where(okq, idx_scores(kin_ref[0]), NINF),
                                       jnp.full((TS, (PG - 1) * PAGE_SIZE), NINF, F32)], axis=1)
        lam = _diff_lambda(lam_ref, lam_init)
        for h in range(H_C):
            a = acc_ref[head_rows(h), :] / l_ref[head_rows(h), :]
            o = a[0:TS] - lam * a[TS:2 * TS]
            o_ref[:, h * LANES:(h + 1) * LANES] = (_rms(o, g_ref[...]) * (1.0 - lam_init)).astype(o_ref.dtype)


def sample_diff_idx(pt, qc, qi, wi, ck, cv, cik, kn, vn, kin, lam_p, g_c, *, b, n_pages, front, lam_init):
    assert n_pages % PG == 0
    n_steps = n_pages // PG
    rows = H_C * 2 * TS
    qspec = lambda w: pl.BlockSpec((TS, w), lambda bb, j, pt: (bb, 0))
    pages4 = [pl.BlockSpec((1, PAGE_SIZE * H_C, 2 * DC), _page_map(g, n_pages, 2)) for g in range(PG)]
    pages_i = [pl.BlockSpec((1, PAGE_SIZE, D_IDX), _page_map(g, n_pages, 2)) for g in range(PG)]
    return pl.pallas_call(
        functools.partial(_sample_a_kernel, n_steps=n_steps, front=front, lam_init=lam_init),
        grid_spec=pltpu.PrefetchScalarGridSpec(
            num_scalar_prefetch=1,
            grid=(b, n_steps + 1),
            in_specs=[qspec(512), qspec(512), qspec(128)] + pages4 + pages4 + pages_i
                     + [qspec(512), qspec(512), pl.BlockSpec((1, PAGE_SIZE, D_IDX), lambda bb, j, pt: (bb, 0, 0)),
                        pl.BlockSpec((4, DC), lambda bb, j, pt: (0, 0)),
                        pl.BlockSpec((1, 2 * DC), lambda bb, j, pt: (0, 0))],
            out_specs=[qspec(512), pl.BlockSpec((1, TS, PG * PAGE_SIZE), lambda bb, j, pt: (bb, 0, j))],
            scratch_shapes=[pltpu.VMEM((rows, 1), F32), pltpu.VMEM((rows, 1), F32), pltpu.VMEM((rows, 2 * DC), F32)]),
        out_shape=[jax.ShapeDtypeStruct((b * TS, 512), BF16),
                   jax.ShapeDtypeStruct((b, TS, (n_steps + 1) * PG * PAGE_SIZE), F32)],
        compiler_params=_cparams(("parallel", "arbitrary")),
        name="sample_diff_idx",
    )(pt, qc, qi, wi, *([ck] * PG), *([cv] * PG), *([cik] * PG), kn, vn, kin, lam_p, g_c.reshape(1, 2 * DC))


def _sample_b_kernel(pt_ref, keys_all_ref, keys_ref, qd_ref, *refs, n_steps, n_sel):
    ck_refs, cv_refs = refs[:PG], refs[PG:2 * PG]
    kn_ref, vn_ref, o_ref, tau_ref, need_ref, before_ref, m_ref, l_ref, acc_ref = refs[2 * PG:]
    j = pl.program_id(1)

    @pl.when(j == 0)
    def _():
        keys = keys_all_ref[0]

        def count_ge(cand):
            return jnp.sum(jnp.where(keys >= cand, 1, 0), axis=1, keepdims=True)

        tau = _kth_threshold(count_ge, (TS, 1), n_sel)
        tau_ref[...] = tau
        need_ref[...] = (n_sel - jnp.sum(jnp.where(keys > tau, 1, 0), axis=1, keepdims=True)).astype(F32)
        before_ref[...] = jnp.zeros(before_ref.shape, F32)
        m_ref[...] = jnp.full(m_ref.shape, NEG, F32)
        l_ref[...] = jnp.zeros(l_ref.shape, F32)
        acc_ref[...] = jnp.zeros(acc_ref.shape, F32)

    strict_upper = jnp.where(lax.broadcasted_iota(jnp.int32, (PAGE_SIZE, PAGE_SIZE), 0)
                             < lax.broadcasted_iota(jnp.int32, (PAGE_SIZE, PAGE_SIZE), 1), 1.0, 0.0).astype(BF16)
    qd_all = jnp.concatenate([qd_ref[:, h * LANES:(h + 1) * LANES] for h in range(H_D)], axis=0)

    def select(key, before):
        tau = tau_ref[...]
        eq = key == tau
        eqf = jnp.where(eq, 1.0, 0.0)
        rank = _dot(eqf.astype(BF16), strict_upper) + before
        sel = ((key > tau) | (eq & (rank < need_ref[...]))) & (key > NINF)
        return sel, before + jnp.sum(eqf, axis=1, keepdims=True)

    def attend(s, sel, v):
        s = jnp.where(jnp.concatenate([sel] * H_D, axis=0), s * (DH_D ** -0.5), NEG)
        _online_softmax_update(s, [v], m_ref, l_ref, acc_ref)

    @pl.when(j < n_steps)
    def _():
        key = keys_ref[0]
        before = before_ref[...]
        sels = []
        for g in range(PG):
            sel, before = select(key[:, g * PAGE_SIZE:(g + 1) * PAGE_SIZE], before)
            sels.append(sel)
        before_ref[...] = before
        k = jnp.concatenate([r[0].astype(BF16) for r in ck_refs], axis=0)
        v = jnp.concatenate([r[0].astype(BF16) for r in cv_refs], axis=0)
        attend(_dot_nt(qd_all, k), jnp.concatenate(sels, axis=1), v)

    @pl.when(j == n_steps)
    def _():
        sel, _ = select(keys_ref[0][:, :PAGE_SIZE], before_ref[...])
        attend(_dot_nt(qd_all, kn_ref[0]), sel, vn_ref[0])
        o = acc_ref[...] / l_ref[...]
        for h in range(H_D):
            o_ref[:, h * LANES:(h + 1) * LANES] = o[h * TS:(h + 1) * TS].astype(o_ref.dtype)


def sample_dsa(pt, keys, qd, ck, cv, kn, vn, *, b, n_pages, n_sel):
    assert n_pages % PG == 0
    n_steps = n_pages // PG
    rows = H_D * TS
    nk = keys.shape[2]
    pages = [pl.BlockSpec((1, PAGE_SIZE, DH_D), _page_map(g, n_pages, 2)) for g in range(PG)]
    nspec = pl.BlockSpec((1, PAGE_SIZE, DH_D), lambda bb, j, pt: (bb, 0, 0))
    return pl.pallas_call(
        functools.partial(_sample_b_kernel, n_steps=n_steps, n_sel=n_sel),
        grid_spec=pltpu.PrefetchScalarGridSpec(
            num_scalar_prefetch=1,
            grid=(b, n_steps + 1),
            in_specs=[pl.BlockSpec((1, TS, nk), lambda bb, j, pt: (bb, 0, 0)),
                      pl.BlockSpec((1, TS, PG * PAGE_SIZE), lambda bb, j, pt: (bb, 0, j)),
                      pl.BlockSpec((TS, 512), lambda bb, j, pt: (bb, 0))] + pages + pages + [nspec, nspec],
            out_specs=pl.BlockSpec((TS, 512), lambda bb, j, pt: (bb, 0)),
            scratch_shapes=[pltpu.VMEM((TS, 1), F32), pltpu.VMEM((TS, 1), F32), pltpu.VMEM((TS, 1), F32),
                            pltpu.VMEM((rows, 1), F32), pltpu.VMEM((rows, 1), F32), pltpu.VMEM((rows, DH_D), F32)]),
        out_shape=jax.ShapeDtypeStruct((b * TS, 512), BF16),
        compiler_params=_cparams(("parallel", "arbitrary")),
        name="sample_dsa",
    )(pt, keys, keys, qd, *([ck] * PG), *([cv] * PG), kn, vn)


REC_CHUNK = 64
REC_SUB = 16
TM_PROJ = 512
TM_REC, TN_REC = 1408, 1408
TM_ATT = 704
TM_ROWS = 384
TM_FFN = 528


def _pad_cols(w, n):
    return jnp.pad(w, ((0, 0), (0, n - w.shape[1])))


def _tile_rows(m, pref):
    return pref if m % pref == 0 else m


def kernel(x_prompt, x_sample, state_hgrn, state_mlstm_C, state_mlstm_n, state_mlstm_m, state_ffn_conv, cache_diff_k, cache_diff_v, cache_dsa_k, cache_dsa_v, cache_idx_k, page_table, meta_tokens, norm_gains, w_in_rec, b_gates_rec, lb_logits, g_norm_hgrn, g_norm_mlstm, w_out_rec, w_in_att, diff_lambda, g_norm_diff, w_out_att, w_ffn_up, ffn_conv_w, ffn_conv_b, w_ffn_down):
    bp, t_in, d = x_prompt.shape
    bs, t_s, _ = x_sample.shape
    depth = norm_gains.shape[0]
    n_pages = page_table.shape[1]
    past_len = n_pages * PAGE_SIZE
    real_p = N_META + t_in
    tp = -(-real_p // QB) * QB
    front_p = tp - real_p
    front_s = TS - t_s
    assert tp % REC_CHUNK == 0 and tp % TM_ROWS == 0 and front_p >= CONV_W - 1 and front_s >= CONV_W - 1
    mp, ms = bp * tp, bs * TS

    meta = jnp.broadcast_to(meta_tokens.astype(x_prompt.dtype)[None], (bp, N_META, d))
    xp = jnp.concatenate([jnp.zeros((bp, front_p, d), x_prompt.dtype), meta, x_prompt], axis=1).reshape(mp, d)
    xs = jnp.concatenate([jnp.zeros((bs, front_s, d), x_sample.dtype), x_sample], axis=1).reshape(ms, d)
    lb_all = jnp.cumsum(jax.nn.softmax(lb_logits.astype(F32), axis=0), axis=0)
    pt_flat = page_table.reshape(-1).astype(jnp.int32)
    sel_p = min(TOPK_MAX, t_in // 4)
    sel_s = min(TOPK_MAX, (past_len + t_s) // 4)
    tmp_s = _tile_rows(ms, TM_PROJ)
    tmr_s = _tile_rows(ms, TM_ROWS)

    pos_p = jnp.arange(tp, dtype=jnp.int32) - front_p
    pos_s = jnp.tile(past_len + jnp.arange(TS, dtype=jnp.int32) - front_s, ms // TS)
    tabs_p = (rope_tables(pos_p, DC, DC // ROT_FRAC // 2), rope_tables(pos_p, DH_D, DH_D // ROT_FRAC // 2))
    tabs_s = (rope_tables(pos_s, DC, DC // ROT_FRAC // 2), rope_tables(pos_s, DH_D, DH_D // ROT_FRAC // 2))

    rec_p, rec_s = [[], [], [], []], [[], [], [], []]
    att_p, att_s = [[], [], [], [], []], [[], [], [], [], []]
    conv_p, conv_s = [], []
    for l in range(depth):
        p = l // 2
        g = norm_gains[l].astype(F32)
        if l % 2 == 0:
            w_in = _pad_cols(w_in_rec[p], REC_N).astype(BF16)
            w_out = w_out_rec[p].astype(BF16)
            prm = (lb_all[p], b_gates_rec[p].astype(F32), g_norm_hgrn[p].astype(F32), g_norm_mlstm[p].astype(F32))
            proj = norm_matmul(xp, g[0], w_in, tm=_tile_rows(mp, TM_REC), tn=TN_REC).reshape(bp, tp, REC_N)
            zs = jnp.zeros((bp, 4, 128, 128), F32)
            y, *st = rec_mixer(proj, *prm, zs, zs, jnp.zeros((bp, 4, 128), F32), jnp.zeros((bp, 4), F32),
                               C=REC_CHUNK, W=REC_SUB, front=front_p)
            xp = matmul_norm_res([y.reshape(mp, -1)], [w_out], g[1], xp, tm=TM_ROWS, t_pad=tp, front=front_p)
            for j in range(4):
                rec_p[j].append(st[j])
            proj = norm_matmul(xs, g[0], w_in, tm=tmp_s, tn=384).reshape(bs, TS, REC_N)
            y, *st = rec_mixer(proj, *prm, state_hgrn[p].astype(F32), state_mlstm_C[p].astype(F32),
                               state_mlstm_n[p].astype(F32), state_mlstm_m[p].astype(F32), C=TS, W=TS, front=front_s)
            xs = matmul_norm_res([y.reshape(ms, -1)], [w_out], g[1], xs, tm=tmr_s, t_pad=TS, front=front_s)
            for j in range(4):
                rec_s[j].append(st[j])
        else:
            lam_init = 0.8 - 0.6 * math.exp(-0.3 * l)
            w_in = _pad_cols(w_in_att[p], ATT_N).astype(BF16)
            w_out = w_out_att[p].astype(BF16)
            dl, gc = diff_lambda[p].astype(F32), g_norm_diff[p].astype(F32)
            proj = norm_matmul(xp, g[0], w_in, tm=_tile_rows(mp, TM_ATT), tn=ATT_N)
            (qc, kc, kcb, vc, vcb, qd, kd, kdb, vd, vdb, qi, ki, kib, wi, vct, vdt) = att_prep(proj, *tabs_p, tm=KB)
            oc = diff_prompt(qc, kcb, vct, dl, gc, b=bp, t_pad=tp, front=front_p, lam_init=lam_init)
            od = dsa_prompt(qi, wi, qd, kib, kdb, vdt, b=bp, t_pad=tp, front=front_p, n_sel=sel_p)
            xp = matmul_norm_res([oc, od], [w_out[:512], w_out[512:]], g[1], xp, tm=TM_ROWS, t_pad=tp, front=front_p)
            for j, (a, shp) in enumerate([(kc, (H_C, 2 * DC)), (vc, (H_C, 2 * DC)), (kd, (DH_D,)), (vd, (DH_D,)), (ki, (D_IDX,))]):
                att_p[j].append(a.reshape((bp, tp) + shp)[:, front_p:])
            proj = norm_matmul(xs, g[0], w_in, tm=tmp_s, tn=384)
            (qc, kc, kcb, vc, vcb, qd, kd, kdb, vd, vdb, qi, ki, kib, wi, _, _) = att_prep(proj, *tabs_s, tm=ms)
            as_page = lambda a: jnp.pad(a.reshape(bs, TS, -1), ((0, 0), (0, PAGE_SIZE - TS), (0, 0)))
            rows_kh = lambda c: c.reshape(c.shape[0], PAGE_SIZE * H_C, 2 * DC)
            oc, keys = sample_diff_idx(pt_flat, qc, qi, wi, rows_kh(cache_diff_k[p]), rows_kh(cache_diff_v[p]), cache_idx_k[p],
                                       kcb, vcb, as_page(kib), dl, gc,
                                       b=bs, n_pages=n_pages, front=front_s, lam_init=lam_init)
            od = sample_dsa(pt_flat, keys, qd, cache_dsa_k[p], cache_dsa_v[p], as_page(kdb), as_page(vdb),
                            b=bs, n_pages=n_pages, n_sel=sel_s)
            xs = matmul_norm_res([oc, od], [w_out[:512], w_out[512:]], g[1], xs, tm=tmr_s, t_pad=TS, front=front_s)
            for j, (a, shp) in enumerate([(kc, (H_C, 2 * DC)), (vc, (H_C, 2 * DC)), (kd, (DH_D,)), (vd, (DH_D,)), (ki, (D_IDX,))]):
                att_s[j].append(a.reshape((bs, TS) + shp)[:, front_s:])
        w_up, w_down = w_ffn_up[l].astype(BF16), w_ffn_down[l].astype(BF16)
        cw, cb = ffn_conv_w[l].astype(F32), ffn_conv_b[l].astype(F32)
        xp, tail = ffn_fused(xp, g[2], w_up, cw, cb, w_down, g[3], tm=TM_FFN, t_pad=tp, front=front_p)
        conv_p.append(tail[:, SUB - (CONV_W - 1):])
        up = norm_matmul(xs, g[2], w_up, tm=tmp_s, tn=512).reshape(bs, TS, 2 * D_FF)
        up = up.at[:, front_s - (CONV_W - 1):front_s, :D_FF].set(state_ffn_conv[l].astype(F32))
        conv_s.append(up[:, TS - (CONV_W - 1):, :D_FF])
        xs = ffn_down(up.reshape(ms, 2 * D_FF), cw, cb, w_down, g[3], xs, tm=tmr_s, t_pad=TS, front=front_s)

    y_p = xp.reshape(bp, tp, d)[:, front_p + N_META:]
    y_s = xs.reshape(bs, TS, d)[:, front_s:]
    return (y_p, y_s,
            jnp.stack(rec_p[0]), jnp.stack(rec_s[0]), jnp.stack(rec_p[1]), jnp.stack(rec_s[1]),
            jnp.stack(rec_p[2]), jnp.stack(rec_s[2]), jnp.stack(rec_p[3]), jnp.stack(rec_s[3]),
            jnp.stack(conv_p), jnp.stack(conv_s),
            jnp.stack(att_p[0]), jnp.stack(att_s[0]), jnp.stack(att_p[1]), jnp.stack(att_s[1]),
            jnp.stack(att_p[2]), jnp.stack(att_s[2]), jnp.stack(att_p[3]), jnp.stack(att_s[3]),
            jnp.stack(att_p[4]), jnp.stack(att_s[4]))
```

```python
import functools
import math

import jax
import jax.numpy as jnp
import numpy as np
from jax import lax
from jax.experimental import pallas as pl
from jax.experimental.pallas import tpu as pltpu

F32 = jnp.float32
BF16 = jnp.bfloat16

D_MODEL = 1024
N_META = 16
H_A, DK_A, DV_A = 4, 128, 128
H_B, DK_B, DV_B = 4, 128, 128
H_C, DC = 4, 64
H_D, DH_D = 4, 128
H_I, D_IDX = 4, 64
TOPK_MAX = 256
D_FF = 2816
CONV_W = 3
ROPE_THETA = 500000.0
ROT_FRAC = 4
EPS = 1e-6
PAGE_SIZE = 128
LANES = 128
NEG = -1e30

REC_N = 8 * 512 + LANES
ATT_N = 2560 + LANES
INT_MIN = -2 ** 31
VMEM_LIMIT = 56 * 1024 * 1024


def _cparams(sem):
    return pltpu.CompilerParams(dimension_semantics=sem, vmem_limit_bytes=VMEM_LIMIT)


def _rms(x, g):
    return x * lax.rsqrt(jnp.mean(x * x, axis=-1, keepdims=True) + EPS) * g


def _dot(a, b):
    return jnp.dot(a, b, preferred_element_type=F32)


def _dot_nt(a, b):
    return lax.dot_general(a, b, (((1,), (1,)), ((), ())), preferred_element_type=F32)


def _dot_tn(a, b):
    return lax.dot_general(a, b, (((0,), (0,)), ((), ())), preferred_element_type=F32)


def _dot_exact_lhs(tri, x):
    hi = x.astype(BF16)
    r1 = x - hi.astype(F32)
    mid = r1.astype(BF16)
    lo = (r1 - mid.astype(F32)).astype(BF16)
    return _dot(tri, hi) + _dot(tri, mid) + _dot(tri, lo)


def _row_valid(i, tm, t_pad, front):
    r = lax.broadcasted_iota(jnp.int32, (tm, 1), 0)
    if t_pad % tm == 0:
        t = (i % (t_pad // tm)) * tm + r
    else:
        assert tm % t_pad == 0 and (t_pad & (t_pad - 1)) == 0
        t = r & (t_pad - 1)
    return t >= front


def _norm_matmul_kernel(x_ref, g_ref, w_ref, o_ref, h_ref):
    @pl.when(pl.program_id(1) == 0)
    def _():
        h_ref[...] = _rms(x_ref[...], g_ref[...]).astype(BF16)

    o_ref[...] = _dot(h_ref[...], w_ref[...])


def norm_matmul(x, g, w, *, tm, tn):
    m, d = x.shape
    n = w.shape[1]
    assert m % tm == 0 and n % tn == 0
    return pl.pallas_call(
        _norm_matmul_kernel,
        grid=(m // tm, n // tn),
        in_specs=[pl.BlockSpec((tm, d), lambda i, j: (i, 0)),
                  pl.BlockSpec((1, d), lambda i, j: (0, 0)),
                  pl.BlockSpec((d, tn), lambda i, j: (0, j))],
        out_specs=pl.BlockSpec((tm, tn), lambda i, j: (i, j)),
        out_shape=jax.ShapeDtypeStruct((m, n), F32),
        scratch_shapes=[pltpu.VMEM((tm, d), BF16)],
        compiler_params=_cparams(("parallel", "arbitrary")),
        name="norm_matmul",
    )(x, g.reshape(1, d), w)


def _matmul_norm_res_kernel(*refs, n_in, tm, t_pad, front):
    a_refs, w_refs = refs[:n_in], refs[n_in:2 * n_in]
    g_ref, x_ref, o_ref = refs[2 * n_in:]
    acc = _dot(a_refs[0][...], w_refs[0][...])
    for a, w in zip(a_refs[1:], w_refs[1:]):
        acc = acc + _dot(a[...], w[...])
    out = x_ref[...] + _rms(acc, g_ref[...])
    o_ref[...] = jnp.where(_row_valid(pl.program_id(0), tm, t_pad, front), out, 0.0)


def matmul_norm_res(a_list, w_list, g, x, *, tm, t_pad, front):
    m, d = x.shape
    assert m % tm == 0
    n_in = len(a_list)
    in_specs = ([pl.BlockSpec((tm, a.shape[1]), lambda i: (i, 0)) for a in a_list]
                + [pl.BlockSpec(w.shape, lambda i: (0, 0)) for w in w_list]
                + [pl.BlockSpec((1, d), lambda i: (0, 0)), pl.BlockSpec((tm, d), lambda i: (i, 0))])
    return pl.pallas_call(
        functools.partial(_matmul_norm_res_kernel, n_in=n_in, tm=tm, t_pad=t_pad, front=front),
        grid=(m // tm,),
        in_specs=in_specs,
        out_specs=pl.BlockSpec((tm, d), lambda i: (i, 0)),
        out_shape=jax.ShapeDtypeStruct((m, d), F32),
        compiler_params=_cparams(("parallel",)),
        name="matmul_norm_res",
    )(*a_list, *w_list, g.reshape(1, d), x)


def _ffn_down_kernel(ug_ref, uv_ref, halo_ref, cw_ref, cb_ref, w_ref, g_ref, x_ref, o_ref, *, tm, t_pad, front):
    ug = ug_ref[...]
    row = lax.broadcasted_iota(jnp.int32, (tm, 1), 0)
    prev1 = jnp.where(row == 0, halo_ref[7:8, :], pltpu.roll(ug, 1, axis=0))
    prev2 = jnp.where(row == 0, halo_ref[6:7, :],
                      jnp.where(row == 1, halo_ref[7:8, :], pltpu.roll(ug, 2, axis=0)))
    conv = cb_ref[...] + cw_ref[0:1, :] * prev2 + cw_ref[1:2, :] * prev1 + cw_ref[2:3, :] * ug
    act = (conv * jax.nn.sigmoid(conv) * uv_ref[...]).astype(BF16)
    out = x_ref[...] + _rms(_dot(act, w_ref[...]), g_ref[...])
    o_ref[...] = jnp.where(_row_valid(pl.program_id(0), tm, t_pad, front), out, 0.0)


def ffn_down(up, cw, cb, w_down, g, x, *, tm, t_pad, front):
    m, d = x.shape
    f = w_down.shape[0]
    assert m % tm == 0 and tm % 8 == 0 and up.shape == (m, 2 * f)
    hb = tm // 8
    return pl.pallas_call(
        functools.partial(_ffn_down_kernel, tm=tm, t_pad=t_pad, front=front),
        grid=(m // tm,),
        in_specs=[pl.BlockSpec((tm, f), lambda i: (i, 0)),
                  pl.BlockSpec((tm, f), lambda i: (i, 1)),
                  pl.BlockSpec((8, f), lambda i: (jnp.maximum(i * hb - 1, 0), 0)),
                  pl.BlockSpec((CONV_W, f), lambda i: (0, 0)),
                  pl.BlockSpec((1, f), lambda i: (0, 0)),
                  pl.BlockSpec((f, d), lambda i: (0, 0)),
                  pl.BlockSpec((1, d), lambda i: (0, 0)),
                  pl.BlockSpec((tm, d), lambda i: (i, 0))],
        out_specs=pl.BlockSpec((tm, d), lambda i: (i, 0)),
        out_shape=jax.ShapeDtypeStruct((m, d), F32),
        compiler_params=_cparams(("parallel",)),
        name="ffn_down",
    )(up, up, up, cw, cb.reshape(1, f), w_down, g.reshape(1, d), x)


FF_CW = 256
FF_HALO = 16


def _ffn_fused_kernel(x_ref, halo_ref, gin_ref, wup_ref, cw_ref, cb_ref, wdn_ref, gout_ref, o_ref, conv_ref, acc_ref,
                      *, tm, t_pad, front):
    i = pl.program_id(0)
    x = x_ref[...]
    h = jnp.concatenate([_rms(halo_ref[...], gin_ref[...]), _rms(x, gin_ref[...])], axis=0).astype(BF16)
    for c in range(D_FF // FF_CW):
        cs = slice(c * FF_CW, (c + 1) * FF_CW)
        u = _dot(h, wup_ref[c])
        ug = u[:, :FF_CW]
        prev1 = pltpu.roll(ug, 1, axis=0)[FF_HALO:]
        prev2 = pltpu.roll(ug, 2, axis=0)[FF_HALO:]
        conv = cb_ref[:, cs] + cw_ref[0:1, cs] * prev2 + cw_ref[1:2, cs] * prev1 + cw_ref[2:3, cs] * ug[FF_HALO:]
        act = (conv * jax.nn.sigmoid(conv) * u[FF_HALO:, FF_CW:]).astype(BF16)
        part = _dot(act, wdn_ref[c])
        if c == 0:
            acc_ref[...] = part
        else:
            acc_ref[...] += part
        conv_ref[0, :, cs] = ug[FF_HALO + tm - SUB:, :]
    out = x + _rms(acc_ref[...], gout_ref[...])
    o_ref[...] = jnp.where(_row_valid(i, tm, t_pad, front), out, 0.0)


def ffn_fused(x, g_in, w_up, cw, cb, w_down, g_out, *, tm, t_pad, front):
    m, d = x.shape
    nc = D_FF // FF_CW
    assert m % tm == 0 and t_pad % tm == 0 and tm % FF_HALO == 0 and D_FF % FF_CW == 0
    per_seq = t_pad // tm
    wup = jnp.concatenate([w_up[:, :D_FF].reshape(d, nc, FF_CW), w_up[:, D_FF:].reshape(d, nc, FF_CW)], axis=2)
    wup = wup.transpose(1, 0, 2)
    wdn = w_down.reshape(nc, FF_CW, d)
    hb = tm // FF_HALO
    const = lambda shape: pl.BlockSpec(shape, lambda i: (0,) * len(shape), pipeline_mode=pl.Buffered(1))
    return pl.pallas_call(
        functools.partial(_ffn_fused_kernel, tm=tm, t_pad=t_pad, front=front),
        grid=(m // tm,),
        in_specs=[pl.BlockSpec((tm, d), lambda i: (i, 0)),
                  pl.BlockSpec((FF_HALO, d), lambda i: (jnp.maximum(i * hb - 1, 0), 0)),
                  const((1, d)), const((nc, d, 2 * FF_CW)), const((CONV_W, D_FF)), const((1, D_FF)),
                  const((nc, FF_CW, d)), const((1, d))],
        out_specs=[pl.BlockSpec((tm, d), lambda i: (i, 0)),
                   pl.BlockSpec((1, SUB, D_FF), lambda i: (i // per_seq, 0, 0))],
        out_shape=[jax.ShapeDtypeStruct((m, d), F32), jax.ShapeDtypeStruct((m // t_pad, SUB, D_FF), F32)],
        scratch_shapes=[pltpu.VMEM((tm, d), F32)],
        compiler_params=_cparams(("arbitrary",)),
        name="ffn_fused",
    )(x, x, g_in.reshape(1, d), wup, cw, cb.reshape(1, D_FF), wdn, g_out.reshape(1, d))


def _log_sigmoid(x):
    return jnp.minimum(x, 0.0) - jnp.log1p(jnp.exp(-jnp.abs(x)))


def _rec_kernel(proj_ref, lb_ref, bg_ref, ga_ref, gb_ref, s0_ref, c0_ref, n0_ref, m0_ref,
                y_ref, s_ref, c_ref, n_ref, m_ref, st_ref, kpad, bpad, vpad, *, C, W, front):
    ci = pl.program_id(1)
    nci = pl.num_programs(1)

    @pl.when(ci == 0)
    def _():
        for h in range(H_A):
            st_ref[h] = s0_ref[0, h].T
        c_ref[...] = c0_ref[...]
        n_ref[...] = n0_ref[...]
        m_ref[...] = m0_ref[...]
        kpad[...] = jnp.zeros_like(kpad)
        bpad[...] = jnp.zeros_like(bpad)
        vpad[...] = jnp.zeros_like(vpad)

    row = ci * C + lax.broadcasted_iota(jnp.int32, (C, 1), 0)
    valid = row >= front
    r_i = lax.broadcasted_iota(jnp.int32, (C, C), 0)
    c_i = lax.broadcasted_iota(jnp.int32, (C, C), 1)
    causal = r_i >= c_i
    tri = jnp.where(causal, 1.0, 0.0).astype(BF16)

    gates = proj_ref[0, :, 8 * 512:8 * 512 + LANES] + bg_ref[...]
    lf_all = jnp.where(valid, _log_sigmoid(gates), 0.0)
    ig_all = jnp.where(valid, gates, NEG)
    b_all = _dot_exact_lhs(tri, lf_all)
    b_all_t = b_all.T
    ig_all_t = ig_all.T

    qk_dots, qc_dots = [], []
    for h in range(H_B):
        qb = proj_ref[0, :, 2048 + h * 128:2048 + (h + 1) * 128].astype(BF16)
        kb = (proj_ref[0, :, 2560 + h * 128:2560 + (h + 1) * 128] * (DK_B ** -0.5)).astype(BF16)
        qk_dots.append(_dot_nt(qb, kb))
        qc_dots.append(_dot(qb, c_ref[0, h].astype(BF16)))

    for h in range(H_B):
        q = proj_ref[0, :, 2048 + h * 128:2048 + (h + 1) * 128]
        k = proj_ref[0, :, 2560 + h * 128:2560 + (h + 1) * 128] * (DK_B ** -0.5)
        v = proj_ref[0, :, 3072 + h * 128:3072 + (h + 1) * 128]
        og = proj_ref[0, :, 3584 + h * 128:3584 + (h + 1) * 128]
        vb = v.astype(BF16)
        b_col = b_all[:, H_B + h:H_B + h + 1]
        b_row = b_all_t[H_B + h:H_B + h + 1, :]
        i_col = ig_all[:, h:h + 1]
        i_row = ig_all_t[h:h + 1, :]
        m_prev = m_ref[0, h:h + 1, 0:1]
        dmat = jnp.where(causal, b_col - b_row + i_row, NEG)
        inter = b_col + m_prev
        mt = jnp.maximum(inter, jnp.max(dmat, axis=1, keepdims=True))
        w = jnp.exp(dmat - mt) * qk_dots[h]
        wi = jnp.exp(inter - mt)
        c_st = c_ref[0, h]
        n_st = n_ref[0, h:h + 1, :]
        num = wi * qc_dots[h] + _dot(w.astype(BF16), vb)
        den = wi * jnp.sum(q * n_st, axis=1, keepdims=True) + jnp.sum(w, axis=1, keepdims=True)
        hc = num / jnp.maximum(jnp.abs(den), jnp.exp(-mt))
        m_new = mt[C - 1:C, :]
        b_last = b_col[C - 1:C, :]
        decay = jnp.exp(b_last + m_prev - m_new)
        kw = k * jnp.exp(b_last - b_col + i_col - m_new)
        c_ref[0, h] = decay * c_st + _dot_tn(kw.astype(BF16), vb)
        n_ref[0, h:h + 1, :] = decay * n_st + jnp.sum(kw, axis=0, keepdims=True)
        m_ref[0, h:h + 1, :] = jnp.broadcast_to(m_new, (1, LANES))
        yb = _rms(hc, gb_ref[...]) * jax.nn.sigmoid(og)
        y_ref[0, :, 512 + h * 128:512 + (h + 1) * 128] = jnp.where(valid, yb, 0.0).astype(y_ref.dtype)

    nj = C // W
    rw = lax.broadcasted_iota(jnp.int32, (W, 1), 0)
    ones_sq = jnp.ones((DK_A, LANES), BF16)
    lbv = lb_ref[...]
    f = lbv + (1.0 - lbv) * jax.nn.sigmoid(proj_ref[0, :, 512:1024])
    logf = jnp.where(valid, jnp.log(f), 0.0)
    kk = jnp.where(valid, 1.0 - f, 0.0)
    qq = proj_ref[0, :, 0:512] * (DK_A ** -0.5)
    vv = proj_ref[0, :, 1024:1536]
    bfull = _dot_exact_lhs(tri, logf)
    kpad[W:W + C, :] = kk
    vpad[W:W + C, :] = vv
    bcs = []
    for j in range(nj):
        rows = slice(j * W, (j + 1) * W)
        bc = bfull[rows] if j == 0 else bfull[rows] - bfull[j * W - 1:j * W]
        bpad[W + j * W:W + (j + 1) * W, :] = bc
        bcs.append(bc)

    heads = [slice(h * 128, (h + 1) * 128) for h in range(H_A)]
    row_sums, outer, decay_last = {}, {}, []
    for j in range(nj):
        rows = slice(j * W, (j + 1) * W)
        bc = bcs[j]
        last = bc[W - 1:W, :]
        decay_last.append(jnp.exp(last))
        kdec = (kk[rows] * jnp.exp(last - bc)).astype(BF16)
        gs = []
        for d in range(W):
            lo = W + j * W - d
            e = jnp.exp(jnp.where(rw >= d, bc - bpad[lo:lo + W, :], NEG))
            gs.append(qq[rows] * kpad[lo:lo + W, :] * e)
        g = jnp.concatenate(gs, axis=0).astype(BF16)
        for h, cs in enumerate(heads):
            row_sums[h, j] = _dot(g[:, cs], ones_sq)
            outer[h, j] = _dot_tn(vv[rows, cs].astype(BF16), kdec[:, cs])

    from_state = {}
    for h, cs in enumerate(heads):
        st = st_ref[h]
        for j in range(nj):
            rows = slice(j * W, (j + 1) * W)
            from_state[h, j] = _dot_nt((qq[rows, cs] * jnp.exp(bcs[j][:, cs])).astype(BF16), st.astype(BF16))
            st = decay_last[j][:, cs] * st + outer[h, j]
        st_ref[h] = st

    for h, cs in enumerate(heads):
        for j in range(nj):
            rows = slice(j * W, (j + 1) * W)
            o = from_state[h, j]
            for d in range(W):
                lo = W + j * W - d
                o = o + row_sums[h, j][d * W:(d + 1) * W] * vpad[lo:lo + W, cs]
            ga = proj_ref[0, rows, 1536 + h * 128:1536 + (h + 1) * 128]
            ya = _rms(o, ga_ref[...]) * (ga * jax.nn.sigmoid(ga))
            vld = (ci * C + j * W + rw) >= front
            y_ref[0, rows, cs] = jnp.where(vld, ya, 0.0).astype(y_ref.dtype)

    @pl.when(ci == nci - 1)
    def _():
        for h in range(H_A):
            s_ref[0, h] = st_ref[h].T


def rec_mixer(proj, lb, bg, g_a, g_b, s0, c0, n0, m0, *, C, W, front):
    b, t, _ = proj.shape
    assert t % C == 0 and C % W == 0
    m0b = jnp.broadcast_to(m0[:, :, None], (b, H_B, LANES))
    bgp = jnp.zeros((1, LANES), F32).at[0, :2 * H_B].set(bg.reshape(-1))
    st_spec = pl.BlockSpec((1, 4, 128, 128), lambda i, c: (i, 0, 0, 0))
    v_spec = pl.BlockSpec((1, 4, LANES), lambda i, c: (i, 0, 0))
    row_spec = lambda n: pl.BlockSpec((1, n), lambda i, c: (0, 0))
    y, s, cc, n, m = pl.pallas_call(
        functools.partial(_rec_kernel, C=C, W=W, front=front),
        grid=(b, t // C),
        in_specs=[pl.BlockSpec((1, C, REC_N), lambda i, c: (i, c, 0)),
                  row_spec(512), row_spec(LANES), row_spec(128), row_spec(128),
                  st_spec, st_spec, v_spec, v_spec],
        out_specs=[pl.BlockSpec((1, C, 1024), lambda i, c: (i, c, 0)), st_spec, st_spec, v_spec, v_spec],
        out_shape=[jax.ShapeDtypeStruct((b, t, 1024), BF16),
                   jax.ShapeDtypeStruct((b, 4, 128, 128), F32),
                   jax.ShapeDtypeStruct((b, 4, 128, 128), F32),
                   jax.ShapeDtypeStruct((b, 4, LANES), F32),
                   jax.ShapeDtypeStruct((b, 4, LANES), F32)],
        scratch_shapes=[pltpu.VMEM((4, 128, 128), F32)] + [pltpu.VMEM((W + C, H_A * DK_A), F32)] * 3,
        compiler_params=_cparams(("parallel", "arbitrary")),
        name="rec_mixer",
    )(proj, lb.reshape(1, 512), bgp, g_a.reshape(1, 128), g_b.reshape(1, 128), s0, c0, n0, m0b)
    return y, s, cc, n, m[:, :, 0]


def rope_tables(pos, period, half):
    r = 2 * half
    inv = ROPE_THETA ** (-jnp.arange(half, dtype=F32) * 2.0 / r)
    ang = pos.astype(F32)[:, None] * inv[None, :]
    cos, sin = jnp.cos(ang), jnp.sin(ang)
    lane = np.arange(LANES) % period
    idx = np.where(lane < half, lane, np.where(lane < r, lane - half, 0))
    first, second = jnp.asarray(lane < half), jnp.asarray((lane >= half) & (lane < r))
    c = jnp.where(first | second, cos[:, idx], 1.0)
    sa = jnp.where(first, -sin[:, idx], 0.0)
    sb = jnp.where(second, sin[:, idx], 0.0)
    return c, sa, sb


def _att_prep_kernel(p_ref, c64, a64, b64, c128, a128, b128,
                     qc_o, kc_o, kcb_o, vc_o, vcb_o, qd_o, kd_o, kdb_o, vd_o, vdb_o, qi_o, ki_o, kib_o, wi_o,
                     vct_o, vdt_o):
    def rot(x, c, sa, sb, half):
        return x * c[...] + pltpu.roll(x, LANES - half, axis=1) * sa[...] + pltpu.roll(x, half, axis=1) * sb[...]

    tm = p_ref.shape[0]
    h64 = D_IDX // ROT_FRAC // 2
    h128 = DH_D // ROT_FRAC // 2
    for t in range(4):
        sl = slice(t * LANES, (t + 1) * LANES)
        qc_o[:, sl] = (rot(p_ref[:, sl], c64, a64, b64, h64) * (DC ** -0.5)).astype(BF16)
        kc = rot(p_ref[:, 512 + t * LANES:512 + (t + 1) * LANES], c64, a64, b64, h64)
        kc_o[pl.ds(t, tm, stride=H_C), :] = kc
        kcb_o[:, sl] = kc.astype(BF16)
        vc = p_ref[:, 1024 + t * LANES:1024 + (t + 1) * LANES]
        vc_o[pl.ds(t, tm, stride=H_C), :] = vc
        vcb_o[:, sl] = vc.astype(BF16)
        vct_o[0, sl, :] = vc.T.astype(BF16)
        qd_o[:, sl] = rot(p_ref[:, 1536 + t * LANES:1536 + (t + 1) * LANES], c128, a128, b128, h128).astype(BF16)
    kd = rot(p_ref[:, 2048:2176], c128, a128, b128, h128)
    kd_o[...] = kd
    kdb_o[...] = kd.astype(BF16)
    vd = p_ref[:, 2176:2304]
    vd_o[...] = vd
    vdb_o[...] = vd.astype(BF16)
    vdt_o[0] = vd.T.astype(BF16)
    for t in range(2):
        qi = rot(p_ref[:, 2304 + t * LANES:2304 + (t + 1) * LANES], c64, a64, b64, h64) * (D_IDX ** -0.5)
        qi_o[:, (2 * t) * LANES:(2 * t + 1) * LANES] = qi.astype(BF16)
        qi_o[:, (2 * t + 1) * LANES:(2 * t + 2) * LANES] = pltpu.roll(qi, D_IDX, axis=1).astype(BF16)
    last = p_ref[:, 2560:2688]
    ki = rot(last, c64, a64, b64, h64)[:, :D_IDX]
    ki_o[...] = ki
    kib_o[...] = ki.astype(BF16)
    wi_o[...] = pltpu.roll(last, D_IDX, axis=1) * (H_I ** -0.5)


def att_prep(proj, tabs64, tabs128, *, tm):
    m = proj.shape[0]
    p = tabs64[0].shape[0]
    assert m % tm == 0 and p % tm == 0
    nper = p // tm
    tab_spec = pl.BlockSpec((tm, LANES), lambda i: (i % nper, 0))
    outs = [(512, BF16, 1), (2 * DC, F32, H_C), (512, BF16, 1), (2 * DC, F32, H_C), (512, BF16, 1), (512, BF16, 1),
            (128, F32, 1), (128, BF16, 1), (128, F32, 1), (128, BF16, 1), (512, BF16, 1), (D_IDX, F32, 1),
            (D_IDX, BF16, 1), (128, F32, 1)]
    outs_t = [512, 128]
    return pl.pallas_call(
        _att_prep_kernel,
        grid=(m // tm,),
        in_specs=[pl.BlockSpec((tm, ATT_N), lambda i: (i, 0))] + [tab_spec] * 6,
        out_specs=([pl.BlockSpec((tm * r, w), lambda i: (i, 0)) for w, _, r in outs]
                   + [pl.BlockSpec((1, w, tm), lambda i: (i, 0, 0)) for w in outs_t]),
        out_shape=([jax.ShapeDtypeStruct((m * r, w), dt) for w, dt, r in outs]
                   + [jax.ShapeDtypeStruct((m // tm, w, tm), BF16) for w in outs_t]),
        compiler_params=_cparams(("parallel",)),
        name="att_prep",
    )(proj, *tabs64, *tabs128)


QB = 128


def _diff_lambda(lam_ref, lam_init):
    dl = lam_ref[...]
    s1 = jnp.sum(dl[0:1, :] * dl[1:2, :], axis=1, keepdims=True)
    s2 = jnp.sum(dl[2:3, :] * dl[3:4, :], axis=1, keepdims=True)
    return jnp.exp(s1) - jnp.exp(s2) + lam_init


KB = 384
SUB = 8


def _group_max(x):
    return jnp.max(x.reshape(x.shape[0] // SUB, SUB, x.shape[1]), axis=0)


def _group_sum(x):
    return jnp.sum(x.reshape(x.shape[0] // SUB, SUB, x.shape[1]), axis=0)


def _key_visible(i, off, n_rep, front):
    krow = lax.broadcasted_iota(jnp.int32, (KB, 1), 0)
    lane = lax.broadcasted_iota(jnp.int32, (1, n_rep * KB), 1)
    q = lane
    for r in range(1, n_rep):
        q = jnp.where(lane >= r * KB, lane - r * KB, q)
    return ((i * KB + q - krow) >= off) & (krow >= front - off)


def _edge_then_middle(i, edge_body, middle_body, init):
    carry = lax.fori_loop(0, jnp.minimum(i, 1) + 1, lambda t, c: edge_body(t * i, c), init)
    return lax.fori_loop(1, i, middle_body, carry)


def _diff_prompt_kernel(q_ref, k_ref, vt_ref, lam_ref, g_ref, o_ref, s_ref, *, front, lam_init):
    i = pl.program_id(1)
    lam = _diff_lambda(lam_ref, lam_init)
    lane = lax.broadcasted_iota(jnp.int32, (KB, LANES), 1)
    for h in range(H_C):
        cs = slice(h * LANES, (h + 1) * LANES)
        qh = q_ref[:, cs]
        qstack = jnp.concatenate([jnp.where(lane < DC, qh, jnp.zeros_like(qh)),
                                  jnp.where(lane >= DC, qh, jnp.zeros_like(qh))], axis=0)

        def scores(kb, masked):
            off = pl.multiple_of(kb * KB, KB)
            st = _dot_nt(k_ref[pl.ds(off, KB), cs], qstack)
            return jnp.where(_key_visible(i, off, 2, front), st, NEG) if masked else st

        def pass_a(kb, mx, masked):
            st = scores(kb, masked)
            s_ref[kb] = st
            return jnp.maximum(mx, _group_max(st))

        mx = _edge_then_middle(i, functools.partial(pass_a, masked=True), functools.partial(pass_a, masked=False),
                               jnp.full((SUB, 2 * KB), NEG, F32))
        m = jnp.max(mx, axis=0, keepdims=True)

        def pass_b(kb, carry):
            l8, acc = carry
            p = jnp.exp(s_ref[kb] - m)
            return l8 + _group_sum(p), acc + _dot(vt_ref[kb, cs, :], p.astype(BF16))

        l8, acc = lax.fori_loop(0, i + 1, pass_b, (jnp.zeros((SUB, 2 * KB), F32), jnp.zeros((LANES, 2 * KB), F32)))
        a = acc / jnp.sum(l8, axis=0, keepdims=True)
        ot = a[:, :KB] - lam * a[:, KB:]
        ot = ot * lax.rsqrt(jnp.mean(ot * ot, axis=0, keepdims=True) + EPS) * g_ref[...] * (1.0 - lam_init)
        o_ref[:, cs] = ot.T.astype(o_ref.dtype)


def diff_prompt(q, k, vt, lam_p, g_c, *, b, t_pad, front, lam_init):
    nkb = t_pad // KB
    return pl.pallas_call(
        functools.partial(_diff_prompt_kernel, front=front, lam_init=lam_init),
        grid=(b, nkb),
        in_specs=[pl.BlockSpec((KB, 512), lambda bb, i: (bb * nkb + i, 0)),
                  pl.BlockSpec((t_pad, 512), lambda bb, i: (bb, 0)),
                  pl.BlockSpec((nkb, 512, KB), lambda bb, i: (bb, 0, 0)),
                  pl.BlockSpec((4, DC), lambda bb, i: (0, 0)),
                  pl.BlockSpec((2 * DC, 1), lambda bb, i: (0, 0))],
        out_specs=pl.BlockSpec((KB, 512), lambda bb, i: (bb * nkb + i, 0)),
        out_shape=jax.ShapeDtypeStruct((b * t_pad, 512), BF16),
        scratch_shapes=[pltpu.VMEM((nkb, KB, 2 * KB), F32)],
        compiler_params=_cparams(("parallel", "arbitrary")),
        name="diff_prompt",
    )(q, k, vt, lam_p, g_c.reshape(2 * DC, 1))


NINF = float("-inf")


def _kth_threshold(count_ge, shape, n_sel):
    zero_i = jnp.zeros(shape, jnp.int32)
    neg = jnp.where(count_ge(jnp.zeros(shape, F32)) < n_sel, 1, 0)
    sign = jnp.where(neg == 1, jnp.int32(INT_MIN), 0)

    def bit_body(t, mag):
        cand = mag | lax.shift_left(jnp.int32(1), 30 - t)
        enough = jnp.where(count_ge(pltpu.bitcast(cand | sign, F32)) >= n_sel, 1, 0)
        return jnp.where(enough + neg == 1, cand, mag)

    mag = lax.fori_loop(0, 31, bit_body, zero_i)
    tau = pltpu.bitcast(jnp.where(neg == 1, (mag + 1) | sign, mag), F32)
    ninf = jnp.full(shape, NINF, F32)
    return jnp.where(count_ge(ninf) >= n_sel, tau, ninf)


def _dsa_prompt_kernel(qi_ref, wi_ref, qd_ref, ki_ref, kd_ref, vdt_ref, o_ref, sc_ref, sel_ref, s_ref, *, front, n_sel):
    i = pl.program_id(1)
    nkb = i + 1
    qi_all = jnp.concatenate([qi_ref[:, h * LANES:h * LANES + D_IDX] for h in range(H_I)], axis=0)
    wt = wi_ref[...].T
    w_row = jnp.concatenate([wt[h:h + 1, :] for h in range(H_I)], axis=1)

    def stage1(kb, c, masked):
        off = pl.multiple_of(kb * KB, KB)
        sct = jnp.maximum(_dot_nt(ki_ref[pl.ds(off, KB), :], qi_all), 0.0) * w_row
        score = sct[:, 0:KB] + sct[:, KB:2 * KB] + sct[:, 2 * KB:3 * KB] + sct[:, 3 * KB:4 * KB]
        sc_ref[kb] = jnp.where(_key_visible(i, off, 1, front), score, NINF) if masked else score
        return c

    _edge_then_middle(i, functools.partial(stage1, masked=True), functools.partial(stage1, masked=False), 0)

    def count(pred):
        def body(kb, acc):
            return acc + _group_sum(jnp.where(pred(sc_ref[kb]), 1, 0))
        return jnp.sum(lax.fori_loop(0, nkb, body, jnp.zeros((SUB, KB), jnp.int32)), axis=0, keepdims=True)

    tau = _kth_threshold(lambda cand: count(lambda sc: sc >= cand), (1, KB), n_sel)
    need = (n_sel - count(lambda sc: sc > tau)).astype(F32)

    strict_lower = jnp.where(lax.broadcasted_iota(jnp.int32, (KB, KB), 1) < lax.broadcasted_iota(jnp.int32, (KB, KB), 0),
                             1.0, 0.0).astype(BF16)

    def select(kb, before):
        sc = sc_ref[kb]
        eq = sc == tau
        eqf = jnp.where(eq, 1.0, 0.0)
        rank = _dot(strict_lower, eqf.astype(BF16)) + before
        sel_ref[kb] = jnp.where(((sc > tau) | (eq & (rank < need))) & (sc > NINF), 0.0, NEG)
        return before + jnp.sum(eqf, axis=0, keepdims=True)

    lax.fori_loop(0, nkb, select, jnp.zeros((1, KB), F32))

    for h in range(H_D):
        cs = slice(h * LANES, (h + 1) * LANES)
        qd = qd_ref[:, cs]

        def stage3(kb, mx):
            off = pl.multiple_of(kb * KB, KB)
            sdt = _dot_nt(kd_ref[pl.ds(off, KB), :], qd) * (DH_D ** -0.5) + sel_ref[kb]
            s_ref[kb] = sdt
            return jnp.maximum(mx, _group_max(sdt))

        m = jnp.max(lax.fori_loop(0, nkb, stage3, jnp.full((SUB, KB), NEG, F32)), axis=0, keepdims=True)

        def stage4(kb, carry):
            l8, acc = carry
            p = jnp.exp(s_ref[kb] - m)
            return l8 + _group_sum(p), acc + _dot(vdt_ref[kb], p.astype(BF16))

        l8, acc = lax.fori_loop(0, nkb, stage4, (jnp.zeros((SUB, KB), F32), jnp.zeros((DH_D, KB), F32)))
        o_ref[:, cs] = (acc / jnp.sum(l8, axis=0, keepdims=True)).T.astype(o_ref.dtype)


def dsa_prompt(qi, wi, qd, ki, kd, vdt, *, b, t_pad, front, n_sel):
    nkb = t_pad // KB
    qspec = lambda w: pl.BlockSpec((KB, w), lambda bb, i: (bb * nkb + i, 0))
    kspec = lambda w: pl.BlockSpec((t_pad, w), lambda bb, i: (bb, 0))
    return pl.pallas_call(
        functools.partial(_dsa_prompt_kernel, front=front, n_sel=n_sel),
        grid=(b, nkb),
        in_specs=[qspec(512), qspec(128), qspec(512), kspec(D_IDX), kspec(DH_D),
                  pl.BlockSpec((nkb, DH_D, KB), lambda bb, i: (bb, 0, 0))],
        out_specs=qspec(512),
        out_shape=jax.ShapeDtypeStruct((b * t_pad, 512), BF16),
        scratch_shapes=[pltpu.VMEM((nkb, KB, KB), F32)] * 3,
        compiler_params=_cparams(("parallel", "arbitrary")),
        name="dsa_prompt",
    )(qi, wi, qd, ki, kd, vdt)


TS = 16


PG = 4


def _page_map(g, n_pages, nd):
    def index(bb, j, pt):
        return (pt[bb * n_pages + jnp.minimum(j * PG + g, n_pages - 1)],) + (0,) * nd
    return index


def _online_softmax_update(s, vs, m_ref, l_ref, acc_ref):
    m_old = m_ref[...]
    m_new = jnp.maximum(m_old, jnp.max(s, axis=1, keepdims=True))
    alpha = jnp.exp(m_old - m_new)
    p = jnp.where(s > 0.5 * NEG, jnp.exp(s - m_new), 0.0)
    l_ref[...] = alpha * l_ref[...] + jnp.sum(p, axis=1, keepdims=True)
    r = s.shape[0] // len(vs)
    pv = jnp.concatenate([_dot(p[g * r:(g + 1) * r].astype(BF16), v) for g, v in enumerate(vs)], axis=0)
    acc_ref[...] = alpha * acc_ref[...] + pv
    m_ref[...] = m_new


def _sample_a_kernel(pt_ref, qc_ref, qi_ref, wi_ref, *refs, n_steps, front, lam_init):
    ck_refs, cv_refs, cik_refs = refs[:PG], refs[PG:2 * PG], refs[2 * PG:3 * PG]
    kn_ref, vn_ref, kin_ref, lam_ref, g_ref, o_ref, keys_ref, m_ref, l_ref, acc_ref = refs[3 * PG:]
    j = pl.program_id(1)
    lane = lax.broadcasted_iota(jnp.int32, (TS, LANES), 1)

    @pl.when(j == 0)
    def _():
        m_ref[...] = jnp.full(m_ref.shape, NEG, F32)
        l_ref[...] = jnp.zeros(l_ref.shape, F32)
        acc_ref[...] = jnp.zeros(acc_ref.shape, F32)

    def qstack(h):
        qh = qc_ref[:, h * LANES:(h + 1) * LANES]
        return jnp.concatenate([jnp.where(lane < DC, qh, jnp.zeros_like(qh)),
                                jnp.where(lane >= DC, qh, jnp.zeros_like(qh))], axis=0)

    def head_rows(h):
        return slice(h * 2 * TS, (h + 1) * 2 * TS)

    qi_all = jnp.concatenate([qi_ref[:, h * LANES:h * LANES + D_IDX] for h in range(H_I)], axis=0)
    w_col = jnp.concatenate([wi_ref[:, h:h + 1] for h in range(H_I)], axis=0)

    def idx_scores(kip):
        sc = jnp.maximum(_dot_nt(qi_all, kip), 0.0) * w_col
        return sc[0:TS] + sc[TS:2 * TS] + sc[2 * TS:3 * TS] + sc[3 * TS:4 * TS]

    @pl.when(j < n_steps)
    def _():
        ss, vs = [], []
        for h in range(H_C):
            rows_h = pl.ds(h, PAGE_SIZE, stride=H_C)
            k = jnp.concatenate([r[0, rows_h, :].astype(BF16) for r in ck_refs], axis=0)
            vs.append(jnp.concatenate([r[0, rows_h, :].astype(BF16) for r in cv_refs], axis=0))
            ss.append(_dot_nt(qstack(h), k))
        _online_softmax_update(jnp.concatenate(ss, axis=0), vs, m_ref, l_ref, acc_ref)
        keys_ref[0] = idx_scores(jnp.concatenate([r[0].astype(BF16) for r in cik_refs], axis=0))

    @pl.when(j == n_steps)
    def _():
        kr = lax.broadcasted_iota(jnp.int32, (1, TS), 1)
        q_of_row = lax.broadcasted_iota(jnp.int32, (2 * TS, 1), 0) & (TS - 1)
        ok = (kr >= front) & (kr <= q_of_row)
        ss = [jnp.where(ok, _dot_nt(qstack(h), kn_ref[:, h * LANES:(h + 1) * LANES]), NEG) for h in range(H_C)]
        vs = [vn_ref[:, h * LANES:(h + 1) * LANES] for h in range(H_C)]
        _online_softmax_update(jnp.concatenate(ss, axis=0), vs, m_ref, l_ref, acc_ref)
        krp = lax.broadcasted_iota(jnp.int32, (1, PAGE_SIZE), 1)
        okq = (krp >= front) & (krp < TS) & (krp <= lax.broadcasted_iota(jnp.int32, (TS, 1), 0))
        keys_ref[0] = jnp.concatenate([jnp.where(okq, idx_scores(kin_ref[0]), NINF),
                                       jnp.full((TS, (PG - 1) * PAGE_SIZE), NINF, F32)], axis=1)
        lam = _diff_lambda(lam_ref, lam_init)
        for h in range(H_C):
            a = acc_ref[head_rows(h), :] / l_ref[head_rows(h), :]
            o = a[0:TS] - lam * a[TS:2 * TS]
            o_ref[:, h * LANES:(h + 1) * LANES] = (_rms(o, g_ref[...]) * (1.0 - lam_init)).astype(o_ref.dtype)


def sample_diff_idx(pt, qc, qi, wi, ck, cv, cik, kn, vn, kin, lam_p, g_c, *, b, n_pages, front, lam_init):
    assert n_pages % PG == 0
    n_steps = n_pages // PG
    rows = H_C * 2 * TS
    qspec = lambda w: pl.BlockSpec((TS, w), lambda bb, j, pt: (bb, 0))
    pages4 = [pl.BlockSpec((1, PAGE_SIZE * H_C, 2 * DC), _page_map(g, n_pages, 2)) for g in range(PG)]
    pages_i = [pl.BlockSpec((1, PAGE_SIZE, D_IDX), _page_map(g, n_pages, 2)) for g in range(PG)]
    return pl.pallas_call(
        functools.partial(_sample_a_kernel, n_steps=n_steps, front=front, lam_init=lam_init),
        grid_spec=pltpu.PrefetchScalarGridSpec(
            num_scalar_prefetch=1,
            grid=(b, n_steps + 1),
            in_specs=[qspec(512), qspec(512), qspec(128)] + pages4 + pages4 + pages_i
                     + [qspec(512), qspec(512), pl.BlockSpec((1, PAGE_SIZE, D_IDX), lambda bb, j, pt: (bb, 0, 0)),
                        pl.BlockSpec((4, DC), lambda bb, j, pt: (0, 0)),
                        pl.BlockSpec((1, 2 * DC), lambda bb, j, pt: (0, 0))],
            out_specs=[qspec(512), pl.BlockSpec((1, TS, PG * PAGE_SIZE), lambda bb, j, pt: (bb, 0, j))],
            scratch_shapes=[pltpu.VMEM((rows, 1), F32), pltpu.VMEM((rows, 1), F32), pltpu.VMEM((rows, 2 * DC), F32)]),
        out_shape=[jax.ShapeDtypeStruct((b * TS, 512), BF16),
                   jax.ShapeDtypeStruct((b, TS, (n_steps + 1) * PG * PAGE_SIZE), F32)],
        compiler_params=_cparams(("parallel", "arbitrary")),
        name="sample_diff_idx",
    )(pt, qc, qi, wi, *([ck] * PG), *([cv] * PG), *([cik] * PG), kn, vn, kin, lam_p, g_c.reshape(1, 2 * DC))


def _sample_b_kernel(pt_ref, keys_all_ref, keys_ref, qd_ref, *refs, n_steps, n_sel):
    ck_refs, cv_refs = refs[:PG], refs[PG:2 * PG]
    kn_ref, vn_ref, o_ref, tau_ref, need_ref, before_ref, m_ref, l_ref, acc_ref = refs[2 * PG:]
    j = pl.program_id(1)

    @pl.when(j == 0)
    def _():
        keys = keys_all_ref[0]

        def count_ge(cand):
            return jnp.sum(jnp.where(keys >= cand, 1, 0), axis=1, keepdims=True)

        tau = _kth_threshold(count_ge, (TS, 1), n_sel)
        tau_ref[...] = tau
        need_ref[...] = (n_sel - jnp.sum(jnp.where(keys > tau, 1, 0), axis=1, keepdims=True)).astype(F32)
        before_ref[...] = jnp.zeros(before_ref.shape, F32)
        m_ref[...] = jnp.full(m_ref.shape, NEG, F32)
        l_ref[...] = jnp.zeros(l_ref.shape, F32)
        acc_ref[...] = jnp.zeros(acc_ref.shape, F32)

    strict_upper = jnp.where(lax.broadcasted_iota(jnp.int32, (PAGE_SIZE, PAGE_SIZE), 0)
                             < lax.broadcasted_iota(jnp.int32, (PAGE_SIZE, PAGE_SIZE), 1), 1.0, 0.0).astype(BF16)
    qd_all = jnp.concatenate([qd_ref[:, h * LANES:(h + 1) * LANES] for h in range(H_D)], axis=0)

    def select(key, before):
        tau = tau_ref[...]
        eq = key == tau
        eqf = jnp.where(eq, 1.0, 0.0)
        rank = _dot(eqf.astype(BF16), strict_upper) + before
        sel = ((key > tau) | (eq & (rank < need_ref[...]))) & (key > NINF)
        return sel, before + jnp.sum(eqf, axis=1, keepdims=True)

    def attend(s, sel, v):
        s = jnp.where(jnp.concatenate([sel] * H_D, axis=0), s * (DH_D ** -0.5), NEG)
        _online_softmax_update(s, [v], m_ref, l_ref, acc_ref)

    @pl.when(j < n_steps)
    def _():
        key = keys_ref[0]
        before = before_ref[...]
        sels = []
        for g in range(PG):
            sel, before = select(key[:, g * PAGE_SIZE:(g + 1) * PAGE_SIZE], before)
            sels.append(sel)
        before_ref[...] = before
        k = jnp.concatenate([r[0].astype(BF16) for r in ck_refs], axis=0)
        v = jnp.concatenate([r[0].astype(BF16) for r in cv_refs], axis=0)
        attend(_dot_nt(qd_all, k), jnp.concatenate(sels, axis=1), v)

    @pl.when(j == n_steps)
    def _():
        sel, _ = select(keys_ref[0][:, :PAGE_SIZE], before_ref[...])
        attend(_dot_nt(qd_all, kn_ref[0]), sel, vn_ref[0])
        o = acc_ref[...] / l_ref[...]
        for h in range(H_D):
            o_ref[:, h * LANES:(h + 1) * LANES] = o[h * TS:(h + 1) * TS].astype(o_ref.dtype)


def sample_dsa(pt, keys, qd, ck, cv, kn, vn, *, b, n_pages, n_sel):
    assert n_pages % PG == 0
    n_steps = n_pages // PG
    rows = H_D * TS
    nk = keys.shape[2]
    pages = [pl.BlockSpec((1, PAGE_SIZE, DH_D), _page_map(g, n_pages, 2)) for g in range(PG)]
    nspec = pl.BlockSpec((1, PAGE_SIZE, DH_D), lambda bb, j, pt: (bb, 0, 0))
    return pl.pallas_call(
        functools.partial(_sample_b_kernel, n_steps=n_steps, n_sel=n_sel),
        grid_spec=pltpu.PrefetchScalarGridSpec(
            num_scalar_prefetch=1,
            grid=(b, n_steps + 1),
            in_specs=[pl.BlockSpec((1, TS, nk), lambda bb, j, pt: (bb, 0, 0)),
                      pl.BlockSpec((1, TS, PG * PAGE_SIZE), lambda bb, j, pt: (bb, 0, j)),
                      pl.BlockSpec((TS, 512), lambda bb, j, pt: (bb, 0))] + pages + pages + [nspec, nspec],
            out_specs=pl.BlockSpec((TS, 512), lambda bb, j, pt: (bb, 0)),
            scratch_shapes=[pltpu.VMEM((TS, 1), F32), pltpu.VMEM((TS, 1), F32), pltpu.VMEM((TS, 1), F32),
                            pltpu.VMEM((rows, 1), F32), pltpu.VMEM((rows, 1), F32), pltpu.VMEM((rows, DH_D), F32)]),
        out_shape=jax.ShapeDtypeStruct((b * TS, 512), BF16),
        compiler_params=_cparams(("parallel", "arbitrary")),
        name="sample_dsa",
    )(pt, keys, keys, qd, *([ck] * PG), *([cv] * PG), kn, vn)


REC_CHUNK = 64
REC_SUB = 16
TM_PROJ = 512
TM_REC, TN_REC = 1408, 1408
TM_ATT = 704
TM_ROWS = 384
TM_FFN = 528


def _pad_cols(w, n):
    return jnp.pad(w, ((0, 0), (0, n - w.shape[1])))


def _tile_rows(m, pref):
    return pref if m % pref == 0 else m


def kernel(x_prompt, x_sample, state_hgrn, state_mlstm_C, state_mlstm_n, state_mlstm_m, state_ffn_conv, cache_diff_k, cache_diff_v, cache_dsa_k, cache_dsa_v, cache_idx_k, page_table, meta_tokens, norm_gains, w_in_rec, b_gates_rec, lb_logits, g_norm_hgrn, g_norm_mlstm, w_out_rec, w_in_att, diff_lambda, g_norm_diff, w_out_att, w_ffn_up, ffn_conv_w, ffn_conv_b, w_ffn_down):
    bp, t_in, d = x_prompt.shape
    bs, t_s, _ = x_sample.shape
    depth = norm_gains.shape[0]
    n_pages = page_table.shape[1]
    past_len = n_pages * PAGE_SIZE
    real_p = N_META + t_in
    tp = -(-real_p // QB) * QB
    front_p = tp - real_p
    front_s = TS - t_s
    assert tp % REC_CHUNK == 0 and tp % TM_ROWS == 0 and front_p >= CONV_W - 1 and front_s >= CONV_W - 1
    mp, ms = bp * tp, bs * TS

    meta = jnp.broadcast_to(meta_tokens.astype(x_prompt.dtype)[None], (bp, N_META, d))
    xp = jnp.concatenate([jnp.zeros((bp, front_p, d), x_prompt.dtype), meta, x_prompt], axis=1).reshape(mp, d)
    xs = jnp.concatenate([jnp.zeros((bs, front_s, d), x_sample.dtype), x_sample], axis=1).reshape(ms, d)
    lb_all = jnp.cumsum(jax.nn.softmax(lb_logits.astype(F32), axis=0), axis=0)
    pt_flat = page_table.reshape(-1).astype(jnp.int32)
    sel_p = min(TOPK_MAX, t_in // 4)
    sel_s = min(TOPK_MAX, (past_len + t_s) // 4)
    tmp_s = _tile_rows(ms, TM_PROJ)
    tmr_s = _tile_rows(ms, TM_ROWS)

    pos_p = jnp.arange(tp, dtype=jnp.int32) - front_p
    pos_s = jnp.tile(past_len + jnp.arange(TS, dtype=jnp.int32) - front_s, ms // TS)
    tabs_p = (rope_tables(pos_p, DC, DC // ROT_FRAC // 2), rope_tables(pos_p, DH_D, DH_D // ROT_FRAC // 2))
    tabs_s = (rope_tables(pos_s, DC, DC // ROT_FRAC // 2), rope_tables(pos_s, DH_D, DH_D // ROT_FRAC // 2))

    rec_p, rec_s = [[], [], [], []], [[], [], [], []]
    att_p, att_s = [[], [], [], [], []], [[], [], [], [], []]
    conv_p, conv_s = [], []
    for l in range(depth):
        p = l // 2
        g = norm_gains[l].astype(F32)
        if l % 2 == 0:
            w_in = _pad_cols(w_in_rec[p], REC_N).astype(BF16)
            w_out = w_out_rec[p].astype(BF16)
            prm = (lb_all[p], b_gates_rec[p].astype(F32), g_norm_hgrn[p].astype(F32), g_norm_mlstm[p].astype(F32))
            proj = norm_matmul(xp, g[0], w_in, tm=_tile_rows(mp, TM_REC), tn=TN_REC).reshape(bp, tp, REC_N)
            zs = jnp.zeros((bp, 4, 128, 128), F32)
            y, *st = rec_mixer(proj, *prm, zs, zs, jnp.zeros((bp, 4, 128), F32), jnp.zeros((bp, 4), F32),
                               C=REC_CHUNK, W=REC_SUB, front=front_p)
            xp = matmul_norm_res([y.reshape(mp, -1)], [w_out], g[1], xp, tm=TM_ROWS, t_pad=tp, front=front_p)
            for j in range(4):
                rec_p[j].append(st[j])
            proj = norm_matmul(xs, g[0], w_in, tm=tmp_s, tn=384).reshape(bs, TS, REC_N)
            y, *st = rec_mixer(proj, *prm, state_hgrn[p].astype(F32), state_mlstm_C[p].astype(F32),
                               state_mlstm_n[p].astype(F32), state_mlstm_m[p].astype(F32), C=TS, W=TS, front=front_s)
            xs = matmul_norm_res([y.reshape(ms, -1)], [w_out], g[1], xs, tm=tmr_s, t_pad=TS, front=front_s)
            for j in range(4):
                rec_s[j].append(st[j])
        else:
            lam_init = 0.8 - 0.6 * math.exp(-0.3 * l)
            w_in = _pad_cols(w_in_att[p], ATT_N).astype(BF16)
            w_out = w_out_att[p].astype(BF16)
            dl, gc = diff_lambda[p].astype(F32), g_norm_diff[p].astype(F32)
            proj = norm_matmul(xp, g[0], w_in, tm=_tile_rows(mp, TM_ATT), tn=ATT_N)
            (qc, kc, kcb, vc, vcb, qd, kd, kdb, vd, vdb, qi, ki, kib, wi, vct, vdt) = att_prep(proj, *tabs_p, tm=KB)
            oc = diff_prompt(qc, kcb, vct, dl, gc, b=bp, t_pad=tp, front=front_p, lam_init=lam_init)
            od = dsa_prompt(qi, wi, qd, kib, kdb, vdt, b=bp, t_pad=tp, front=front_p, n_sel=sel_p)
            xp = matmul_norm_res([oc, od], [w_out[:512], w_out[512:]], g[1], xp, tm=TM_ROWS, t_pad=tp, front=front_p)
            for j, (a, shp) in enumerate([(kc, (H_C, 2 * DC)), (vc, (H_C, 2 * DC)), (kd, (DH_D,)), (vd, (DH_D,)), (ki, (D_IDX,))]):
                att_p[j].append(a.reshape((bp, tp) + shp)[:, front_p:])
            proj = norm_matmul(xs, g[0], w_in, tm=tmp_s, tn=384)
            (qc, kc, kcb, vc, vcb, qd, kd, kdb, vd, vdb, qi, ki, kib, wi, _, _) = att_prep(proj, *tabs_s, tm=ms)
            as_page = lambda a: jnp.pad(a.reshape(bs, TS, -1), ((0, 0), (0, PAGE_SIZE - TS), (0, 0)))
            rows_kh = lambda c: c.reshape(c.shape[0], PAGE_SIZE * H_C, 2 * DC)
            oc, keys = sample_diff_idx(pt_flat, qc, qi, wi, rows_kh(cache_diff_k[p]), rows_kh(cache_diff_v[p]), cache_idx_k[p],
                                       kcb, vcb, as_page(kib), dl, gc,
                                       b=bs, n_pages=n_pages, front=front_s, lam_init=lam_init)
            od = sample_dsa(pt_flat, keys, qd, cache_dsa_k[p], cache_dsa_v[p], as_page(kdb), as_page(vdb),
                            b=bs, n_pages=n_pages, n_sel=sel_s)
            xs = matmul_norm_res([oc, od], [w_out[:512], w_out[512:]], g[1], xs, tm=tmr_s, t_pad=TS, front=front_s)
            for j, (a, shp) in enumerate([(kc, (H_C, 2 * DC)), (vc, (H_C, 2 * DC)), (kd, (DH_D,)), (vd, (DH_D,)), (ki, (D_IDX,))]):
                att_s[j].append(a.reshape((bs, TS) + shp)[:, front_s:])
        w_up, w_down = w_ffn_up[l].astype(BF16), w_ffn_down[l].astype(BF16)
        cw, cb = ffn_conv_w[l].astype(F32), ffn_conv_b[l].astype(F32)
        xp, tail = ffn_fused(xp, g[2], w_up, cw, cb, w_down, g[3], tm=TM_FFN, t_pad=tp, front=front_p)
        conv_p.append(tail[:, SUB - (CONV_W - 1):])
        up = norm_matmul(xs, g[2], w_up, tm=tmp_s, tn=512).reshape(bs, TS, 2 * D_FF)
        up = up.at[:, front_s - (CONV_W - 1):front_s, :D_FF].set(state_ffn_conv[l].astype(F32))
        conv_s.append(up[:, TS - (CONV_W - 1):, :D_FF])
        xs = ffn_down(up.reshape(ms, 2 * D_FF), cw, cb, w_down, g[3], xs, tm=tmr_s, t_pad=TS, front=front_s)

    y_p = xp.reshape(bp, tp, d)[:, front_p + N_META:]
    y_s = xs.reshape(bs, TS, d)[:, front_s:]
    stack = lambda xs: xs[0][None] if len(xs) == 1 else jnp.stack(xs)
    return (y_p, y_s,
            stack(rec_p[0]), stack(rec_s[0]), stack(rec_p[1]), stack(rec_s[1]),
            stack(rec_p[2]), stack(rec_s[2]), stack(rec_p[3]), stack(rec_s[3]),
            stack(conv_p), stack(conv_s),
            stack(att_p[0]), stack(att_s[0]), stack(att_p[1]), stack(att_s[1]),
            stack(att_p[2]), stack(att_s[2]), stack(att_p[3]), stack(att_s[3]),
            stack(att_p[4]), stack(att_s[4]))
```

```python
import functools
import math

import jax
import jax.numpy as jnp
import numpy as np
from jax import lax
from jax.experimental import pallas as pl
from jax.experimental.pallas import tpu as pltpu

F32 = jnp.float32
BF16 = jnp.bfloat16

D_MODEL = 1024
N_META = 16
H_A, DK_A, DV_A = 4, 128, 128
H_B, DK_B, DV_B = 4, 128, 128
H_C, DC = 4, 64
H_D, DH_D = 4, 128
H_I, D_IDX = 4, 64
TOPK_MAX = 256
D_FF = 2816
CONV_W = 3
ROPE_THETA = 500000.0
ROT_FRAC = 4
EPS = 1e-6
PAGE_SIZE = 128
LANES = 128
NEG = -1e30

REC_N = 8 * 512 + LANES
ATT_N = 2560 + LANES
INT_MIN = -2 ** 31
VMEM_LIMIT = 56 * 1024 * 1024


def _cparams(sem):
    return pltpu.CompilerParams(dimension_semantics=sem, vmem_limit_bytes=VMEM_LIMIT)


def _rms(x, g):
    return x * lax.rsqrt(jnp.mean(x * x, axis=-1, keepdims=True) + EPS) * g


def _dot(a, b):
    return jnp.dot(a, b, preferred_element_type=F32)


def _dot_nt(a, b):
    return lax.dot_general(a, b, (((1,), (1,)), ((), ())), preferred_element_type=F32)


def _dot_tn(a, b):
    return lax.dot_general(a, b, (((0,), (0,)), ((), ())), preferred_element_type=F32)


def _dot_exact_lhs(tri, x):
    hi = x.astype(BF16)
    r1 = x - hi.astype(F32)
    mid = r1.astype(BF16)
    lo = (r1 - mid.astype(F32)).astype(BF16)
    return _dot(tri, hi) + _dot(tri, mid) + _dot(tri, lo)


def _row_valid(i, tm, t_pad, front):
    r = lax.broadcasted_iota(jnp.int32, (tm, 1), 0)
    if t_pad % tm == 0:
        t = (i % (t_pad // tm)) * tm + r
    else:
        assert tm % t_pad == 0 and (t_pad & (t_pad - 1)) == 0
        t = r & (t_pad - 1)
    return t >= front


def _norm_matmul_kernel(x_ref, g_ref, w_ref, o_ref, h_ref):
    @pl.when(pl.program_id(1) == 0)
    def _():
        h_ref[...] = _rms(x_ref[...], g_ref[...]).astype(BF16)

    o_ref[...] = _dot(h_ref[...], w_ref[...])


def norm_matmul(x, g, w, *, tm, tn):
    m, d = x.shape
    n = w.shape[1]
    assert m % tm == 0 and n % tn == 0
    return pl.pallas_call(
        _norm_matmul_kernel,
        grid=(m // tm, n // tn),
        in_specs=[pl.BlockSpec((tm, d), lambda i, j: (i, 0)),
                  pl.BlockSpec((1, d), lambda i, j: (0, 0)),
                  pl.BlockSpec((d, tn), lambda i, j: (0, j))],
        out_specs=pl.BlockSpec((tm, tn), lambda i, j: (i, j)),
        out_shape=jax.ShapeDtypeStruct((m, n), F32),
        scratch_shapes=[pltpu.VMEM((tm, d), BF16)],
        compiler_params=_cparams(("parallel", "arbitrary")),
        name="norm_matmul",
    )(x, g.reshape(1, d), w)


def _matmul_norm_res_kernel(*refs, n_in, tm, t_pad, front):
    a_refs, w_refs = refs[:n_in], refs[n_in:2 * n_in]
    g_ref, x_ref, o_ref = refs[2 * n_in:]
    acc = _dot(a_refs[0][...], w_refs[0][...])
    for a, w in zip(a_refs[1:], w_refs[1:]):
        acc = acc + _dot(a[...], w[...])
    out = x_ref[...] + _rms(acc, g_ref[...])
    o_ref[...] = jnp.where(_row_valid(pl.program_id(0), tm, t_pad, front), out, 0.0)


def matmul_norm_res(a_list, w_list, g, x, *, tm, t_pad, front):
    m, d = x.shape
    assert m % tm == 0
    n_in = len(a_list)
    in_specs = ([pl.BlockSpec((tm, a.shape[1]), lambda i: (i, 0)) for a in a_list]
                + [pl.BlockSpec(w.shape, lambda i: (0, 0)) for w in w_list]
                + [pl.BlockSpec((1, d), lambda i: (0, 0)), pl.BlockSpec((tm, d), lambda i: (i, 0))])
    return pl.pallas_call(
        functools.partial(_matmul_norm_res_kernel, n_in=n_in, tm=tm, t_pad=t_pad, front=front),
        grid=(m // tm,),
        in_specs=in_specs,
        out_specs=pl.BlockSpec((tm, d), lambda i: (i, 0)),
        out_shape=jax.ShapeDtypeStruct((m, d), F32),
        compiler_params=_cparams(("parallel",)),
        name="matmul_norm_res",
    )(*a_list, *w_list, g.reshape(1, d), x)


def _ffn_down_kernel(ug_ref, uv_ref, buf_ref, cw_ref, cb_ref, w_ref, g_ref, x_ref, o_ref, *, tm, t_pad, front):
    t = lax.broadcasted_iota(jnp.int32, (tm, 1), 0) & (t_pad - 1)
    ug = jnp.where((t >= front - (CONV_W - 1)) & (t < front), buf_ref[...], ug_ref[...])
    conv = (cb_ref[...] + cw_ref[0:1, :] * pltpu.roll(ug, 2, axis=0) + cw_ref[1:2, :] * pltpu.roll(ug, 1, axis=0)
            + cw_ref[2:3, :] * ug)
    act = (conv * jax.nn.sigmoid(conv) * uv_ref[...]).astype(BF16)
    out = x_ref[...] + _rms(_dot(act, w_ref[...]), g_ref[...])
    o_ref[...] = jnp.where(t >= front, out, 0.0)


def ffn_down(up, buf, cw, cb, w_down, g, x, *, tm, t_pad, front):
    m, d = x.shape
    f = w_down.shape[0]
    assert m % tm == 0 and tm % t_pad == 0 and (t_pad & (t_pad - 1)) == 0 and front >= CONV_W - 1
    assert up.shape == (m, 2 * f) and buf.shape == (m, f)
    return pl.pallas_call(
        functools.partial(_ffn_down_kernel, tm=tm, t_pad=t_pad, front=front),
        grid=(m // tm,),
        in_specs=[pl.BlockSpec((tm, f), lambda i: (i, 0)),
                  pl.BlockSpec((tm, f), lambda i: (i, 1)),
                  pl.BlockSpec((tm, f), lambda i: (i, 0)),
                  pl.BlockSpec((CONV_W, f), lambda i: (0, 0)),
                  pl.BlockSpec((1, f), lambda i: (0, 0)),
                  pl.BlockSpec((f, d), lambda i: (0, 0)),
                  pl.BlockSpec((1, d), lambda i: (0, 0)),
                  pl.BlockSpec((tm, d), lambda i: (i, 0))],
        out_specs=pl.BlockSpec((tm, d), lambda i: (i, 0)),
        out_shape=jax.ShapeDtypeStruct((m, d), F32),
        compiler_params=_cparams(("parallel",)),
        name="ffn_down",
    )(up, up, buf, cw, cb.reshape(1, f), w_down, g.reshape(1, d), x)


FF_CW = 256
FF_HALO = 16


def _ffn_fused_kernel(x_ref, halo_ref, gin_ref, wup_ref, cw_ref, cb_ref, wdn_ref, gout_ref, o_ref, conv_ref, acc_ref,
                      *, tm, t_pad, front):
    i = pl.program_id(0)
    x = x_ref[...]
    h = jnp.concatenate([_rms(halo_ref[...], gin_ref[...]), _rms(x, gin_ref[...])], axis=0).astype(BF16)
    for c in range(D_FF // FF_CW):
        cs = slice(c * FF_CW, (c + 1) * FF_CW)
        ug = _dot(h, wup_ref[:, cs])
        uv = _dot(h[FF_HALO:], wup_ref[:, D_FF + c * FF_CW:D_FF + (c + 1) * FF_CW])
        prev1 = pltpu.roll(ug, 1, axis=0)[FF_HALO:]
        prev2 = pltpu.roll(ug, 2, axis=0)[FF_HALO:]
        conv = cb_ref[:, cs] + cw_ref[0:1, cs] * prev2 + cw_ref[1:2, cs] * prev1 + cw_ref[2:3, cs] * ug[FF_HALO:]
        act = (conv * jax.nn.sigmoid(conv) * uv).astype(BF16)
        part = _dot(act, wdn_ref[cs, :])
        if c == 0:
            acc_ref[...] = part
        else:
            acc_ref[...] += part
        conv_ref[0, :, cs] = ug[FF_HALO + tm - SUB:, :]
    out = x + _rms(acc_ref[...], gout_ref[...])
    o_ref[...] = jnp.where(_row_valid(i, tm, t_pad, front), out, 0.0)


def ffn_fused(x, g_in, w_up, cw, cb, w_down, g_out, *, tm, t_pad, front):
    m, d = x.shape
    nc = D_FF // FF_CW
    assert m % tm == 0 and t_pad % tm == 0 and tm % FF_HALO == 0 and D_FF % FF_CW == 0
    per_seq = t_pad // tm
    hb = tm // FF_HALO
    const = lambda shape: pl.BlockSpec(shape, lambda i: (0,) * len(shape), pipeline_mode=pl.Buffered(1))
    return pl.pallas_call(
        functools.partial(_ffn_fused_kernel, tm=tm, t_pad=t_pad, front=front),
        grid=(m // tm,),
        in_specs=[pl.BlockSpec((tm, d), lambda i: (i, 0)),
                  pl.BlockSpec((FF_HALO, d), lambda i: (jnp.maximum(i * hb - 1, 0), 0)),
                  const((1, d)), const((d, 2 * D_FF)), const((CONV_W, D_FF)), const((1, D_FF)),
                  const((D_FF, d)), const((1, d))],
        out_specs=[pl.BlockSpec((tm, d), lambda i: (i, 0)),
                   pl.BlockSpec((1, SUB, D_FF), lambda i: (i // per_seq, 0, 0))],
        out_shape=[jax.ShapeDtypeStruct((m, d), F32), jax.ShapeDtypeStruct((m // t_pad, SUB, D_FF), F32)],
        scratch_shapes=[pltpu.VMEM((tm, d), F32)],
        compiler_params=_cparams(("arbitrary",)),
        name="ffn_fused",
    )(x, x, g_in.reshape(1, d), w_up, cw, cb.reshape(1, D_FF), w_down, g_out.reshape(1, d))


def _log_sigmoid(x):
    return jnp.minimum(x, 0.0) - jnp.log1p(jnp.exp(-jnp.abs(x)))


def _rec_kernel(proj_ref, lb_ref, bg_ref, ga_ref, gb_ref, s0_ref, c0_ref, n0_ref, m0_ref,
                y_ref, s_ref, c_ref, n_ref, m_ref, st_ref, kpad, bpad, vpad, *, C, W, front):
    ci = pl.program_id(1)
    nci = pl.num_programs(1)

    @pl.when(ci == 0)
    def _():
        for h in range(H_A):
            st_ref[h] = s0_ref[0, h].T
        c_ref[...] = c0_ref[...]
        n_ref[...] = n0_ref[...]
        m_ref[...] = m0_ref[...]
        kpad[...] = jnp.zeros_like(kpad)
        bpad[...] = jnp.zeros_like(bpad)
        vpad[...] = jnp.zeros_like(vpad)

    row = ci * C + lax.broadcasted_iota(jnp.int32, (C, 1), 0)
    valid = row >= front
    r_i = lax.broadcasted_iota(jnp.int32, (C, C), 0)
    c_i = lax.broadcasted_iota(jnp.int32, (C, C), 1)
    causal = r_i >= c_i
    tri = jnp.where(causal, 1.0, 0.0).astype(BF16)

    gates = proj_ref[0, :, 8 * 512:8 * 512 + LANES] + bg_ref[...]
    lf_all = jnp.where(valid, _log_sigmoid(gates), 0.0)
    ig_all = jnp.where(valid, gates, NEG)
    b_all = _dot_exact_lhs(tri, lf_all)
    b_all_t = b_all.T
    ig_all_t = ig_all.T

    qk_dots, qc_dots = [], []
    for h in range(H_B):
        qb = proj_ref[0, :, 2048 + h * 128:2048 + (h + 1) * 128].astype(BF16)
        kb = (proj_ref[0, :, 2560 + h * 128:2560 + (h + 1) * 128] * (DK_B ** -0.5)).astype(BF16)
        qk_dots.append(_dot_nt(qb, kb))
        qc_dots.append(_dot(qb, c_ref[0, h].astype(BF16)))

    for h in range(H_B):
        q = proj_ref[0, :, 2048 + h * 128:2048 + (h + 1) * 128]
        k = proj_ref[0, :, 2560 + h * 128:2560 + (h + 1) * 128] * (DK_B ** -0.5)
        v = proj_ref[0, :, 3072 + h * 128:3072 + (h + 1) * 128]
        og = proj_ref[0, :, 3584 + h * 128:3584 + (h + 1) * 128]
        vb = v.astype(BF16)
        b_col = b_all[:, H_B + h:H_B + h + 1]
        b_row = b_all_t[H_B + h:H_B + h + 1, :]
        i_col = ig_all[:, h:h + 1]
        i_row = ig_all_t[h:h + 1, :]
        m_prev = m_ref[0, h:h + 1, 0:1]
        dmat = jnp.where(causal, b_col - b_row + i_row, NEG)
        inter = b_col + m_prev
        mt = jnp.maximum(inter, jnp.max(dmat, axis=1, keepdims=True))
        w = jnp.exp(dmat - mt) * qk_dots[h]
        wi = jnp.exp(inter - mt)
        c_st = c_ref[0, h]
        n_st = n_ref[0, h:h + 1, :]
        num = wi * qc_dots[h] + _dot(w.astype(BF16), vb)
        den = wi * jnp.sum(q * n_st, axis=1, keepdims=True) + jnp.sum(w, axis=1, keepdims=True)
        hc = num / jnp.maximum(jnp.abs(den), jnp.exp(-mt))
        m_new = mt[C - 1:C, :]
        b_last = b_col[C - 1:C, :]
        decay = jnp.exp(b_last + m_prev - m_new)
        kw = k * jnp.exp(b_last - b_col + i_col - m_new)
        c_ref[0, h] = decay * c_st + _dot_tn(kw.astype(BF16), vb)
        n_ref[0, h:h + 1, :] = decay * n_st + jnp.sum(kw, axis=0, keepdims=True)
        m_ref[0, h:h + 1, :] = jnp.broadcast_to(m_new, (1, LANES))
        yb = _rms(hc, gb_ref[...]) * jax.nn.sigmoid(og)
        y_ref[0, :, 512 + h * 128:512 + (h + 1) * 128] = jnp.where(valid, yb, 0.0).astype(y_ref.dtype)

    nj = C // W
    rw = lax.broadcasted_iota(jnp.int32, (W, 1), 0)
    ones_sq = jnp.ones((DK_A, LANES), BF16)
    lbv = lb_ref[...]
    f = lbv + (1.0 - lbv) * jax.nn.sigmoid(proj_ref[0, :, 512:1024])
    logf = jnp.where(valid, jnp.log(f), 0.0)
    kk = jnp.where(valid, 1.0 - f, 0.0)
    qq = proj_ref[0, :, 0:512] * (DK_A ** -0.5)
    vv = proj_ref[0, :, 1024:1536]
    bfull = _dot_exact_lhs(tri, logf)
    kpad[W:W + C, :] = kk
    vpad[W:W + C, :] = vv
    bcs = []
    for j in range(nj):
        rows = slice(j * W, (j + 1) * W)
        bc = bfull[rows] if j == 0 else bfull[rows] - bfull[j * W - 1:j * W]
        bpad[W + j * W:W + (j + 1) * W, :] = bc
        bcs.append(bc)

    heads = [slice(h * 128, (h + 1) * 128) for h in range(H_A)]
    row_sums, outer, decay_last = {}, {}, []
    for j in range(nj):
        rows = slice(j * W, (j + 1) * W)
        bc = bcs[j]
        last = bc[W - 1:W, :]
        decay_last.append(jnp.exp(last))
        kdec = (kk[rows] * jnp.exp(last - bc)).astype(BF16)
        gs = []
        for d in range(W):
            lo = W + j * W - d
            e = jnp.exp(jnp.where(rw >= d, bc - bpad[lo:lo + W, :], NEG))
            gs.append(qq[rows] * kpad[lo:lo + W, :] * e)
        g = jnp.concatenate(gs, axis=0).astype(BF16)
        for h, cs in enumerate(heads):
            row_sums[h, j] = _dot(g[:, cs], ones_sq)
            outer[h, j] = _dot_tn(vv[rows, cs].astype(BF16), kdec[:, cs])

    from_state = {}
    for h, cs in enumerate(heads):
        st = st_ref[h]
        for j in range(nj):
            rows = slice(j * W, (j + 1) * W)
            from_state[h, j] = _dot_nt((qq[rows, cs] * jnp.exp(bcs[j][:, cs])).astype(BF16), st.astype(BF16))
            st = decay_last[j][:, cs] * st + outer[h, j]
        st_ref[h] = st

    for h, cs in enumerate(heads):
        for j in range(nj):
            rows = slice(j * W, (j + 1) * W)
            o = from_state[h, j]
            for d in range(W):
                lo = W + j * W - d
                o = o + row_sums[h, j][d * W:(d + 1) * W] * vpad[lo:lo + W, cs]
            ga = proj_ref[0, rows, 1536 + h * 128:1536 + (h + 1) * 128]
            ya = _rms(o, ga_ref[...]) * (ga * jax.nn.sigmoid(ga))
            vld = (ci * C + j * W + rw) >= front
            y_ref[0, rows, cs] = jnp.where(vld, ya, 0.0).astype(y_ref.dtype)

    @pl.when(ci == nci - 1)
    def _():
        for h in range(H_A):
            s_ref[0, h] = st_ref[h].T


def rec_mixer(proj, lb, bg, g_a, g_b, s0, c0, n0, m0, *, C, W, front):
    b, t, _ = proj.shape
    assert t % C == 0 and C % W == 0
    m0b = jnp.broadcast_to(m0[:, :, None], (b, H_B, LANES))
    bgp = jnp.zeros((1, LANES), F32).at[0, :2 * H_B].set(bg.reshape(-1))
    st_spec = pl.BlockSpec((1, 4, 128, 128), lambda i, c: (i, 0, 0, 0))
    v_spec = pl.BlockSpec((1, 4, LANES), lambda i, c: (i, 0, 0))
    row_spec = lambda n: pl.BlockSpec((1, n), lambda i, c: (0, 0))
    y, s, cc, n, m = pl.pallas_call(
        functools.partial(_rec_kernel, C=C, W=W, front=front),
        grid=(b, t // C),
        in_specs=[pl.BlockSpec((1, C, REC_N), lambda i, c: (i, c, 0)),
                  row_spec(512), row_spec(LANES), row_spec(128), row_spec(128),
                  st_spec, st_spec, v_spec, v_spec],
        out_specs=[pl.BlockSpec((1, C, 1024), lambda i, c: (i, c, 0)), st_spec, st_spec, v_spec, v_spec],
        out_shape=[jax.ShapeDtypeStruct((b, t, 1024), BF16),
                   jax.ShapeDtypeStruct((b, 4, 128, 128), F32),
                   jax.ShapeDtypeStruct((b, 4, 128, 128), F32),
                   jax.ShapeDtypeStruct((b, 4, LANES), F32),
                   jax.ShapeDtypeStruct((b, 4, LANES), F32)],
        scratch_shapes=[pltpu.VMEM((4, 128, 128), F32)] + [pltpu.VMEM((W + C, H_A * DK_A), F32)] * 3,
        compiler_params=_cparams(("parallel", "arbitrary")),
        name="rec_mixer",
    )(proj, lb.reshape(1, 512), bgp, g_a.reshape(1, 128), g_b.reshape(1, 128), s0, c0, n0, m0b)
    return y, s, cc, n, m[:, :, 0]


def rope_tables(pos, period, half):
    r = 2 * half
    inv = ROPE_THETA ** (-jnp.arange(half, dtype=F32) * 2.0 / r)
    ang = pos.astype(F32)[:, None] * inv[None, :]
    cos, sin = jnp.cos(ang), jnp.sin(ang)
    lane = np.arange(LANES) % period
    idx = np.where(lane < half, lane, np.where(lane < r, lane - half, 0))
    first, second = jnp.asarray(lane < half), jnp.asarray((lane >= half) & (lane < r))
    c = jnp.where(first | second, cos[:, idx], 1.0)
    sa = jnp.where(first, -sin[:, idx], 0.0)
    sb = jnp.where(second, sin[:, idx], 0.0)
    return c, sa, sb


def _att_prep_kernel(p_ref, c64, a64, b64, c128, a128, b128,
                     qc_o, kc_o, kcb_o, vc_o, vcb_o, qd_o, kd_o, kdb_o, vd_o, vdb_o, qi_o, ki_o, kib_o, wi_o,
                     vct_o, vdt_o):
    def rot(x, c, sa, sb, half):
        return x * c[...] + pltpu.roll(x, LANES - half, axis=1) * sa[...] + pltpu.roll(x, half, axis=1) * sb[...]

    tm = p_ref.shape[0]
    h64 = D_IDX // ROT_FRAC // 2
    h128 = DH_D // ROT_FRAC // 2
    for t in range(4):
        sl = slice(t * LANES, (t + 1) * LANES)
        qc_o[:, sl] = (rot(p_ref[:, sl], c64, a64, b64, h64) * (DC ** -0.5)).astype(BF16)
        kc = rot(p_ref[:, 512 + t * LANES:512 + (t + 1) * LANES], c64, a64, b64, h64)
        kc_o[pl.ds(t, tm, stride=H_C), :] = kc
        kcb_o[:, sl] = kc.astype(BF16)
        vc = p_ref[:, 1024 + t * LANES:1024 + (t + 1) * LANES]
        vc_o[pl.ds(t, tm, stride=H_C), :] = vc
        vcb_o[:, sl] = vc.astype(BF16)
        vct_o[0, sl, :] = vc.T.astype(BF16)
        qd_o[:, sl] = rot(p_ref[:, 1536 + t * LANES:1536 + (t + 1) * LANES], c128, a128, b128, h128).astype(BF16)
    kd = rot(p_ref[:, 2048:2176], c128, a128, b128, h128)
    kd_o[...] = kd
    kdb_o[...] = kd.astype(BF16)
    vd = p_ref[:, 2176:2304]
    vd_o[...] = vd
    vdb_o[...] = vd.astype(BF16)
    vdt_o[0] = vd.T.astype(BF16)
    for t in range(2):
        qi = rot(p_ref[:, 2304 + t * LANES:2304 + (t + 1) * LANES], c64, a64, b64, h64) * (D_IDX ** -0.5)
        qi_o[:, (2 * t) * LANES:(2 * t + 1) * LANES] = qi.astype(BF16)
        qi_o[:, (2 * t + 1) * LANES:(2 * t + 2) * LANES] = pltpu.roll(qi, D_IDX, axis=1).astype(BF16)
    last = p_ref[:, 2560:2688]
    ki = rot(last, c64, a64, b64, h64)[:, :D_IDX]
    ki_o[...] = ki
    kib_o[...] = ki.astype(BF16)
    wi_o[...] = pltpu.roll(last, D_IDX, axis=1) * (H_I ** -0.5)


def att_prep(proj, tabs64, tabs128, *, tm):
    m = proj.shape[0]
    p = tabs64[0].shape[0]
    assert m % tm == 0 and p % tm == 0
    nper = p // tm
    tab_spec = pl.BlockSpec((tm, LANES), lambda i: (i % nper, 0))
    outs = [(512, BF16, 1), (2 * DC, F32, H_C), (512, BF16, 1), (2 * DC, F32, H_C), (512, BF16, 1), (512, BF16, 1),
            (128, F32, 1), (128, BF16, 1), (128, F32, 1), (128, BF16, 1), (512, BF16, 1), (D_IDX, F32, 1),
            (D_IDX, BF16, 1), (128, F32, 1)]
    outs_t = [512, 128]
    return pl.pallas_call(
        _att_prep_kernel,
        grid=(m // tm,),
        in_specs=[pl.BlockSpec((tm, ATT_N), lambda i: (i, 0))] + [tab_spec] * 6,
        out_specs=([pl.BlockSpec((tm * r, w), lambda i: (i, 0)) for w, _, r in outs]
                   + [pl.BlockSpec((1, w, tm), lambda i: (i, 0, 0)) for w in outs_t]),
        out_shape=([jax.ShapeDtypeStruct((m * r, w), dt) for w, dt, r in outs]
                   + [jax.ShapeDtypeStruct((m // tm, w, tm), BF16) for w in outs_t]),
        compiler_params=_cparams(("parallel",)),
        name="att_prep",
    )(proj, *tabs64, *tabs128)


QB = 128


def _diff_lambda(lam_ref, lam_init):
    dl = lam_ref[...]
    s1 = jnp.sum(dl[0:1, :] * dl[1:2, :], axis=1, keepdims=True)
    s2 = jnp.sum(dl[2:3, :] * dl[3:4, :], axis=1, keepdims=True)
    return jnp.exp(s1) - jnp.exp(s2) + lam_init


KB = 384
SUB = 8


def _group_max(x):
    return jnp.max(x.reshape(x.shape[0] // SUB, SUB, x.shape[1]), axis=0)


def _group_sum(x):
    return jnp.sum(x.reshape(x.shape[0] // SUB, SUB, x.shape[1]), axis=0)


def _key_visible(i, off, n_rep, front):
    krow = lax.broadcasted_iota(jnp.int32, (KB, 1), 0)
    lane = lax.broadcasted_iota(jnp.int32, (1, n_rep * KB), 1)
    q = lane
    for r in range(1, n_rep):
        q = jnp.where(lane >= r * KB, lane - r * KB, q)
    return ((i * KB + q - krow) >= off) & (krow >= front - off)


def _edge_then_middle(i, edge_body, middle_body, init):
    carry = lax.fori_loop(0, jnp.minimum(i, 1) + 1, lambda t, c: edge_body(t * i, c), init)
    return lax.fori_loop(1, i, middle_body, carry)


def _diff_prompt_kernel(q_ref, k_ref, vt_ref, lam_ref, g_ref, o_ref, s_ref, *, front, lam_init):
    i = pl.program_id(1)
    lam = _diff_lambda(lam_ref, lam_init)
    lane = lax.broadcasted_iota(jnp.int32, (KB, LANES), 1)
    for h in range(H_C):
        cs = slice(h * LANES, (h + 1) * LANES)
        qh = q_ref[:, cs]
        qstack = jnp.concatenate([jnp.where(lane < DC, qh, jnp.zeros_like(qh)),
                                  jnp.where(lane >= DC, qh, jnp.zeros_like(qh))], axis=0)

        def scores(kb, masked):
            off = pl.multiple_of(kb * KB, KB)
            st = _dot_nt(k_ref[pl.ds(off, KB), cs], qstack)
            return jnp.where(_key_visible(i, off, 2, front), st, NEG) if masked else st

        def pass_a(kb, mx, masked):
            st = scores(kb, masked)
            s_ref[kb] = st
            return jnp.maximum(mx, _group_max(st))

        mx = _edge_then_middle(i, functools.partial(pass_a, masked=True), functools.partial(pass_a, masked=False),
                               jnp.full((SUB, 2 * KB), NEG, F32))
        m = jnp.max(mx, axis=0, keepdims=True)

        def pass_b(kb, carry):
            l8, acc = carry
            p = jnp.exp(s_ref[kb] - m)
            return l8 + _group_sum(p), acc + _dot(vt_ref[kb, cs, :], p.astype(BF16))

        l8, acc = lax.fori_loop(0, i + 1, pass_b, (jnp.zeros((SUB, 2 * KB), F32), jnp.zeros((LANES, 2 * KB), F32)))
        a = acc / jnp.sum(l8, axis=0, keepdims=True)
        ot = a[:, :KB] - lam * a[:, KB:]
        ot = ot * lax.rsqrt(jnp.mean(ot * ot, axis=0, keepdims=True) + EPS) * g_ref[...] * (1.0 - lam_init)
        o_ref[:, cs] = ot.T.astype(o_ref.dtype)


def diff_prompt(q, k, vt, lam_p, g_c, *, b, t_pad, front, lam_init):
    nkb = t_pad // KB
    return pl.pallas_call(
        functools.partial(_diff_prompt_kernel, front=front, lam_init=lam_init),
        grid=(b, nkb),
        in_specs=[pl.BlockSpec((KB, 512), lambda bb, i: (bb * nkb + i, 0)),
                  pl.BlockSpec((t_pad, 512), lambda bb, i: (bb, 0)),
                  pl.BlockSpec((nkb, 512, KB), lambda bb, i: (bb, 0, 0)),
                  pl.BlockSpec((4, DC), lambda bb, i: (0, 0)),
                  pl.BlockSpec((2 * DC, 1), lambda bb, i: (0, 0))],
        out_specs=pl.BlockSpec((KB, 512), lambda bb, i: (bb * nkb + i, 0)),
        out_shape=jax.ShapeDtypeStruct((b * t_pad, 512), BF16),
        scratch_shapes=[pltpu.VMEM((nkb, KB, 2 * KB), F32)],
        compiler_params=_cparams(("parallel", "arbitrary")),
        name="diff_prompt",
    )(q, k, vt, lam_p, g_c.reshape(2 * DC, 1))


NINF = float("-inf")


def _kth_threshold(count_ge, shape, n_sel):
    zero_i = jnp.zeros(shape, jnp.int32)
    neg = jnp.where(count_ge(jnp.zeros(shape, F32)) < n_sel, 1, 0)
    sign = jnp.where(neg == 1, jnp.int32(INT_MIN), 0)

    def bit_body(t, mag):
        cand = mag | lax.shift_left(jnp.int32(1), 30 - t)
        enough = jnp.where(count_ge(pltpu.bitcast(cand | sign, F32)) >= n_sel, 1, 0)
        return jnp.where(enough + neg == 1, cand, mag)

    mag = lax.fori_loop(0, 31, bit_body, zero_i)
    tau = pltpu.bitcast(jnp.where(neg == 1, (mag + 1) | sign, mag), F32)
    ninf = jnp.full(shape, NINF, F32)
    return jnp.where(count_ge(ninf) >= n_sel, tau, ninf)


def _dsa_prompt_kernel(qi_ref, wi_ref, qd_ref, ki_ref, kd_ref, vdt_ref, o_ref, sc_ref, sel_ref, s_ref, *, front, n_sel):
    i = pl.program_id(1)
    nkb = i + 1
    qi_all = jnp.concatenate([qi_ref[:, h * LANES:h * LANES + D_IDX] for h in range(H_I)], axis=0)
    wt = wi_ref[...].T
    w_row = jnp.concatenate([wt[h:h + 1, :] for h in range(H_I)], axis=1)

    def stage1(kb, c, masked):
        off = pl.multiple_of(kb * KB, KB)
        sct = jnp.maximum(_dot_nt(ki_ref[pl.ds(off, KB), :], qi_all), 0.0) * w_row
        score = sct[:, 0:KB] + sct[:, KB:2 * KB] + sct[:, 2 * KB:3 * KB] + sct[:, 3 * KB:4 * KB]
        sc_ref[kb] = jnp.where(_key_visible(i, off, 1, front), score, NINF) if masked else score
        return c

    _edge_then_middle(i, functools.partial(stage1, masked=True), functools.partial(stage1, masked=False), 0)

    def count(pred):
        def body(kb, acc):
            return acc + _group_sum(jnp.where(pred(sc_ref[kb]), 1, 0))
        return jnp.sum(lax.fori_loop(0, nkb, body, jnp.zeros((SUB, KB), jnp.int32)), axis=0, keepdims=True)

    tau = _kth_threshold(lambda cand: count(lambda sc: sc >= cand), (1, KB), n_sel)
    need = (n_sel - count(lambda sc: sc > tau)).astype(F32)

    strict_lower = jnp.where(lax.broadcasted_iota(jnp.int32, (KB, KB), 1) < lax.broadcasted_iota(jnp.int32, (KB, KB), 0),
                             1.0, 0.0).astype(BF16)

    def select(kb, before):
        sc = sc_ref[kb]
        eq = sc == tau
        eqf = jnp.where(eq, 1.0, 0.0)
        rank = _dot(strict_lower, eqf.astype(BF16)) + before
        sel_ref[kb] = jnp.where(((sc > tau) | (eq & (rank < need))) & (sc > NINF), 0.0, NEG)
        return before + jnp.sum(eqf, axis=0, keepdims=True)

    lax.fori_loop(0, nkb, select, jnp.zeros((1, KB), F32))

    for h in range(H_D):
        cs = slice(h * LANES, (h + 1) * LANES)
        qd = qd_ref[:, cs]

        def stage3(kb, mx):
            off = pl.multiple_of(kb * KB, KB)
            sdt = _dot_nt(kd_ref[pl.ds(off, KB), :], qd) * (DH_D ** -0.5) + sel_ref[kb]
            s_ref[kb] = sdt
            return jnp.maximum(mx, _group_max(sdt))

        m = jnp.max(lax.fori_loop(0, nkb, stage3, jnp.full((SUB, KB), NEG, F32)), axis=0, keepdims=True)

        def stage4(kb, carry):
            l8, acc = carry
            p = jnp.exp(s_ref[kb] - m)
            return l8 + _group_sum(p), acc + _dot(vdt_ref[kb], p.astype(BF16))

        l8, acc = lax.fori_loop(0, nkb, stage4, (jnp.zeros((SUB, KB), F32), jnp.zeros((DH_D, KB), F32)))
        o_ref[:, cs] = (acc / jnp.sum(l8, axis=0, keepdims=True)).T.astype(o_ref.dtype)


def dsa_prompt(qi, wi, qd, ki, kd, vdt, *, b, t_pad, front, n_sel):
    nkb = t_pad // KB
    qspec = lambda w: pl.BlockSpec((KB, w), lambda bb, i: (bb * nkb + i, 0))
    kspec = lambda w: pl.BlockSpec((t_pad, w), lambda bb, i: (bb, 0))
    return pl.pallas_call(
        functools.partial(_dsa_prompt_kernel, front=front, n_sel=n_sel),
        grid=(b, nkb),
        in_specs=[qspec(512), qspec(128), qspec(512), kspec(D_IDX), kspec(DH_D),
                  pl.BlockSpec((nkb, DH_D, KB), lambda bb, i: (bb, 0, 0))],
        out_specs=qspec(512),
        out_shape=jax.ShapeDtypeStruct((b * t_pad, 512), BF16),
        scratch_shapes=[pltpu.VMEM((nkb, KB, KB), F32)] * 3,
        compiler_params=_cparams(("parallel", "arbitrary")),
        name="dsa_prompt",
    )(qi, wi, qd, ki, kd, vdt)


TS = 16


PG = 8


def _page_map(g, n_pages, nd):
    def index(bb, j, pt):
        return (pt[bb * n_pages + jnp.minimum(j * PG + g, n_pages - 1)],) + (0,) * nd
    return index


def _online_softmax_update(s, vs, m_ref, l_ref, acc_ref):
    m_old = m_ref[...]
    m_new = jnp.maximum(m_old, jnp.max(s, axis=1, keepdims=True))
    alpha = jnp.exp(m_old - m_new)
    p = jnp.where(s > 0.5 * NEG, jnp.exp(s - m_new), 0.0)
    l_ref[...] = alpha * l_ref[...] + jnp.sum(p, axis=1, keepdims=True)
    r = s.shape[0] // len(vs)
    pv = jnp.concatenate([_dot(p[g * r:(g + 1) * r].astype(BF16), v) for g, v in enumerate(vs)], axis=0)
    acc_ref[...] = alpha * acc_ref[...] + pv
    m_ref[...] = m_new


def _sample_a_kernel(pt_ref, qc_ref, qi_ref, wi_ref, *refs, n_steps, front, lam_init):
    ck_refs, cv_refs, cik_refs = refs[:PG], refs[PG:2 * PG], refs[2 * PG:3 * PG]
    kn_ref, vn_ref, kin_ref, lam_ref, g_ref, o_ref, keys_ref, m_ref, l_ref, acc_ref = refs[3 * PG:]
    j = pl.program_id(1)
    lane = lax.broadcasted_iota(jnp.int32, (TS, LANES), 1)

    @pl.when(j == 0)
    def _():
        m_ref[...] = jnp.full(m_ref.shape, NEG, F32)
        l_ref[...] = jnp.zeros(l_ref.shape, F32)
        acc_ref[...] = jnp.zeros(acc_ref.shape, F32)

    def qstack(h):
        qh = qc_ref[:, h * LANES:(h + 1) * LANES]
        return jnp.concatenate([jnp.where(lane < DC, qh, jnp.zeros_like(qh)),
                                jnp.where(lane >= DC, qh, jnp.zeros_like(qh))], axis=0)

    def head_rows(h):
        return slice(h * 2 * TS, (h + 1) * 2 * TS)

    qi_all = jnp.concatenate([qi_ref[:, h * LANES:h * LANES + D_IDX] for h in range(H_I)], axis=0)
    w_col = jnp.concatenate([wi_ref[:, h:h + 1] for h in range(H_I)], axis=0)

    def idx_scores(kipt):
        sc = jnp.maximum(_dot(qi_all, kipt), 0.0) * w_col
        return sc[0:TS] + sc[TS:2 * TS] + sc[2 * TS:3 * TS] + sc[3 * TS:4 * TS]

    @pl.when(j < n_steps)
    def _():
        ss, vs = [], []
        for h in range(H_C):
            rows_h = pl.ds(h, PAGE_SIZE, stride=H_C)
            k = jnp.concatenate([r[0, rows_h, :].astype(BF16) for r in ck_refs], axis=0)
            vs.append(jnp.concatenate([r[0, rows_h, :].astype(BF16) for r in cv_refs], axis=0))
            ss.append(_dot_nt(qstack(h), k))
        _online_softmax_update(jnp.concatenate(ss, axis=0), vs, m_ref, l_ref, acc_ref)
        keys_ref[0] = idx_scores(jnp.concatenate([r[0].astype(BF16) for r in cik_refs], axis=1))

    @pl.when(j == n_steps)
    def _():
        kr = lax.broadcasted_iota(jnp.int32, (1, TS), 1)
        q_of_row = lax.broadcasted_iota(jnp.int32, (2 * TS, 1), 0) & (TS - 1)
        ok = (kr >= front) & (kr <= q_of_row)
        ss = [jnp.where(ok, _dot_nt(qstack(h), kn_ref[:, h * LANES:(h + 1) * LANES]), NEG) for h in range(H_C)]
        vs = [vn_ref[:, h * LANES:(h + 1) * LANES] for h in range(H_C)]
        _online_softmax_update(jnp.concatenate(ss, axis=0), vs, m_ref, l_ref, acc_ref)
        krp = lax.broadcasted_iota(jnp.int32, (1, PAGE_SIZE), 1)
        okq = (krp >= front) & (krp < TS) & (krp <= lax.broadcasted_iota(jnp.int32, (TS, 1), 0))
        keys_ref[0] = jnp.concatenate([jnp.where(okq, idx_scores(kin_ref[0]), NINF),
                                       jnp.full((TS, (PG - 1) * PAGE_SIZE), NINF, F32)], axis=1)
        lam = _diff_lambda(lam_ref, lam_init)
        for h in range(H_C):
            a = acc_ref[head_rows(h), :] / l_ref[head_rows(h), :]
            o = a[0:TS] - lam * a[TS:2 * TS]
            o_ref[:, h * LANES:(h + 1) * LANES] = (_rms(o, g_ref[...]) * (1.0 - lam_init)).astype(o_ref.dtype)


def sample_diff_idx(pt, qc, qi, wi, ck, cv, cik, kn, vn, kin, lam_p, g_c, *, b, n_pages, front, lam_init):
    assert n_pages % PG == 0
    n_steps = n_pages // PG
    rows = H_C * 2 * TS
    qspec = lambda w: pl.BlockSpec((TS, w), lambda bb, j, pt: (bb, 0))
    pages4 = [pl.BlockSpec((1, PAGE_SIZE * H_C, 2 * DC), _page_map(g, n_pages, 2)) for g in range(PG)]
    pages_i = [pl.BlockSpec((1, D_IDX, PAGE_SIZE), _page_map(g, n_pages, 2)) for g in range(PG)]
    return pl.pallas_call(
        functools.partial(_sample_a_kernel, n_steps=n_steps, front=front, lam_init=lam_init),
        grid_spec=pltpu.PrefetchScalarGridSpec(
            num_scalar_prefetch=1,
            grid=(b, n_steps + 1),
            in_specs=[qspec(512), qspec(512), qspec(128)] + pages4 + pages4 + pages_i
                     + [qspec(512), qspec(512), pl.BlockSpec((1, D_IDX, PAGE_SIZE), lambda bb, j, pt: (bb, 0, 0)),
                        pl.BlockSpec((4, DC), lambda bb, j, pt: (0, 0)),
                        pl.BlockSpec((1, 2 * DC), lambda bb, j, pt: (0, 0))],
            out_specs=[qspec(512), pl.BlockSpec((1, TS, PG * PAGE_SIZE), lambda bb, j, pt: (bb, 0, j))],
            scratch_shapes=[pltpu.VMEM((rows, 1), F32), pltpu.VMEM((rows, 1), F32), pltpu.VMEM((rows, 2 * DC), F32)]),
        out_shape=[jax.ShapeDtypeStruct((b * TS, 512), BF16),
                   jax.ShapeDtypeStruct((b, TS, (n_steps + 1) * PG * PAGE_SIZE), F32)],
        compiler_params=_cparams(("parallel", "arbitrary")),
        name="sample_diff_idx",
    )(pt, qc, qi, wi, *([ck] * PG), *([cv] * PG), *([cik] * PG), kn, vn, kin, lam_p, g_c.reshape(1, 2 * DC))


def _sample_b_kernel(pt_ref, keys_all_ref, keys_ref, qd_ref, *refs, n_steps, n_sel):
    ck_refs, cv_refs = refs[:PG], refs[PG:2 * PG]
    kn_ref, vn_ref, o_ref, tau_ref, need_ref, before_ref, m_ref, l_ref, acc_ref = refs[2 * PG:]
    j = pl.program_id(1)

    @pl.when(j == 0)
    def _():
        keys = keys_all_ref[0]

        def count_ge(cand):
            return jnp.sum(jnp.where(keys >= cand, 1, 0), axis=1, keepdims=True)

        tau = _kth_threshold(count_ge, (TS, 1), n_sel)
        tau_ref[...] = tau
        need_ref[...] = (n_sel - jnp.sum(jnp.where(keys > tau, 1, 0), axis=1, keepdims=True)).astype(F32)
        before_ref[...] = jnp.zeros(before_ref.shape, F32)
        m_ref[...] = jnp.full(m_ref.shape, NEG, F32)
        l_ref[...] = jnp.zeros(l_ref.shape, F32)
        acc_ref[...] = jnp.zeros(acc_ref.shape, F32)

    strict_upper = jnp.where(lax.broadcasted_iota(jnp.int32, (PAGE_SIZE, PAGE_SIZE), 0)
                             < lax.broadcasted_iota(jnp.int32, (PAGE_SIZE, PAGE_SIZE), 1), 1.0, 0.0).astype(BF16)
    qd_all = jnp.concatenate([qd_ref[:, h * LANES:(h + 1) * LANES] for h in range(H_D)], axis=0)

    def select(key, before):
        tau = tau_ref[...]
        eq = key == tau
        eqf = jnp.where(eq, 1.0, 0.0)
        rank = _dot(eqf.astype(BF16), strict_upper) + before
        sel = ((key > tau) | (eq & (rank < need_ref[...]))) & (key > NINF)
        return sel, before + jnp.sum(eqf, axis=1, keepdims=True)

    def attend(s, sel, v):
        s = jnp.where(jnp.concatenate([sel] * H_D, axis=0), s * (DH_D ** -0.5), NEG)
        _online_softmax_update(s, [v], m_ref, l_ref, acc_ref)

    @pl.when(j < n_steps)
    def _():
        key = keys_ref[0]
        before = before_ref[...]
        sels = []
        for g in range(PG):
            sel, before = select(key[:, g * PAGE_SIZE:(g + 1) * PAGE_SIZE], before)
            sels.append(sel)
        before_ref[...] = before
        k = jnp.concatenate([r[0].astype(BF16) for r in ck_refs], axis=0)
        v = jnp.concatenate([r[0].astype(BF16) for r in cv_refs], axis=0)
        attend(_dot_nt(qd_all, k), jnp.concatenate(sels, axis=1), v)

    @pl.when(j == n_steps)
    def _():
        sel, _ = select(keys_ref[0][:, :PAGE_SIZE], before_ref[...])
        attend(_dot_nt(qd_all, kn_ref[0]), sel, vn_ref[0])
        o = acc_ref[...] / l_ref[...]
        for h in range(H_D):
            o_ref[:, h * LANES:(h + 1) * LANES] = o[h * TS:(h + 1) * TS].astype(o_ref.dtype)


def sample_dsa(pt, keys, qd, ck, cv, kn, vn, *, b, n_pages, n_sel):
    assert n_pages % PG == 0
    n_steps = n_pages // PG
    rows = H_D * TS
    nk = keys.shape[2]
    pages = [pl.BlockSpec((1, PAGE_SIZE, DH_D), _page_map(g, n_pages, 2)) for g in range(PG)]
    nspec = pl.BlockSpec((1, PAGE_SIZE, DH_D), lambda bb, j, pt: (bb, 0, 0))
    return pl.pallas_call(
        functools.partial(_sample_b_kernel, n_steps=n_steps, n_sel=n_sel),
        grid_spec=pltpu.PrefetchScalarGridSpec(
            num_scalar_prefetch=1,
            grid=(b, n_steps + 1),
            in_specs=[pl.BlockSpec((1, TS, nk), lambda bb, j, pt: (bb, 0, 0)),
                      pl.BlockSpec((1, TS, PG * PAGE_SIZE), lambda bb, j, pt: (bb, 0, j)),
                      pl.BlockSpec((TS, 512), lambda bb, j, pt: (bb, 0))] + pages + pages + [nspec, nspec],
            out_specs=pl.BlockSpec((TS, 512), lambda bb, j, pt: (bb, 0)),
            scratch_shapes=[pltpu.VMEM((TS, 1), F32), pltpu.VMEM((TS, 1), F32), pltpu.VMEM((TS, 1), F32),
                            pltpu.VMEM((rows, 1), F32), pltpu.VMEM((rows, 1), F32), pltpu.VMEM((rows, DH_D), F32)]),
        out_shape=jax.ShapeDtypeStruct((b * TS, 512), BF16),
        compiler_params=_cparams(("parallel", "arbitrary")),
        name="sample_dsa",
    )(pt, keys, keys, qd, *([ck] * PG), *([cv] * PG), kn, vn)


REC_CHUNK = 64
REC_SUB = 16
TM_PROJ = 512
TM_REC, TN_REC = 1408, 1408
TM_ATT = 704
TM_ROWS = 384
TM_FFN = 528


def _pad_cols(w, n):
    return jnp.pad(w, ((0, 0), (0, n - w.shape[1])))


def _tile_rows(m, pref):
    return pref if m % pref == 0 else m


def kernel(x_prompt, x_sample, state_hgrn, state_mlstm_C, state_mlstm_n, state_mlstm_m, state_ffn_conv, cache_diff_k, cache_diff_v, cache_dsa_k, cache_dsa_v, cache_idx_k, page_table, meta_tokens, norm_gains, w_in_rec, b_gates_rec, lb_logits, g_norm_hgrn, g_norm_mlstm, w_out_rec, w_in_att, diff_lambda, g_norm_diff, w_out_att, w_ffn_up, ffn_conv_w, ffn_conv_b, w_ffn_down):
    bp, t_in, d = x_prompt.shape
    bs, t_s, _ = x_sample.shape
    depth = norm_gains.shape[0]
    n_pages = page_table.shape[1]
    past_len = n_pages * PAGE_SIZE
    real_p = N_META + t_in
    tp = -(-real_p // QB) * QB
    front_p = tp - real_p
    front_s = TS - t_s
    assert tp % REC_CHUNK == 0 and tp % TM_ROWS == 0 and front_p >= CONV_W - 1 and front_s >= CONV_W - 1
    mp, ms = bp * tp, bs * TS

    meta = jnp.broadcast_to(meta_tokens.astype(x_prompt.dtype)[None], (bp, N_META, d))
    xp = jnp.concatenate([jnp.zeros((bp, front_p, d), x_prompt.dtype), meta, x_prompt], axis=1).reshape(mp, d)
    xs = jnp.concatenate([jnp.zeros((bs, front_s, d), x_sample.dtype), x_sample], axis=1).reshape(ms, d)
    lb_all = jnp.cumsum(jax.nn.softmax(lb_logits.astype(F32), axis=0), axis=0)
    pt_flat = page_table.reshape(-1).astype(jnp.int32)
    sel_p = min(TOPK_MAX, t_in // 4)
    sel_s = min(TOPK_MAX, (past_len + t_s) // 4)
    tmp_s = _tile_rows(ms, TM_PROJ)
    tmr_s = _tile_rows(ms, TM_ROWS)

    pos_p = jnp.arange(tp, dtype=jnp.int32) - front_p
    pos_s = jnp.tile(past_len + jnp.arange(TS, dtype=jnp.int32) - front_s, ms // TS)
    tabs_p = (rope_tables(pos_p, DC, DC // ROT_FRAC // 2), rope_tables(pos_p, DH_D, DH_D // ROT_FRAC // 2))
    tabs_s = (rope_tables(pos_s, DC, DC // ROT_FRAC // 2), rope_tables(pos_s, DH_D, DH_D // ROT_FRAC // 2))

    rec_p, rec_s = [[], [], [], []], [[], [], [], []]
    att_p, att_s = [[], [], [], [], []], [[], [], [], [], []]
    conv_p, conv_s = [], []
    for l in range(depth):
        p = l // 2
        g = norm_gains[l].astype(F32)
        if l % 2 == 0:
            w_in = _pad_cols(w_in_rec[p], REC_N).astype(BF16)
            w_out = w_out_rec[p].astype(BF16)
            prm = (lb_all[p], b_gates_rec[p].astype(F32), g_norm_hgrn[p].astype(F32), g_norm_mlstm[p].astype(F32))
            proj = norm_matmul(xp, g[0], w_in, tm=_tile_rows(mp, TM_REC), tn=TN_REC).reshape(bp, tp, REC_N)
            zs = jnp.zeros((bp, 4, 128, 128), F32)
            y, *st = rec_mixer(proj, *prm, zs, zs, jnp.zeros((bp, 4, 128), F32), jnp.zeros((bp, 4), F32),
                               C=REC_CHUNK, W=REC_SUB, front=front_p)
            xp = matmul_norm_res([y.reshape(mp, -1)], [w_out], g[1], xp, tm=TM_ROWS, t_pad=tp, front=front_p)
            for j in range(4):
                rec_p[j].append(st[j])
            proj = norm_matmul(xs, g[0], w_in, tm=tmp_s, tn=384).reshape(bs, TS, REC_N)
            y, *st = rec_mixer(proj, *prm, state_hgrn[p].astype(F32), state_mlstm_C[p].astype(F32),
                               state_mlstm_n[p].astype(F32), state_mlstm_m[p].astype(F32), C=TS, W=TS, front=front_s)
            xs = matmul_norm_res([y.reshape(ms, -1)], [w_out], g[1], xs, tm=tmr_s, t_pad=TS, front=front_s)
            for j in range(4):
                rec_s[j].append(st[j])
        else:
            lam_init = 0.8 - 0.6 * math.exp(-0.3 * l)
            w_in = _pad_cols(w_in_att[p], ATT_N).astype(BF16)
            w_out = w_out_att[p].astype(BF16)
            dl, gc = diff_lambda[p].astype(F32), g_norm_diff[p].astype(F32)
            proj = norm_matmul(xp, g[0], w_in, tm=_tile_rows(mp, TM_ATT), tn=ATT_N)
            (qc, kc, kcb, vc, vcb, qd, kd, kdb, vd, vdb, qi, ki, kib, wi, vct, vdt) = att_prep(proj, *tabs_p, tm=KB)
            oc = diff_prompt(qc, kcb, vct, dl, gc, b=bp, t_pad=tp, front=front_p, lam_init=lam_init)
            od = dsa_prompt(qi, wi, qd, kib, kdb, vdt, b=bp, t_pad=tp, front=front_p, n_sel=sel_p)
            xp = matmul_norm_res([oc, od], [w_out[:512], w_out[512:]], g[1], xp, tm=TM_ROWS, t_pad=tp, front=front_p)
            for j, (a, shp) in enumerate([(kc, (H_C, 2 * DC)), (vc, (H_C, 2 * DC)), (kd, (DH_D,)), (vd, (DH_D,)), (ki, (D_IDX,))]):
                att_p[j].append(a.reshape((bp, tp) + shp)[:, front_p:])
            proj = norm_matmul(xs, g[0], w_in, tm=tmp_s, tn=384)
            (qc, kc, kcb, vc, vcb, qd, kd, kdb, vd, vdb, qi, ki, kib, wi, _, _) = att_prep(proj, *tabs_s, tm=ms)
            as_page = lambda a: jnp.pad(a.reshape(bs, TS, -1), ((0, 0), (0, PAGE_SIZE - TS), (0, 0)))
            rows_kh = lambda c: c.reshape(c.shape[0], PAGE_SIZE * H_C, 2 * DC)
            oc, keys = sample_diff_idx(pt_flat, qc, qi, wi, rows_kh(cache_diff_k[p]), rows_kh(cache_diff_v[p]),
                                       jnp.swapaxes(cache_idx_k[p], 1, 2), kcb, vcb, jnp.swapaxes(as_page(kib), 1, 2), dl, gc,
                                       b=bs, n_pages=n_pages, front=front_s, lam_init=lam_init)
            od = sample_dsa(pt_flat, keys, qd, cache_dsa_k[p], cache_dsa_v[p], as_page(kdb), as_page(vdb),
                            b=bs, n_pages=n_pages, n_sel=sel_s)
            xs = matmul_norm_res([oc, od], [w_out[:512], w_out[512:]], g[1], xs, tm=tmr_s, t_pad=TS, front=front_s)
            for j, (a, shp) in enumerate([(kc, (H_C, 2 * DC)), (vc, (H_C, 2 * DC)), (kd, (DH_D,)), (vd, (DH_D,)), (ki, (D_IDX,))]):
                att_s[j].append(a.reshape((bs, TS) + shp)[:, front_s:])
        w_up, w_down = w_ffn_up[l].astype(BF16), w_ffn_down[l].astype(BF16)
        cw, cb = ffn_conv_w[l].astype(F32), ffn_conv_b[l].astype(F32)
        xp, tail = ffn_fused(xp, g[2], w_up, cw, cb, w_down, g[3], tm=TM_FFN, t_pad=tp, front=front_p)
        conv_p.append(tail[:, SUB - (CONV_W - 1):])
        up = norm_matmul(xs, g[2], w_up, tm=tmp_s, tn=512)
        conv_s.append(up.reshape(bs, TS, 2 * D_FF)[:, TS - (CONV_W - 1):, :D_FF])
        buf = jnp.pad(state_ffn_conv[l].astype(F32), ((0, 0), (front_s - (CONV_W - 1), TS - front_s), (0, 0)))
        xs = ffn_down(up, buf.reshape(ms, D_FF), cw, cb, w_down, g[3], xs, tm=tmr_s, t_pad=TS, front=front_s)

    y_p = xp.reshape(bp, tp, d)[:, front_p + N_META:]
    y_s = xs.reshape(bs, TS, d)[:, front_s:]
    stack = lambda xs: xs[0][None] if len(xs) == 1 else jnp.stack(xs)
    return (y_p, y_s,
            stack(rec_p[0]), stack(rec_s[0]), stack(rec_p[1]), stack(rec_s[1]),
            stack(rec_p[2]), stack(rec_s[2]), stack(rec_p[3]), stack(rec_s[3]),
            stack(conv_p), stack(conv_s),
            stack(att_p[0]), stack(att_s[0]), stack(att_p[1]), stack(att_s[1]),
            stack(att_p[2]), stack(att_s[2]), stack(att_p[3]), stack(att_s[3]),
            stack(att_p[4]), stack(att_s[4]))
```

```python
import functools
import math

import jax
import jax.numpy as jnp
import numpy as np
from jax import lax
from jax.experimental import pallas as pl
from jax.experimental.pallas import tpu as pltpu

F32 = jnp.float32
BF16 = jnp.bfloat16

D_MODEL = 1024
N_META = 16
H_A, DK_A, DV_A = 4, 128, 128
H_B, DK_B, DV_B = 4, 128, 128
H_C, DC = 4, 64
H_D, DH_D = 4, 128
H_I, D_IDX = 4, 64
TOPK_MAX = 256
D_FF = 2816
CONV_W = 3
ROPE_THETA = 500000.0
ROT_FRAC = 4
EPS = 1e-6
PAGE_SIZE = 128
LANES = 128
NEG = -1e30

REC_N = 8 * 512 + LANES
ATT_N = 2560 + LANES
INT_MIN = -2 ** 31
VMEM_LIMIT = 56 * 1024 * 1024


def _cparams(sem):
    return pltpu.CompilerParams(dimension_semantics=sem, vmem_limit_bytes=VMEM_LIMIT)


def _rms(x, g):
    return x * lax.rsqrt(jnp.mean(x * x, axis=-1, keepdims=True) + EPS) * g


def _dot(a, b):
    return jnp.dot(a, b, preferred_element_type=F32)


def _dot_nt(a, b):
    return lax.dot_general(a, b, (((1,), (1,)), ((), ())), preferred_element_type=F32)


def _dot_tn(a, b):
    return lax.dot_general(a, b, (((0,), (0,)), ((), ())), preferred_element_type=F32)


def _dot_exact_lhs(tri, x):
    hi = x.astype(BF16)
    r1 = x - hi.astype(F32)
    mid = r1.astype(BF16)
    lo = (r1 - mid.astype(F32)).astype(BF16)
    return _dot(tri, hi) + _dot(tri, mid) + _dot(tri, lo)


def _row_valid(i, tm, t_pad, front):
    r = lax.broadcasted_iota(jnp.int32, (tm, 1), 0)
    if t_pad % tm == 0:
        t = (i % (t_pad // tm)) * tm + r
    else:
        assert tm % t_pad == 0 and (t_pad & (t_pad - 1)) == 0
        t = r & (t_pad - 1)
    return t >= front


def _norm_matmul_kernel(x_ref, g_ref, w_ref, o_ref, h_ref):
    @pl.when(pl.program_id(1) == 0)
    def _():
        h_ref[...] = _rms(x_ref[...], g_ref[...]).astype(BF16)

    o_ref[...] = _dot(h_ref[...], w_ref[...])


def norm_matmul(x, g, w, *, tm, tn):
    m, d = x.shape
    n = w.shape[1]
    assert m % tm == 0 and n % tn == 0
    return pl.pallas_call(
        _norm_matmul_kernel,
        grid=(m // tm, n // tn),
        in_specs=[pl.BlockSpec((tm, d), lambda i, j: (i, 0)),
                  pl.BlockSpec((1, d), lambda i, j: (0, 0)),
                  pl.BlockSpec((d, tn), lambda i, j: (0, j))],
        out_specs=pl.BlockSpec((tm, tn), lambda i, j: (i, j)),
        out_shape=jax.ShapeDtypeStruct((m, n), F32),
        scratch_shapes=[pltpu.VMEM((tm, d), BF16)],
        compiler_params=_cparams(("parallel", "arbitrary")),
        name="norm_matmul",
    )(x, g.reshape(1, d), w)


def _matmul_norm_res_kernel(*refs, n_in, tm, t_pad, front):
    a_refs, w_refs = refs[:n_in], refs[n_in:2 * n_in]
    g_ref, x_ref, o_ref = refs[2 * n_in:]
    acc = _dot(a_refs[0][...], w_refs[0][...])
    for a, w in zip(a_refs[1:], w_refs[1:]):
        acc = acc + _dot(a[...], w[...])
    out = x_ref[...] + _rms(acc, g_ref[...])
    o_ref[...] = jnp.where(_row_valid(pl.program_id(0), tm, t_pad, front), out, 0.0)


def matmul_norm_res(a_list, w_list, g, x, *, tm, t_pad, front):
    m, d = x.shape
    assert m % tm == 0
    n_in = len(a_list)
    in_specs = ([pl.BlockSpec((tm, a.shape[1]), lambda i: (i, 0)) for a in a_list]
                + [pl.BlockSpec(w.shape, lambda i: (0, 0)) for w in w_list]
                + [pl.BlockSpec((1, d), lambda i: (0, 0)), pl.BlockSpec((tm, d), lambda i: (i, 0))])
    return pl.pallas_call(
        functools.partial(_matmul_norm_res_kernel, n_in=n_in, tm=tm, t_pad=t_pad, front=front),
        grid=(m // tm,),
        in_specs=in_specs,
        out_specs=pl.BlockSpec((tm, d), lambda i: (i, 0)),
        out_shape=jax.ShapeDtypeStruct((m, d), F32),
        compiler_params=_cparams(("parallel",)),
        name="matmul_norm_res",
    )(*a_list, *w_list, g.reshape(1, d), x)


def _ffn_down_kernel(ug_ref, uv_ref, buf_ref, cw_ref, cb_ref, w_ref, g_ref, x_ref, o_ref, *, tm, t_pad, front):
    t = lax.broadcasted_iota(jnp.int32, (tm, 1), 0) & (t_pad - 1)
    ug = jnp.where((t >= front - (CONV_W - 1)) & (t < front), buf_ref[...], ug_ref[...])
    conv = (cb_ref[...] + cw_ref[0:1, :] * pltpu.roll(ug, 2, axis=0) + cw_ref[1:2, :] * pltpu.roll(ug, 1, axis=0)
            + cw_ref[2:3, :] * ug)
    act = (conv * jax.nn.sigmoid(conv) * uv_ref[...]).astype(BF16)
    out = x_ref[...] + _rms(_dot(act, w_ref[...]), g_ref[...])
    o_ref[...] = jnp.where(t >= front, out, 0.0)


def ffn_down(up, buf, cw, cb, w_down, g, x, *, tm, t_pad, front):
    m, d = x.shape
    f = w_down.shape[0]
    assert m % tm == 0 and tm % t_pad == 0 and (t_pad & (t_pad - 1)) == 0 and front >= CONV_W - 1
    assert up.shape == (m, 2 * f) and buf.shape == (m, f)
    return pl.pallas_call(
        functools.partial(_ffn_down_kernel, tm=tm, t_pad=t_pad, front=front),
        grid=(m // tm,),
        in_specs=[pl.BlockSpec((tm, f), lambda i: (i, 0)),
                  pl.BlockSpec((tm, f), lambda i: (i, 1)),
                  pl.BlockSpec((tm, f), lambda i: (i, 0)),
                  pl.BlockSpec((CONV_W, f), lambda i: (0, 0)),
                  pl.BlockSpec((1, f), lambda i: (0, 0)),
                  pl.BlockSpec((f, d), lambda i: (0, 0)),
                  pl.BlockSpec((1, d), lambda i: (0, 0)),
                  pl.BlockSpec((tm, d), lambda i: (i, 0))],
        out_specs=pl.BlockSpec((tm, d), lambda i: (i, 0)),
        out_shape=jax.ShapeDtypeStruct((m, d), F32),
        compiler_params=_cparams(("parallel",)),
        name="ffn_down",
    )(up, up, buf, cw, cb.reshape(1, f), w_down, g.reshape(1, d), x)


FF_CW = 256
FF_HALO = 16


def _ffn_fused_kernel(x_ref, halo_ref, gin_ref, wup_ref, cw_ref, cb_ref, wdn_ref, gout_ref, o_ref, conv_ref, acc_ref,
                      *, tm, t_pad, front):
    i = pl.program_id(0)
    x = x_ref[...]
    h = jnp.concatenate([_rms(halo_ref[...], gin_ref[...]), _rms(x, gin_ref[...])], axis=0).astype(BF16)
    for c in range(D_FF // FF_CW):
        cs = slice(c * FF_CW, (c + 1) * FF_CW)
        ug = _dot(h, wup_ref[:, cs])
        uv = _dot(h[FF_HALO:], wup_ref[:, D_FF + c * FF_CW:D_FF + (c + 1) * FF_CW])
        prev1 = pltpu.roll(ug, 1, axis=0)[FF_HALO:]
        prev2 = pltpu.roll(ug, 2, axis=0)[FF_HALO:]
        conv = cb_ref[:, cs] + cw_ref[0:1, cs] * prev2 + cw_ref[1:2, cs] * prev1 + cw_ref[2:3, cs] * ug[FF_HALO:]
        act = (conv * jax.nn.sigmoid(conv) * uv).astype(BF16)
        part = _dot(act, wdn_ref[cs, :])
        if c == 0:
            acc_ref[...] = part
        else:
            acc_ref[...] += part
        conv_ref[0, :, cs] = ug[FF_HALO + tm - SUB:, :]
    out = x + _rms(acc_ref[...], gout_ref[...])
    o_ref[...] = jnp.where(_row_valid(i, tm, t_pad, front), out, 0.0)


def ffn_fused(x, g_in, w_up, cw, cb, w_down, g_out, *, tm, t_pad, front):
    m, d = x.shape
    nc = D_FF // FF_CW
    assert m % tm == 0 and t_pad % tm == 0 and tm % FF_HALO == 0 and D_FF % FF_CW == 0
    per_seq = t_pad // tm
    hb = tm // FF_HALO
    const = lambda shape: pl.BlockSpec(shape, lambda i: (0,) * len(shape), pipeline_mode=pl.Buffered(1))
    return pl.pallas_call(
        functools.partial(_ffn_fused_kernel, tm=tm, t_pad=t_pad, front=front),
        grid=(m // tm,),
        in_specs=[pl.BlockSpec((tm, d), lambda i: (i, 0)),
                  pl.BlockSpec((FF_HALO, d), lambda i: (jnp.maximum(i * hb - 1, 0), 0)),
                  const((1, d)), const((d, 2 * D_FF)), const((CONV_W, D_FF)), const((1, D_FF)),
                  const((D_FF, d)), const((1, d))],
        out_specs=[pl.BlockSpec((tm, d), lambda i: (i, 0)),
                   pl.BlockSpec((1, SUB, D_FF), lambda i: (i // per_seq, 0, 0))],
        out_shape=[jax.ShapeDtypeStruct((m, d), F32), jax.ShapeDtypeStruct((m // t_pad, SUB, D_FF), F32)],
        scratch_shapes=[pltpu.VMEM((tm, d), F32)],
        compiler_params=_cparams(("arbitrary",)),
        name="ffn_fused",
    )(x, x, g_in.reshape(1, d), w_up, cw, cb.reshape(1, D_FF), w_down, g_out.reshape(1, d))


def _log_sigmoid(x):
    return jnp.minimum(x, 0.0) - jnp.log1p(jnp.exp(-jnp.abs(x)))


def _rec_kernel(proj_ref, lb_ref, bg_ref, ga_ref, gb_ref, s0_ref, c0_ref, n0_ref, m0_ref,
                y_ref, s_ref, c_ref, n_ref, m_ref, st_ref, kpad, bpad, vpad, *, C, W, front):
    ci = pl.program_id(1)
    nci = pl.num_programs(1)

    @pl.when(ci == 0)
    def _():
        for h in range(H_A):
            st_ref[h] = s0_ref[0, h].T
        c_ref[...] = c0_ref[...]
        n_ref[...] = n0_ref[...]
        m_ref[...] = m0_ref[...]
        kpad[...] = jnp.zeros_like(kpad)
        bpad[...] = jnp.zeros_like(bpad)
        vpad[...] = jnp.zeros_like(vpad)

    row = ci * C + lax.broadcasted_iota(jnp.int32, (C, 1), 0)
    valid = row >= front
    r_i = lax.broadcasted_iota(jnp.int32, (C, C), 0)
    c_i = lax.broadcasted_iota(jnp.int32, (C, C), 1)
    causal = r_i >= c_i
    tri = jnp.where(causal, 1.0, 0.0).astype(BF16)

    gates = proj_ref[0, :, 8 * 512:8 * 512 + LANES] + bg_ref[...]
    lf_all = jnp.where(valid, _log_sigmoid(gates), 0.0)
    ig_all = jnp.where(valid, gates, NEG)
    b_all = _dot_exact_lhs(tri, lf_all)
    b_all_t = b_all.T
    ig_all_t = ig_all.T

    qk_dots, qc_dots = [], []
    for h in range(H_B):
        qb = proj_ref[0, :, 2048 + h * 128:2048 + (h + 1) * 128].astype(BF16)
        kb = (proj_ref[0, :, 2560 + h * 128:2560 + (h + 1) * 128] * (DK_B ** -0.5)).astype(BF16)
        qk_dots.append(_dot_nt(qb, kb))
        qc_dots.append(_dot(qb, c_ref[0, h].astype(BF16)))

    for h in range(H_B):
        q = proj_ref[0, :, 2048 + h * 128:2048 + (h + 1) * 128]
        k = proj_ref[0, :, 2560 + h * 128:2560 + (h + 1) * 128] * (DK_B ** -0.5)
        v = proj_ref[0, :, 3072 + h * 128:3072 + (h + 1) * 128]
        og = proj_ref[0, :, 3584 + h * 128:3584 + (h + 1) * 128]
        vb = v.astype(BF16)
        b_col = b_all[:, H_B + h:H_B + h + 1]
        b_row = b_all_t[H_B + h:H_B + h + 1, :]
        i_col = ig_all[:, h:h + 1]
        i_row = ig_all_t[h:h + 1, :]
        m_prev = m_ref[0, h:h + 1, 0:1]
        dmat = jnp.where(causal, b_col - b_row + i_row, NEG)
        inter = b_col + m_prev
        mt = jnp.maximum(inter, jnp.max(dmat, axis=1, keepdims=True))
        w = jnp.exp(dmat - mt) * qk_dots[h]
        wi = jnp.exp(inter - mt)
        c_st = c_ref[0, h]
        n_st = n_ref[0, h:h + 1, :]
        num = wi * qc_dots[h] + _dot(w.astype(BF16), vb)
        den = wi * jnp.sum(q * n_st, axis=1, keepdims=True) + jnp.sum(w, axis=1, keepdims=True)
        hc = num / jnp.maximum(jnp.abs(den), jnp.exp(-mt))
        m_new = mt[C - 1:C, :]
        b_last = b_col[C - 1:C, :]
        decay = jnp.exp(b_last + m_prev - m_new)
        kw = k * jnp.exp(b_last - b_col + i_col - m_new)
        c_ref[0, h] = decay * c_st + _dot_tn(kw.astype(BF16), vb)
        n_ref[0, h:h + 1, :] = decay * n_st + jnp.sum(kw, axis=0, keepdims=True)
        m_ref[0, h:h + 1, :] = jnp.broadcast_to(m_new, (1, LANES))
        yb = _rms(hc, gb_ref[...]) * jax.nn.sigmoid(og)
        y_ref[0, :, 512 + h * 128:512 + (h + 1) * 128] = jnp.where(valid, yb, 0.0).astype(y_ref.dtype)

    nj = C // W
    rw = lax.broadcasted_iota(jnp.int32, (W, 1), 0)
    ones_sq = jnp.ones((DK_A, LANES), BF16)
    lbv = lb_ref[...]
    f = lbv + (1.0 - lbv) * jax.nn.sigmoid(proj_ref[0, :, 512:1024])
    logf = jnp.where(valid, jnp.log(f), 0.0)
    kk = jnp.where(valid, 1.0 - f, 0.0)
    qq = proj_ref[0, :, 0:512] * (DK_A ** -0.5)
    vv = proj_ref[0, :, 1024:1536]
    bfull = _dot_exact_lhs(tri, logf)
    kpad[W:W + C, :] = kk
    vpad[W:W + C, :] = vv
    bcs = []
    for j in range(nj):
        rows = slice(j * W, (j + 1) * W)
        bc = bfull[rows] if j == 0 else bfull[rows] - bfull[j * W - 1:j * W]
        bpad[W + j * W:W + (j + 1) * W, :] = bc
        bcs.append(bc)

    heads = [slice(h * 128, (h + 1) * 128) for h in range(H_A)]
    row_sums, outer, decay_last = {}, {}, []
    for j in range(nj):
        rows = slice(j * W, (j + 1) * W)
        bc = bcs[j]
        last = bc[W - 1:W, :]
        decay_last.append(jnp.exp(last))
        kdec = (kk[rows] * jnp.exp(last - bc)).astype(BF16)
        gs = []
        for d in range(W):
            lo = W + j * W - d
            e = jnp.exp(jnp.where(rw >= d, bc - bpad[lo:lo + W, :], NEG))
            gs.append(qq[rows] * kpad[lo:lo + W, :] * e)
        g = jnp.concatenate(gs, axis=0).astype(BF16)
        for h, cs in enumerate(heads):
            row_sums[h, j] = _dot(g[:, cs], ones_sq)
            outer[h, j] = _dot_tn(vv[rows, cs].astype(BF16), kdec[:, cs])

    from_state = {}
    for h, cs in enumerate(heads):
        st = st_ref[h]
        for j in range(nj):
            rows = slice(j * W, (j + 1) * W)
            from_state[h, j] = _dot_nt((qq[rows, cs] * jnp.exp(bcs[j][:, cs])).astype(BF16), st.astype(BF16))
            st = decay_last[j][:, cs] * st + outer[h, j]
        st_ref[h] = st

    for h, cs in enumerate(heads):
        for j in range(nj):
            rows = slice(j * W, (j + 1) * W)
            o = from_state[h, j]
            for d in range(W):
                lo = W + j * W - d
                o = o + row_sums[h, j][d * W:(d + 1) * W] * vpad[lo:lo + W, cs]
            ga = proj_ref[0, rows, 1536 + h * 128:1536 + (h + 1) * 128]
            ya = _rms(o, ga_ref[...]) * (ga * jax.nn.sigmoid(ga))
            vld = (ci * C + j * W + rw) >= front
            y_ref[0, rows, cs] = jnp.where(vld, ya, 0.0).astype(y_ref.dtype)

    @pl.when(ci == nci - 1)
    def _():
        for h in range(H_A):
            s_ref[0, h] = st_ref[h].T


def rec_mixer(proj, lb, bg, g_a, g_b, s0, c0, n0, m0, *, C, W, front):
    b, t, _ = proj.shape
    assert t % C == 0 and C % W == 0
    m0b = jnp.broadcast_to(m0[:, :, None], (b, H_B, LANES))
    bgp = jnp.zeros((1, LANES), F32).at[0, :2 * H_B].set(bg.reshape(-1))
    st_spec = pl.BlockSpec((1, 4, 128, 128), lambda i, c: (i, 0, 0, 0))
    v_spec = pl.BlockSpec((1, 4, LANES), lambda i, c: (i, 0, 0))
    row_spec = lambda n: pl.BlockSpec((1, n), lambda i, c: (0, 0))
    y, s, cc, n, m = pl.pallas_call(
        functools.partial(_rec_kernel, C=C, W=W, front=front),
        grid=(b, t // C),
        in_specs=[pl.BlockSpec((1, C, REC_N), lambda i, c: (i, c, 0)),
                  row_spec(512), row_spec(LANES), row_spec(128), row_spec(128),
                  st_spec, st_spec, v_spec, v_spec],
        out_specs=[pl.BlockSpec((1, C, 1024), lambda i, c: (i, c, 0)), st_spec, st_spec, v_spec, v_spec],
        out_shape=[jax.ShapeDtypeStruct((b, t, 1024), BF16),
                   jax.ShapeDtypeStruct((b, 4, 128, 128), F32),
                   jax.ShapeDtypeStruct((b, 4, 128, 128), F32),
                   jax.ShapeDtypeStruct((b, 4, LANES), F32),
                   jax.ShapeDtypeStruct((b, 4, LANES), F32)],
        scratch_shapes=[pltpu.VMEM((4, 128, 128), F32)] + [pltpu.VMEM((W + C, H_A * DK_A), F32)] * 3,
        compiler_params=_cparams(("parallel", "arbitrary")),
        name="rec_mixer",
    )(proj, lb.reshape(1, 512), bgp, g_a.reshape(1, 128), g_b.reshape(1, 128), s0, c0, n0, m0b)
    return y, s, cc, n, m[:, :, 0]


def rope_tables(pos, period, half):
    r = 2 * half
    inv = ROPE_THETA ** (-jnp.arange(half, dtype=F32) * 2.0 / r)
    ang = pos.astype(F32)[:, None] * inv[None, :]
    cos, sin = jnp.cos(ang), jnp.sin(ang)
    lane = np.arange(LANES) % period
    idx = np.where(lane < half, lane, np.where(lane < r, lane - half, 0))
    first, second = jnp.asarray(lane < half), jnp.asarray((lane >= half) & (lane < r))
    c = jnp.where(first | second, cos[:, idx], 1.0)
    sa = jnp.where(first, -sin[:, idx], 0.0)
    sb = jnp.where(second, sin[:, idx], 0.0)
    return c, sa, sb


def _att_prep_kernel(p_ref, c64, a64, b64, c128, a128, b128,
                     qc_o, kc_o, kcb_o, vc_o, vcb_o, qd_o, kd_o, kdb_o, vd_o, vdb_o, qi_o, ki_o, kib_o, wi_o,
                     vct_o, vdt_o):
    def rot(x, c, sa, sb, half):
        return x * c[...] + pltpu.roll(x, LANES - half, axis=1) * sa[...] + pltpu.roll(x, half, axis=1) * sb[...]

    tm = p_ref.shape[0]
    h64 = D_IDX // ROT_FRAC // 2
    h128 = DH_D // ROT_FRAC // 2
    for t in range(4):
        sl = slice(t * LANES, (t + 1) * LANES)
        qc_o[:, sl] = (rot(p_ref[:, sl], c64, a64, b64, h64) * (DC ** -0.5)).astype(BF16)
        kc = rot(p_ref[:, 512 + t * LANES:512 + (t + 1) * LANES], c64, a64, b64, h64)
        kc_o[pl.ds(t, tm, stride=H_C), :] = kc
        kcb_o[:, sl] = kc.astype(BF16)
        vc = p_ref[:, 1024 + t * LANES:1024 + (t + 1) * LANES]
        vc_o[pl.ds(t, tm, stride=H_C), :] = vc
        vcb_o[:, sl] = vc.astype(BF16)
        vct_o[0, sl, :] = vc.T.astype(BF16)
        qd_o[:, sl] = rot(p_ref[:, 1536 + t * LANES:1536 + (t + 1) * LANES], c128, a128, b128, h128).astype(BF16)
    kd = rot(p_ref[:, 2048:2176], c128, a128, b128, h128)
    kd_o[...] = kd
    kdb_o[...] = kd.astype(BF16)
    vd = p_ref[:, 2176:2304]
    vd_o[...] = vd
    vdb_o[...] = vd.astype(BF16)
    vdt_o[0] = vd.T.astype(BF16)
    for t in range(2):
        qi = rot(p_ref[:, 2304 + t * LANES:2304 + (t + 1) * LANES], c64, a64, b64, h64) * (D_IDX ** -0.5)
        qi_o[:, (2 * t) * LANES:(2 * t + 1) * LANES] = qi.astype(BF16)
        qi_o[:, (2 * t + 1) * LANES:(2 * t + 2) * LANES] = pltpu.roll(qi, D_IDX, axis=1).astype(BF16)
    last = p_ref[:, 2560:2688]
    ki = rot(last, c64, a64, b64, h64)[:, :D_IDX]
    ki_o[...] = ki
    kib_o[...] = ki.astype(BF16)
    wi_o[...] = pltpu.roll(last, D_IDX, axis=1) * (H_I ** -0.5)


def att_prep(proj, tabs64, tabs128, *, tm):
    m = proj.shape[0]
    p = tabs64[0].shape[0]
    assert m % tm == 0 and p % tm == 0
    nper = p // tm
    tab_spec = pl.BlockSpec((tm, LANES), lambda i: (i % nper, 0))
    outs = [(512, BF16, 1), (2 * DC, F32, H_C), (512, BF16, 1), (2 * DC, F32, H_C), (512, BF16, 1), (512, BF16, 1),
            (128, F32, 1), (128, BF16, 1), (128, F32, 1), (128, BF16, 1), (512, BF16, 1), (D_IDX, F32, 1),
            (D_IDX, BF16, 1), (128, F32, 1)]
    outs_t = [512, 128]
    return pl.pallas_call(
        _att_prep_kernel,
        grid=(m // tm,),
        in_specs=[pl.BlockSpec((tm, ATT_N), lambda i: (i, 0))] + [tab_spec] * 6,
        out_specs=([pl.BlockSpec((tm * r, w), lambda i: (i, 0)) for w, _, r in outs]
                   + [pl.BlockSpec((1, w, tm), lambda i: (i, 0, 0)) for w in outs_t]),
        out_shape=([jax.ShapeDtypeStruct((m * r, w), dt) for w, dt, r in outs]
                   + [jax.ShapeDtypeStruct((m // tm, w, tm), BF16) for w in outs_t]),
        compiler_params=_cparams(("parallel",)),
        name="att_prep",
    )(proj, *tabs64, *tabs128)


QB = 128


def _diff_lambda(lam_ref, lam_init):
    dl = lam_ref[...]
    s1 = jnp.sum(dl[0:1, :] * dl[1:2, :], axis=1, keepdims=True)
    s2 = jnp.sum(dl[2:3, :] * dl[3:4, :], axis=1, keepdims=True)
    return jnp.exp(s1) - jnp.exp(s2) + lam_init


KB = 384
SUB = 8


def _group_max(x):
    return jnp.max(x.reshape(x.shape[0] // SUB, SUB, x.shape[1]), axis=0)


def _group_sum(x):
    return jnp.sum(x.reshape(x.shape[0] // SUB, SUB, x.shape[1]), axis=0)


def _key_visible(i, off, n_rep, front):
    krow = lax.broadcasted_iota(jnp.int32, (KB, 1), 0)
    lane = lax.broadcasted_iota(jnp.int32, (1, n_rep * KB), 1)
    q = lane
    for r in range(1, n_rep):
        q = jnp.where(lane >= r * KB, lane - r * KB, q)
    return ((i * KB + q - krow) >= off) & (krow >= front - off)


def _fori_pairs(lo, hi, one, two, init):
    n = jnp.maximum(hi - lo, 0)
    odd = n % 2
    carry = lax.fori_loop(0, odd, lambda t, c: one(lo, c), init)
    return lax.fori_loop(0, n // 2, lambda t, c: two(lo + odd + 2 * t, c), carry)


def _edge_then_middle(i, edge_body, middle_body, init):
    carry = lax.fori_loop(0, jnp.minimum(i, 1) + 1, lambda t, c: edge_body(t * i, c), init)
    return _fori_pairs(1, i, middle_body, lambda kb, c: middle_body(kb + 1, middle_body(kb, c)), carry)


def _diff_prompt_kernel(q_ref, k_ref, vt_ref, lam_ref, g_ref, o_ref, s_ref, *, front, lam_init):
    i = pl.program_id(1)
    lam = _diff_lambda(lam_ref, lam_init)
    lane = lax.broadcasted_iota(jnp.int32, (KB, LANES), 1)
    for h in range(H_C):
        cs = slice(h * LANES, (h + 1) * LANES)
        qh = q_ref[:, cs]
        qstack = jnp.concatenate([jnp.where(lane < DC, qh, jnp.zeros_like(qh)),
                                  jnp.where(lane >= DC, qh, jnp.zeros_like(qh))], axis=0)

        def scores(kb, masked):
            off = pl.multiple_of(kb * KB, KB)
            st = _dot_nt(k_ref[pl.ds(off, KB), cs], qstack)
            return jnp.where(_key_visible(i, off, 2, front), st, NEG) if masked else st

        def pass_a(kb, mx, masked):
            st = scores(kb, masked)
            s_ref[kb] = st
            return jnp.maximum(mx, _group_max(st))

        mx = _edge_then_middle(i, functools.partial(pass_a, masked=True), functools.partial(pass_a, masked=False),
                               jnp.full((SUB, 2 * KB), NEG, F32))
        m = jnp.max(mx, axis=0, keepdims=True)

        def pass_b(kb, carry):
            l8, acc = carry
            p = jnp.exp(s_ref[kb] - m)
            return l8 + _group_sum(p), acc + _dot(vt_ref[kb, cs, :], p.astype(BF16))

        def pass_b2(kb, carry):
            l8, acc = carry
            p = jnp.exp(jnp.concatenate([s_ref[kb], s_ref[kb + 1]], axis=0) - m)
            vt2 = jnp.concatenate([vt_ref[kb, cs, :], vt_ref[kb + 1, cs, :]], axis=1)
            return l8 + _group_sum(p), acc + _dot(vt2, p.astype(BF16))

        l8, acc = _fori_pairs(0, i + 1, pass_b, pass_b2,
                              (jnp.zeros((SUB, 2 * KB), F32), jnp.zeros((LANES, 2 * KB), F32)))
        a = acc / jnp.sum(l8, axis=0, keepdims=True)
        ot = a[:, :KB] - lam * a[:, KB:]
        ot = ot * lax.rsqrt(jnp.mean(ot * ot, axis=0, keepdims=True) + EPS) * g_ref[...] * (1.0 - lam_init)
        o_ref[:, cs] = ot.T.astype(o_ref.dtype)


def diff_prompt(q, k, vt, lam_p, g_c, *, b, t_pad, front, lam_init):
    nkb = t_pad // KB
    return pl.pallas_call(
        functools.partial(_diff_prompt_kernel, front=front, lam_init=lam_init),
        grid=(b, nkb),
        in_specs=[pl.BlockSpec((KB, 512), lambda bb, i: (bb * nkb + i, 0)),
                  pl.BlockSpec((t_pad, 512), lambda bb, i: (bb, 0)),
                  pl.BlockSpec((nkb, 512, KB), lambda bb, i: (bb, 0, 0)),
                  pl.BlockSpec((4, DC), lambda bb, i: (0, 0)),
                  pl.BlockSpec((2 * DC, 1), lambda bb, i: (0, 0))],
        out_specs=pl.BlockSpec((KB, 512), lambda bb, i: (bb * nkb + i, 0)),
        out_shape=jax.ShapeDtypeStruct((b * t_pad, 512), BF16),
        scratch_shapes=[pltpu.VMEM((nkb, KB, 2 * KB), F32)],
        compiler_params=_cparams(("parallel", "arbitrary")),
        name="diff_prompt",
    )(q, k, vt, lam_p, g_c.reshape(2 * DC, 1))


NINF = float("-inf")


def _kth_threshold(count_ge, shape, n_sel):
    zero_i = jnp.zeros(shape, jnp.int32)
    neg = jnp.where(count_ge(jnp.zeros(shape, F32)) < n_sel, 1, 0)
    sign = jnp.where(neg == 1, jnp.int32(INT_MIN), 0)

    def bit_body(t, mag):
        cand = mag | lax.shift_left(jnp.int32(1), 30 - t)
        enough = jnp.where(count_ge(pltpu.bitcast(cand | sign, F32)) >= n_sel, 1, 0)
        return jnp.where(enough + neg == 1, cand, mag)

    mag = lax.fori_loop(0, 31, bit_body, zero_i)
    tau = pltpu.bitcast(jnp.where(neg == 1, (mag + 1) | sign, mag), F32)
    ninf = jnp.full(shape, NINF, F32)
    return jnp.where(count_ge(ninf) >= n_sel, tau, ninf)


def _dsa_prompt_kernel(qi_ref, wi_ref, qd_ref, ki_ref, kd_ref, vdt_ref, o_ref, sc_ref, sel_ref, s_ref, *, front, n_sel):
    i = pl.program_id(1)
    nkb = i + 1
    qi_all = jnp.concatenate([qi_ref[:, h * LANES:h * LANES + D_IDX] for h in range(H_I)], axis=0)
    wt = wi_ref[...].T
    w_row = jnp.concatenate([wt[h:h + 1, :] for h in range(H_I)], axis=1)

    def stage1(kb, c, masked):
        off = pl.multiple_of(kb * KB, KB)
        sct = jnp.maximum(_dot_nt(ki_ref[pl.ds(off, KB), :], qi_all), 0.0) * w_row
        score = sct[:, 0:KB] + sct[:, KB:2 * KB] + sct[:, 2 * KB:3 * KB] + sct[:, 3 * KB:4 * KB]
        sc_ref[kb] = jnp.where(_key_visible(i, off, 1, front), score, NINF) if masked else score
        return c

    _edge_then_middle(i, functools.partial(stage1, masked=True), functools.partial(stage1, masked=False), 0)

    def count(pred):
        def body(kb, acc):
            return acc + _group_sum(jnp.where(pred(sc_ref[kb]), 1, 0))
        return jnp.sum(lax.fori_loop(0, nkb, body, jnp.zeros((SUB, KB), jnp.int32)), axis=0, keepdims=True)

    tau = _kth_threshold(lambda cand: count(lambda sc: sc >= cand), (1, KB), n_sel)
    need = (n_sel - count(lambda sc: sc > tau)).astype(F32)

    strict_lower = jnp.where(lax.broadcasted_iota(jnp.int32, (KB, KB), 1) < lax.broadcasted_iota(jnp.int32, (KB, KB), 0),
                             1.0, 0.0).astype(BF16)

    def select(kb, before):
        sc = sc_ref[kb]
        eq = sc == tau
        eqf = jnp.where(eq, 1.0, 0.0)
        rank = _dot(strict_lower, eqf.astype(BF16)) + before
        sel_ref[kb] = jnp.where(((sc > tau) | (eq & (rank < need))) & (sc > NINF), 0.0, NEG)
        return before + jnp.sum(eqf, axis=0, keepdims=True)

    lax.fori_loop(0, nkb, select, jnp.zeros((1, KB), F32))

    for h in range(H_D):
        cs = slice(h * LANES, (h + 1) * LANES)
        qd = qd_ref[:, cs]

        def stage3(kb, mx):
            off = pl.multiple_of(kb * KB, KB)
            sdt = _dot_nt(kd_ref[pl.ds(off, KB), :], qd) * (DH_D ** -0.5) + sel_ref[kb]
            s_ref[kb] = sdt
            return jnp.maximum(mx, _group_max(sdt))

        m = jnp.max(_fori_pairs(0, nkb, stage3, lambda kb, c: stage3(kb + 1, stage3(kb, c)),
                                jnp.full((SUB, KB), NEG, F32)), axis=0, keepdims=True)

        def stage4(kb, carry):
            l8, acc = carry
            p = jnp.exp(s_ref[kb] - m)
            return l8 + _group_sum(p), acc + _dot(vdt_ref[kb], p.astype(BF16))

        def stage4_pair(kb, carry):
            l8, acc = carry
            p = jnp.exp(jnp.concatenate([s_ref[kb], s_ref[kb + 1]], axis=0) - m)
            vt2 = jnp.concatenate([vdt_ref[kb], vdt_ref[kb + 1]], axis=1)
            return l8 + _group_sum(p), acc + _dot(vt2, p.astype(BF16))

        l8, acc = _fori_pairs(0, nkb, stage4, stage4_pair, (jnp.zeros((SUB, KB), F32), jnp.zeros((DH_D, KB), F32)))
        o_ref[:, cs] = (acc / jnp.sum(l8, axis=0, keepdims=True)).T.astype(o_ref.dtype)


def dsa_prompt(qi, wi, qd, ki, kd, vdt, *, b, t_pad, front, n_sel):
    nkb = t_pad // KB
    qspec = lambda w: pl.BlockSpec((KB, w), lambda bb, i: (bb * nkb + i, 0))
    kspec = lambda w: pl.BlockSpec((t_pad, w), lambda bb, i: (bb, 0))
    return pl.pallas_call(
        functools.partial(_dsa_prompt_kernel, front=front, n_sel=n_sel),
        grid=(b, nkb),
        in_specs=[qspec(512), qspec(128), qspec(512), kspec(D_IDX), kspec(DH_D),
                  pl.BlockSpec((nkb, DH_D, KB), lambda bb, i: (bb, 0, 0))],
        out_specs=qspec(512),
        out_shape=jax.ShapeDtypeStruct((b * t_pad, 512), BF16),
        scratch_shapes=[pltpu.VMEM((nkb, KB, KB), F32)] * 3,
        compiler_params=_cparams(("parallel", "arbitrary")),
        name="dsa_prompt",
    )(qi, wi, qd, ki, kd, vdt)


TS = 16


PG = 8


def _page_map(g, n_pages, nd):
    def index(bb, j, pt):
        return (pt[bb * n_pages + jnp.minimum(j * PG + g, n_pages - 1)],) + (0,) * nd
    return index


def _online_softmax_update(s, vs, m_ref, l_ref, acc_ref):
    m_old = m_ref[...]
    m_new = jnp.maximum(m_old, jnp.max(s, axis=1, keepdims=True))
    alpha = jnp.exp(m_old - m_new)
    p = jnp.where(s > 0.5 * NEG, jnp.exp(s - m_new), 0.0)
    l_ref[...] = alpha * l_ref[...] + jnp.sum(p, axis=1, keepdims=True)
    r = s.shape[0] // len(vs)
    pv = jnp.concatenate([_dot(p[g * r:(g + 1) * r].astype(BF16), v) for g, v in enumerate(vs)], axis=0)
    acc_ref[...] = alpha * acc_ref[...] + pv
    m_ref[...] = m_new


def _sample_a_kernel(pt_ref, qc_ref, qi_ref, wi_ref, *refs, n_steps, front, lam_init):
    ck_refs, cv_refs, cik_refs = refs[:PG], refs[PG:2 * PG], refs[2 * PG:3 * PG]
    kn_ref, vn_ref, kin_ref, lam_ref, g_ref, o_ref, keys_ref, m_ref, l_ref, acc_ref = refs[3 * PG:]
    j = pl.program_id(1)
    lane = lax.broadcasted_iota(jnp.int32, (TS, LANES), 1)

    @pl.when(j == 0)
    def _():
        m_ref[...] = jnp.full(m_ref.shape, NEG, F32)
        l_ref[...] = jnp.zeros(l_ref.shape, F32)
        acc_ref[...] = jnp.zeros(acc_ref.shape, F32)

    def qstack(h):
        qh = qc_ref[:, h * LANES:(h + 1) * LANES]
        return jnp.concatenate([jnp.where(lane < DC, qh, jnp.zeros_like(qh)),
                                jnp.where(lane >= DC, qh, jnp.zeros_like(qh))], axis=0)

    def head_rows(h):
        return slice(h * 2 * TS, (h + 1) * 2 * TS)

    qi_all = jnp.concatenate([qi_ref[:, h * LANES:h * LANES + D_IDX] for h in range(H_I)], axis=0)
    w_col = jnp.concatenate([wi_ref[:, h:h + 1] for h in range(H_I)], axis=0)

    def idx_scores(kipt):
        sc = jnp.maximum(_dot(qi_all, kipt), 0.0) * w_col
        return sc[0:TS] + sc[TS:2 * TS] + sc[2 * TS:3 * TS] + sc[3 * TS:4 * TS]

    @pl.when(j < n_steps)
    def _():
        ss, vs = [], []
        for h in range(H_C):
            rows_h = pl.ds(h, PAGE_SIZE, stride=H_C)
            k = jnp.concatenate([r[0, rows_h, :].astype(BF16) for r in ck_refs], axis=0)
            vs.append(jnp.concatenate([r[0, rows_h, :].astype(BF16) for r in cv_refs], axis=0))
            ss.append(_dot_nt(qstack(h), k))
        _online_softmax_update(jnp.concatenate(ss, axis=0), vs, m_ref, l_ref, acc_ref)
        keys_ref[0] = idx_scores(jnp.concatenate([r[0].astype(BF16) for r in cik_refs], axis=1))

    @pl.when(j == n_steps)
    def _():
        kr = lax.broadcasted_iota(jnp.int32, (1, TS), 1)
        q_of_row = lax.broadcasted_iota(jnp.int32, (2 * TS, 1), 0) & (TS - 1)
        ok = (kr >= front) & (kr <= q_of_row)
        ss = [jnp.where(ok, _dot_nt(qstack(h), kn_ref[:, h * LANES:(h + 1) * LANES]), NEG) for h in range(H_C)]
        vs = [vn_ref[:, h * LANES:(h + 1) * LANES] for h in range(H_C)]
        _online_softmax_update(jnp.concatenate(ss, axis=0), vs, m_ref, l_ref, acc_ref)
        krp = lax.broadcasted_iota(jnp.int32, (1, PAGE_SIZE), 1)
        okq = (krp >= front) & (krp < TS) & (krp <= lax.broadcasted_iota(jnp.int32, (TS, 1), 0))
        keys_ref[0] = jnp.concatenate([jnp.where(okq, idx_scores(kin_ref[0]), NINF),
                                       jnp.full((TS, (PG - 1) * PAGE_SIZE), NINF, F32)], axis=1)
        lam = _diff_lambda(lam_ref, lam_init)
        for h in range(H_C):
            a = acc_ref[head_rows(h), :] / l_ref[head_rows(h), :]
            o = a[0:TS] - lam * a[TS:2 * TS]
            o_ref[:, h * LANES:(h + 1) * LANES] = (_rms(o, g_ref[...]) * (1.0 - lam_init)).astype(o_ref.dtype)


def sample_diff_idx(pt, qc, qi, wi, ck, cv, cik, kn, vn, kin, lam_p, g_c, *, b, n_pages, front, lam_init):
    assert n_pages % PG == 0
    n_steps = n_pages // PG
    rows = H_C * 2 * TS
    qspec = lambda w: pl.BlockSpec((TS, w), lambda bb, j, pt: (bb, 0))
    pages4 = [pl.BlockSpec((1, PAGE_SIZE * H_C, 2 * DC), _page_map(g, n_pages, 2)) for g in range(PG)]
    pages_i = [pl.BlockSpec((1, D_IDX, PAGE_SIZE), _page_map(g, n_pages, 2)) for g in range(PG)]
    return pl.pallas_call(
        functools.partial(_sample_a_kernel, n_steps=n_steps, front=front, lam_init=lam_init),
        grid_spec=pltpu.PrefetchScalarGridSpec(
            num_scalar_prefetch=1,
            grid=(b, n_steps + 1),
            in_specs=[qspec(512), qspec(512), qspec(128)] + pages4 + pages4 + pages_i
                     + [qspec(512), qspec(512), pl.BlockSpec((1, D_IDX, PAGE_SIZE), lambda bb, j, pt: (bb, 0, 0)),
                        pl.BlockSpec((4, DC), lambda bb, j, pt: (0, 0)),
                        pl.BlockSpec((1, 2 * DC), lambda bb, j, pt: (0, 0))],
            out_specs=[qspec(512), pl.BlockSpec((1, TS, PG * PAGE_SIZE), lambda bb, j, pt: (bb, 0, j))],
            scratch_shapes=[pltpu.VMEM((rows, 1), F32), pltpu.VMEM((rows, 1), F32), pltpu.VMEM((rows, 2 * DC), F32)]),
        out_shape=[jax.ShapeDtypeStruct((b * TS, 512), BF16),
                   jax.ShapeDtypeStruct((b, TS, (n_steps + 1) * PG * PAGE_SIZE), F32)],
        compiler_params=_cparams(("parallel", "arbitrary")),
        name="sample_diff_idx",
    )(pt, qc, qi, wi, *([ck] * PG), *([cv] * PG), *([cik] * PG), kn, vn, kin, lam_p, g_c.reshape(1, 2 * DC))


def _sample_b_kernel(pt_ref, keys_all_ref, keys_ref, qd_ref, *refs, n_steps, n_sel):
    ck_refs, cv_refs = refs[:PG], refs[PG:2 * PG]
    kn_ref, vn_ref, o_ref, tau_ref, need_ref, before_ref, m_ref, l_ref, acc_ref = refs[2 * PG:]
    j = pl.program_id(1)

    @pl.when(j == 0)
    def _():
        keys = keys_all_ref[0]

        def count_ge(cand):
            return jnp.sum(jnp.where(keys >= cand, 1, 0), axis=1, keepdims=True)

        tau = _kth_threshold(count_ge, (TS, 1), n_sel)
        tau_ref[...] = tau
        need_ref[...] = (n_sel - jnp.sum(jnp.where(keys > tau, 1, 0), axis=1, keepdims=True)).astype(F32)
        before_ref[...] = jnp.zeros(before_ref.shape, F32)
        m_ref[...] = jnp.full(m_ref.shape, NEG, F32)
        l_ref[...] = jnp.zeros(l_ref.shape, F32)
        acc_ref[...] = jnp.zeros(acc_ref.shape, F32)

    strict_upper = jnp.where(lax.broadcasted_iota(jnp.int32, (PAGE_SIZE, PAGE_SIZE), 0)
                             < lax.broadcasted_iota(jnp.int32, (PAGE_SIZE, PAGE_SIZE), 1), 1.0, 0.0).astype(BF16)
    qd_all = jnp.concatenate([qd_ref[:, h * LANES:(h + 1) * LANES] for h in range(H_D)], axis=0)

    def select(key, before):
        tau = tau_ref[...]
        eq = key == tau
        eqf = jnp.where(eq, 1.0, 0.0)
        rank = _dot(eqf.astype(BF16), strict_upper) + before
        sel = ((key > tau) | (eq & (rank < need_ref[...]))) & (key > NINF)
        return sel, before + jnp.sum(eqf, axis=1, keepdims=True)

    def attend(s, sel, v):
        s = jnp.where(jnp.concatenate([sel] * H_D, axis=0), s * (DH_D ** -0.5), NEG)
        _online_softmax_update(s, [v], m_ref, l_ref, acc_ref)

    @pl.when(j < n_steps)
    def _():
        key = keys_ref[0]
        before = before_ref[...]
        sels = []
        for g in range(PG):
            sel, before = select(key[:, g * PAGE_SIZE:(g + 1) * PAGE_SIZE], before)
            sels.append(sel)
        before_ref[...] = before
        k = jnp.concatenate([r[0].astype(BF16) for r in ck_refs], axis=0)
        v = jnp.concatenate([r[0].astype(BF16) for r in cv_refs], axis=0)
        attend(_dot_nt(qd_all, k), jnp.concatenate(sels, axis=1), v)

    @pl.when(j == n_steps)
    def _():
        sel, _ = select(keys_ref[0][:, :PAGE_SIZE], before_ref[...])
        attend(_dot_nt(qd_all, kn_ref[0]), sel, vn_ref[0])
        o = acc_ref[...] / l_ref[...]
        for h in range(H_D):
            o_ref[:, h * LANES:(h + 1) * LANES] = o[h * TS:(h + 1) * TS].astype(o_ref.dtype)


def sample_dsa(pt, keys, qd, ck, cv, kn, vn, *, b, n_pages, n_sel):
    assert n_pages % PG == 0
    n_steps = n_pages // PG
    rows = H_D * TS
    nk = keys.shape[2]
    pages = [pl.BlockSpec((1, PAGE_SIZE, DH_D), _page_map(g, n_pages, 2)) for g in range(PG)]
    nspec = pl.BlockSpec((1, PAGE_SIZE, DH_D), lambda bb, j, pt: (bb, 0, 0))
    return pl.pallas_call(
        functools.partial(_sample_b_kernel, n_steps=n_steps, n_sel=n_sel),
        grid_spec=pltpu.PrefetchScalarGridSpec(
            num_scalar_prefetch=1,
            grid=(b, n_steps + 1),
            in_specs=[pl.BlockSpec((1, TS, nk), lambda bb, j, pt: (bb, 0, 0)),
                      pl.BlockSpec((1, TS, PG * PAGE_SIZE), lambda bb, j, pt: (bb, 0, j)),
                      pl.BlockSpec((TS, 512), lambda bb, j, pt: (bb, 0))] + pages + pages + [nspec, nspec],
            out_specs=pl.BlockSpec((TS, 512), lambda bb, j, pt: (bb, 0)),
            scratch_shapes=[pltpu.VMEM((TS, 1), F32), pltpu.VMEM((TS, 1), F32), pltpu.VMEM((TS, 1), F32),
                            pltpu.VMEM((rows, 1), F32), pltpu.VMEM((rows, 1), F32), pltpu.VMEM((rows, DH_D), F32)]),
        out_shape=jax.ShapeDtypeStruct((b * TS, 512), BF16),
        compiler_params=_cparams(("parallel", "arbitrary")),
        name="sample_dsa",
    )(pt, keys, keys, qd, *([ck] * PG), *([cv] * PG), kn, vn)


REC_CHUNK = 64
REC_SUB = 16
TM_PROJ = 512
TM_REC, TN_REC = 1408, 1408
TM_ATT = 704
TM_ROWS = 384
TM_FFN = 528


def _pad_cols(w, n):
    return jnp.pad(w, ((0, 0), (0, n - w.shape[1])))


def _tile_rows(m, pref):
    return pref if m % pref == 0 else m


def kernel(x_prompt, x_sample, state_hgrn, state_mlstm_C, state_mlstm_n, state_mlstm_m, state_ffn_conv, cache_diff_k, cache_diff_v, cache_dsa_k, cache_dsa_v, cache_idx_k, page_table, meta_tokens, norm_gains, w_in_rec, b_gates_rec, lb_logits, g_norm_hgrn, g_norm_mlstm, w_out_rec, w_in_att, diff_lambda, g_norm_diff, w_out_att, w_ffn_up, ffn_conv_w, ffn_conv_b, w_ffn_down):
    bp, t_in, d = x_prompt.shape
    bs, t_s, _ = x_sample.shape
    depth = norm_gains.shape[0]
    n_pages = page_table.shape[1]
    past_len = n_pages * PAGE_SIZE
    real_p = N_META + t_in
    tp = -(-real_p // QB) * QB
    front_p = tp - real_p
    front_s = TS - t_s
    assert tp % REC_CHUNK == 0 and tp % TM_ROWS == 0 and front_p >= CONV_W - 1 and front_s >= CONV_W - 1
    mp, ms = bp * tp, bs * TS

    meta = jnp.broadcast_to(meta_tokens.astype(x_prompt.dtype)[None], (bp, N_META, d))
    xp = jnp.concatenate([jnp.zeros((bp, front_p, d), x_prompt.dtype), meta, x_prompt], axis=1).reshape(mp, d)
    xs = jnp.concatenate([jnp.zeros((bs, front_s, d), x_sample.dtype), x_sample], axis=1).reshape(ms, d)
    lb_all = jnp.cumsum(jax.nn.softmax(lb_logits.astype(F32), axis=0), axis=0)
    pt_flat = page_table.reshape(-1).astype(jnp.int32)
    sel_p = min(TOPK_MAX, t_in // 4)
    sel_s = min(TOPK_MAX, (past_len + t_s) // 4)
    tmp_s = _tile_rows(ms, TM_PROJ)
    tmr_s = _tile_rows(ms, TM_ROWS)

    pos_p = jnp.arange(tp, dtype=jnp.int32) - front_p
    pos_s = jnp.tile(past_len + jnp.arange(TS, dtype=jnp.int32) - front_s, ms // TS)
    tabs_p = (rope_tables(pos_p, DC, DC // ROT_FRAC // 2), rope_tables(pos_p, DH_D, DH_D // ROT_FRAC // 2))
    tabs_s = (rope_tables(pos_s, DC, DC // ROT_FRAC // 2), rope_tables(pos_s, DH_D, DH_D // ROT_FRAC // 2))

    rec_p, rec_s = [[], [], [], []], [[], [], [], []]
    att_p, att_s = [[], [], [], [], []], [[], [], [], [], []]
    conv_p, conv_s = [], []
    for l in range(depth):
        p = l // 2
        g = norm_gains[l].astype(F32)
        if l % 2 == 0:
            w_in = _pad_cols(w_in_rec[p], REC_N).astype(BF16)
            w_out = w_out_rec[p].astype(BF16)
            prm = (lb_all[p], b_gates_rec[p].astype(F32), g_norm_hgrn[p].astype(F32), g_norm_mlstm[p].astype(F32))
            proj = norm_matmul(xp, g[0], w_in, tm=_tile_rows(mp, TM_REC), tn=TN_REC).reshape(bp, tp, REC_N)
            zs = jnp.zeros((bp, 4, 128, 128), F32)
            y, *st = rec_mixer(proj, *prm, zs, zs, jnp.zeros((bp, 4, 128), F32), jnp.zeros((bp, 4), F32),
                               C=REC_CHUNK, W=REC_SUB, front=front_p)
            xp = matmul_norm_res([y.reshape(mp, -1)], [w_out], g[1], xp, tm=TM_ROWS, t_pad=tp, front=front_p)
            for j in range(4):
                rec_p[j].append(st[j])
            proj = norm_matmul(xs, g[0], w_in, tm=tmp_s, tn=384).reshape(bs, TS, REC_N)
            y, *st = rec_mixer(proj, *prm, state_hgrn[p].astype(F32), state_mlstm_C[p].astype(F32),
                               state_mlstm_n[p].astype(F32), state_mlstm_m[p].astype(F32), C=TS, W=TS, front=front_s)
            xs = matmul_norm_res([y.reshape(ms, -1)], [w_out], g[1], xs, tm=tmr_s, t_pad=TS, front=front_s)
            for j in range(4):
                rec_s[j].append(st[j])
        else:
            lam_init = 0.8 - 0.6 * math.exp(-0.3 * l)
            w_in = _pad_cols(w_in_att[p], ATT_N).astype(BF16)
            w_out = w_out_att[p].astype(BF16)
            dl, gc = diff_lambda[p].astype(F32), g_norm_diff[p].astype(F32)
            proj = norm_matmul(xp, g[0], w_in, tm=_tile_rows(mp, TM_ATT), tn=ATT_N)
            (qc, kc, kcb, vc, vcb, qd, kd, kdb, vd, vdb, qi, ki, kib, wi, vct, vdt) = att_prep(proj, *tabs_p, tm=KB)
            oc = diff_prompt(qc, kcb, vct, dl, gc, b=bp, t_pad=tp, front=front_p, lam_init=lam_init)
            od = dsa_prompt(qi, wi, qd, kib, kdb, vdt, b=bp, t_pad=tp, front=front_p, n_sel=sel_p)
            xp = matmul_norm_res([oc, od], [w_out[:512], w_out[512:]], g[1], xp, tm=TM_ROWS, t_pad=tp, front=front_p)
            for j, (a, shp) in enumerate([(kc, (H_C, 2 * DC)), (vc, (H_C, 2 * DC)), (kd, (DH_D,)), (vd, (DH_D,)), (ki, (D_IDX,))]):
                att_p[j].append(a.reshape((bp, tp) + shp)[:, front_p:])
            proj = norm_matmul(xs, g[0], w_in, tm=tmp_s, tn=384)
            (qc, kc, kcb, vc, vcb, qd, kd, kdb, vd, vdb, qi, ki, kib, wi, _, _) = att_prep(proj, *tabs_s, tm=ms)
            as_page = lambda a: jnp.pad(a.reshape(bs, TS, -1), ((0, 0), (0, PAGE_SIZE - TS), (0, 0)))
            rows_kh = lambda c: c.reshape(c.shape[0], PAGE_SIZE * H_C, 2 * DC)
            oc, keys = sample_diff_idx(pt_flat, qc, qi, wi, rows_kh(cache_diff_k[p]), rows_kh(cache_diff_v[p]),
                                       jnp.swapaxes(cache_idx_k[p], 1, 2), kcb, vcb, jnp.swapaxes(as_page(kib), 1, 2), dl, gc,
                                       b=bs, n_pages=n_pages, front=front_s, lam_init=lam_init)
            od = sample_dsa(pt_flat, keys, qd, cache_dsa_k[p], cache_dsa_v[p], as_page(kdb), as_page(vdb),
                            b=bs, n_pages=n_pages, n_sel=sel_s)
            xs = matmul_norm_res([oc, od], [w_out[:512], w_out[512:]], g[1], xs, tm=tmr_s, t_pad=TS, front=front_s)
            for j, (a, shp) in enumerate([(kc, (H_C, 2 * DC)), (vc, (H_C, 2 * DC)), (kd, (DH_D,)), (vd, (DH_D,)), (ki, (D_IDX,))]):
                att_s[j].append(a.reshape((bs, TS) + shp)[:, front_s:])
        w_up, w_down = w_ffn_up[l].astype(BF16), w_ffn_down[l].astype(BF16)
        cw, cb = ffn_conv_w[l].astype(F32), ffn_conv_b[l].astype(F32)
        xp, tail = ffn_fused(xp, g[2], w_up, cw, cb, w_down, g[3], tm=TM_FFN, t_pad=tp, front=front_p)
        conv_p.append(tail[:, SUB - (CONV_W - 1):])
        up = norm_matmul(xs, g[2], w_up, tm=tmp_s, tn=512)
        conv_s.append(up.reshape(bs, TS, 2 * D_FF)[:, TS - (CONV_W - 1):, :D_FF])
        buf = jnp.pad(state_ffn_conv[l].astype(F32), ((0, 0), (front_s - (CONV_W - 1), TS - front_s), (0, 0)))
        xs = ffn_down(up, buf.reshape(ms, D_FF), cw, cb, w_down, g[3], xs, tm=tmr_s, t_pad=TS, front=front_s)

    y_p = xp.reshape(bp, tp, d)[:, front_p + N_META:]
    y_s = xs.reshape(bs, TS, d)[:, front_s:]
    stack = lambda xs: xs[0][None] if len(xs) == 1 else jnp.stack(xs)
    return (y_p, y_s,
            stack(rec_p[0]), stack(rec_s[0]), stack(rec_p[1]), stack(rec_s[1]),
            stack(rec_p[2]), stack(rec_s[2]), stack(rec_p[3]), stack(rec_s[3]),
            stack(conv_p), stack(conv_s),
            stack(att_p[0]), stack(att_s[0]), stack(att_p[1]), stack(att_s[1]),
            stack(att_p[2]), stack(att_s[2]), stack(att_p[3]), stack(att_s[3]),
            stack(att_p[4]), stack(att_s[4]))
```

```python
import functools
import math

import jax
import jax.numpy as jnp
import numpy as np
from jax import lax
from jax.experimental import pallas as pl
from jax.experimental.pallas import tpu as pltpu

F32 = jnp.float32
BF16 = jnp.bfloat16

D_MODEL = 1024
N_META = 16
H_A, DK_A, DV_A = 4, 128, 128
H_B, DK_B, DV_B = 4, 128, 128
H_C, DC = 4, 64
H_D, DH_D = 4, 128
H_I, D_IDX = 4, 64
TOPK_MAX = 256
D_FF = 2816
CONV_W = 3
ROPE_THETA = 500000.0
ROT_FRAC = 4
EPS = 1e-6
PAGE_SIZE = 128
LANES = 128
NEG = -1e30

REC_N = 8 * 512 + LANES
ATT_N = 2560 + LANES
INT_MIN = -2 ** 31
VMEM_LIMIT = 56 * 1024 * 1024


def _cparams(sem):
    return pltpu.CompilerParams(dimension_semantics=sem, vmem_limit_bytes=VMEM_LIMIT)


def _rms(x, g):
    return x * lax.rsqrt(jnp.mean(x * x, axis=-1, keepdims=True) + EPS) * g


def _dot(a, b):
    return jnp.dot(a, b, preferred_element_type=F32)


def _dot_nt(a, b):
    return lax.dot_general(a, b, (((1,), (1,)), ((), ())), preferred_element_type=F32)


def _dot_tn(a, b):
    return lax.dot_general(a, b, (((0,), (0,)), ((), ())), preferred_element_type=F32)


def _dot_exact_lhs(tri, x):
    hi = x.astype(BF16)
    r1 = x - hi.astype(F32)
    mid = r1.astype(BF16)
    lo = (r1 - mid.astype(F32)).astype(BF16)
    return _dot(tri, hi) + _dot(tri, mid) + _dot(tri, lo)


def _row_valid(i, tm, t_pad, front):
    r = lax.broadcasted_iota(jnp.int32, (tm, 1), 0)
    if t_pad % tm == 0:
        t = (i % (t_pad // tm)) * tm + r
    else:
        assert tm % t_pad == 0 and (t_pad & (t_pad - 1)) == 0
        t = r & (t_pad - 1)
    return t >= front


def _norm_matmul_kernel(x_ref, g_ref, w_ref, o_ref, h_ref):
    @pl.when(pl.program_id(1) == 0)
    def _():
        h_ref[...] = _rms(x_ref[...], g_ref[...]).astype(BF16)

    o_ref[...] = _dot(h_ref[...], w_ref[...])


def norm_matmul(x, g, w, *, tm, tn):
    m, d = x.shape
    n = w.shape[1]
    assert m % tm == 0 and n % tn == 0
    return pl.pallas_call(
        _norm_matmul_kernel,
        grid=(m // tm, n // tn),
        in_specs=[pl.BlockSpec((tm, d), lambda i, j: (i, 0)),
                  pl.BlockSpec((1, d), lambda i, j: (0, 0)),
                  pl.BlockSpec((d, tn), lambda i, j: (0, j))],
        out_specs=pl.BlockSpec((tm, tn), lambda i, j: (i, j)),
        out_shape=jax.ShapeDtypeStruct((m, n), F32),
        scratch_shapes=[pltpu.VMEM((tm, d), BF16)],
        compiler_params=_cparams(("parallel", "arbitrary")),
        name="norm_matmul",
    )(x, g.reshape(1, d), w)


def _matmul_norm_res_kernel(*refs, n_in, tm, t_pad, front):
    a_refs, w_refs = refs[:n_in], refs[n_in:2 * n_in]
    g_ref, x_ref, o_ref = refs[2 * n_in:]
    acc = _dot(a_refs[0][...], w_refs[0][...])
    for a, w in zip(a_refs[1:], w_refs[1:]):
        acc = acc + _dot(a[...], w[...])
    out = x_ref[...] + _rms(acc, g_ref[...])
    o_ref[...] = jnp.where(_row_valid(pl.program_id(0), tm, t_pad, front), out, 0.0)


def matmul_norm_res(a_list, w_list, g, x, *, tm, t_pad, front):
    m, d = x.shape
    assert m % tm == 0
    n_in = len(a_list)
    in_specs = ([pl.BlockSpec((tm, a.shape[1]), lambda i: (i, 0)) for a in a_list]
                + [pl.BlockSpec(w.shape, lambda i: (0, 0)) for w in w_list]
                + [pl.BlockSpec((1, d), lambda i: (0, 0)), pl.BlockSpec((tm, d), lambda i: (i, 0))])
    return pl.pallas_call(
        functools.partial(_matmul_norm_res_kernel, n_in=n_in, tm=tm, t_pad=t_pad, front=front),
        grid=(m // tm,),
        in_specs=in_specs,
        out_specs=pl.BlockSpec((tm, d), lambda i: (i, 0)),
        out_shape=jax.ShapeDtypeStruct((m, d), F32),
        compiler_params=_cparams(("parallel",)),
        name="matmul_norm_res",
    )(*a_list, *w_list, g.reshape(1, d), x)


def _ffn_down_kernel(ug_ref, uv_ref, buf_ref, cw_ref, cb_ref, w_ref, g_ref, x_ref, o_ref, *, tm, t_pad, front):
    t = lax.broadcasted_iota(jnp.int32, (tm, 1), 0) & (t_pad - 1)
    ug = jnp.where((t >= front - (CONV_W - 1)) & (t < front), buf_ref[...], ug_ref[...])
    conv = (cb_ref[...] + cw_ref[0:1, :] * pltpu.roll(ug, 2, axis=0) + cw_ref[1:2, :] * pltpu.roll(ug, 1, axis=0)
            + cw_ref[2:3, :] * ug)
    act = (conv * jax.nn.sigmoid(conv) * uv_ref[...]).astype(BF16)
    out = x_ref[...] + _rms(_dot(act, w_ref[...]), g_ref[...])
    o_ref[...] = jnp.where(t >= front, out, 0.0)


def ffn_down(up, buf, cw, cb, w_down, g, x, *, tm, t_pad, front):
    m, d = x.shape
    f = w_down.shape[0]
    assert m % tm == 0 and tm % t_pad == 0 and (t_pad & (t_pad - 1)) == 0 and front >= CONV_W - 1
    assert up.shape == (m, 2 * f) and buf.shape == (m, f)
    return pl.pallas_call(
        functools.partial(_ffn_down_kernel, tm=tm, t_pad=t_pad, front=front),
        grid=(m // tm,),
        in_specs=[pl.BlockSpec((tm, f), lambda i: (i, 0)),
                  pl.BlockSpec((tm, f), lambda i: (i, 1)),
                  pl.BlockSpec((tm, f), lambda i: (i, 0)),
                  pl.BlockSpec((CONV_W, f), lambda i: (0, 0)),
                  pl.BlockSpec((1, f), lambda i: (0, 0)),
                  pl.BlockSpec((f, d), lambda i: (0, 0)),
                  pl.BlockSpec((1, d), lambda i: (0, 0)),
                  pl.BlockSpec((tm, d), lambda i: (i, 0))],
        out_specs=pl.BlockSpec((tm, d), lambda i: (i, 0)),
        out_shape=jax.ShapeDtypeStruct((m, d), F32),
        compiler_params=_cparams(("parallel",)),
        name="ffn_down",
    )(up, up, buf, cw, cb.reshape(1, f), w_down, g.reshape(1, d), x)


FF_CW = 256
FF_HALO = 16


def _ffn_fused_kernel(x_ref, halo_ref, gin_ref, wup_ref, cw_ref, cb_ref, wdn_ref, gout_ref, o_ref, conv_ref, acc_ref,
                      *, tm, t_pad, front):
    i = pl.program_id(0)
    x = x_ref[...]
    h = jnp.concatenate([_rms(halo_ref[...], gin_ref[...]), _rms(x, gin_ref[...])], axis=0).astype(BF16)
    for c in range(D_FF // FF_CW):
        cs = slice(c * FF_CW, (c + 1) * FF_CW)
        ug = _dot(h, wup_ref[:, cs])
        uv = _dot(h[FF_HALO:], wup_ref[:, D_FF + c * FF_CW:D_FF + (c + 1) * FF_CW])
        prev1 = pltpu.roll(ug, 1, axis=0)[FF_HALO:]
        prev2 = pltpu.roll(ug, 2, axis=0)[FF_HALO:]
        conv = cb_ref[:, cs] + cw_ref[0:1, cs] * prev2 + cw_ref[1:2, cs] * prev1 + cw_ref[2:3, cs] * ug[FF_HALO:]
        act = (conv * jax.nn.sigmoid(conv) * uv).astype(BF16)
        part = _dot(act, wdn_ref[cs, :])
        if c == 0:
            acc_ref[...] = part
        else:
            acc_ref[...] += part
        conv_ref[0, :, cs] = ug[FF_HALO + tm - SUB:, :]
    out = x + _rms(acc_ref[...], gout_ref[...])
    o_ref[...] = jnp.where(_row_valid(i, tm, t_pad, front), out, 0.0)


def ffn_fused(x, g_in, w_up, cw, cb, w_down, g_out, *, tm, t_pad, front):
    m, d = x.shape
    nc = D_FF // FF_CW
    assert m % tm == 0 and t_pad % tm == 0 and tm % FF_HALO == 0 and D_FF % FF_CW == 0
    per_seq = t_pad // tm
    hb = tm // FF_HALO
    const = lambda shape: pl.BlockSpec(shape, lambda i: (0,) * len(shape), pipeline_mode=pl.Buffered(1))
    return pl.pallas_call(
        functools.partial(_ffn_fused_kernel, tm=tm, t_pad=t_pad, front=front),
        grid=(m // tm,),
        in_specs=[pl.BlockSpec((tm, d), lambda i: (i, 0)),
                  pl.BlockSpec((FF_HALO, d), lambda i: (jnp.maximum(i * hb - 1, 0), 0)),
                  const((1, d)), const((d, 2 * D_FF)), const((CONV_W, D_FF)), const((1, D_FF)),
                  const((D_FF, d)), const((1, d))],
        out_specs=[pl.BlockSpec((tm, d), lambda i: (i, 0)),
                   pl.BlockSpec((1, SUB, D_FF), lambda i: (i // per_seq, 0, 0))],
        out_shape=[jax.ShapeDtypeStruct((m, d), F32), jax.ShapeDtypeStruct((m // t_pad, SUB, D_FF), F32)],
        scratch_shapes=[pltpu.VMEM((tm, d), F32)],
        compiler_params=_cparams(("arbitrary",)),
        name="ffn_fused",
    )(x, x, g_in.reshape(1, d), w_up, cw, cb.reshape(1, D_FF), w_down, g_out.reshape(1, d))


def _log_sigmoid(x):
    return jnp.minimum(x, 0.0) - jnp.log1p(jnp.exp(-jnp.abs(x)))


def _rec_kernel(proj_ref, lb_ref, bg_ref, ga_ref, gb_ref, s0_ref, c0_ref, n0_ref, m0_ref,
                y_ref, s_ref, c_ref, n_ref, m_ref, st_ref, *, C, W, front):
    ci = pl.program_id(1)
    nci = pl.num_programs(1)

    @pl.when(ci == 0)
    def _():
        for h in range(H_A):
            st_ref[h] = s0_ref[0, h].T
        c_ref[...] = c0_ref[...]
        n_ref[...] = n0_ref[...]
        m_ref[...] = m0_ref[...]

    row = ci * C + lax.broadcasted_iota(jnp.int32, (C, 1), 0)
    valid = row >= front
    r_i = lax.broadcasted_iota(jnp.int32, (C, C), 0)
    c_i = lax.broadcasted_iota(jnp.int32, (C, C), 1)
    causal = r_i >= c_i
    tri = jnp.where(causal, 1.0, 0.0).astype(BF16)

    gates = proj_ref[0, :, 8 * 512:8 * 512 + LANES] + bg_ref[...]
    lf_all = jnp.where(valid, _log_sigmoid(gates), 0.0)
    ig_all = jnp.where(valid, gates, NEG)
    b_all = _dot_exact_lhs(tri, lf_all)
    b_all_t = b_all.T
    ig_all_t = ig_all.T

    qk_dots, qc_dots = [], []
    for h in range(H_B):
        qb = proj_ref[0, :, 2048 + h * 128:2048 + (h + 1) * 128].astype(BF16)
        kb = (proj_ref[0, :, 2560 + h * 128:2560 + (h + 1) * 128] * (DK_B ** -0.5)).astype(BF16)
        qk_dots.append(_dot_nt(qb, kb))
        qc_dots.append(_dot(qb, c_ref[0, h].astype(BF16)))

    for h in range(H_B):
        q = proj_ref[0, :, 2048 + h * 128:2048 + (h + 1) * 128]
        k = proj_ref[0, :, 2560 + h * 128:2560 + (h + 1) * 128] * (DK_B ** -0.5)
        v = proj_ref[0, :, 3072 + h * 128:3072 + (h + 1) * 128]
        og = proj_ref[0, :, 3584 + h * 128:3584 + (h + 1) * 128]
        vb = v.astype(BF16)
        b_col = b_all[:, H_B + h:H_B + h + 1]
        b_row = b_all_t[H_B + h:H_B + h + 1, :]
        i_col = ig_all[:, h:h + 1]
        i_row = ig_all_t[h:h + 1, :]
        m_prev = m_ref[0, h:h + 1, 0:1]
        dmat = jnp.where(causal, b_col - b_row + i_row, NEG)
        inter = b_col + m_prev
        mt = jnp.maximum(inter, jnp.max(dmat, axis=1, keepdims=True))
        w = jnp.exp(dmat - mt) * qk_dots[h]
        wi = jnp.exp(inter - mt)
        c_st = c_ref[0, h]
        n_st = n_ref[0, h:h + 1, :]
        num = wi * qc_dots[h] + _dot(w.astype(BF16), vb)
        den = wi * jnp.sum(q * n_st, axis=1, keepdims=True) + jnp.sum(w, axis=1, keepdims=True)
        hc = num / jnp.maximum(jnp.abs(den), jnp.exp(-mt))
        m_new = mt[C - 1:C, :]
        b_last = b_col[C - 1:C, :]
        decay = jnp.exp(b_last + m_prev - m_new)
        kw = k * jnp.exp(b_last - b_col + i_col - m_new)
        c_ref[0, h] = decay * c_st + _dot_tn(kw.astype(BF16), vb)
        n_ref[0, h:h + 1, :] = decay * n_st + jnp.sum(kw, axis=0, keepdims=True)
        m_ref[0, h:h + 1, :] = jnp.broadcast_to(m_new, (1, LANES))
        yb = _rms(hc, gb_ref[...]) * jax.nn.sigmoid(og)
        y_ref[0, :, 512 + h * 128:512 + (h + 1) * 128] = jnp.where(valid, yb, 0.0).astype(y_ref.dtype)

    nj = C // W
    rw = lax.broadcasted_iota(jnp.int32, (W, 1), 0)
    ones_sq = jnp.ones((DK_A, LANES), BF16)
    lbv = lb_ref[...]
    f = lbv + (1.0 - lbv) * jax.nn.sigmoid(proj_ref[0, :, 512:1024])
    logf = jnp.where(valid, jnp.log(f), 0.0)
    kk = jnp.where(valid, 1.0 - f, 0.0)
    qq = proj_ref[0, :, 0:512] * (DK_A ** -0.5)
    vv = proj_ref[0, :, 1024:1536]
    bfull = _dot_exact_lhs(tri, logf)
    bcs = []
    for j in range(nj):
        rows = slice(j * W, (j + 1) * W)
        bcs.append(bfull[rows] if j == 0 else bfull[rows] - bfull[j * W - 1:j * W])

    heads = [slice(h * 128, (h + 1) * 128) for h in range(H_A)]
    row_sums, outer, decay_last = {}, {}, []
    for j in range(nj):
        rows = slice(j * W, (j + 1) * W)
        bc = bcs[j]
        last = bc[W - 1:W, :]
        decay_last.append(jnp.exp(last))
        kdec = (kk[rows] * jnp.exp(last - bc)).astype(BF16)
        gs = []
        for s in range(W):
            e = jnp.exp(jnp.where(rw >= s, bc - bc[s:s + 1, :], NEG))
            gs.append(qq[rows] * kk[j * W + s:j * W + s + 1, :] * e)
        g = jnp.concatenate(gs, axis=0).astype(BF16)
        for h, cs in enumerate(heads):
            row_sums[h, j] = _dot(g[:, cs], ones_sq)
            outer[h, j] = _dot_tn(vv[rows, cs].astype(BF16), kdec[:, cs])

    from_state = {}
    for h, cs in enumerate(heads):
        st = st_ref[h]
        for j in range(nj):
            rows = slice(j * W, (j + 1) * W)
            from_state[h, j] = _dot_nt((qq[rows, cs] * jnp.exp(bcs[j][:, cs])).astype(BF16), st.astype(BF16))
            st = decay_last[j][:, cs] * st + outer[h, j]
        st_ref[h] = st

    for h, cs in enumerate(heads):
        for j in range(nj):
            rows = slice(j * W, (j + 1) * W)
            o = from_state[h, j]
            for s in range(W):
                o = o + row_sums[h, j][s * W:(s + 1) * W] * vv[j * W + s:j * W + s + 1, cs]
            ga = proj_ref[0, rows, 1536 + h * 128:1536 + (h + 1) * 128]
            ya = _rms(o, ga_ref[...]) * (ga * jax.nn.sigmoid(ga))
            vld = (ci * C + j * W + rw) >= front
            y_ref[0, rows, cs] = jnp.where(vld, ya, 0.0).astype(y_ref.dtype)

    @pl.when(ci == nci - 1)
    def _():
        for h in range(H_A):
            s_ref[0, h] = st_ref[h].T


def rec_mixer(proj, lb, bg, g_a, g_b, s0, c0, n0, m0, *, C, W, front):
    b, t, _ = proj.shape
    assert t % C == 0 and C % W == 0
    m0b = jnp.broadcast_to(m0[:, :, None], (b, H_B, LANES))
    bgp = jnp.zeros((1, LANES), F32).at[0, :2 * H_B].set(bg.reshape(-1))
    st_spec = pl.BlockSpec((1, 4, 128, 128), lambda i, c: (i, 0, 0, 0))
    v_spec = pl.BlockSpec((1, 4, LANES), lambda i, c: (i, 0, 0))
    row_spec = lambda n: pl.BlockSpec((1, n), lambda i, c: (0, 0))
    y, s, cc, n, m = pl.pallas_call(
        functools.partial(_rec_kernel, C=C, W=W, front=front),
        grid=(b, t // C),
        in_specs=[pl.BlockSpec((1, C, REC_N), lambda i, c: (i, c, 0)),
                  row_spec(512), row_spec(LANES), row_spec(128), row_spec(128),
                  st_spec, st_spec, v_spec, v_spec],
        out_specs=[pl.BlockSpec((1, C, 1024), lambda i, c: (i, c, 0)), st_spec, st_spec, v_spec, v_spec],
        out_shape=[jax.ShapeDtypeStruct((b, t, 1024), BF16),
                   jax.ShapeDtypeStruct((b, 4, 128, 128), F32),
                   jax.ShapeDtypeStruct((b, 4, 128, 128), F32),
                   jax.ShapeDtypeStruct((b, 4, LANES), F32),
                   jax.ShapeDtypeStruct((b, 4, LANES), F32)],
        scratch_shapes=[pltpu.VMEM((4, 128, 128), F32)],
        compiler_params=_cparams(("parallel", "arbitrary")),
        name="rec_mixer",
    )(proj, lb.reshape(1, 512), bgp, g_a.reshape(1, 128), g_b.reshape(1, 128), s0, c0, n0, m0b)
    return y, s, cc, n, m[:, :, 0]


def rope_tables(pos, period, half):
    r = 2 * half
    inv = ROPE_THETA ** (-jnp.arange(half, dtype=F32) * 2.0 / r)
    ang = pos.astype(F32)[:, None] * inv[None, :]
    cos, sin = jnp.cos(ang), jnp.sin(ang)
    lane = np.arange(LANES) % period
    idx = np.where(lane < half, lane, np.where(lane < r, lane - half, 0))
    first, second = jnp.asarray(lane < half), jnp.asarray((lane >= half) & (lane < r))
    c = jnp.where(first | second, cos[:, idx], 1.0)
    sa = jnp.where(first, -sin[:, idx], 0.0)
    sb = jnp.where(second, sin[:, idx], 0.0)
    return c, sa, sb


def _att_prep_kernel(p_ref, c64, a64, b64, c128, a128, b128,
                     qc_o, kc_o, kcb_o, vc_o, vcb_o, qd_o, kd_o, kdb_o, vd_o, vdb_o, qi_o, ki_o, kib_o, wi_o,
                     vct_o, vdt_o):
    def rot(x, c, sa, sb, half):
        return x * c[...] + pltpu.roll(x, LANES - half, axis=1) * sa[...] + pltpu.roll(x, half, axis=1) * sb[...]

    tm = p_ref.shape[0]
    h64 = D_IDX // ROT_FRAC // 2
    h128 = DH_D // ROT_FRAC // 2
    for t in range(4):
        sl = slice(t * LANES, (t + 1) * LANES)
        qc_o[:, sl] = (rot(p_ref[:, sl], c64, a64, b64, h64) * (DC ** -0.5)).astype(BF16)
        kc = rot(p_ref[:, 512 + t * LANES:512 + (t + 1) * LANES], c64, a64, b64, h64)
        kc_o[pl.ds(t, tm, stride=H_C), :] = kc
        kcb_o[:, sl] = kc.astype(BF16)
        vc = p_ref[:, 1024 + t * LANES:1024 + (t + 1) * LANES]
        vc_o[pl.ds(t, tm, stride=H_C), :] = vc
        vcb_o[:, sl] = vc.astype(BF16)
        vct_o[0, sl, :] = vc.T.astype(BF16)
        qd_o[:, sl] = rot(p_ref[:, 1536 + t * LANES:1536 + (t + 1) * LANES], c128, a128, b128, h128).astype(BF16)
    kd = rot(p_ref[:, 2048:2176], c128, a128, b128, h128)
    kd_o[...] = kd
    kdb_o[...] = kd.astype(BF16)
    vd = p_ref[:, 2176:2304]
    vd_o[...] = vd
    vdb_o[...] = vd.astype(BF16)
    vdt_o[0] = vd.T.astype(BF16)
    for t in range(2):
        qi = rot(p_ref[:, 2304 + t * LANES:2304 + (t + 1) * LANES], c64, a64, b64, h64) * (D_IDX ** -0.5)
        qi_o[:, (2 * t) * LANES:(2 * t + 1) * LANES] = qi.astype(BF16)
        qi_o[:, (2 * t + 1) * LANES:(2 * t + 2) * LANES] = pltpu.roll(qi, D_IDX, axis=1).astype(BF16)
    last = p_ref[:, 2560:2688]
    ki = rot(last, c64, a64, b64, h64)[:, :D_IDX]
    ki_o[...] = ki
    kib_o[...] = ki.astype(BF16)
    wi_o[...] = pltpu.roll(last, D_IDX, axis=1) * (H_I ** -0.5)


def att_prep(proj, tabs64, tabs128, *, tm):
    m = proj.shape[0]
    p = tabs64[0].shape[0]
    assert m % tm == 0 and p % tm == 0
    nper = p // tm
    tab_spec = pl.BlockSpec((tm, LANES), lambda i: (i % nper, 0))
    outs = [(512, BF16, 1), (2 * DC, F32, H_C), (512, BF16, 1), (2 * DC, F32, H_C), (512, BF16, 1), (512, BF16, 1),
            (128, F32, 1), (128, BF16, 1), (128, F32, 1), (128, BF16, 1), (512, BF16, 1), (D_IDX, F32, 1),
            (D_IDX, BF16, 1), (128, F32, 1)]
    outs_t = [512, 128]
    return pl.pallas_call(
        _att_prep_kernel,
        grid=(m // tm,),
        in_specs=[pl.BlockSpec((tm, ATT_N), lambda i: (i, 0))] + [tab_spec] * 6,
        out_specs=([pl.BlockSpec((tm * r, w), lambda i: (i, 0)) for w, _, r in outs]
                   + [pl.BlockSpec((1, w, tm), lambda i: (i, 0, 0)) for w in outs_t]),
        out_shape=([jax.ShapeDtypeStruct((m * r, w), dt) for w, dt, r in outs]
                   + [jax.ShapeDtypeStruct((m // tm, w, tm), BF16) for w in outs_t]),
        compiler_params=_cparams(("parallel",)),
        name="att_prep",
    )(proj, *tabs64, *tabs128)


QB = 128


def _diff_lambda(lam_ref, lam_init):
    dl = lam_ref[...]
    s1 = jnp.sum(dl[0:1, :] * dl[1:2, :], axis=1, keepdims=True)
    s2 = jnp.sum(dl[2:3, :] * dl[3:4, :], axis=1, keepdims=True)
    return jnp.exp(s1) - jnp.exp(s2) + lam_init


KB = 384
SUB = 8


def _group_max(x):
    return jnp.max(x.reshape(x.shape[0] // SUB, SUB, x.shape[1]), axis=0)


def _group_sum(x):
    return jnp.sum(x.reshape(x.shape[0] // SUB, SUB, x.shape[1]), axis=0)


def _key_visible(i, off, n_rep, front):
    krow = lax.broadcasted_iota(jnp.int32, (KB, 1), 0)
    lane = lax.broadcasted_iota(jnp.int32, (1, n_rep * KB), 1)
    q = lane
    for r in range(1, n_rep):
        q = jnp.where(lane >= r * KB, lane - r * KB, q)
    return ((i * KB + q - krow) >= off) & (krow >= front - off)


def _fori_pairs(lo, hi, one, two, init):
    n = jnp.maximum(hi - lo, 0)
    odd = n % 2
    carry = lax.fori_loop(0, odd, lambda t, c: one(lo, c), init)
    return lax.fori_loop(0, n // 2, lambda t, c: two(lo + odd + 2 * t, c), carry)


def _edge_then_middle(i, edge_body, middle_body, init):
    carry = lax.fori_loop(0, jnp.minimum(i, 1) + 1, lambda t, c: edge_body(t * i, c), init)
    return _fori_pairs(1, i, middle_body, lambda kb, c: middle_body(kb + 1, middle_body(kb, c)), carry)


def _diff_prompt_kernel(q_ref, k_ref, vt_ref, lam_ref, g_ref, o_ref, s_ref, *, front, lam_init):
    i = pl.program_id(1)
    lam = _diff_lambda(lam_ref, lam_init)
    lane = lax.broadcasted_iota(jnp.int32, (KB, LANES), 1)
    for h in range(H_C):
        cs = slice(h * LANES, (h + 1) * LANES)
        qh = q_ref[:, cs]
        qstack = jnp.concatenate([jnp.where(lane < DC, qh, jnp.zeros_like(qh)),
                                  jnp.where(lane >= DC, qh, jnp.zeros_like(qh))], axis=0)

        def scores(kb, masked):
            off = pl.multiple_of(kb * KB, KB)
            st = _dot_nt(k_ref[pl.ds(off, KB), cs], qstack)
            return jnp.where(_key_visible(i, off, 2, front), st, NEG) if masked else st

        def pass_a(kb, mx, masked):
            st = scores(kb, masked)
            s_ref[kb] = st
            return jnp.maximum(mx, _group_max(st))

        mx = _edge_then_middle(i, functools.partial(pass_a, masked=True), functools.partial(pass_a, masked=False),
                               jnp.full((SUB, 2 * KB), NEG, F32))
        m = jnp.max(mx, axis=0, keepdims=True)

        def pass_b(kb, carry):
            l8, acc = carry
            p = jnp.exp(s_ref[kb] - m)
            return l8 + _group_sum(p), acc + _dot(vt_ref[kb, cs, :], p.astype(BF16))

        def pass_b2(kb, carry):
            l8, acc = carry
            p = jnp.exp(jnp.concatenate([s_ref[kb], s_ref[kb + 1]], axis=0) - m)
            vt2 = jnp.concatenate([vt_ref[kb, cs, :], vt_ref[kb + 1, cs, :]], axis=1)
            return l8 + _group_sum(p), acc + _dot(vt2, p.astype(BF16))

        l8, acc = _fori_pairs(0, i + 1, pass_b, pass_b2,
                              (jnp.zeros((SUB, 2 * KB), F32), jnp.zeros((LANES, 2 * KB), F32)))
        a = acc / jnp.sum(l8, axis=0, keepdims=True)
        ot = a[:, :KB] - lam * a[:, KB:]
        ot = ot * lax.rsqrt(jnp.mean(ot * ot, axis=0, keepdims=True) + EPS) * g_ref[...] * (1.0 - lam_init)
        o_ref[:, cs] = ot.T.astype(o_ref.dtype)


def diff_prompt(q, k, vt, lam_p, g_c, *, b, t_pad, front, lam_init):
    nkb = t_pad // KB
    return pl.pallas_call(
        functools.partial(_diff_prompt_kernel, front=front, lam_init=lam_init),
        grid=(b, nkb),
        in_specs=[pl.BlockSpec((KB, 512), lambda bb, i: (bb * nkb + i, 0)),
                  pl.BlockSpec((t_pad, 512), lambda bb, i: (bb, 0)),
                  pl.BlockSpec((nkb, 512, KB), lambda bb, i: (bb, 0, 0)),
                  pl.BlockSpec((4, DC), lambda bb, i: (0, 0)),
                  pl.BlockSpec((2 * DC, 1), lambda bb, i: (0, 0))],
        out_specs=pl.BlockSpec((KB, 512), lambda bb, i: (bb * nkb + i, 0)),
        out_shape=jax.ShapeDtypeStruct((b * t_pad, 512), BF16),
        scratch_shapes=[pltpu.VMEM((nkb, KB, 2 * KB), F32)],
        compiler_params=_cparams(("parallel", "arbitrary")),
        name="diff_prompt",
    )(q, k, vt, lam_p, g_c.reshape(2 * DC, 1))


NINF = float("-inf")


def _kth_threshold(count_ge, shape, n_sel):
    zero_i = jnp.zeros(shape, jnp.int32)
    neg = jnp.where(count_ge(jnp.zeros(shape, F32)) < n_sel, 1, 0)
    sign = jnp.where(neg == 1, jnp.int32(INT_MIN), 0)

    def bit_body(t, mag):
        cand = mag | lax.shift_left(jnp.int32(1), 30 - t)
        enough = jnp.where(count_ge(pltpu.bitcast(cand | sign, F32)) >= n_sel, 1, 0)
        return jnp.where(enough + neg == 1, cand, mag)

    mag = lax.fori_loop(0, 31, bit_body, zero_i)
    tau = pltpu.bitcast(jnp.where(neg == 1, (mag + 1) | sign, mag), F32)
    ninf = jnp.full(shape, NINF, F32)
    return jnp.where(count_ge(ninf) >= n_sel, tau, ninf)


def _dsa_prompt_kernel(qi_ref, wi_ref, qd_ref, ki_ref, kd_ref, vdt_ref, o_ref, sc_ref, sel_ref, s_ref, *, front, n_sel):
    i = pl.program_id(1)
    nkb = i + 1
    qi_all = jnp.concatenate([qi_ref[:, h * LANES:h * LANES + D_IDX] for h in range(H_I)], axis=0)
    wt = wi_ref[...].T
    w_row = jnp.concatenate([wt[h:h + 1, :] for h in range(H_I)], axis=1)

    def stage1(kb, c, masked):
        off = pl.multiple_of(kb * KB, KB)
        sct = jnp.maximum(_dot_nt(ki_ref[pl.ds(off, KB), :], qi_all), 0.0) * w_row
        score = sct[:, 0:KB] + sct[:, KB:2 * KB] + sct[:, 2 * KB:3 * KB] + sct[:, 3 * KB:4 * KB]
        sc_ref[kb] = jnp.where(_key_visible(i, off, 1, front), score, NINF) if masked else score
        return c

    _edge_then_middle(i, functools.partial(stage1, masked=True), functools.partial(stage1, masked=False), 0)

    def count(pred):
        def body(kb, acc):
            return acc + _group_sum(jnp.where(pred(sc_ref[kb]), 1, 0))
        return jnp.sum(lax.fori_loop(0, nkb, body, jnp.zeros((SUB, KB), jnp.int32)), axis=0, keepdims=True)

    tau = _kth_threshold(lambda cand: count(lambda sc: sc >= cand), (1, KB), n_sel)
    need = (n_sel - count(lambda sc: sc > tau)).astype(F32)

    strict_lower = jnp.where(lax.broadcasted_iota(jnp.int32, (KB, KB), 1) < lax.broadcasted_iota(jnp.int32, (KB, KB), 0),
                             1.0, 0.0).astype(BF16)

    def select(kb, before):
        sc = sc_ref[kb]
        eq = sc == tau
        eqf = jnp.where(eq, 1.0, 0.0)
        rank = _dot(strict_lower, eqf.astype(BF16)) + before
        sel_ref[kb] = jnp.where(((sc > tau) | (eq & (rank < need))) & (sc > NINF), 0.0, NEG)
        return before + jnp.sum(eqf, axis=0, keepdims=True)

    lax.fori_loop(0, nkb, select, jnp.zeros((1, KB), F32))

    for h in range(H_D):
        cs = slice(h * LANES, (h + 1) * LANES)
        qd = qd_ref[:, cs]

        def stage3(kb, mx):
            off = pl.multiple_of(kb * KB, KB)
            sdt = _dot_nt(kd_ref[pl.ds(off, KB), :], qd) * (DH_D ** -0.5) + sel_ref[kb]
            s_ref[kb] = sdt
            return jnp.maximum(mx, _group_max(sdt))

        m = jnp.max(_fori_pairs(0, nkb, stage3, lambda kb, c: stage3(kb + 1, stage3(kb, c)),
                                jnp.full((SUB, KB), NEG, F32)), axis=0, keepdims=True)

        def stage4(kb, carry):
            l8, acc = carry
            p = jnp.exp(s_ref[kb] - m)
            return l8 + _group_sum(p), acc + _dot(vdt_ref[kb], p.astype(BF16))

        def stage4_pair(kb, carry):
            l8, acc = carry
            p = jnp.exp(jnp.concatenate([s_ref[kb], s_ref[kb + 1]], axis=0) - m)
            vt2 = jnp.concatenate([vdt_ref[kb], vdt_ref[kb + 1]], axis=1)
            return l8 + _group_sum(p), acc + _dot(vt2, p.astype(BF16))

        l8, acc = _fori_pairs(0, nkb, stage4, stage4_pair, (jnp.zeros((SUB, KB), F32), jnp.zeros((DH_D, KB), F32)))
        o_ref[:, cs] = (acc / jnp.sum(l8, axis=0, keepdims=True)).T.astype(o_ref.dtype)


def dsa_prompt(qi, wi, qd, ki, kd, vdt, *, b, t_pad, front, n_sel):
    nkb = t_pad // KB
    qspec = lambda w: pl.BlockSpec((KB, w), lambda bb, i: (bb * nkb + i, 0))
    kspec = lambda w: pl.BlockSpec((t_pad, w), lambda bb, i: (bb, 0))
    return pl.pallas_call(
        functools.partial(_dsa_prompt_kernel, front=front, n_sel=n_sel),
        grid=(b, nkb),
        in_specs=[qspec(512), qspec(128), qspec(512), kspec(D_IDX), kspec(DH_D),
                  pl.BlockSpec((nkb, DH_D, KB), lambda bb, i: (bb, 0, 0))],
        out_specs=qspec(512),
        out_shape=jax.ShapeDtypeStruct((b * t_pad, 512), BF16),
        scratch_shapes=[pltpu.VMEM((nkb, KB, KB), F32)] * 3,
        compiler_params=_cparams(("parallel", "arbitrary")),
        name="dsa_prompt",
    )(qi, wi, qd, ki, kd, vdt)


TS = 16


PG = 8
PAGE_BUFFERS = 2


def _page_map(g, n_pages, nd):
    def index(bb, j, pt):
        return (pt[bb * n_pages + jnp.minimum(j * PG + g, n_pages - 1)],) + (0,) * nd
    return index


def _online_softmax_update(s, vs, m_ref, l_ref, acc_ref):
    m_old = m_ref[...]
    m_new = jnp.maximum(m_old, jnp.max(s, axis=1, keepdims=True))
    alpha = jnp.exp(m_old - m_new)
    p = jnp.where(s > 0.5 * NEG, jnp.exp(s - m_new), 0.0)
    l_ref[...] = alpha * l_ref[...] + jnp.sum(p, axis=1, keepdims=True)
    r = s.shape[0] // len(vs)
    pv = jnp.concatenate([_dot(p[g * r:(g + 1) * r].astype(BF16), v) for g, v in enumerate(vs)], axis=0)
    acc_ref[...] = alpha * acc_ref[...] + pv
    m_ref[...] = m_new


def _sample_a_kernel(pt_ref, qc_ref, qi_ref, wi_ref, *refs, n_steps, front, lam_init):
    ck_refs, cv_refs, cik_refs = refs[:PG], refs[PG:2 * PG], refs[2 * PG:3 * PG]
    kn_ref, vn_ref, kin_ref, lam_ref, g_ref, o_ref, keys_ref, m_ref, l_ref, acc_ref = refs[3 * PG:]
    j = pl.program_id(1)
    lane = lax.broadcasted_iota(jnp.int32, (TS, LANES), 1)

    @pl.when(j == 0)
    def _():
        m_ref[...] = jnp.full(m_ref.shape, NEG, F32)
        l_ref[...] = jnp.zeros(l_ref.shape, F32)
        acc_ref[...] = jnp.zeros(acc_ref.shape, F32)

    def qstack(h):
        qh = qc_ref[:, h * LANES:(h + 1) * LANES]
        return jnp.concatenate([jnp.where(lane < DC, qh, jnp.zeros_like(qh)),
                                jnp.where(lane >= DC, qh, jnp.zeros_like(qh))], axis=0)

    def head_rows(h):
        return slice(h * 2 * TS, (h + 1) * 2 * TS)

    qi_all = jnp.concatenate([qi_ref[:, h * LANES:h * LANES + D_IDX] for h in range(H_I)], axis=0)
    w_col = jnp.concatenate([wi_ref[:, h:h + 1] for h in range(H_I)], axis=0)

    def idx_scores(kipt):
        sc = jnp.maximum(_dot(qi_all, kipt), 0.0) * w_col
        return sc[0:TS] + sc[TS:2 * TS] + sc[2 * TS:3 * TS] + sc[3 * TS:4 * TS]

    @pl.when(j < n_steps)
    def _():
        ss, vs = [], []
        for h in range(H_C):
            rows_h = pl.ds(h, PAGE_SIZE, stride=H_C)
            k = jnp.concatenate([r[0, rows_h, :].astype(BF16) for r in ck_refs], axis=0)
            vs.append(jnp.concatenate([r[0, rows_h, :].astype(BF16) for r in cv_refs], axis=0))
            ss.append(_dot_nt(qstack(h), k))
        _online_softmax_update(jnp.concatenate(ss, axis=0), vs, m_ref, l_ref, acc_ref)
        keys_ref[0] = idx_scores(jnp.concatenate([r[0].astype(BF16) for r in cik_refs], axis=1))

    @pl.when(j == n_steps)
    def _():
        kr = lax.broadcasted_iota(jnp.int32, (1, TS), 1)
        q_of_row = lax.broadcasted_iota(jnp.int32, (2 * TS, 1), 0) & (TS - 1)
        ok = (kr >= front) & (kr <= q_of_row)
        ss = [jnp.where(ok, _dot_nt(qstack(h), kn_ref[:, h * LANES:(h + 1) * LANES]), NEG) for h in range(H_C)]
        vs = [vn_ref[:, h * LANES:(h + 1) * LANES] for h in range(H_C)]
        _online_softmax_update(jnp.concatenate(ss, axis=0), vs, m_ref, l_ref, acc_ref)
        krp = lax.broadcasted_iota(jnp.int32, (1, PAGE_SIZE), 1)
        okq = (krp >= front) & (krp < TS) & (krp <= lax.broadcasted_iota(jnp.int32, (TS, 1), 0))
        keys_ref[0] = jnp.concatenate([jnp.where(okq, idx_scores(kin_ref[0]), NINF),
                                       jnp.full((TS, (PG - 1) * PAGE_SIZE), NINF, F32)], axis=1)
        lam = _diff_lambda(lam_ref, lam_init)
        for h in range(H_C):
            a = acc_ref[head_rows(h), :] / l_ref[head_rows(h), :]
            o = a[0:TS] - lam * a[TS:2 * TS]
            o_ref[:, h * LANES:(h + 1) * LANES] = (_rms(o, g_ref[...]) * (1.0 - lam_init)).astype(o_ref.dtype)


def sample_diff_idx(pt, qc, qi, wi, ck, cv, cik, kn, vn, kin, lam_p, g_c, *, b, n_pages, front, lam_init):
    assert n_pages % PG == 0
    n_steps = n_pages // PG
    rows = H_C * 2 * TS
    qspec = lambda w: pl.BlockSpec((TS, w), lambda bb, j, pt: (bb, 0))
    deep = pl.Buffered(PAGE_BUFFERS)
    pages4 = [pl.BlockSpec((1, PAGE_SIZE * H_C, 2 * DC), _page_map(g, n_pages, 2), pipeline_mode=deep) for g in range(PG)]
    pages_i = [pl.BlockSpec((1, D_IDX, PAGE_SIZE), _page_map(g, n_pages, 2), pipeline_mode=deep) for g in range(PG)]
    return pl.pallas_call(
        functools.partial(_sample_a_kernel, n_steps=n_steps, front=front, lam_init=lam_init),
        grid_spec=pltpu.PrefetchScalarGridSpec(
            num_scalar_prefetch=1,
            grid=(b, n_steps + 1),
            in_specs=[qspec(512), qspec(512), qspec(128)] + pages4 + pages4 + pages_i
                     + [qspec(512), qspec(512), pl.BlockSpec((1, D_IDX, PAGE_SIZE), lambda bb, j, pt: (bb, 0, 0)),
                        pl.BlockSpec((4, DC), lambda bb, j, pt: (0, 0)),
                        pl.BlockSpec((1, 2 * DC), lambda bb, j, pt: (0, 0))],
            out_specs=[qspec(512), pl.BlockSpec((1, TS, PG * PAGE_SIZE), lambda bb, j, pt: (bb, 0, j))],
            scratch_shapes=[pltpu.VMEM((rows, 1), F32), pltpu.VMEM((rows, 1), F32), pltpu.VMEM((rows, 2 * DC), F32)]),
        out_shape=[jax.ShapeDtypeStruct((b * TS, 512), BF16),
                   jax.ShapeDtypeStruct((b, TS, (n_steps + 1) * PG * PAGE_SIZE), F32)],
        compiler_params=_cparams(("parallel", "arbitrary")),
        name="sample_diff_idx",
    )(pt, qc, qi, wi, *([ck] * PG), *([cv] * PG), *([cik] * PG), kn, vn, kin, lam_p, g_c.reshape(1, 2 * DC))


def _sample_threshold_kernel(keys_ref, tau_ref, need_ref, *, n_sel):
    rows = keys_ref.shape[0]

    def count(pred):
        return jnp.sum(jnp.where(pred(keys_ref[...]), 1, 0), axis=1, keepdims=True)

    tau = _kth_threshold(lambda cand: count(lambda sc: sc >= cand), (rows, 1), n_sel)
    tau_ref[...] = tau
    need_ref[...] = (n_sel - count(lambda sc: sc > tau)).astype(F32)


def sample_threshold(keys, *, n_sel):
    rows, nk = keys.shape
    return pl.pallas_call(
        functools.partial(_sample_threshold_kernel, n_sel=n_sel),
        grid=(1,),
        in_specs=[pl.BlockSpec((rows, nk), lambda i: (0, 0), pipeline_mode=pl.Buffered(1))],
        out_specs=[pl.BlockSpec((rows, 1), lambda i: (0, 0))] * 2,
        out_shape=[jax.ShapeDtypeStruct((rows, 1), F32)] * 2,
        compiler_params=_cparams(("arbitrary",)),
        name="sample_threshold",
    )(keys)


def _sample_b_kernel(pt_ref, tau_ref, need_ref, keys_ref, qd_ref, *refs, n_steps):
    ck_refs, cv_refs = refs[:PG], refs[PG:2 * PG]
    kn_ref, vn_ref, o_ref, before_ref, m_ref, l_ref, acc_ref = refs[2 * PG:]
    j = pl.program_id(1)

    @pl.when(j == 0)
    def _():
        before_ref[...] = jnp.zeros(before_ref.shape, F32)
        m_ref[...] = jnp.full(m_ref.shape, NEG, F32)
        l_ref[...] = jnp.zeros(l_ref.shape, F32)
        acc_ref[...] = jnp.zeros(acc_ref.shape, F32)

    strict_upper = jnp.where(lax.broadcasted_iota(jnp.int32, (PAGE_SIZE, PAGE_SIZE), 0)
                             < lax.broadcasted_iota(jnp.int32, (PAGE_SIZE, PAGE_SIZE), 1), 1.0, 0.0).astype(BF16)
    qd_all = jnp.concatenate([qd_ref[:, h * LANES:(h + 1) * LANES] for h in range(H_D)], axis=0)

    def select(key, before):
        tau = tau_ref[...]
        eq = key == tau
        eqf = jnp.where(eq, 1.0, 0.0)
        rank = _dot(eqf.astype(BF16), strict_upper) + before
        sel = ((key > tau) | (eq & (rank < need_ref[...]))) & (key > NINF)
        return sel, before + jnp.sum(eqf, axis=1, keepdims=True)

    def attend(s, sel, v):
        s = jnp.where(jnp.concatenate([sel] * H_D, axis=0), s * (DH_D ** -0.5), NEG)
        _online_softmax_update(s, [v], m_ref, l_ref, acc_ref)

    @pl.when(j < n_steps)
    def _():
        key = keys_ref[0]
        before = before_ref[...]
        sels = []
        for g in range(PG):
            sel, before = select(key[:, g * PAGE_SIZE:(g + 1) * PAGE_SIZE], before)
            sels.append(sel)
        before_ref[...] = before
        k = jnp.concatenate([r[0].astype(BF16) for r in ck_refs], axis=0)
        v = jnp.concatenate([r[0].astype(BF16) for r in cv_refs], axis=0)
        attend(_dot_nt(qd_all, k), jnp.concatenate(sels, axis=1), v)

    @pl.when(j == n_steps)
    def _():
        sel, _ = select(keys_ref[0][:, :PAGE_SIZE], before_ref[...])
        attend(_dot_nt(qd_all, kn_ref[0]), sel, vn_ref[0])
        o = acc_ref[...] / l_ref[...]
        for h in range(H_D):
            o_ref[:, h * LANES:(h + 1) * LANES] = o[h * TS:(h + 1) * TS].astype(o_ref.dtype)


def sample_dsa(pt, keys, qd, ck, cv, kn, vn, *, b, n_pages, n_sel):
    assert n_pages % PG == 0
    n_steps = n_pages // PG
    rows = H_D * TS
    tau, need = sample_threshold(keys.reshape(b * TS, keys.shape[2]), n_sel=n_sel)
    pages = [pl.BlockSpec((1, PAGE_SIZE, DH_D), _page_map(g, n_pages, 2), pipeline_mode=pl.Buffered(PAGE_BUFFERS))
             for g in range(PG)]
    nspec = pl.BlockSpec((1, PAGE_SIZE, DH_D), lambda bb, j, pt: (bb, 0, 0))
    col = pl.BlockSpec((TS, 1), lambda bb, j, pt: (bb, 0))
    return pl.pallas_call(
        functools.partial(_sample_b_kernel, n_steps=n_steps),
        grid_spec=pltpu.PrefetchScalarGridSpec(
            num_scalar_prefetch=1,
            grid=(b, n_steps + 1),
            in_specs=[col, col,
                      pl.BlockSpec((1, TS, PG * PAGE_SIZE), lambda bb, j, pt: (bb, 0, j)),
                      pl.BlockSpec((TS, 512), lambda bb, j, pt: (bb, 0))] + pages + pages + [nspec, nspec],
            out_specs=pl.BlockSpec((TS, 512), lambda bb, j, pt: (bb, 0)),
            scratch_shapes=[pltpu.VMEM((TS, 1), F32),
                            pltpu.VMEM((rows, 1), F32), pltpu.VMEM((rows, 1), F32), pltpu.VMEM((rows, DH_D), F32)]),
        out_shape=jax.ShapeDtypeStruct((b * TS, 512), BF16),
        compiler_params=_cparams(("parallel", "arbitrary")),
        name="sample_dsa",
    )(pt, tau, need, keys, qd, *([ck] * PG), *([cv] * PG), kn, vn)


REC_CHUNK = 64
REC_SUB = 16
TM_PROJ = 512
TM_REC, TN_REC = 1408, 1408
TM_ATT = 704
TM_ROWS = 384
TM_FFN = 528


def _pad_cols(w, n):
    return jnp.pad(w, ((0, 0), (0, n - w.shape[1])))


def _tile_rows(m, pref):
    return pref if m % pref == 0 else m


def kernel(x_prompt, x_sample, state_hgrn, state_mlstm_C, state_mlstm_n, state_mlstm_m, state_ffn_conv, cache_diff_k, cache_diff_v, cache_dsa_k, cache_dsa_v, cache_idx_k, page_table, meta_tokens, norm_gains, w_in_rec, b_gates_rec, lb_logits, g_norm_hgrn, g_norm_mlstm, w_out_rec, w_in_att, diff_lambda, g_norm_diff, w_out_att, w_ffn_up, ffn_conv_w, ffn_conv_b, w_ffn_down):
    bp, t_in, d = x_prompt.shape
    bs, t_s, _ = x_sample.shape
    depth = norm_gains.shape[0]
    n_pages = page_table.shape[1]
    past_len = n_pages * PAGE_SIZE
    real_p = N_META + t_in
    tp = -(-real_p // QB) * QB
    front_p = tp - real_p
    front_s = TS - t_s
    assert tp % REC_CHUNK == 0 and tp % TM_ROWS == 0 and front_p >= CONV_W - 1 and front_s >= CONV_W - 1
    mp, ms = bp * tp, bs * TS

    meta = jnp.broadcast_to(meta_tokens.astype(x_prompt.dtype)[None], (bp, N_META, d))
    xp = jnp.concatenate([jnp.zeros((bp, front_p, d), x_prompt.dtype), meta, x_prompt], axis=1).reshape(mp, d)
    xs = jnp.concatenate([jnp.zeros((bs, front_s, d), x_sample.dtype), x_sample], axis=1).reshape(ms, d)
    lb_all = jnp.cumsum(jax.nn.softmax(lb_logits.astype(F32), axis=0), axis=0)
    pt_flat = page_table.reshape(-1).astype(jnp.int32)
    sel_p = min(TOPK_MAX, t_in // 4)
    sel_s = min(TOPK_MAX, (past_len + t_s) // 4)
    tmp_s = _tile_rows(ms, TM_PROJ)
    tmr_s = _tile_rows(ms, TM_ROWS)

    pos_p = jnp.arange(tp, dtype=jnp.int32) - front_p
    pos_s = jnp.tile(past_len + jnp.arange(TS, dtype=jnp.int32) - front_s, ms // TS)
    tabs_p = (rope_tables(pos_p, DC, DC // ROT_FRAC // 2), rope_tables(pos_p, DH_D, DH_D // ROT_FRAC // 2))
    tabs_s = (rope_tables(pos_s, DC, DC // ROT_FRAC // 2), rope_tables(pos_s, DH_D, DH_D // ROT_FRAC // 2))

    rec_p, rec_s = [[], [], [], []], [[], [], [], []]
    att_p, att_s = [[], [], [], [], []], [[], [], [], [], []]
    conv_p, conv_s = [], []
    for l in range(depth):
        p = l // 2
        g = norm_gains[l].astype(F32)
        if l % 2 == 0:
            w_in = _pad_cols(w_in_rec[p], REC_N).astype(BF16)
            w_out = w_out_rec[p].astype(BF16)
            prm = (lb_all[p], b_gates_rec[p].astype(F32), g_norm_hgrn[p].astype(F32), g_norm_mlstm[p].astype(F32))
            proj = norm_matmul(xp, g[0], w_in, tm=_tile_rows(mp, TM_REC), tn=TN_REC).reshape(bp, tp, REC_N)
            zs = jnp.zeros((bp, 4, 128, 128), F32)
            y, *st = rec_mixer(proj, *prm, zs, zs, jnp.zeros((bp, 4, 128), F32), jnp.zeros((bp, 4), F32),
                               C=REC_CHUNK, W=REC_SUB, front=front_p)
            xp = matmul_norm_res([y.reshape(mp, -1)], [w_out], g[1], xp, tm=TM_ROWS, t_pad=tp, front=front_p)
            for j in range(4):
                rec_p[j].append(st[j])
            proj = norm_matmul(xs, g[0], w_in, tm=tmp_s, tn=384).reshape(bs, TS, REC_N)
            y, *st = rec_mixer(proj, *prm, state_hgrn[p].astype(F32), state_mlstm_C[p].astype(F32),
                               state_mlstm_n[p].astype(F32), state_mlstm_m[p].astype(F32), C=TS, W=TS, front=front_s)
            xs = matmul_norm_res([y.reshape(ms, -1)], [w_out], g[1], xs, tm=tmr_s, t_pad=TS, front=front_s)
            for j in range(4):
                rec_s[j].append(st[j])
        else:
            lam_init = 0.8 - 0.6 * math.exp(-0.3 * l)
            w_in = _pad_cols(w_in_att[p], ATT_N).astype(BF16)
            w_out = w_out_att[p].astype(BF16)
            dl, gc = diff_lambda[p].astype(F32), g_norm_diff[p].astype(F32)
            proj = norm_matmul(xp, g[0], w_in, tm=_tile_rows(mp, TM_ATT), tn=ATT_N)
            (qc, kc, kcb, vc, vcb, qd, kd, kdb, vd, vdb, qi, ki, kib, wi, vct, vdt) = att_prep(proj, *tabs_p, tm=KB)
            oc = diff_prompt(qc, kcb, vct, dl, gc, b=bp, t_pad=tp, front=front_p, lam_init=lam_init)
            od = dsa_prompt(qi, wi, qd, kib, kdb, vdt, b=bp, t_pad=tp, front=front_p, n_sel=sel_p)
            xp = matmul_norm_res([oc, od], [w_out[:512], w_out[512:]], g[1], xp, tm=TM_ROWS, t_pad=tp, front=front_p)
            for j, (a, shp) in enumerate([(kc, (H_C, 2 * DC)), (vc, (H_C, 2 * DC)), (kd, (DH_D,)), (vd, (DH_D,)), (ki, (D_IDX,))]):
                att_p[j].append(a.reshape((bp, tp) + shp)[:, front_p:])
            proj = norm_matmul(xs, g[0], w_in, tm=tmp_s, tn=384)
            (qc, kc, kcb, vc, vcb, qd, kd, kdb, vd, vdb, qi, ki, kib, wi, _, _) = att_prep(proj, *tabs_s, tm=ms)
            as_page = lambda a: jnp.pad(a.reshape(bs, TS, -1), ((0, 0), (0, PAGE_SIZE - TS), (0, 0)))
            rows_kh = lambda c: c.reshape(c.shape[0], PAGE_SIZE * H_C, 2 * DC)
            oc, keys = sample_diff_idx(pt_flat, qc, qi, wi, rows_kh(cache_diff_k[p]), rows_kh(cache_diff_v[p]),
                                       jnp.swapaxes(cache_idx_k[p], 1, 2), kcb, vcb, jnp.swapaxes(as_page(kib), 1, 2), dl, gc,
                                       b=bs, n_pages=n_pages, front=front_s, lam_init=lam_init)
            od = sample_dsa(pt_flat, keys, qd, cache_dsa_k[p], cache_dsa_v[p], as_page(kdb), as_page(vdb),
                            b=bs, n_pages=n_pages, n_sel=sel_s)
            xs = matmul_norm_res([oc, od], [w_out[:512], w_out[512:]], g[1], xs, tm=tmr_s, t_pad=TS, front=front_s)
            for j, (a, shp) in enumerate([(kc, (H_C, 2 * DC)), (vc, (H_C, 2 * DC)), (kd, (DH_D,)), (vd, (DH_D,)), (ki, (D_IDX,))]):
                att_s[j].append(a.reshape((bs, TS) + shp)[:, front_s:])
        w_up, w_down = w_ffn_up[l].astype(BF16), w_ffn_down[l].astype(BF16)
        cw, cb = ffn_conv_w[l].astype(F32), ffn_conv_b[l].astype(F32)
        xp, tail = ffn_fused(xp, g[2], w_up, cw, cb, w_down, g[3], tm=TM_FFN, t_pad=tp, front=front_p)
        conv_p.append(tail[:, SUB - (CONV_W - 1):])
        up = norm_matmul(xs, g[2], w_up, tm=tmp_s, tn=512)
        conv_s.append(up.reshape(bs, TS, 2 * D_FF)[:, TS - (CONV_W - 1):, :D_FF])
        buf = jnp.pad(state_ffn_conv[l].astype(F32), ((0, 0), (front_s - (CONV_W - 1), TS - front_s), (0, 0)))
        xs = ffn_down(up, buf.reshape(ms, D_FF), cw, cb, w_down, g[3], xs, tm=tmr_s, t_pad=TS, front=front_s)

    y_p = xp.reshape(bp, tp, d)[:, front_p + N_META:]
    y_s = xs.reshape(bs, TS, d)[:, front_s:]
    stack = lambda xs: xs[0][None] if len(xs) == 1 else jnp.stack(xs)
    return (y_p, y_s,
            stack(rec_p[0]), stack(rec_s[0]), stack(rec_p[1]), stack(rec_s[1]),
            stack(rec_p[2]), stack(rec_s[2]), stack(rec_p[3]), stack(rec_s[3]),
            stack(conv_p), stack(conv_s),
            stack(att_p[0]), stack(att_s[0]), stack(att_p[1]), stack(att_s[1]),
            stack(att_p[2]), stack(att_s[2]), stack(att_p[3]), stack(att_s[3]),
            stack(att_p[4]), stack(att_s[4]))
```

```python
import functools
import math

import jax
import jax.numpy as jnp
import numpy as np
from jax import lax
from jax.experimental import pallas as pl
from jax.experimental.pallas import tpu as pltpu

F32 = jnp.float32
BF16 = jnp.bfloat16

D_MODEL = 1024
N_META = 16
H_A, DK_A, DV_A = 4, 128, 128
H_B, DK_B, DV_B = 4, 128, 128
H_C, DC = 4, 64
H_D, DH_D = 4, 128
H_I, D_IDX = 4, 64
TOPK_MAX = 256
D_FF = 2816
CONV_W = 3
ROPE_THETA = 500000.0
ROT_FRAC = 4
EPS = 1e-6
PAGE_SIZE = 128
LANES = 128
NEG = -1e30

REC_N = 8 * 512 + LANES
ATT_N = 2560 + LANES
INT_MIN = -2 ** 31
VMEM_LIMIT = 56 * 1024 * 1024


def _cparams(sem):
    return pltpu.CompilerParams(dimension_semantics=sem, vmem_limit_bytes=VMEM_LIMIT)


def _rms(x, g):
    return x * lax.rsqrt(jnp.mean(x * x, axis=-1, keepdims=True) + EPS) * g


def _dot(a, b):
    return jnp.dot(a, b, preferred_element_type=F32)


def _dot_nt(a, b):
    return lax.dot_general(a, b, (((1,), (1,)), ((), ())), preferred_element_type=F32)


def _dot_tn(a, b):
    return lax.dot_general(a, b, (((0,), (0,)), ((), ())), preferred_element_type=F32)


def _dot_exact_lhs(tri, x):
    hi = x.astype(BF16)
    r1 = x - hi.astype(F32)
    mid = r1.astype(BF16)
    lo = (r1 - mid.astype(F32)).astype(BF16)
    return _dot(tri, hi) + _dot(tri, mid) + _dot(tri, lo)


def _row_valid(i, tm, t_pad, front):
    r = lax.broadcasted_iota(jnp.int32, (tm, 1), 0)
    if t_pad % tm == 0:
        t = (i % (t_pad // tm)) * tm + r
    else:
        assert tm % t_pad == 0 and (t_pad & (t_pad - 1)) == 0
        t = r & (t_pad - 1)
    return t >= front


def _norm_matmul_kernel(x_ref, g_ref, w_ref, o_ref, h_ref):
    @pl.when(pl.program_id(1) == 0)
    def _():
        h_ref[...] = _rms(x_ref[...], g_ref[...]).astype(BF16)

    o_ref[...] = _dot(h_ref[...], w_ref[...])


def norm_matmul(x, g, w, *, tm, tn):
    m, d = x.shape
    n = w.shape[1]
    assert m % tm == 0 and n % tn == 0
    return pl.pallas_call(
        _norm_matmul_kernel,
        grid=(m // tm, n // tn),
        in_specs=[pl.BlockSpec((tm, d), lambda i, j: (i, 0)),
                  pl.BlockSpec((1, d), lambda i, j: (0, 0)),
                  pl.BlockSpec((d, tn), lambda i, j: (0, j))],
        out_specs=pl.BlockSpec((tm, tn), lambda i, j: (i, j)),
        out_shape=jax.ShapeDtypeStruct((m, n), F32),
        scratch_shapes=[pltpu.VMEM((tm, d), BF16)],
        compiler_params=_cparams(("parallel", "arbitrary")),
        name="norm_matmul",
    )(x, g.reshape(1, d), w)


def _matmul_norm_res_kernel(*refs, n_in, tm, t_pad, front):
    a_refs, w_refs = refs[:n_in], refs[n_in:2 * n_in]
    g_ref, x_ref, o_ref = refs[2 * n_in:]
    acc = _dot(a_refs[0][...], w_refs[0][...])
    for a, w in zip(a_refs[1:], w_refs[1:]):
        acc = acc + _dot(a[...], w[...])
    out = x_ref[...] + _rms(acc, g_ref[...])
    o_ref[...] = jnp.where(_row_valid(pl.program_id(0), tm, t_pad, front), out, 0.0)


def matmul_norm_res(a_list, w_list, g, x, *, tm, t_pad, front):
    m, d = x.shape
    assert m % tm == 0
    n_in = len(a_list)
    in_specs = ([pl.BlockSpec((tm, a.shape[1]), lambda i: (i, 0)) for a in a_list]
                + [pl.BlockSpec(w.shape, lambda i: (0, 0)) for w in w_list]
                + [pl.BlockSpec((1, d), lambda i: (0, 0)), pl.BlockSpec((tm, d), lambda i: (i, 0))])
    return pl.pallas_call(
        functools.partial(_matmul_norm_res_kernel, n_in=n_in, tm=tm, t_pad=t_pad, front=front),
        grid=(m // tm,),
        in_specs=in_specs,
        out_specs=pl.BlockSpec((tm, d), lambda i: (i, 0)),
        out_shape=jax.ShapeDtypeStruct((m, d), F32),
        compiler_params=_cparams(("parallel",)),
        name="matmul_norm_res",
    )(*a_list, *w_list, g.reshape(1, d), x)


def _ffn_down_kernel(ug_ref, uv_ref, buf_ref, cw_ref, cb_ref, w_ref, g_ref, x_ref, o_ref, *, tm, t_pad, front):
    t = lax.broadcasted_iota(jnp.int32, (tm, 1), 0) & (t_pad - 1)
    ug = jnp.where((t >= front - (CONV_W - 1)) & (t < front), buf_ref[...], ug_ref[...])
    conv = (cb_ref[...] + cw_ref[0:1, :] * pltpu.roll(ug, 2, axis=0) + cw_ref[1:2, :] * pltpu.roll(ug, 1, axis=0)
            + cw_ref[2:3, :] * ug)
    act = (conv * jax.nn.sigmoid(conv) * uv_ref[...]).astype(BF16)
    out = x_ref[...] + _rms(_dot(act, w_ref[...]), g_ref[...])
    o_ref[...] = jnp.where(t >= front, out, 0.0)


def ffn_down(up, buf, cw, cb, w_down, g, x, *, tm, t_pad, front):
    m, d = x.shape
    f = w_down.shape[0]
    assert m % tm == 0 and tm % t_pad == 0 and (t_pad & (t_pad - 1)) == 0 and front >= CONV_W - 1
    assert up.shape == (m, 2 * f) and buf.shape == (m, f)
    return pl.pallas_call(
        functools.partial(_ffn_down_kernel, tm=tm, t_pad=t_pad, front=front),
        grid=(m // tm,),
        in_specs=[pl.BlockSpec((tm, f), lambda i: (i, 0)),
                  pl.BlockSpec((tm, f), lambda i: (i, 1)),
                  pl.BlockSpec((tm, f), lambda i: (i, 0)),
                  pl.BlockSpec((CONV_W, f), lambda i: (0, 0)),
                  pl.BlockSpec((1, f), lambda i: (0, 0)),
                  pl.BlockSpec((f, d), lambda i: (0, 0)),
                  pl.BlockSpec((1, d), lambda i: (0, 0)),
                  pl.BlockSpec((tm, d), lambda i: (i, 0))],
        out_specs=pl.BlockSpec((tm, d), lambda i: (i, 0)),
        out_shape=jax.ShapeDtypeStruct((m, d), F32),
        compiler_params=_cparams(("parallel",)),
        name="ffn_down",
    )(up, up, buf, cw, cb.reshape(1, f), w_down, g.reshape(1, d), x)


FF_CW = 256
FF_HALO = 16


def _ffn_fused_kernel(x_ref, halo_ref, gin_ref, wup_ref, cw_ref, cb_ref, wdn_ref, gout_ref, o_ref, conv_ref, acc_ref,
                      *, tm, t_pad, front):
    i = pl.program_id(0)
    x = x_ref[...]
    h = jnp.concatenate([_rms(halo_ref[...], gin_ref[...]), _rms(x, gin_ref[...])], axis=0).astype(BF16)
    for c in range(D_FF // FF_CW):
        cs = slice(c * FF_CW, (c + 1) * FF_CW)
        ug = _dot(h, wup_ref[:, cs])
        uv = _dot(h[FF_HALO:], wup_ref[:, D_FF + c * FF_CW:D_FF + (c + 1) * FF_CW])
        prev1 = pltpu.roll(ug, 1, axis=0)[FF_HALO:]
        prev2 = pltpu.roll(ug, 2, axis=0)[FF_HALO:]
        conv = cb_ref[:, cs] + cw_ref[0:1, cs] * prev2 + cw_ref[1:2, cs] * prev1 + cw_ref[2:3, cs] * ug[FF_HALO:]
        act = (conv * jax.nn.sigmoid(conv) * uv).astype(BF16)
        part = _dot(act, wdn_ref[cs, :])
        if c == 0:
            acc_ref[...] = part
        else:
            acc_ref[...] += part
        conv_ref[0, :, cs] = ug[FF_HALO + tm - SUB:, :]
    out = x + _rms(acc_ref[...], gout_ref[...])
    o_ref[...] = jnp.where(_row_valid(i, tm, t_pad, front), out, 0.0)


def ffn_fused(x, g_in, w_up, cw, cb, w_down, g_out, *, tm, t_pad, front):
    m, d = x.shape
    nc = D_FF // FF_CW
    assert m % tm == 0 and t_pad % tm == 0 and tm % FF_HALO == 0 and D_FF % FF_CW == 0
    per_seq = t_pad // tm
    hb = tm // FF_HALO
    const = lambda shape: pl.BlockSpec(shape, lambda i: (0,) * len(shape), pipeline_mode=pl.Buffered(1))
    return pl.pallas_call(
        functools.partial(_ffn_fused_kernel, tm=tm, t_pad=t_pad, front=front),
        grid=(m // tm,),
        in_specs=[pl.BlockSpec((tm, d), lambda i: (i, 0)),
                  pl.BlockSpec((FF_HALO, d), lambda i: (jnp.maximum(i * hb - 1, 0), 0)),
                  const((1, d)), const((d, 2 * D_FF)), const((CONV_W, D_FF)), const((1, D_FF)),
                  const((D_FF, d)), const((1, d))],
        out_specs=[pl.BlockSpec((tm, d), lambda i: (i, 0)),
                   pl.BlockSpec((1, SUB, D_FF), lambda i: (i // per_seq, 0, 0))],
        out_shape=[jax.ShapeDtypeStruct((m, d), F32), jax.ShapeDtypeStruct((m // t_pad, SUB, D_FF), F32)],
        scratch_shapes=[pltpu.VMEM((tm, d), F32)],
        compiler_params=_cparams(("arbitrary",)),
        name="ffn_fused",
    )(x, x, g_in.reshape(1, d), w_up, cw, cb.reshape(1, D_FF), w_down, g_out.reshape(1, d))


def _log_sigmoid(x):
    return jnp.minimum(x, 0.0) - jnp.log1p(jnp.exp(-jnp.abs(x)))


def _rec_kernel(proj_ref, lb_ref, bg_ref, ga_ref, gb_ref, s0_ref, c0_ref, n0_ref, m0_ref,
                y_ref, s_ref, c_ref, n_ref, m_ref, st_ref, *, C, W, front):
    ci = pl.program_id(1)
    nci = pl.num_programs(1)

    @pl.when(ci == 0)
    def _():
        for h in range(H_A):
            st_ref[h] = s0_ref[0, h].T
        c_ref[...] = c0_ref[...]
        n_ref[...] = n0_ref[...]
        m_ref[...] = m0_ref[...]

    row = ci * C + lax.broadcasted_iota(jnp.int32, (C, 1), 0)
    valid = row >= front
    r_i = lax.broadcasted_iota(jnp.int32, (C, C), 0)
    c_i = lax.broadcasted_iota(jnp.int32, (C, C), 1)
    causal = r_i >= c_i
    tri = jnp.where(causal, 1.0, 0.0).astype(BF16)

    gates = proj_ref[0, :, 8 * 512:8 * 512 + LANES] + bg_ref[...]
    lf_all = jnp.where(valid, _log_sigmoid(gates), 0.0)
    ig_all = jnp.where(valid, gates, NEG)
    b_all = _dot_exact_lhs(tri, lf_all)
    b_all_t = b_all.T
    ig_all_t = ig_all.T

    qk_dots, qc_dots = [], []
    for h in range(H_B):
        qb = proj_ref[0, :, 2048 + h * 128:2048 + (h + 1) * 128].astype(BF16)
        kb = (proj_ref[0, :, 2560 + h * 128:2560 + (h + 1) * 128] * (DK_B ** -0.5)).astype(BF16)
        qk_dots.append(_dot_nt(qb, kb))
        qc_dots.append(_dot(qb, c_ref[0, h].astype(BF16)))

    for h in range(H_B):
        q = proj_ref[0, :, 2048 + h * 128:2048 + (h + 1) * 128]
        k = proj_ref[0, :, 2560 + h * 128:2560 + (h + 1) * 128] * (DK_B ** -0.5)
        v = proj_ref[0, :, 3072 + h * 128:3072 + (h + 1) * 128]
        og = proj_ref[0, :, 3584 + h * 128:3584 + (h + 1) * 128]
        vb = v.astype(BF16)
        b_col = b_all[:, H_B + h:H_B + h + 1]
        b_row = b_all_t[H_B + h:H_B + h + 1, :]
        i_col = ig_all[:, h:h + 1]
        i_row = ig_all_t[h:h + 1, :]
        m_prev = m_ref[0, h:h + 1, 0:1]
        dmat = jnp.where(causal, b_col - b_row + i_row, NEG)
        inter = b_col + m_prev
        mt = jnp.maximum(inter, jnp.max(dmat, axis=1, keepdims=True))
        w = jnp.exp(dmat - mt) * qk_dots[h]
        wi = jnp.exp(inter - mt)
        c_st = c_ref[0, h]
        n_st = n_ref[0, h:h + 1, :]
        num = wi * qc_dots[h] + _dot(w.astype(BF16), vb)
        den = wi * jnp.sum(q * n_st, axis=1, keepdims=True) + jnp.sum(w, axis=1, keepdims=True)
        hc = num / jnp.maximum(jnp.abs(den), jnp.exp(-mt))
        m_new = mt[C - 1:C, :]
        b_last = b_col[C - 1:C, :]
        decay = jnp.exp(b_last + m_prev - m_new)
        kw = k * jnp.exp(b_last - b_col + i_col - m_new)
        c_ref[0, h] = decay * c_st + _dot_tn(kw.astype(BF16), vb)
        n_ref[0, h:h + 1, :] = decay * n_st + jnp.sum(kw, axis=0, keepdims=True)
        m_ref[0, h:h + 1, :] = jnp.broadcast_to(m_new, (1, LANES))
        yb = _rms(hc, gb_ref[...]) * jax.nn.sigmoid(og)
        y_ref[0, :, 512 + h * 128:512 + (h + 1) * 128] = jnp.where(valid, yb, 0.0).astype(y_ref.dtype)

    nj = C // W
    rw = lax.broadcasted_iota(jnp.int32, (W, 1), 0)
    ones_sq = jnp.ones((DK_A, LANES), BF16)
    lbv = lb_ref[...]
    f = lbv + (1.0 - lbv) * jax.nn.sigmoid(proj_ref[0, :, 512:1024])
    logf = jnp.where(valid, jnp.log(f), 0.0)
    kk = jnp.where(valid, 1.0 - f, 0.0)
    qq = proj_ref[0, :, 0:512] * (DK_A ** -0.5)
    vv = proj_ref[0, :, 1024:1536]
    bfull = _dot_exact_lhs(tri, logf)
    bcs = []
    for j in range(nj):
        rows = slice(j * W, (j + 1) * W)
        bcs.append(bfull[rows] if j == 0 else bfull[rows] - bfull[j * W - 1:j * W])

    heads = [slice(h * 128, (h + 1) * 128) for h in range(H_A)]
    row_sums, outer, decay_last = {}, {}, []
    for j in range(nj):
        rows = slice(j * W, (j + 1) * W)
        bc = bcs[j]
        last = bc[W - 1:W, :]
        decay_last.append(jnp.exp(last))
        kdec = (kk[rows] * jnp.exp(last - bc)).astype(BF16)
        gs = []
        for s in range(W):
            e = jnp.exp(jnp.where(rw >= s, bc - bc[s:s + 1, :], NEG))
            gs.append(qq[rows] * kk[j * W + s:j * W + s + 1, :] * e)
        g = jnp.concatenate(gs, axis=0).astype(BF16)
        for h, cs in enumerate(heads):
            row_sums[h, j] = _dot(g[:, cs], ones_sq)
            outer[h, j] = _dot_tn(vv[rows, cs].astype(BF16), kdec[:, cs])

    from_state = {}
    for h, cs in enumerate(heads):
        st = st_ref[h]
        for j in range(nj):
            rows = slice(j * W, (j + 1) * W)
            from_state[h, j] = _dot_nt((qq[rows, cs] * jnp.exp(bcs[j][:, cs])).astype(BF16), st.astype(BF16))
            st = decay_last[j][:, cs] * st + outer[h, j]
        st_ref[h] = st

    for h, cs in enumerate(heads):
        for j in range(nj):
            rows = slice(j * W, (j + 1) * W)
            o = from_state[h, j]
            for s in range(W):
                o = o + row_sums[h, j][s * W:(s + 1) * W] * vv[j * W + s:j * W + s + 1, cs]
            ga = proj_ref[0, rows, 1536 + h * 128:1536 + (h + 1) * 128]
            ya = _rms(o, ga_ref[...]) * (ga * jax.nn.sigmoid(ga))
            vld = (ci * C + j * W + rw) >= front
            y_ref[0, rows, cs] = jnp.where(vld, ya, 0.0).astype(y_ref.dtype)

    @pl.when(ci == nci - 1)
    def _():
        for h in range(H_A):
            s_ref[0, h] = st_ref[h].T


def rec_mixer(proj, lb, bg, g_a, g_b, s0, c0, n0, m0, *, C, W, front):
    b, t, _ = proj.shape
    assert t % C == 0 and C % W == 0
    m0b = jnp.broadcast_to(m0[:, :, None], (b, H_B, LANES))
    bgp = jnp.zeros((1, LANES), F32).at[0, :2 * H_B].set(bg.reshape(-1))
    st_spec = pl.BlockSpec((1, 4, 128, 128), lambda i, c: (i, 0, 0, 0))
    v_spec = pl.BlockSpec((1, 4, LANES), lambda i, c: (i, 0, 0))
    row_spec = lambda n: pl.BlockSpec((1, n), lambda i, c: (0, 0))
    y, s, cc, n, m = pl.pallas_call(
        functools.partial(_rec_kernel, C=C, W=W, front=front),
        grid=(b, t // C),
        in_specs=[pl.BlockSpec((1, C, REC_N), lambda i, c: (i, c, 0)),
                  row_spec(512), row_spec(LANES), row_spec(128), row_spec(128),
                  st_spec, st_spec, v_spec, v_spec],
        out_specs=[pl.BlockSpec((1, C, 1024), lambda i, c: (i, c, 0)), st_spec, st_spec, v_spec, v_spec],
        out_shape=[jax.ShapeDtypeStruct((b, t, 1024), BF16),
                   jax.ShapeDtypeStruct((b, 4, 128, 128), F32),
                   jax.ShapeDtypeStruct((b, 4, 128, 128), F32),
                   jax.ShapeDtypeStruct((b, 4, LANES), F32),
                   jax.ShapeDtypeStruct((b, 4, LANES), F32)],
        scratch_shapes=[pltpu.VMEM((4, 128, 128), F32)],
        compiler_params=_cparams(("parallel", "arbitrary")),
        name="rec_mixer",
    )(proj, lb.reshape(1, 512), bgp, g_a.reshape(1, 128), g_b.reshape(1, 128), s0, c0, n0, m0b)
    return y, s, cc, n, m[:, :, 0]


def rope_tables(pos, period, half):
    r = 2 * half
    inv = ROPE_THETA ** (-jnp.arange(half, dtype=F32) * 2.0 / r)
    ang = pos.astype(F32)[:, None] * inv[None, :]
    cos, sin = jnp.cos(ang), jnp.sin(ang)
    lane = np.arange(LANES) % period
    idx = np.where(lane < half, lane, np.where(lane < r, lane - half, 0))
    first, second = jnp.asarray(lane < half), jnp.asarray((lane >= half) & (lane < r))
    c = jnp.where(first | second, cos[:, idx], 1.0)
    sa = jnp.where(first, -sin[:, idx], 0.0)
    sb = jnp.where(second, sin[:, idx], 0.0)
    return c, sa, sb


def _att_prep_kernel(x_ref, g_ref, w_ref, c64, a64, b64, c128, a128, b128,
                     qc_o, kc_o, kcb_o, vc_o, vcb_o, qd_o, kd_o, kdb_o, vd_o, vdb_o, qi_o, ki_o, kib_o, wi_o,
                     vct_o, vdt_o, p_ref):
    def rot(x, c, sa, sb, half):
        return x * c[...] + pltpu.roll(x, LANES - half, axis=1) * sa[...] + pltpu.roll(x, half, axis=1) * sb[...]

    p_ref[...] = _dot(_rms(x_ref[...], g_ref[...]).astype(BF16), w_ref[...])
    tm = p_ref.shape[0]
    h64 = D_IDX // ROT_FRAC // 2
    h128 = DH_D // ROT_FRAC // 2
    for t in range(4):
        sl = slice(t * LANES, (t + 1) * LANES)
        qc_o[:, sl] = (rot(p_ref[:, sl], c64, a64, b64, h64) * (DC ** -0.5)).astype(BF16)
        kc = rot(p_ref[:, 512 + t * LANES:512 + (t + 1) * LANES], c64, a64, b64, h64)
        kc_o[pl.ds(t, tm, stride=H_C), :] = kc
        kcb_o[:, sl] = kc.astype(BF16)
        vc = p_ref[:, 1024 + t * LANES:1024 + (t + 1) * LANES]
        vc_o[pl.ds(t, tm, stride=H_C), :] = vc
        vcb_o[:, sl] = vc.astype(BF16)
        vct_o[0, sl, :] = vc.T.astype(BF16)
        qd_o[:, sl] = rot(p_ref[:, 1536 + t * LANES:1536 + (t + 1) * LANES], c128, a128, b128, h128).astype(BF16)
    kd = rot(p_ref[:, 2048:2176], c128, a128, b128, h128)
    kd_o[...] = kd
    kdb_o[...] = kd.astype(BF16)
    vd = p_ref[:, 2176:2304]
    vd_o[...] = vd
    vdb_o[...] = vd.astype(BF16)
    vdt_o[0] = vd.T.astype(BF16)
    for t in range(2):
        qi = rot(p_ref[:, 2304 + t * LANES:2304 + (t + 1) * LANES], c64, a64, b64, h64) * (D_IDX ** -0.5)
        qi_o[:, (2 * t) * LANES:(2 * t + 1) * LANES] = qi.astype(BF16)
        qi_o[:, (2 * t + 1) * LANES:(2 * t + 2) * LANES] = pltpu.roll(qi, D_IDX, axis=1).astype(BF16)
    last = p_ref[:, 2560:2688]
    ki = rot(last, c64, a64, b64, h64)[:, :D_IDX]
    ki_o[...] = ki
    kib_o[...] = ki.astype(BF16)
    wi_o[...] = pltpu.roll(last, D_IDX, axis=1) * (H_I ** -0.5)


def att_prep(x, g, w, tabs64, tabs128, *, tm):
    m, d = x.shape
    p = tabs64[0].shape[0]
    assert m % tm == 0 and p % tm == 0
    nper = p // tm
    tab_spec = pl.BlockSpec((tm, LANES), lambda i: (i % nper, 0))
    outs = [(512, BF16, 1), (2 * DC, F32, H_C), (512, BF16, 1), (2 * DC, F32, H_C), (512, BF16, 1), (512, BF16, 1),
            (128, F32, 1), (128, BF16, 1), (128, F32, 1), (128, BF16, 1), (512, BF16, 1), (D_IDX, F32, 1),
            (D_IDX, BF16, 1), (128, F32, 1)]
    outs_t = [512, 128]
    return pl.pallas_call(
        _att_prep_kernel,
        grid=(m // tm,),
        in_specs=[pl.BlockSpec((tm, d), lambda i: (i, 0)),
                  pl.BlockSpec((1, d), lambda i: (0, 0)),
                  pl.BlockSpec((d, ATT_N), lambda i: (0, 0), pipeline_mode=pl.Buffered(1))] + [tab_spec] * 6,
        out_specs=([pl.BlockSpec((tm * r, w), lambda i: (i, 0)) for w, _, r in outs]
                   + [pl.BlockSpec((1, w, tm), lambda i: (i, 0, 0)) for w in outs_t]),
        out_shape=([jax.ShapeDtypeStruct((m * r, w), dt) for w, dt, r in outs]
                   + [jax.ShapeDtypeStruct((m // tm, w, tm), BF16) for w in outs_t]),
        scratch_shapes=[pltpu.VMEM((tm, ATT_N), F32)],
        compiler_params=_cparams(("parallel",)),
        name="att_prep",
    )(x, g.reshape(1, d), w, *tabs64, *tabs128)


QB = 128


def _diff_lambda(lam_ref, lam_init):
    dl = lam_ref[...]
    s1 = jnp.sum(dl[0:1, :] * dl[1:2, :], axis=1, keepdims=True)
    s2 = jnp.sum(dl[2:3, :] * dl[3:4, :], axis=1, keepdims=True)
    return jnp.exp(s1) - jnp.exp(s2) + lam_init


KB = 384
SUB = 8


def _group_max(x):
    return jnp.max(x.reshape(x.shape[0] // SUB, SUB, x.shape[1]), axis=0)


def _group_sum(x):
    return jnp.sum(x.reshape(x.shape[0] // SUB, SUB, x.shape[1]), axis=0)


def _key_visible(i, off, n_rep, front):
    krow = lax.broadcasted_iota(jnp.int32, (KB, 1), 0)
    lane = lax.broadcasted_iota(jnp.int32, (1, n_rep * KB), 1)
    q = lane
    for r in range(1, n_rep):
        q = jnp.where(lane >= r * KB, lane - r * KB, q)
    return ((i * KB + q - krow) >= off) & (krow >= front - off)


def _fori_pairs(lo, hi, one, two, init):
    n = jnp.maximum(hi - lo, 0)
    odd = n % 2
    carry = lax.fori_loop(0, odd, lambda t, c: one(lo, c), init)
    return lax.fori_loop(0, n // 2, lambda t, c: two(lo + odd + 2 * t, c), carry)


def _edge_then_middle(i, edge_body, middle_body, init):
    carry = lax.fori_loop(0, jnp.minimum(i, 1) + 1, lambda t, c: edge_body(t * i, c), init)
    return _fori_pairs(1, i, middle_body, lambda kb, c: middle_body(kb + 1, middle_body(kb, c)), carry)


def _diff_prompt_kernel(q_ref, k_ref, vt_ref, lam_ref, g_ref, o_ref, s_ref, *, front, lam_init):
    i = pl.program_id(1)
    lam = _diff_lambda(lam_ref, lam_init)
    lane = lax.broadcasted_iota(jnp.int32, (KB, LANES), 1)
    for h in range(H_C):
        cs = slice(h * LANES, (h + 1) * LANES)
        qh = q_ref[:, cs]
        qstack = jnp.concatenate([jnp.where(lane < DC, qh, jnp.zeros_like(qh)),
                                  jnp.where(lane >= DC, qh, jnp.zeros_like(qh))], axis=0)

        def scores(kb, masked):
            off = pl.multiple_of(kb * KB, KB)
            st = _dot_nt(k_ref[pl.ds(off, KB), cs], qstack)
            return jnp.where(_key_visible(i, off, 2, front), st, NEG) if masked else st

        def pass_a(kb, mx, masked):
            st = scores(kb, masked)
            s_ref[kb] = st
            return jnp.maximum(mx, _group_max(st))

        mx = _edge_then_middle(i, functools.partial(pass_a, masked=True), functools.partial(pass_a, masked=False),
                               jnp.full((SUB, 2 * KB), NEG, F32))
        m = jnp.max(mx, axis=0, keepdims=True)

        def pass_b(kb, carry):
            l8, acc = carry
            p = jnp.exp(s_ref[kb] - m)
            return l8 + _group_sum(p), acc + _dot(vt_ref[kb, cs, :], p.astype(BF16))

        def pass_b2(kb, carry):
            l8, acc = carry
            p = jnp.exp(jnp.concatenate([s_ref[kb], s_ref[kb + 1]], axis=0) - m)
            vt2 = jnp.concatenate([vt_ref[kb, cs, :], vt_ref[kb + 1, cs, :]], axis=1)
            return l8 + _group_sum(p), acc + _dot(vt2, p.astype(BF16))

        l8, acc = _fori_pairs(0, i + 1, pass_b, pass_b2,
                              (jnp.zeros((SUB, 2 * KB), F32), jnp.zeros((LANES, 2 * KB), F32)))
        a = acc / jnp.sum(l8, axis=0, keepdims=True)
        ot = a[:, :KB] - lam * a[:, KB:]
        ot = ot * lax.rsqrt(jnp.mean(ot * ot, axis=0, keepdims=True) + EPS) * g_ref[...] * (1.0 - lam_init)
        o_ref[:, cs] = ot.T.astype(o_ref.dtype)


def diff_prompt(q, k, vt, lam_p, g_c, *, b, t_pad, front, lam_init):
    nkb = t_pad // KB
    return pl.pallas_call(
        functools.partial(_diff_prompt_kernel, front=front, lam_init=lam_init),
        grid=(b, nkb),
        in_specs=[pl.BlockSpec((KB, 512), lambda bb, i: (bb * nkb + i, 0)),
                  pl.BlockSpec((t_pad, 512), lambda bb, i: (bb, 0)),
                  pl.BlockSpec((nkb, 512, KB), lambda bb, i: (bb, 0, 0)),
                  pl.BlockSpec((4, DC), lambda bb, i: (0, 0)),
                  pl.BlockSpec((2 * DC, 1), lambda bb, i: (0, 0))],
        out_specs=pl.BlockSpec((KB, 512), lambda bb, i: (bb * nkb + i, 0)),
        out_shape=jax.ShapeDtypeStruct((b * t_pad, 512), BF16),
        scratch_shapes=[pltpu.VMEM((nkb, KB, 2 * KB), F32)],
        compiler_params=_cparams(("parallel", "arbitrary")),
        name="diff_prompt",
    )(q, k, vt, lam_p, g_c.reshape(2 * DC, 1))


NINF = float("-inf")


def _kth_threshold(count_ge, shape, n_sel):
    zero_i = jnp.zeros(shape, jnp.int32)
    neg = jnp.where(count_ge(jnp.zeros(shape, F32)) < n_sel, 1, 0)
    sign = jnp.where(neg == 1, jnp.int32(INT_MIN), 0)

    def bit_body(t, mag):
        cand = mag | lax.shift_left(jnp.int32(1), 30 - t)
        enough = jnp.where(count_ge(pltpu.bitcast(cand | sign, F32)) >= n_sel, 1, 0)
        return jnp.where(enough + neg == 1, cand, mag)

    mag = lax.fori_loop(0, 31, bit_body, zero_i)
    tau = pltpu.bitcast(jnp.where(neg == 1, (mag + 1) | sign, mag), F32)
    ninf = jnp.full(shape, NINF, F32)
    return jnp.where(count_ge(ninf) >= n_sel, tau, ninf)


def _raise_to_kth_score(tau, count, min_where, n_sel):
    def cond(c):
        return jnp.max(c[1]) >= n_sel

    def body(c):
        tau, above = c
        tau = jnp.where(above >= n_sel, min_where(lambda sc: sc > tau), tau)
        return tau, count(lambda sc: sc > tau)

    return lax.while_loop(cond, body, (tau, count(lambda sc: sc > tau)))


def _dsa_prompt_kernel(qi_ref, wi_ref, qd_ref, ki_ref, kd_ref, vdt_ref, o_ref, sc_ref, sel_ref, s_ref, *, front, n_sel):
    i = pl.program_id(1)
    nkb = i + 1
    qi_all = jnp.concatenate([qi_ref[:, h * LANES:h * LANES + D_IDX] for h in range(H_I)], axis=0)
    wt = wi_ref[...].T
    w_row = jnp.concatenate([wt[h:h + 1, :] for h in range(H_I)], axis=1)

    def stage1(kb, c, masked):
        off = pl.multiple_of(kb * KB, KB)
        sct = jnp.maximum(_dot_nt(ki_ref[pl.ds(off, KB), :], qi_all), 0.0) * w_row
        score = sct[:, 0:KB] + sct[:, KB:2 * KB] + sct[:, 2 * KB:3 * KB] + sct[:, 3 * KB:4 * KB]
        sc_ref[kb] = jnp.where(_key_visible(i, off, 1, front), score, NINF) if masked else score
        return c

    _edge_then_middle(i, functools.partial(stage1, masked=True), functools.partial(stage1, masked=False), 0)

    def count(pred):
        def body(kb, acc):
            return acc + _group_sum(jnp.where(pred(sc_ref[kb]), 1, 0))
        return jnp.sum(lax.fori_loop(0, nkb, body, jnp.zeros((SUB, KB), jnp.int32)), axis=0, keepdims=True)

    def min_where(pred):
        def body(kb, acc):
            sc = sc_ref[kb]
            return jnp.minimum(acc, -_group_max(jnp.where(pred(sc), -sc, NINF)))
        return jnp.min(lax.fori_loop(0, nkb, body, jnp.full((SUB, KB), -NINF, F32)), axis=0, keepdims=True)

    tau = _kth_threshold(lambda cand: count(lambda sc: sc >= cand), (1, KB), n_sel)
    tau, above = _raise_to_kth_score(tau, count, min_where, n_sel)
    need = (n_sel - above).astype(F32)

    strict_lower = jnp.where(lax.broadcasted_iota(jnp.int32, (KB, KB), 1) < lax.broadcasted_iota(jnp.int32, (KB, KB), 0),
                             1.0, 0.0).astype(BF16)

    def select(kb, before):
        sc = sc_ref[kb]
        eq = sc == tau
        eqf = jnp.where(eq, 1.0, 0.0)
        rank = _dot(strict_lower, eqf.astype(BF16)) + before
        sel_ref[kb] = jnp.where(((sc > tau) | (eq & (rank < need))) & (sc > NINF), 0.0, NEG)
        return before + jnp.sum(eqf, axis=0, keepdims=True)

    lax.fori_loop(0, nkb, select, jnp.zeros((1, KB), F32))

    for h in range(H_D):
        cs = slice(h * LANES, (h + 1) * LANES)
        qd = qd_ref[:, cs]

        def stage3(kb, mx):
            off = pl.multiple_of(kb * KB, KB)
            sdt = _dot_nt(kd_ref[pl.ds(off, KB), :], qd) * (DH_D ** -0.5) + sel_ref[kb]
            s_ref[kb] = sdt
            return jnp.maximum(mx, _group_max(sdt))

        m = jnp.max(_fori_pairs(0, nkb, stage3, lambda kb, c: stage3(kb + 1, stage3(kb, c)),
                                jnp.full((SUB, KB), NEG, F32)), axis=0, keepdims=True)

        def stage4(kb, carry):
            l8, acc = carry
            p = jnp.exp(s_ref[kb] - m)
            return l8 + _group_sum(p), acc + _dot(vdt_ref[kb], p.astype(BF16))

        def stage4_pair(kb, carry):
            l8, acc = carry
            p = jnp.exp(jnp.concatenate([s_ref[kb], s_ref[kb + 1]], axis=0) - m)
            vt2 = jnp.concatenate([vdt_ref[kb], vdt_ref[kb + 1]], axis=1)
            return l8 + _group_sum(p), acc + _dot(vt2, p.astype(BF16))

        l8, acc = _fori_pairs(0, nkb, stage4, stage4_pair, (jnp.zeros((SUB, KB), F32), jnp.zeros((DH_D, KB), F32)))
        o_ref[:, cs] = (acc / jnp.sum(l8, axis=0, keepdims=True)).T.astype(o_ref.dtype)


def dsa_prompt(qi, wi, qd, ki, kd, vdt, *, b, t_pad, front, n_sel):
    nkb = t_pad // KB
    qspec = lambda w: pl.BlockSpec((KB, w), lambda bb, i: (bb * nkb + i, 0))
    kspec = lambda w: pl.BlockSpec((t_pad, w), lambda bb, i: (bb, 0))
    return pl.pallas_call(
        functools.partial(_dsa_prompt_kernel, front=front, n_sel=n_sel),
        grid=(b, nkb),
        in_specs=[qspec(512), qspec(128), qspec(512), kspec(D_IDX), kspec(DH_D),
                  pl.BlockSpec((nkb, DH_D, KB), lambda bb, i: (bb, 0, 0))],
        out_specs=qspec(512),
        out_shape=jax.ShapeDtypeStruct((b * t_pad, 512), BF16),
        scratch_shapes=[pltpu.VMEM((nkb, KB, KB), F32)] * 3,
        compiler_params=_cparams(("parallel", "arbitrary")),
        name="dsa_prompt",
    )(qi, wi, qd, ki, kd, vdt)


TS = 16


PG = 8
PAGE_BUFFERS = 2


def _page_map(g, n_pages, nd):
    def index(bb, j, pt):
        return (pt[bb * n_pages + jnp.minimum(j * PG + g, n_pages - 1)],) + (0,) * nd
    return index


def _online_softmax_update(s, vs, m_ref, l_ref, acc_ref):
    m_old = m_ref[...]
    m_new = jnp.maximum(m_old, jnp.max(s, axis=1, keepdims=True))
    alpha = jnp.exp(m_old - m_new)
    p = jnp.where(s > 0.5 * NEG, jnp.exp(s - m_new), 0.0)
    l_ref[...] = alpha * l_ref[...] + jnp.sum(p, axis=1, keepdims=True)
    r = s.shape[0] // len(vs)
    pv = jnp.concatenate([_dot(p[g * r:(g + 1) * r].astype(BF16), v) for g, v in enumerate(vs)], axis=0)
    acc_ref[...] = alpha * acc_ref[...] + pv
    m_ref[...] = m_new


def _sample_a_kernel(pt_ref, qc_ref, qi_ref, wi_ref, *refs, n_steps, front, lam_init):
    ck_refs, cv_refs, cik_refs = refs[:PG], refs[PG:2 * PG], refs[2 * PG:3 * PG]
    kn_ref, vn_ref, kin_ref, lam_ref, g_ref, o_ref, keys_ref, m_ref, l_ref, acc_ref = refs[3 * PG:]
    j = pl.program_id(1)
    lane = lax.broadcasted_iota(jnp.int32, (TS, LANES), 1)

    @pl.when(j == 0)
    def _():
        m_ref[...] = jnp.full(m_ref.shape, NEG, F32)
        l_ref[...] = jnp.zeros(l_ref.shape, F32)
        acc_ref[...] = jnp.zeros(acc_ref.shape, F32)

    def qstack(h):
        qh = qc_ref[:, h * LANES:(h + 1) * LANES]
        return jnp.concatenate([jnp.where(lane < DC, qh, jnp.zeros_like(qh)),
                                jnp.where(lane >= DC, qh, jnp.zeros_like(qh))], axis=0)

    def head_rows(h):
        return slice(h * 2 * TS, (h + 1) * 2 * TS)

    qi_all = jnp.concatenate([qi_ref[:, h * LANES:h * LANES + D_IDX] for h in range(H_I)], axis=0)
    w_col = jnp.concatenate([wi_ref[:, h:h + 1] for h in range(H_I)], axis=0)

    def idx_scores(kipt):
        sc = jnp.maximum(_dot(qi_all, kipt), 0.0) * w_col
        return sc[0:TS] + sc[TS:2 * TS] + sc[2 * TS:3 * TS] + sc[3 * TS:4 * TS]

    @pl.when(j < n_steps)
    def _():
        ss, vs = [], []
        for h in range(H_C):
            rows_h = pl.ds(h, PAGE_SIZE, stride=H_C)
            k = jnp.concatenate([r[0, rows_h, :].astype(BF16) for r in ck_refs], axis=0)
            vs.append(jnp.concatenate([r[0, rows_h, :].astype(BF16) for r in cv_refs], axis=0))
            ss.append(_dot_nt(qstack(h), k))
        _online_softmax_update(jnp.concatenate(ss, axis=0), vs, m_ref, l_ref, acc_ref)
        keys_ref[0] = idx_scores(jnp.concatenate([r[0].astype(BF16) for r in cik_refs], axis=1))

    @pl.when(j == n_steps)
    def _():
        kr = lax.broadcasted_iota(jnp.int32, (1, TS), 1)
        q_of_row = lax.broadcasted_iota(jnp.int32, (2 * TS, 1), 0) & (TS - 1)
        ok = (kr >= front) & (kr <= q_of_row)
        ss = [jnp.where(ok, _dot_nt(qstack(h), kn_ref[:, h * LANES:(h + 1) * LANES]), NEG) for h in range(H_C)]
        vs = [vn_ref[:, h * LANES:(h + 1) * LANES] for h in range(H_C)]
        _online_softmax_update(jnp.concatenate(ss, axis=0), vs, m_ref, l_ref, acc_ref)
        krp = lax.broadcasted_iota(jnp.int32, (1, PAGE_SIZE), 1)
        okq = (krp >= front) & (krp < TS) & (krp <= lax.broadcasted_iota(jnp.int32, (TS, 1), 0))
        keys_ref[0] = jnp.concatenate([jnp.where(okq, idx_scores(kin_ref[0]), NINF),
                                       jnp.full((TS, (PG - 1) * PAGE_SIZE), NINF, F32)], axis=1)
        lam = _diff_lambda(lam_ref, lam_init)
        for h in range(H_C):
            a = acc_ref[head_rows(h), :] / l_ref[head_rows(h), :]
            o = a[0:TS] - lam * a[TS:2 * TS]
            o_ref[:, h * LANES:(h + 1) * LANES] = (_rms(o, g_ref[...]) * (1.0 - lam_init)).astype(o_ref.dtype)


def sample_diff_idx(pt, qc, qi, wi, ck, cv, cik, kn, vn, kin, lam_p, g_c, *, b, n_pages, front, lam_init):
    assert n_pages % PG == 0
    n_steps = n_pages // PG
    rows = H_C * 2 * TS
    qspec = lambda w: pl.BlockSpec((TS, w), lambda bb, j, pt: (bb, 0))
    deep = pl.Buffered(PAGE_BUFFERS)
    pages4 = [pl.BlockSpec((1, PAGE_SIZE * H_C, 2 * DC), _page_map(g, n_pages, 2), pipeline_mode=deep) for g in range(PG)]
    pages_i = [pl.BlockSpec((1, D_IDX, PAGE_SIZE), _page_map(g, n_pages, 2), pipeline_mode=deep) for g in range(PG)]
    return pl.pallas_call(
        functools.partial(_sample_a_kernel, n_steps=n_steps, front=front, lam_init=lam_init),
        grid_spec=pltpu.PrefetchScalarGridSpec(
            num_scalar_prefetch=1,
            grid=(b, n_steps + 1),
            in_specs=[qspec(512), qspec(512), qspec(128)] + pages4 + pages4 + pages_i
                     + [qspec(512), qspec(512), pl.BlockSpec((1, D_IDX, PAGE_SIZE), lambda bb, j, pt: (bb, 0, 0)),
                        pl.BlockSpec((4, DC), lambda bb, j, pt: (0, 0)),
                        pl.BlockSpec((1, 2 * DC), lambda bb, j, pt: (0, 0))],
            out_specs=[qspec(512), pl.BlockSpec((1, TS, PG * PAGE_SIZE), lambda bb, j, pt: (bb, 0, j))],
            scratch_shapes=[pltpu.VMEM((rows, 1), F32), pltpu.VMEM((rows, 1), F32), pltpu.VMEM((rows, 2 * DC), F32)]),
        out_shape=[jax.ShapeDtypeStruct((b * TS, 512), BF16),
                   jax.ShapeDtypeStruct((b, TS, (n_steps + 1) * PG * PAGE_SIZE), F32)],
        compiler_params=_cparams(("parallel", "arbitrary")),
        name="sample_diff_idx",
    )(pt, qc, qi, wi, *([ck] * PG), *([cv] * PG), *([cik] * PG), kn, vn, kin, lam_p, g_c.reshape(1, 2 * DC))


def _sample_threshold_kernel(keys_ref, tau_ref, need_ref, *, n_sel):
    rows = keys_ref.shape[0]

    def count(pred):
        return jnp.sum(jnp.where(pred(keys_ref[...]), 1, 0), axis=1, keepdims=True)

    def min_where(pred):
        sc = keys_ref[...]
        return jnp.min(jnp.where(pred(sc), sc, -NINF), axis=1, keepdims=True)

    tau = _kth_threshold(lambda cand: count(lambda sc: sc >= cand), (rows, 1), n_sel)
    tau, above = _raise_to_kth_score(tau, count, min_where, n_sel)
    tau_ref[...] = tau
    need_ref[...] = (n_sel - above).astype(F32)


def sample_threshold(keys, *, n_sel):
    rows, nk = keys.shape
    return pl.pallas_call(
        functools.partial(_sample_threshold_kernel, n_sel=n_sel),
        grid=(1,),
        in_specs=[pl.BlockSpec((rows, nk), lambda i: (0, 0), pipeline_mode=pl.Buffered(1))],
        out_specs=[pl.BlockSpec((rows, 1), lambda i: (0, 0))] * 2,
        out_shape=[jax.ShapeDtypeStruct((rows, 1), F32)] * 2,
        compiler_params=_cparams(("arbitrary",)),
        name="sample_threshold",
    )(keys)


def _sample_b_kernel(pt_ref, tau_ref, need_ref, keys_ref, qd_ref, *refs, n_steps):
    ck_refs, cv_refs = refs[:PG], refs[PG:2 * PG]
    kn_ref, vn_ref, o_ref, before_ref, m_ref, l_ref, acc_ref = refs[2 * PG:]
    j = pl.program_id(1)

    @pl.when(j == 0)
    def _():
        before_ref[...] = jnp.zeros(before_ref.shape, F32)
        m_ref[...] = jnp.full(m_ref.shape, NEG, F32)
        l_ref[...] = jnp.zeros(l_ref.shape, F32)
        acc_ref[...] = jnp.zeros(acc_ref.shape, F32)

    strict_upper = jnp.where(lax.broadcasted_iota(jnp.int32, (PAGE_SIZE, PAGE_SIZE), 0)
                             < lax.broadcasted_iota(jnp.int32, (PAGE_SIZE, PAGE_SIZE), 1), 1.0, 0.0).astype(BF16)
    qd_all = jnp.concatenate([qd_ref[:, h * LANES:(h + 1) * LANES] for h in range(H_D)], axis=0)

    def select(key, before):
        tau = tau_ref[...]
        eq = key == tau
        eqf = jnp.where(eq, 1.0, 0.0)
        rank = _dot(eqf.astype(BF16), strict_upper) + before
        sel = ((key > tau) | (eq & (rank < need_ref[...]))) & (key > NINF)
        return sel, before + jnp.sum(eqf, axis=1, keepdims=True)

    def attend(s, sel, v):
        s = jnp.where(jnp.concatenate([sel] * H_D, axis=0), s * (DH_D ** -0.5), NEG)
        _online_softmax_update(s, [v], m_ref, l_ref, acc_ref)

    @pl.when(j < n_steps)
    def _():
        key = keys_ref[0]
        before = before_ref[...]
        sels = []
        for g in range(PG):
            sel, before = select(key[:, g * PAGE_SIZE:(g + 1) * PAGE_SIZE], before)
            sels.append(sel)
        before_ref[...] = before
        k = jnp.concatenate([r[0].astype(BF16) for r in ck_refs], axis=0)
        v = jnp.concatenate([r[0].astype(BF16) for r in cv_refs], axis=0)
        attend(_dot_nt(qd_all, k), jnp.concatenate(sels, axis=1), v)

    @pl.when(j == n_steps)
    def _():
        sel, _ = select(keys_ref[0][:, :PAGE_SIZE], before_ref[...])
        attend(_dot_nt(qd_all, kn_ref[0]), sel, vn_ref[0])
        o = acc_ref[...] / l_ref[...]
        for h in range(H_D):
            o_ref[:, h * LANES:(h + 1) * LANES] = o[h * TS:(h + 1) * TS].astype(o_ref.dtype)


def sample_dsa(pt, keys, qd, ck, cv, kn, vn, *, b, n_pages, n_sel):
    assert n_pages % PG == 0
    n_steps = n_pages // PG
    rows = H_D * TS
    tau, need = sample_threshold(keys.reshape(b * TS, keys.shape[2]), n_sel=n_sel)
    pages = [pl.BlockSpec((1, PAGE_SIZE, DH_D), _page_map(g, n_pages, 2), pipeline_mode=pl.Buffered(PAGE_BUFFERS))
             for g in range(PG)]
    nspec = pl.BlockSpec((1, PAGE_SIZE, DH_D), lambda bb, j, pt: (bb, 0, 0))
    col = pl.BlockSpec((TS, 1), lambda bb, j, pt: (bb, 0))
    return pl.pallas_call(
        functools.partial(_sample_b_kernel, n_steps=n_steps),
        grid_spec=pltpu.PrefetchScalarGridSpec(
            num_scalar_prefetch=1,
            grid=(b, n_steps + 1),
            in_specs=[col, col,
                      pl.BlockSpec((1, TS, PG * PAGE_SIZE), lambda bb, j, pt: (bb, 0, j)),
                      pl.BlockSpec((TS, 512), lambda bb, j, pt: (bb, 0))] + pages + pages + [nspec, nspec],
            out_specs=pl.BlockSpec((TS, 512), lambda bb, j, pt: (bb, 0)),
            scratch_shapes=[pltpu.VMEM((TS, 1), F32),
                            pltpu.VMEM((rows, 1), F32), pltpu.VMEM((rows, 1), F32), pltpu.VMEM((rows, DH_D), F32)]),
        out_shape=jax.ShapeDtypeStruct((b * TS, 512), BF16),
        compiler_params=_cparams(("parallel", "arbitrary")),
        name="sample_dsa",
    )(pt, tau, need, keys, qd, *([ck] * PG), *([cv] * PG), kn, vn)


REC_CHUNK = 64
REC_SUB = 16
TM_PROJ = 512
TM_REC, TN_REC = 1408, 1408
TM_ROWS = 384
TM_FFN = 528


def _pad_cols(w, n):
    return jnp.pad(w, ((0, 0), (0, n - w.shape[1])))


def _tile_rows(m, pref):
    return pref if m % pref == 0 else m


def kernel(x_prompt, x_sample, state_hgrn, state_mlstm_C, state_mlstm_n, state_mlstm_m, state_ffn_conv, cache_diff_k, cache_diff_v, cache_dsa_k, cache_dsa_v, cache_idx_k, page_table, meta_tokens, norm_gains, w_in_rec, b_gates_rec, lb_logits, g_norm_hgrn, g_norm_mlstm, w_out_rec, w_in_att, diff_lambda, g_norm_diff, w_out_att, w_ffn_up, ffn_conv_w, ffn_conv_b, w_ffn_down):
    bp, t_in, d = x_prompt.shape
    bs, t_s, _ = x_sample.shape
    depth = norm_gains.shape[0]
    n_pages = page_table.shape[1]
    past_len = n_pages * PAGE_SIZE
    real_p = N_META + t_in
    tp = -(-real_p // QB) * QB
    front_p = tp - real_p
    front_s = TS - t_s
    assert tp % REC_CHUNK == 0 and tp % TM_ROWS == 0 and front_p >= CONV_W - 1 and front_s >= CONV_W - 1
    mp, ms = bp * tp, bs * TS

    meta = jnp.broadcast_to(meta_tokens.astype(x_prompt.dtype)[None], (bp, N_META, d))
    xp = jnp.concatenate([jnp.zeros((bp, front_p, d), x_prompt.dtype), meta, x_prompt], axis=1).reshape(mp, d)
    xs = jnp.concatenate([jnp.zeros((bs, front_s, d), x_sample.dtype), x_sample], axis=1).reshape(ms, d)
    lb_all = jnp.cumsum(jax.nn.softmax(lb_logits.astype(F32), axis=0), axis=0)
    pt_flat = page_table.reshape(-1).astype(jnp.int32)
    sel_p = min(TOPK_MAX, t_in // 4)
    sel_s = min(TOPK_MAX, (past_len + t_s) // 4)
    tmp_s = _tile_rows(ms, TM_PROJ)
    tmr_s = _tile_rows(ms, TM_ROWS)

    pos_p = jnp.arange(tp, dtype=jnp.int32) - front_p
    pos_s = jnp.tile(past_len + jnp.arange(TS, dtype=jnp.int32) - front_s, ms // TS)
    tabs_p = (rope_tables(pos_p, DC, DC // ROT_FRAC // 2), rope_tables(pos_p, DH_D, DH_D // ROT_FRAC // 2))
    tabs_s = (rope_tables(pos_s, DC, DC // ROT_FRAC // 2), rope_tables(pos_s, DH_D, DH_D // ROT_FRAC // 2))

    rec_p, rec_s = [[], [], [], []], [[], [], [], []]
    att_p, att_s = [[], [], [], [], []], [[], [], [], [], []]
    conv_p, conv_s = [], []
    for l in range(depth):
        p = l // 2
        g = norm_gains[l].astype(F32)
        if l % 2 == 0:
            w_in = _pad_cols(w_in_rec[p], REC_N).astype(BF16)
            w_out = w_out_rec[p].astype(BF16)
            prm = (lb_all[p], b_gates_rec[p].astype(F32), g_norm_hgrn[p].astype(F32), g_norm_mlstm[p].astype(F32))
            proj = norm_matmul(xp, g[0], w_in, tm=_tile_rows(mp, TM_REC), tn=TN_REC).reshape(bp, tp, REC_N)
            zs = jnp.zeros((bp, 4, 128, 128), F32)
            y, *st = rec_mixer(proj, *prm, zs, zs, jnp.zeros((bp, 4, 128), F32), jnp.zeros((bp, 4), F32),
                               C=REC_CHUNK, W=REC_SUB, front=front_p)
            xp = matmul_norm_res([y.reshape(mp, -1)], [w_out], g[1], xp, tm=TM_ROWS, t_pad=tp, front=front_p)
            for j in range(4):
                rec_p[j].append(st[j])
            proj = norm_matmul(xs, g[0], w_in, tm=tmp_s, tn=384).reshape(bs, TS, REC_N)
            y, *st = rec_mixer(proj, *prm, state_hgrn[p].astype(F32), state_mlstm_C[p].astype(F32),
                               state_mlstm_n[p].astype(F32), state_mlstm_m[p].astype(F32), C=TS, W=TS, front=front_s)
            xs = matmul_norm_res([y.reshape(ms, -1)], [w_out], g[1], xs, tm=tmr_s, t_pad=TS, front=front_s)
            for j in range(4):
                rec_s[j].append(st[j])
        else:
            lam_init = 0.8 - 0.6 * math.exp(-0.3 * l)
            w_in = _pad_cols(w_in_att[p], ATT_N).astype(BF16)
            w_out = w_out_att[p].astype(BF16)
            dl, gc = diff_lambda[p].astype(F32), g_norm_diff[p].astype(F32)
            (qc, kc, kcb, vc, vcb, qd, kd, kdb, vd, vdb, qi, ki, kib, wi, vct, vdt) = att_prep(xp, g[0], w_in, *tabs_p, tm=KB)
            oc = diff_prompt(qc, kcb, vct, dl, gc, b=bp, t_pad=tp, front=front_p, lam_init=lam_init)
            od = dsa_prompt(qi, wi, qd, kib, kdb, vdt, b=bp, t_pad=tp, front=front_p, n_sel=sel_p)
            xp = matmul_norm_res([oc, od], [w_out[:512], w_out[512:]], g[1], xp, tm=TM_ROWS, t_pad=tp, front=front_p)
            for j, (a, shp) in enumerate([(kc, (H_C, 2 * DC)), (vc, (H_C, 2 * DC)), (kd, (DH_D,)), (vd, (DH_D,)), (ki, (D_IDX,))]):
                att_p[j].append(a.reshape((bp, tp) + shp)[:, front_p:])
            (qc, kc, kcb, vc, vcb, qd, kd, kdb, vd, vdb, qi, ki, kib, wi, _, _) = att_prep(xs, g[0], w_in, *tabs_s, tm=ms)
            as_page = lambda a: jnp.pad(a.reshape(bs, TS, -1), ((0, 0), (0, PAGE_SIZE - TS), (0, 0)))
            rows_kh = lambda c: c.reshape(c.shape[0], PAGE_SIZE * H_C, 2 * DC)
            oc, keys = sample_diff_idx(pt_flat, qc, qi, wi, rows_kh(cache_diff_k[p]), rows_kh(cache_diff_v[p]),
                                       jnp.swapaxes(cache_idx_k[p], 1, 2), kcb, vcb, jnp.swapaxes(as_page(kib), 1, 2), dl, gc,
                                       b=bs, n_pages=n_pages, front=front_s, lam_init=lam_init)
            od = sample_dsa(pt_flat, keys, qd, cache_dsa_k[p], cache_dsa_v[p], as_page(kdb), as_page(vdb),
                            b=bs, n_pages=n_pages, n_sel=sel_s)
            xs = matmul_norm_res([oc, od], [w_out[:512], w_out[512:]], g[1], xs, tm=tmr_s, t_pad=TS, front=front_s)
            for j, (a, shp) in enumerate([(kc, (H_C, 2 * DC)), (vc, (H_C, 2 * DC)), (kd, (DH_D,)), (vd, (DH_D,)), (ki, (D_IDX,))]):
                att_s[j].append(a.reshape((bs, TS) + shp)[:, front_s:])
        w_up, w_down = w_ffn_up[l].astype(BF16), w_ffn_down[l].astype(BF16)
        cw, cb = ffn_conv_w[l].astype(F32), ffn_conv_b[l].astype(F32)
        xp, tail = ffn_fused(xp, g[2], w_up, cw, cb, w_down, g[3], tm=TM_FFN, t_pad=tp, front=front_p)
        conv_p.append(tail[:, SUB - (CONV_W - 1):])
        up = norm_matmul(xs, g[2], w_up, tm=tmp_s, tn=512)
        conv_s.append(up.reshape(bs, TS, 2 * D_FF)[:, TS - (CONV_W - 1):, :D_FF])
        buf = jnp.pad(state_ffn_conv[l].astype(F32), ((0, 0), (front_s - (CONV_W - 1), TS - front_s), (0, 0)))
        xs = ffn_down(up, buf.reshape(ms, D_FF), cw, cb, w_down, g[3], xs, tm=tmr_s, t_pad=TS, front=front_s)

    y_p = xp.reshape(bp, tp, d)[:, front_p + N_META:]
    y_s = xs.reshape(bs, TS, d)[:, front_s:]
    stack = lambda xs: xs[0][None] if len(xs) == 1 else jnp.stack(xs)
    return (y_p, y_s,
            stack(rec_p[0]), stack(rec_s[0]), stack(rec_p[1]), stack(rec_s[1]),
            stack(rec_p[2]), stack(rec_s[2]), stack(rec_p[3]), stack(rec_s[3]),
            stack(conv_p), stack(conv_s),
            stack(att_p[0]), stack(att_s[0]), stack(att_p[1]), stack(att_s[1]),
            stack(att_p[2]), stack(att_s[2]), stack(att_p[3]), stack(att_s[3]),
            stack(att_p[4]), stack(att_s[4]))
```

```python
import functools
import math

import jax
import jax.numpy as jnp
import numpy as np
from jax import lax
from jax.experimental import pallas as pl
from jax.experimental.pallas import tpu as pltpu

F32 = jnp.float32
BF16 = jnp.bfloat16

D_MODEL = 1024
N_META = 16
H_A, DK_A, DV_A = 4, 128, 128
H_B, DK_B, DV_B = 4, 128, 128
H_C, DC = 4, 64
H_D, DH_D = 4, 128
H_I, D_IDX = 4, 64
TOPK_MAX = 256
D_FF = 2816
CONV_W = 3
ROPE_THETA = 500000.0
ROT_FRAC = 4
EPS = 1e-6
PAGE_SIZE = 128
LANES = 128
NEG = -1e30

REC_N = 8 * 512 + LANES
ATT_N = 2560 + LANES
INT_MIN = -2 ** 31
VMEM_LIMIT = 56 * 1024 * 1024


def _cparams(sem):
    return pltpu.CompilerParams(dimension_semantics=sem, vmem_limit_bytes=VMEM_LIMIT)


def _rms(x, g):
    return x * lax.rsqrt(jnp.mean(x * x, axis=-1, keepdims=True) + EPS) * g


def _dot(a, b):
    return jnp.dot(a, b, preferred_element_type=F32)


def _dot_nt(a, b):
    return lax.dot_general(a, b, (((1,), (1,)), ((), ())), preferred_element_type=F32)


def _dot_tn(a, b):
    return lax.dot_general(a, b, (((0,), (0,)), ((), ())), preferred_element_type=F32)


def _dot_exact_lhs(tri, x):
    hi = x.astype(BF16)
    r1 = x - hi.astype(F32)
    mid = r1.astype(BF16)
    lo = (r1 - mid.astype(F32)).astype(BF16)
    return _dot(tri, hi) + _dot(tri, mid) + _dot(tri, lo)


def _row_valid(i, tm, t_pad, front):
    r = lax.broadcasted_iota(jnp.int32, (tm, 1), 0)
    if t_pad % tm == 0:
        t = (i % (t_pad // tm)) * tm + r
    else:
        assert tm % t_pad == 0 and (t_pad & (t_pad - 1)) == 0
        t = r & (t_pad - 1)
    return t >= front


def _norm_matmul_kernel(x_ref, g_ref, w_ref, o_ref, h_ref):
    @pl.when(pl.program_id(1) == 0)
    def _():
        h_ref[...] = _rms(x_ref[...], g_ref[...]).astype(BF16)

    o_ref[...] = _dot(h_ref[...], w_ref[...])


def norm_matmul(x, g, w, *, tm, tn):
    m, d = x.shape
    n = w.shape[1]
    assert m % tm == 0 and n % tn == 0
    return pl.pallas_call(
        _norm_matmul_kernel,
        grid=(m // tm, n // tn),
        in_specs=[pl.BlockSpec((tm, d), lambda i, j: (i, 0)),
                  pl.BlockSpec((1, d), lambda i, j: (0, 0)),
                  pl.BlockSpec((d, tn), lambda i, j: (0, j))],
        out_specs=pl.BlockSpec((tm, tn), lambda i, j: (i, j)),
        out_shape=jax.ShapeDtypeStruct((m, n), F32),
        scratch_shapes=[pltpu.VMEM((tm, d), BF16)],
        compiler_params=_cparams(("parallel", "arbitrary")),
        name="norm_matmul",
    )(x, g.reshape(1, d), w)


def _matmul_norm_res_kernel(*refs, n_in, tm, t_pad, front):
    a_refs, w_refs = refs[:n_in], refs[n_in:2 * n_in]
    g_ref, x_ref, o_ref = refs[2 * n_in:]
    acc = _dot(a_refs[0][...], w_refs[0][...])
    for a, w in zip(a_refs[1:], w_refs[1:]):
        acc = acc + _dot(a[...], w[...])
    out = x_ref[...] + _rms(acc, g_ref[...])
    o_ref[...] = jnp.where(_row_valid(pl.program_id(0), tm, t_pad, front), out, 0.0)


def matmul_norm_res(a_list, w_list, g, x, *, tm, t_pad, front):
    m, d = x.shape
    assert m % tm == 0
    n_in = len(a_list)
    in_specs = ([pl.BlockSpec((tm, a.shape[1]), lambda i: (i, 0)) for a in a_list]
                + [pl.BlockSpec(w.shape, lambda i: (0, 0)) for w in w_list]
                + [pl.BlockSpec((1, d), lambda i: (0, 0)), pl.BlockSpec((tm, d), lambda i: (i, 0))])
    return pl.pallas_call(
        functools.partial(_matmul_norm_res_kernel, n_in=n_in, tm=tm, t_pad=t_pad, front=front),
        grid=(m // tm,),
        in_specs=in_specs,
        out_specs=pl.BlockSpec((tm, d), lambda i: (i, 0)),
        out_shape=jax.ShapeDtypeStruct((m, d), F32),
        compiler_params=_cparams(("parallel",)),
        name="matmul_norm_res",
    )(*a_list, *w_list, g.reshape(1, d), x)


def _ffn_down_kernel(ug_ref, uv_ref, buf_ref, cw_ref, cb_ref, w_ref, g_ref, x_ref, o_ref, *, tm, t_pad, front):
    t = lax.broadcasted_iota(jnp.int32, (tm, 1), 0) & (t_pad - 1)
    ug = jnp.where((t >= front - (CONV_W - 1)) & (t < front), buf_ref[...], ug_ref[...])
    conv = (cb_ref[...] + cw_ref[0:1, :] * pltpu.roll(ug, 2, axis=0) + cw_ref[1:2, :] * pltpu.roll(ug, 1, axis=0)
            + cw_ref[2:3, :] * ug)
    act = (conv * jax.nn.sigmoid(conv) * uv_ref[...]).astype(BF16)
    out = x_ref[...] + _rms(_dot(act, w_ref[...]), g_ref[...])
    o_ref[...] = jnp.where(t >= front, out, 0.0)


def ffn_down(up, buf, cw, cb, w_down, g, x, *, tm, t_pad, front):
    m, d = x.shape
    f = w_down.shape[0]
    assert m % tm == 0 and tm % t_pad == 0 and (t_pad & (t_pad - 1)) == 0 and front >= CONV_W - 1
    assert up.shape == (m, 2 * f) and buf.shape == (m, f)
    return pl.pallas_call(
        functools.partial(_ffn_down_kernel, tm=tm, t_pad=t_pad, front=front),
        grid=(m // tm,),
        in_specs=[pl.BlockSpec((tm, f), lambda i: (i, 0)),
                  pl.BlockSpec((tm, f), lambda i: (i, 1)),
                  pl.BlockSpec((tm, f), lambda i: (i, 0)),
                  pl.BlockSpec((CONV_W, f), lambda i: (0, 0)),
                  pl.BlockSpec((1, f), lambda i: (0, 0)),
                  pl.BlockSpec((f, d), lambda i: (0, 0)),
                  pl.BlockSpec((1, d), lambda i: (0, 0)),
                  pl.BlockSpec((tm, d), lambda i: (i, 0))],
        out_specs=pl.BlockSpec((tm, d), lambda i: (i, 0)),
        out_shape=jax.ShapeDtypeStruct((m, d), F32),
        compiler_params=_cparams(("parallel",)),
        name="ffn_down",
    )(up, up, buf, cw, cb.reshape(1, f), w_down, g.reshape(1, d), x)


FF_CW = 256
FF_HALO = 16


def _ffn_fused_kernel(x_ref, halo_ref, gin_ref, wup_ref, cw_ref, cb_ref, wdn_ref, gout_ref, o_ref, conv_ref, acc_ref,
                      *, tm, t_pad, front):
    i = pl.program_id(0)
    x = x_ref[...]
    h = jnp.concatenate([_rms(halo_ref[...], gin_ref[...]), _rms(x, gin_ref[...])], axis=0).astype(BF16)
    for c in range(D_FF // FF_CW):
        cs = slice(c * FF_CW, (c + 1) * FF_CW)
        ug = _dot(h, wup_ref[:, cs])
        uv = _dot(h[FF_HALO:], wup_ref[:, D_FF + c * FF_CW:D_FF + (c + 1) * FF_CW])
        prev1 = pltpu.roll(ug, 1, axis=0)[FF_HALO:]
        prev2 = pltpu.roll(ug, 2, axis=0)[FF_HALO:]
        conv = cb_ref[:, cs] + cw_ref[0:1, cs] * prev2 + cw_ref[1:2, cs] * prev1 + cw_ref[2:3, cs] * ug[FF_HALO:]
        act = (conv * jax.nn.sigmoid(conv) * uv).astype(BF16)
        part = _dot(act, wdn_ref[cs, :])
        if c == 0:
            acc_ref[...] = part
        else:
            acc_ref[...] += part
        conv_ref[0, :, cs] = ug[FF_HALO + tm - SUB:, :]
    out = x + _rms(acc_ref[...], gout_ref[...])
    o_ref[...] = jnp.where(_row_valid(i, tm, t_pad, front), out, 0.0)


def ffn_fused(x, g_in, w_up, cw, cb, w_down, g_out, *, tm, t_pad, front):
    m, d = x.shape
    assert m % tm == 0 and t_pad % tm == 0 and tm % FF_HALO == 0 and D_FF % FF_CW == 0
    per_seq = t_pad // tm
    hb = tm // FF_HALO
    const = lambda shape: pl.BlockSpec(shape, lambda i: (0,) * len(shape), pipeline_mode=pl.Buffered(1))
    return pl.pallas_call(
        functools.partial(_ffn_fused_kernel, tm=tm, t_pad=t_pad, front=front),
        grid=(m // tm,),
        in_specs=[pl.BlockSpec((tm, d), lambda i: (i, 0)),
                  pl.BlockSpec((FF_HALO, d), lambda i: (jnp.maximum(i * hb - 1, 0), 0)),
                  const((1, d)), const((d, 2 * D_FF)), const((CONV_W, D_FF)), const((1, D_FF)),
                  const((D_FF, d)), const((1, d))],
        out_specs=[pl.BlockSpec((tm, d), lambda i: (i, 0)),
                   pl.BlockSpec((1, SUB, D_FF), lambda i: (i // per_seq, 0, 0))],
        out_shape=[jax.ShapeDtypeStruct((m, d), F32), jax.ShapeDtypeStruct((m // t_pad, SUB, D_FF), F32)],
        scratch_shapes=[pltpu.VMEM((tm, d), F32)],
        compiler_params=_cparams(("arbitrary",)),
        name="ffn_fused",
    )(x, x, g_in.reshape(1, d), w_up, cw, cb.reshape(1, D_FF), w_down, g_out.reshape(1, d))


def _log_sigmoid(x):
    return jnp.minimum(x, 0.0) - jnp.log1p(jnp.exp(-jnp.abs(x)))


def _rec_kernel(proj_ref, lb_ref, bg_ref, ga_ref, gb_ref, s0_ref, c0_ref, n0_ref, m0_ref,
                y_ref, s_ref, c_ref, n_ref, m_ref, st_ref, *, C, W, front):
    ci = pl.program_id(1)
    nci = pl.num_programs(1)

    @pl.when(ci == 0)
    def _():
        for h in range(H_A):
            st_ref[h] = s0_ref[0, h].T
        c_ref[...] = c0_ref[...]
        n_ref[...] = n0_ref[...]
        m_ref[...] = m0_ref[...]

    row = ci * C + lax.broadcasted_iota(jnp.int32, (C, 1), 0)
    valid = row >= front
    r_i = lax.broadcasted_iota(jnp.int32, (C, C), 0)
    c_i = lax.broadcasted_iota(jnp.int32, (C, C), 1)
    causal = r_i >= c_i
    tri = jnp.where(causal, 1.0, 0.0).astype(BF16)

    gates = proj_ref[0, :, 8 * 512:8 * 512 + LANES] + bg_ref[...]
    lf_all = jnp.where(valid, _log_sigmoid(gates), 0.0)
    ig_all = jnp.where(valid, gates, NEG)
    b_all = _dot_exact_lhs(tri, lf_all)
    b_all_t = b_all.T
    ig_all_t = ig_all.T

    qk_dots, qc_dots = [], []
    for h in range(H_B):
        qb = proj_ref[0, :, 2048 + h * 128:2048 + (h + 1) * 128].astype(BF16)
        kb = (proj_ref[0, :, 2560 + h * 128:2560 + (h + 1) * 128] * (DK_B ** -0.5)).astype(BF16)
        qk_dots.append(_dot_nt(qb, kb))
        qc_dots.append(_dot(qb, c_ref[0, h].astype(BF16)))

    for h in range(H_B):
        q = proj_ref[0, :, 2048 + h * 128:2048 + (h + 1) * 128]
        k = proj_ref[0, :, 2560 + h * 128:2560 + (h + 1) * 128] * (DK_B ** -0.5)
        v = proj_ref[0, :, 3072 + h * 128:3072 + (h + 1) * 128]
        og = proj_ref[0, :, 3584 + h * 128:3584 + (h + 1) * 128]
        vb = v.astype(BF16)
        b_col = b_all[:, H_B + h:H_B + h + 1]
        b_row = b_all_t[H_B + h:H_B + h + 1, :]
        i_col = ig_all[:, h:h + 1]
        i_row = ig_all_t[h:h + 1, :]
        m_prev = m_ref[0, h:h + 1, 0:1]
        dmat = jnp.where(causal, b_col - b_row + i_row, NEG)
        inter = b_col + m_prev
        mt = jnp.maximum(inter, jnp.max(dmat, axis=1, keepdims=True))
        w = jnp.exp(dmat - mt) * qk_dots[h]
        wi = jnp.exp(inter - mt)
        c_st = c_ref[0, h]
        n_st = n_ref[0, h:h + 1, :]
        num = wi * qc_dots[h] + _dot(w.astype(BF16), vb)
        den = wi * jnp.sum(q * n_st, axis=1, keepdims=True) + jnp.sum(w, axis=1, keepdims=True)
        hc = num / jnp.maximum(jnp.abs(den), jnp.exp(-mt))
        m_new = mt[C - 1:C, :]
        b_last = b_col[C - 1:C, :]
        decay = jnp.exp(b_last + m_prev - m_new)
        kw = k * jnp.exp(b_last - b_col + i_col - m_new)
        c_ref[0, h] = decay * c_st + _dot_tn(kw.astype(BF16), vb)
        n_ref[0, h:h + 1, :] = decay * n_st + jnp.sum(kw, axis=0, keepdims=True)
        m_ref[0, h:h + 1, :] = jnp.broadcast_to(m_new, (1, LANES))
        yb = _rms(hc, gb_ref[...]) * jax.nn.sigmoid(og)
        y_ref[0, :, 512 + h * 128:512 + (h + 1) * 128] = jnp.where(valid, yb, 0.0).astype(y_ref.dtype)

    nj = C // W
    rw = lax.broadcasted_iota(jnp.int32, (W, 1), 0)
    ones_sq = jnp.ones((DK_A, LANES), BF16)
    lbv = lb_ref[...]
    f = lbv + (1.0 - lbv) * jax.nn.sigmoid(proj_ref[0, :, 512:1024])
    logf = jnp.where(valid, jnp.log(f), 0.0)
    kk = jnp.where(valid, 1.0 - f, 0.0)
    qq = proj_ref[0, :, 0:512] * (DK_A ** -0.5)
    vv = proj_ref[0, :, 1024:1536]
    bfull = _dot_exact_lhs(tri, logf)
    bcs = []
    for j in range(nj):
        rows = slice(j * W, (j + 1) * W)
        bcs.append(bfull[rows] if j == 0 else bfull[rows] - bfull[j * W - 1:j * W])

    heads = [slice(h * 128, (h + 1) * 128) for h in range(H_A)]
    row_sums, outer, decay_last = {}, {}, []
    for j in range(nj):
        rows = slice(j * W, (j + 1) * W)
        bc = bcs[j]
        last = bc[W - 1:W, :]
        decay_last.append(jnp.exp(last))
        kdec = (kk[rows] * jnp.exp(last - bc)).astype(BF16)
        gs = []
        for s in range(W):
            e = jnp.exp(jnp.where(rw >= s, bc - bc[s:s + 1, :], NEG))
            gs.append(qq[rows] * kk[j * W + s:j * W + s + 1, :] * e)
        g = jnp.concatenate(gs, axis=0).astype(BF16)
        for h, cs in enumerate(heads):
            row_sums[h, j] = _dot(g[:, cs], ones_sq)
            outer[h, j] = _dot_tn(vv[rows, cs].astype(BF16), kdec[:, cs])

    from_state = {}
    for h, cs in enumerate(heads):
        st = st_ref[h]
        for j in range(nj):
            rows = slice(j * W, (j + 1) * W)
            from_state[h, j] = _dot_nt((qq[rows, cs] * jnp.exp(bcs[j][:, cs])).astype(BF16), st.astype(BF16))
            st = decay_last[j][:, cs] * st + outer[h, j]
        st_ref[h] = st

    for h, cs in enumerate(heads):
        for j in range(nj):
            rows = slice(j * W, (j + 1) * W)
            o = from_state[h, j]
            for s in range(W):
                o = o + row_sums[h, j][s * W:(s + 1) * W] * vv[j * W + s:j * W + s + 1, cs]
            ga = proj_ref[0, rows, 1536 + h * 128:1536 + (h + 1) * 128]
            ya = _rms(o, ga_ref[...]) * (ga * jax.nn.sigmoid(ga))
            vld = (ci * C + j * W + rw) >= front
            y_ref[0, rows, cs] = jnp.where(vld, ya, 0.0).astype(y_ref.dtype)

    @pl.when(ci == nci - 1)
    def _():
        for h in range(H_A):
            s_ref[0, h] = st_ref[h].T


def rec_mixer(proj, lb, bg, g_a, g_b, s0, c0, n0, m0, *, C, W, front):
    b, t, _ = proj.shape
    assert t % C == 0 and C % W == 0
    m0b = jnp.broadcast_to(m0[:, :, None], (b, H_B, LANES))
    bgp = jnp.zeros((1, LANES), F32).at[0, :2 * H_B].set(bg.reshape(-1))
    st_spec = pl.BlockSpec((1, 4, 128, 128), lambda i, c: (i, 0, 0, 0))
    v_spec = pl.BlockSpec((1, 4, LANES), lambda i, c: (i, 0, 0))
    row_spec = lambda n: pl.BlockSpec((1, n), lambda i, c: (0, 0))
    y, s, cc, n, m = pl.pallas_call(
        functools.partial(_rec_kernel, C=C, W=W, front=front),
        grid=(b, t // C),
        in_specs=[pl.BlockSpec((1, C, REC_N), lambda i, c: (i, c, 0)),
                  row_spec(512), row_spec(LANES), row_spec(128), row_spec(128),
                  st_spec, st_spec, v_spec, v_spec],
        out_specs=[pl.BlockSpec((1, C, 1024), lambda i, c: (i, c, 0)), st_spec, st_spec, v_spec, v_spec],
        out_shape=[jax.ShapeDtypeStruct((b, t, 1024), BF16),
                   jax.ShapeDtypeStruct((b, 4, 128, 128), F32),
                   jax.ShapeDtypeStruct((b, 4, 128, 128), F32),
                   jax.ShapeDtypeStruct((b, 4, LANES), F32),
                   jax.ShapeDtypeStruct((b, 4, LANES), F32)],
        scratch_shapes=[pltpu.VMEM((4, 128, 128), F32)],
        compiler_params=_cparams(("parallel", "arbitrary")),
        name="rec_mixer",
    )(proj, lb.reshape(1, 512), bgp, g_a.reshape(1, 128), g_b.reshape(1, 128), s0, c0, n0, m0b)
    return y, s, cc, n, m[:, :, 0]


def rope_tables(pos, period, half):
    r = 2 * half
    inv = ROPE_THETA ** (-jnp.arange(half, dtype=F32) * 2.0 / r)
    ang = pos.astype(F32)[:, None] * inv[None, :]
    cos, sin = jnp.cos(ang), jnp.sin(ang)
    lane = np.arange(LANES) % period
    idx = np.where(lane < half, lane, np.where(lane < r, lane - half, 0))
    first, second = jnp.asarray(lane < half), jnp.asarray((lane >= half) & (lane < r))
    c = jnp.where(first | second, cos[:, idx], 1.0)
    sa = jnp.where(first, -sin[:, idx], 0.0)
    sb = jnp.where(second, sin[:, idx], 0.0)
    return c, sa, sb


def _att_prep_kernel(x_ref, g_ref, w_ref, c64, a64, b64, c128, a128, b128,
                     qc_o, kc_o, kcb_o, vc_o, vcb_o, qd_o, kd_o, kdb_o, vd_o, vdb_o, qi_o, ki_o, kib_o, wi_o,
                     vct_o, vdt_o, p_ref):
    def rot(x, c, sa, sb, half):
        return x * c[...] + pltpu.roll(x, LANES - half, axis=1) * sa[...] + pltpu.roll(x, half, axis=1) * sb[...]

    p_ref[...] = _dot(_rms(x_ref[...], g_ref[...]).astype(BF16), w_ref[...])
    tm = p_ref.shape[0]
    h64 = D_IDX // ROT_FRAC // 2
    h128 = DH_D // ROT_FRAC // 2
    for t in range(4):
        sl = slice(t * LANES, (t + 1) * LANES)
        qc_o[:, sl] = (rot(p_ref[:, sl], c64, a64, b64, h64) * (DC ** -0.5)).astype(BF16)
        kc = rot(p_ref[:, 512 + t * LANES:512 + (t + 1) * LANES], c64, a64, b64, h64)
        kc_o[pl.ds(t, tm, stride=H_C), :] = kc
        kcb_o[:, sl] = kc.astype(BF16)
        vc = p_ref[:, 1024 + t * LANES:1024 + (t + 1) * LANES]
        vc_o[pl.ds(t, tm, stride=H_C), :] = vc
        vcb_o[:, sl] = vc.astype(BF16)
        vct_o[0, sl, :] = vc.T.astype(BF16)
        qd_o[:, sl] = rot(p_ref[:, 1536 + t * LANES:1536 + (t + 1) * LANES], c128, a128, b128, h128).astype(BF16)
    kd = rot(p_ref[:, 2048:2176], c128, a128, b128, h128)
    kd_o[...] = kd
    kdb_o[...] = kd.astype(BF16)
    vd = p_ref[:, 2176:2304]
    vd_o[...] = vd
    vdb_o[...] = vd.astype(BF16)
    vdt_o[0] = vd.T.astype(BF16)
    for t in range(2):
        qi = rot(p_ref[:, 2304 + t * LANES:2304 + (t + 1) * LANES], c64, a64, b64, h64) * (D_IDX ** -0.5)
        qi_o[:, (2 * t) * LANES:(2 * t + 1) * LANES] = qi.astype(BF16)
        qi_o[:, (2 * t + 1) * LANES:(2 * t + 2) * LANES] = pltpu.roll(qi, D_IDX, axis=1).astype(BF16)
    last = p_ref[:, 2560:2688]
    ki = rot(last, c64, a64, b64, h64)[:, :D_IDX]
    ki_o[...] = ki
    kib_o[...] = ki.astype(BF16)
    wi_o[...] = pltpu.roll(last, D_IDX, axis=1) * (H_I ** -0.5)


def att_prep(x, g, w, tabs64, tabs128, *, tm):
    m, d = x.shape
    p = tabs64[0].shape[0]
    assert m % tm == 0 and p % tm == 0
    nper = p // tm
    tab_spec = pl.BlockSpec((tm, LANES), lambda i: (i % nper, 0))
    outs = [(512, BF16, 1), (2 * DC, F32, H_C), (512, BF16, 1), (2 * DC, F32, H_C), (512, BF16, 1), (512, BF16, 1),
            (128, F32, 1), (128, BF16, 1), (128, F32, 1), (128, BF16, 1), (512, BF16, 1), (D_IDX, F32, 1),
            (D_IDX, BF16, 1), (128, F32, 1)]
    outs_t = [512, 128]
    return pl.pallas_call(
        _att_prep_kernel,
        grid=(m // tm,),
        in_specs=[pl.BlockSpec((tm, d), lambda i: (i, 0)),
                  pl.BlockSpec((1, d), lambda i: (0, 0)),
                  pl.BlockSpec((d, ATT_N), lambda i: (0, 0), pipeline_mode=pl.Buffered(1))] + [tab_spec] * 6,
        out_specs=([pl.BlockSpec((tm * r, w), lambda i: (i, 0)) for w, _, r in outs]
                   + [pl.BlockSpec((1, w, tm), lambda i: (i, 0, 0)) for w in outs_t]),
        out_shape=([jax.ShapeDtypeStruct((m * r, w), dt) for w, dt, r in outs]
                   + [jax.ShapeDtypeStruct((m // tm, w, tm), BF16) for w in outs_t]),
        scratch_shapes=[pltpu.VMEM((tm, ATT_N), F32)],
        compiler_params=_cparams(("parallel",)),
        name="att_prep",
    )(x, g.reshape(1, d), w, *tabs64, *tabs128)


QB = 128


def _diff_lambda(lam_ref, lam_init):
    dl = lam_ref[...]
    s1 = jnp.sum(dl[0:1, :] * dl[1:2, :], axis=1, keepdims=True)
    s2 = jnp.sum(dl[2:3, :] * dl[3:4, :], axis=1, keepdims=True)
    return jnp.exp(s1) - jnp.exp(s2) + lam_init


KB = 384
SUB = 8


def _group_max(x):
    return jnp.max(x.reshape(x.shape[0] // SUB, SUB, x.shape[1]), axis=0)


def _group_sum(x):
    return jnp.sum(x.reshape(x.shape[0] // SUB, SUB, x.shape[1]), axis=0)


def _key_visible(i, off, n_rep, front):
    krow = lax.broadcasted_iota(jnp.int32, (KB, 1), 0)
    lane = lax.broadcasted_iota(jnp.int32, (1, n_rep * KB), 1)
    q = lane
    for r in range(1, n_rep):
        q = jnp.where(lane >= r * KB, lane - r * KB, q)
    return ((i * KB + q - krow) >= off) & (krow >= front - off)


def _fori_pairs(lo, hi, one, two, init):
    n = jnp.maximum(hi - lo, 0)
    odd = n % 2
    carry = lax.fori_loop(0, odd, lambda t, c: one(lo, c), init)
    return lax.fori_loop(0, n // 2, lambda t, c: two(lo + odd + 2 * t, c), carry)


def _edge_then_middle(i, edge_body, middle_body, init):
    carry = lax.fori_loop(0, jnp.minimum(i, 1) + 1, lambda t, c: edge_body(t * i, c), init)
    return _fori_pairs(1, i, middle_body, lambda kb, c: middle_body(kb + 1, middle_body(kb, c)), carry)


def _diff_prompt_kernel(q_ref, k_ref, vt_ref, lam_ref, g_ref, o_ref, s_ref, *, front, lam_init):
    i = pl.program_id(1)
    lam = _diff_lambda(lam_ref, lam_init)
    lane = lax.broadcasted_iota(jnp.int32, (KB, LANES), 1)
    for h in range(H_C):
        cs = slice(h * LANES, (h + 1) * LANES)
        qh = q_ref[:, cs]
        qstack = jnp.concatenate([jnp.where(lane < DC, qh, jnp.zeros_like(qh)),
                                  jnp.where(lane >= DC, qh, jnp.zeros_like(qh))], axis=0)

        def scores(kb, masked):
            off = pl.multiple_of(kb * KB, KB)
            st = _dot_nt(k_ref[pl.ds(off, KB), cs], qstack)
            return jnp.where(_key_visible(i, off, 2, front), st, NEG) if masked else st

        def pass_a(kb, mx, masked):
            st = scores(kb, masked)
            s_ref[kb] = st
            return jnp.maximum(mx, _group_max(st))

        mx = _edge_then_middle(i, functools.partial(pass_a, masked=True), functools.partial(pass_a, masked=False),
                               jnp.full((SUB, 2 * KB), NEG, F32))
        m = jnp.max(mx, axis=0, keepdims=True)

        def pass_b(kb, carry):
            l8, acc = carry
            p = jnp.exp(s_ref[kb] - m)
            return l8 + _group_sum(p), acc + _dot(vt_ref[kb, cs, :], p.astype(BF16))

        def pass_b2(kb, carry):
            l8, acc = carry
            p = jnp.exp(jnp.concatenate([s_ref[kb], s_ref[kb + 1]], axis=0) - m)
            vt2 = jnp.concatenate([vt_ref[kb, cs, :], vt_ref[kb + 1, cs, :]], axis=1)
            return l8 + _group_sum(p), acc + _dot(vt2, p.astype(BF16))

        l8, acc = _fori_pairs(0, i + 1, pass_b, pass_b2,
                              (jnp.zeros((SUB, 2 * KB), F32), jnp.zeros((LANES, 2 * KB), F32)))
        a = acc / jnp.sum(l8, axis=0, keepdims=True)
        ot = a[:, :KB] - lam * a[:, KB:]
        ot = ot * lax.rsqrt(jnp.mean(ot * ot, axis=0, keepdims=True) + EPS) * g_ref[...] * (1.0 - lam_init)
        o_ref[:, cs] = ot.T.astype(o_ref.dtype)


def diff_prompt(q, k, vt, lam_p, g_c, *, b, t_pad, front, lam_init):
    nkb = t_pad // KB
    return pl.pallas_call(
        functools.partial(_diff_prompt_kernel, front=front, lam_init=lam_init),
        grid=(b, nkb),
        in_specs=[pl.BlockSpec((KB, 512), lambda bb, i: (bb * nkb + i, 0)),
                  pl.BlockSpec((t_pad, 512), lambda bb, i: (bb, 0)),
                  pl.BlockSpec((nkb, 512, KB), lambda bb, i: (bb, 0, 0)),
                  pl.BlockSpec((4, DC), lambda bb, i: (0, 0)),
                  pl.BlockSpec((2 * DC, 1), lambda bb, i: (0, 0))],
        out_specs=pl.BlockSpec((KB, 512), lambda bb, i: (bb * nkb + i, 0)),
        out_shape=jax.ShapeDtypeStruct((b * t_pad, 512), BF16),
        scratch_shapes=[pltpu.VMEM((nkb, KB, 2 * KB), F32)],
        compiler_params=_cparams(("parallel", "arbitrary")),
        name="diff_prompt",
    )(q, k, vt, lam_p, g_c.reshape(2 * DC, 1))


NINF = float("-inf")


def _kth_threshold(count_ge, shape, n_sel):
    zero_i = jnp.zeros(shape, jnp.int32)
    neg = jnp.where(count_ge(jnp.zeros(shape, F32)) < n_sel, 1, 0)
    sign = jnp.where(neg == 1, jnp.int32(INT_MIN), 0)

    def bit_body(t, mag):
        cand = mag | lax.shift_left(jnp.int32(1), 30 - t)
        enough = jnp.where(count_ge(pltpu.bitcast(cand | sign, F32)) >= n_sel, 1, 0)
        return jnp.where(enough + neg == 1, cand, mag)

    mag = lax.fori_loop(0, 31, bit_body, zero_i)
    tau = pltpu.bitcast(jnp.where(neg == 1, (mag + 1) | sign, mag), F32)
    ninf = jnp.full(shape, NINF, F32)
    return jnp.where(count_ge(ninf) >= n_sel, tau, ninf)


def _raise_to_kth_score(tau, count, min_where, n_sel):
    def cond(c):
        return jnp.max(c[1]) >= n_sel

    def body(c):
        tau, above = c
        tau = jnp.where(above >= n_sel, min_where(lambda sc: sc > tau), tau)
        return tau, count(lambda sc: sc > tau)

    return lax.while_loop(cond, body, (tau, count(lambda sc: sc > tau)))


def _dsa_prompt_kernel(qi_ref, wi_ref, qd_ref, ki_ref, kd_ref, vdt_ref, o_ref, sc_ref, sel_ref, s_ref, *, front, n_sel):
    i = pl.program_id(1)
    nkb = i + 1
    qi_all = jnp.concatenate([qi_ref[:, h * LANES:h * LANES + D_IDX] for h in range(H_I)], axis=0)
    wt = wi_ref[...].T
    w_row = jnp.concatenate([wt[h:h + 1, :] for h in range(H_I)], axis=1)

    def stage1(kb, c, masked):
        off = pl.multiple_of(kb * KB, KB)
        sct = jnp.maximum(_dot_nt(ki_ref[pl.ds(off, KB), :], qi_all), 0.0) * w_row
        score = sct[:, 0:KB] + sct[:, KB:2 * KB] + sct[:, 2 * KB:3 * KB] + sct[:, 3 * KB:4 * KB]
        sc_ref[kb] = jnp.where(_key_visible(i, off, 1, front), score, NINF) if masked else score
        return c

    _edge_then_middle(i, functools.partial(stage1, masked=True), functools.partial(stage1, masked=False), 0)

    def count(pred):
        def body(kb, acc):
            return acc + _group_sum(jnp.where(pred(sc_ref[kb]), 1, 0))
        return jnp.sum(lax.fori_loop(0, nkb, body, jnp.zeros((SUB, KB), jnp.int32)), axis=0, keepdims=True)

    def min_where(pred):
        def body(kb, acc):
            sc = sc_ref[kb]
            return jnp.minimum(acc, -_group_max(jnp.where(pred(sc), -sc, NINF)))
        return jnp.min(lax.fori_loop(0, nkb, body, jnp.full((SUB, KB), -NINF, F32)), axis=0, keepdims=True)

    tau = _kth_threshold(lambda cand: count(lambda sc: sc >= cand), (1, KB), n_sel)
    tau, above = _raise_to_kth_score(tau, count, min_where, n_sel)
    need = (n_sel - above).astype(F32)

    strict_lower = jnp.where(lax.broadcasted_iota(jnp.int32, (KB, KB), 1) < lax.broadcasted_iota(jnp.int32, (KB, KB), 0),
                             1.0, 0.0).astype(BF16)

    def select_ranked(kb, before):
        sc = sc_ref[kb]
        eq = sc == tau
        eqf = jnp.where(eq, 1.0, 0.0)
        rank = _dot(strict_lower, eqf.astype(BF16)) + before
        sel_ref[kb] = jnp.where(((sc > tau) | (eq & (rank < need))) & (sc > NINF), 0.0, NEG)
        return before + jnp.sum(eqf, axis=0, keepdims=True)

    def select_all_ties(kb, c):
        sc = sc_ref[kb]
        sel_ref[kb] = jnp.where((sc >= tau) & (sc > NINF), 0.0, NEG)
        return c

    at_tau = count(lambda sc: sc == tau).astype(F32)
    surplus = jnp.max(jnp.where(tau > NINF, at_tau - need, 0.0))

    @pl.when(surplus <= 0.0)
    def _():
        lax.fori_loop(0, nkb, select_all_ties, 0)

    @pl.when(surplus > 0.0)
    def _():
        lax.fori_loop(0, nkb, select_ranked, jnp.zeros((1, KB), F32))

    for h in range(H_D):
        cs = slice(h * LANES, (h + 1) * LANES)
        qd = qd_ref[:, cs]

        def stage3(kb, mx):
            off = pl.multiple_of(kb * KB, KB)
            sdt = _dot_nt(kd_ref[pl.ds(off, KB), :], qd) * (DH_D ** -0.5) + sel_ref[kb]
            s_ref[kb] = sdt
            return jnp.maximum(mx, _group_max(sdt))

        m = jnp.max(_fori_pairs(0, nkb, stage3, lambda kb, c: stage3(kb + 1, stage3(kb, c)),
                                jnp.full((SUB, KB), NEG, F32)), axis=0, keepdims=True)

        def stage4(kb, carry):
            l8, acc = carry
            p = jnp.exp(s_ref[kb] - m)
            return l8 + _group_sum(p), acc + _dot(vdt_ref[kb], p.astype(BF16))

        def stage4_pair(kb, carry):
            l8, acc = carry
            p = jnp.exp(jnp.concatenate([s_ref[kb], s_ref[kb + 1]], axis=0) - m)
            vt2 = jnp.concatenate([vdt_ref[kb], vdt_ref[kb + 1]], axis=1)
            return l8 + _group_sum(p), acc + _dot(vt2, p.astype(BF16))

        l8, acc = _fori_pairs(0, nkb, stage4, stage4_pair, (jnp.zeros((SUB, KB), F32), jnp.zeros((DH_D, KB), F32)))
        o_ref[:, cs] = (acc / jnp.sum(l8, axis=0, keepdims=True)).T.astype(o_ref.dtype)


def dsa_prompt(qi, wi, qd, ki, kd, vdt, *, b, t_pad, front, n_sel):
    nkb = t_pad // KB
    qspec = lambda w: pl.BlockSpec((KB, w), lambda bb, i: (bb * nkb + i, 0))
    kspec = lambda w: pl.BlockSpec((t_pad, w), lambda bb, i: (bb, 0))
    return pl.pallas_call(
        functools.partial(_dsa_prompt_kernel, front=front, n_sel=n_sel),
        grid=(b, nkb),
        in_specs=[qspec(512), qspec(128), qspec(512), kspec(D_IDX), kspec(DH_D),
                  pl.BlockSpec((nkb, DH_D, KB), lambda bb, i: (bb, 0, 0))],
        out_specs=qspec(512),
        out_shape=jax.ShapeDtypeStruct((b * t_pad, 512), BF16),
        scratch_shapes=[pltpu.VMEM((nkb, KB, KB), F32)] * 3,
        compiler_params=_cparams(("parallel", "arbitrary")),
        name="dsa_prompt",
    )(qi, wi, qd, ki, kd, vdt)


TS = 16


PG = 16
PAGE_BUFFERS = 2


def _page_map(g, n_pages, nd):
    def index(bb, j, pt):
        return (pt[bb * n_pages + jnp.minimum(j * PG + g, n_pages - 1)],) + (0,) * nd
    return index


def _online_softmax_update(s, vs, m_ref, l_ref, acc_ref):
    m_old = m_ref[...]
    m_new = jnp.maximum(m_old, jnp.max(s, axis=1, keepdims=True))
    alpha = jnp.exp(m_old - m_new)
    p = jnp.where(s > 0.5 * NEG, jnp.exp(s - m_new), 0.0)
    l_ref[...] = alpha * l_ref[...] + jnp.sum(p, axis=1, keepdims=True)
    r = s.shape[0] // len(vs)
    pv = jnp.concatenate([_dot(p[g * r:(g + 1) * r].astype(BF16), v) for g, v in enumerate(vs)], axis=0)
    acc_ref[...] = alpha * acc_ref[...] + pv
    m_ref[...] = m_new


def _sample_a_kernel(pt_ref, qc_ref, qi_ref, wi_ref, *refs, n_steps, front, lam_init):
    ck_refs, cv_refs, cik_refs = refs[:PG], refs[PG:2 * PG], refs[2 * PG:3 * PG]
    kn_ref, vn_ref, kin_ref, lam_ref, g_ref, o_ref, keys_ref, m_ref, l_ref, acc_ref = refs[3 * PG:]
    j = pl.program_id(1)
    lane = lax.broadcasted_iota(jnp.int32, (TS, LANES), 1)

    @pl.when(j == 0)
    def _():
        m_ref[...] = jnp.full(m_ref.shape, NEG, F32)
        l_ref[...] = jnp.zeros(l_ref.shape, F32)
        acc_ref[...] = jnp.zeros(acc_ref.shape, F32)

    def qstack(h):
        qh = qc_ref[:, h * LANES:(h + 1) * LANES]
        return jnp.concatenate([jnp.where(lane < DC, qh, jnp.zeros_like(qh)),
                                jnp.where(lane >= DC, qh, jnp.zeros_like(qh))], axis=0)

    def head_rows(h):
        return slice(h * 2 * TS, (h + 1) * 2 * TS)

    qi_all = jnp.concatenate([qi_ref[:, h * LANES:h * LANES + D_IDX] for h in range(H_I)], axis=0)
    w_col = jnp.concatenate([wi_ref[:, h:h + 1] for h in range(H_I)], axis=0)

    def idx_scores(kipt):
        sc = jnp.maximum(_dot(qi_all, kipt), 0.0) * w_col
        return sc[0:TS] + sc[TS:2 * TS] + sc[2 * TS:3 * TS] + sc[3 * TS:4 * TS]

    @pl.when(j < n_steps)
    def _():
        ss, vs = [], []
        for h in range(H_C):
            rows_h = pl.ds(h, PAGE_SIZE, stride=H_C)
            k = jnp.concatenate([r[0, rows_h, :].astype(BF16) for r in ck_refs], axis=0)
            vs.append(jnp.concatenate([r[0, rows_h, :].astype(BF16) for r in cv_refs], axis=0))
            ss.append(_dot_nt(qstack(h), k))
        _online_softmax_update(jnp.concatenate(ss, axis=0), vs, m_ref, l_ref, acc_ref)
        keys_ref[0] = idx_scores(jnp.concatenate([r[0].astype(BF16) for r in cik_refs], axis=1))

    @pl.when(j == n_steps)
    def _():
        kr = lax.broadcasted_iota(jnp.int32, (1, TS), 1)
        q_of_row = lax.broadcasted_iota(jnp.int32, (2 * TS, 1), 0) & (TS - 1)
        ok = (kr >= front) & (kr <= q_of_row)
        ss = [jnp.where(ok, _dot_nt(qstack(h), kn_ref[:, h * LANES:(h + 1) * LANES]), NEG) for h in range(H_C)]
        vs = [vn_ref[:, h * LANES:(h + 1) * LANES] for h in range(H_C)]
        _online_softmax_update(jnp.concatenate(ss, axis=0), vs, m_ref, l_ref, acc_ref)
        krp = lax.broadcasted_iota(jnp.int32, (1, PAGE_SIZE), 1)
        okq = (krp >= front) & (krp < TS) & (krp <= lax.broadcasted_iota(jnp.int32, (TS, 1), 0))
        keys_ref[0] = jnp.concatenate([jnp.where(okq, idx_scores(kin_ref[0]), NINF),
                                       jnp.full((TS, (PG - 1) * PAGE_SIZE), NINF, F32)], axis=1)
        lam = _diff_lambda(lam_ref, lam_init)
        for h in range(H_C):
            a = acc_ref[head_rows(h), :] / l_ref[head_rows(h), :]
            o = a[0:TS] - lam * a[TS:2 * TS]
            o_ref[:, h * LANES:(h + 1) * LANES] = (_rms(o, g_ref[...]) * (1.0 - lam_init)).astype(o_ref.dtype)


def sample_diff_idx(pt, qc, qi, wi, ck, cv, cik, kn, vn, kin, lam_p, g_c, *, b, n_pages, front, lam_init):
    assert n_pages % PG == 0
    n_steps = n_pages // PG
    rows = H_C * 2 * TS
    qspec = lambda w: pl.BlockSpec((TS, w), lambda bb, j, pt: (bb, 0))
    deep = pl.Buffered(PAGE_BUFFERS)
    pages4 = [pl.BlockSpec((1, PAGE_SIZE * H_C, 2 * DC), _page_map(g, n_pages, 2), pipeline_mode=deep) for g in range(PG)]
    pages_i = [pl.BlockSpec((1, D_IDX, PAGE_SIZE), _page_map(g, n_pages, 2), pipeline_mode=deep) for g in range(PG)]
    return pl.pallas_call(
        functools.partial(_sample_a_kernel, n_steps=n_steps, front=front, lam_init=lam_init),
        grid_spec=pltpu.PrefetchScalarGridSpec(
            num_scalar_prefetch=1,
            grid=(b, n_steps + 1),
            in_specs=[qspec(512), qspec(512), qspec(128)] + pages4 + pages4 + pages_i
                     + [qspec(512), qspec(512), pl.BlockSpec((1, D_IDX, PAGE_SIZE), lambda bb, j, pt: (bb, 0, 0)),
                        pl.BlockSpec((4, DC), lambda bb, j, pt: (0, 0)),
                        pl.BlockSpec((1, 2 * DC), lambda bb, j, pt: (0, 0))],
            out_specs=[qspec(512), pl.BlockSpec((1, TS, PG * PAGE_SIZE), lambda bb, j, pt: (bb, 0, j))],
            scratch_shapes=[pltpu.VMEM((rows, 1), F32), pltpu.VMEM((rows, 1), F32), pltpu.VMEM((rows, 2 * DC), F32)]),
        out_shape=[jax.ShapeDtypeStruct((b * TS, 512), BF16),
                   jax.ShapeDtypeStruct((b, TS, (n_steps + 1) * PG * PAGE_SIZE), F32)],
        compiler_params=_cparams(("parallel", "arbitrary")),
        name="sample_diff_idx",
    )(pt, qc, qi, wi, *([ck] * PG), *([cv] * PG), *([cik] * PG), kn, vn, kin, lam_p, g_c.reshape(1, 2 * DC))


def _sample_threshold_kernel(keys_ref, tau_ref, need_ref, *, n_sel):
    rows = keys_ref.shape[0]

    def count(pred):
        return jnp.sum(jnp.where(pred(keys_ref[...]), 1, 0), axis=1, keepdims=True)

    def min_where(pred):
        sc = keys_ref[...]
        return jnp.min(jnp.where(pred(sc), sc, -NINF), axis=1, keepdims=True)

    tau = _kth_threshold(lambda cand: count(lambda sc: sc >= cand), (rows, 1), n_sel)
    tau, above = _raise_to_kth_score(tau, count, min_where, n_sel)
    tau_ref[...] = tau
    need_ref[...] = (n_sel - above).astype(F32)


def sample_threshold(keys, *, n_sel):
    rows, nk = keys.shape
    return pl.pallas_call(
        functools.partial(_sample_threshold_kernel, n_sel=n_sel),
        grid=(1,),
        in_specs=[pl.BlockSpec((rows, nk), lambda i: (0, 0), pipeline_mode=pl.Buffered(1))],
        out_specs=[pl.BlockSpec((rows, 1), lambda i: (0, 0))] * 2,
        out_shape=[jax.ShapeDtypeStruct((rows, 1), F32)] * 2,
        compiler_params=_cparams(("arbitrary",)),
        name="sample_threshold",
    )(keys)


def _sample_b_kernel(pt_ref, tau_ref, need_ref, keys_ref, qd_ref, *refs, n_steps):
    ck_refs, cv_refs = refs[:PG], refs[PG:2 * PG]
    kn_ref, vn_ref, o_ref, before_ref, m_ref, l_ref, acc_ref = refs[2 * PG:]
    j = pl.program_id(1)

    @pl.when(j == 0)
    def _():
        before_ref[...] = jnp.zeros(before_ref.shape, F32)
        m_ref[...] = jnp.full(m_ref.shape, NEG, F32)
        l_ref[...] = jnp.zeros(l_ref.shape, F32)
        acc_ref[...] = jnp.zeros(acc_ref.shape, F32)

    strict_upper = jnp.where(lax.broadcasted_iota(jnp.int32, (PAGE_SIZE, PAGE_SIZE), 0)
                             < lax.broadcasted_iota(jnp.int32, (PAGE_SIZE, PAGE_SIZE), 1), 1.0, 0.0).astype(BF16)
    qd_all = jnp.concatenate([qd_ref[:, h * LANES:(h + 1) * LANES] for h in range(H_D)], axis=0)

    def select(key, before):
        tau = tau_ref[...]
        eq = key == tau
        eqf = jnp.where(eq, 1.0, 0.0)
        rank = _dot(eqf.astype(BF16), strict_upper) + before
        sel = ((key > tau) | (eq & (rank < need_ref[...]))) & (key > NINF)
        return sel, before + jnp.sum(eqf, axis=1, keepdims=True)

    def attend(s, sel, v):
        s = jnp.where(jnp.concatenate([sel] * H_D, axis=0), s * (DH_D ** -0.5), NEG)
        _online_softmax_update(s, [v], m_ref, l_ref, acc_ref)

    @pl.when(j < n_steps)
    def _():
        key = keys_ref[0]
        before = before_ref[...]
        sels = []
        for g in range(PG):
            sel, before = select(key[:, g * PAGE_SIZE:(g + 1) * PAGE_SIZE], before)
            sels.append(sel)
        before_ref[...] = before
        k = jnp.concatenate([r[0].astype(BF16) for r in ck_refs], axis=0)
        v = jnp.concatenate([r[0].astype(BF16) for r in cv_refs], axis=0)
        attend(_dot_nt(qd_all, k), jnp.concatenate(sels, axis=1), v)

    @pl.when(j == n_steps)
    def _():
        sel, _ = select(keys_ref[0][:, :PAGE_SIZE], before_ref[...])
        attend(_dot_nt(qd_all, kn_ref[0]), sel, vn_ref[0])
        o = acc_ref[...] / l_ref[...]
        for h in range(H_D):
            o_ref[:, h * LANES:(h + 1) * LANES] = o[h * TS:(h + 1) * TS].astype(o_ref.dtype)


def sample_dsa(pt, keys, qd, ck, cv, kn, vn, *, b, n_pages, n_sel):
    assert n_pages % PG == 0
    n_steps = n_pages // PG
    rows = H_D * TS
    tau, need = sample_threshold(keys.reshape(b * TS, keys.shape[2]), n_sel=n_sel)
    pages = [pl.BlockSpec((1, PAGE_SIZE, DH_D), _page_map(g, n_pages, 2), pipeline_mode=pl.Buffered(PAGE_BUFFERS))
             for g in range(PG)]
    nspec = pl.BlockSpec((1, PAGE_SIZE, DH_D), lambda bb, j, pt: (bb, 0, 0))
    col = pl.BlockSpec((TS, 1), lambda bb, j, pt: (bb, 0))
    return pl.pallas_call(
        functools.partial(_sample_b_kernel, n_steps=n_steps),
        grid_spec=pltpu.PrefetchScalarGridSpec(
            num_scalar_prefetch=1,
            grid=(b, n_steps + 1),
            in_specs=[col, col,
                      pl.BlockSpec((1, TS, PG * PAGE_SIZE), lambda bb, j, pt: (bb, 0, j)),
                      pl.BlockSpec((TS, 512), lambda bb, j, pt: (bb, 0))] + pages + pages + [nspec, nspec],
            out_specs=pl.BlockSpec((TS, 512), lambda bb, j, pt: (bb, 0)),
            scratch_shapes=[pltpu.VMEM((TS, 1), F32),
                            pltpu.VMEM((rows, 1), F32), pltpu.VMEM((rows, 1), F32), pltpu.VMEM((rows, DH_D), F32)]),
        out_shape=jax.ShapeDtypeStruct((b * TS, 512), BF16),
        compiler_params=_cparams(("parallel", "arbitrary")),
        name="sample_dsa",
    )(pt, tau, need, keys, qd, *([ck] * PG), *([cv] * PG), kn, vn)


REC_CHUNK = 64
REC_SUB = 16
TM_PROJ = 512
TM_REC, TN_REC = 1408, 1408
TM_ROWS = 704
TM_FFN = 528


def _pad_cols(w, n):
    return jnp.pad(w, ((0, 0), (0, n - w.shape[1])))


def _tile_rows(m, pref):
    return pref if m % pref == 0 else m


def kernel(x_prompt, x_sample, state_hgrn, state_mlstm_C, state_mlstm_n, state_mlstm_m, state_ffn_conv, cache_diff_k, cache_diff_v, cache_dsa_k, cache_dsa_v, cache_idx_k, page_table, meta_tokens, norm_gains, w_in_rec, b_gates_rec, lb_logits, g_norm_hgrn, g_norm_mlstm, w_out_rec, w_in_att, diff_lambda, g_norm_diff, w_out_att, w_ffn_up, ffn_conv_w, ffn_conv_b, w_ffn_down):
    bp, t_in, d = x_prompt.shape
    bs, t_s, _ = x_sample.shape
    depth = norm_gains.shape[0]
    n_pages = page_table.shape[1]
    past_len = n_pages * PAGE_SIZE
    real_p = N_META + t_in
    tp = -(-real_p // QB) * QB
    front_p = tp - real_p
    front_s = TS - t_s
    assert tp % REC_CHUNK == 0 and tp % TM_ROWS == 0 and front_p >= CONV_W - 1 and front_s >= CONV_W - 1
    mp, ms = bp * tp, bs * TS

    meta = jnp.broadcast_to(meta_tokens.astype(x_prompt.dtype)[None], (bp, N_META, d))
    xp = jnp.concatenate([jnp.zeros((bp, front_p, d), x_prompt.dtype), meta, x_prompt], axis=1).reshape(mp, d)
    xs = jnp.concatenate([jnp.zeros((bs, front_s, d), x_sample.dtype), x_sample], axis=1).reshape(ms, d)
    lb_all = jnp.cumsum(jax.nn.softmax(lb_logits.astype(F32), axis=0), axis=0)
    pt_flat = page_table.reshape(-1).astype(jnp.int32)
    sel_p = min(TOPK_MAX, t_in // 4)
    sel_s = min(TOPK_MAX, (past_len + t_s) // 4)
    tmp_s = _tile_rows(ms, TM_PROJ)
    tmr_s = _tile_rows(ms, TM_ROWS)

    pos_p = jnp.arange(tp, dtype=jnp.int32) - front_p
    pos_s = jnp.tile(past_len + jnp.arange(TS, dtype=jnp.int32) - front_s, ms // TS)
    tabs_p = (rope_tables(pos_p, DC, DC // ROT_FRAC // 2), rope_tables(pos_p, DH_D, DH_D // ROT_FRAC // 2))
    tabs_s = (rope_tables(pos_s, DC, DC // ROT_FRAC // 2), rope_tables(pos_s, DH_D, DH_D // ROT_FRAC // 2))

    rec_p, rec_s = [[], [], [], []], [[], [], [], []]
    att_p, att_s = [[], [], [], [], []], [[], [], [], [], []]
    conv_p, conv_s = [], []
    for l in range(depth):
        p = l // 2
        g = norm_gains[l].astype(F32)
        if l % 2 == 0:
            w_in = _pad_cols(w_in_rec[p], REC_N).astype(BF16)
            w_out = w_out_rec[p].astype(BF16)
            prm = (lb_all[p], b_gates_rec[p].astype(F32), g_norm_hgrn[p].astype(F32), g_norm_mlstm[p].astype(F32))
            proj = norm_matmul(xp, g[0], w_in, tm=_tile_rows(mp, TM_REC), tn=TN_REC).reshape(bp, tp, REC_N)
            zs = jnp.zeros((bp, 4, 128, 128), F32)
            y, *st = rec_mixer(proj, *prm, zs, zs, jnp.zeros((bp, 4, 128), F32), jnp.zeros((bp, 4), F32),
                               C=REC_CHUNK, W=REC_SUB, front=front_p)
            xp = matmul_norm_res([y.reshape(mp, -1)], [w_out], g[1], xp, tm=TM_ROWS, t_pad=tp, front=front_p)
            for j in range(4):
                rec_p[j].append(st[j])
            proj = norm_matmul(xs, g[0], w_in, tm=tmp_s, tn=384).reshape(bs, TS, REC_N)
            y, *st = rec_mixer(proj, *prm, state_hgrn[p].astype(F32), state_mlstm_C[p].astype(F32),
                               state_mlstm_n[p].astype(F32), state_mlstm_m[p].astype(F32), C=TS, W=TS, front=front_s)
            xs = matmul_norm_res([y.reshape(ms, -1)], [w_out], g[1], xs, tm=tmr_s, t_pad=TS, front=front_s)
            for j in range(4):
                rec_s[j].append(st[j])
        else:
            lam_init = 0.8 - 0.6 * math.exp(-0.3 * l)
            w_in = _pad_cols(w_in_att[p], ATT_N).astype(BF16)
            w_out = w_out_att[p].astype(BF16)
            dl, gc = diff_lambda[p].astype(F32), g_norm_diff[p].astype(F32)
            (qc, kc, kcb, vc, vcb, qd, kd, kdb, vd, vdb, qi, ki, kib, wi, vct, vdt) = att_prep(xp, g[0], w_in, *tabs_p, tm=KB)
            oc = diff_prompt(qc, kcb, vct, dl, gc, b=bp, t_pad=tp, front=front_p, lam_init=lam_init)
            od = dsa_prompt(qi, wi, qd, kib, kdb, vdt, b=bp, t_pad=tp, front=front_p, n_sel=sel_p)
            xp = matmul_norm_res([oc, od], [w_out[:512], w_out[512:]], g[1], xp, tm=TM_ROWS, t_pad=tp, front=front_p)
            for j, (a, shp) in enumerate([(kc, (H_C, 2 * DC)), (vc, (H_C, 2 * DC)), (kd, (DH_D,)), (vd, (DH_D,)), (ki, (D_IDX,))]):
                att_p[j].append(a.reshape((bp, tp) + shp)[:, front_p:])
            (qc, kc, kcb, vc, vcb, qd, kd, kdb, vd, vdb, qi, ki, kib, wi, _, _) = att_prep(xs, g[0], w_in, *tabs_s, tm=ms)
            as_page = lambda a: jnp.pad(a.reshape(bs, TS, -1), ((0, 0), (0, PAGE_SIZE - TS), (0, 0)))
            rows_kh = lambda c: c.reshape(c.shape[0], PAGE_SIZE * H_C, 2 * DC)
            oc, keys = sample_diff_idx(pt_flat, qc, qi, wi, rows_kh(cache_diff_k[p]), rows_kh(cache_diff_v[p]),
                                       jnp.swapaxes(cache_idx_k[p], 1, 2), kcb, vcb, jnp.swapaxes(as_page(kib), 1, 2), dl, gc,
                                       b=bs, n_pages=n_pages, front=front_s, lam_init=lam_init)
            od = sample_dsa(pt_flat, keys, qd, cache_dsa_k[p], cache_dsa_v[p], as_page(kdb), as_page(vdb),
                            b=bs, n_pages=n_pages, n_sel=sel_s)
            xs = matmul_norm_res([oc, od], [w_out[:512], w_out[512:]], g[1], xs, tm=tmr_s, t_pad=TS, front=front_s)
            for j, (a, shp) in enumerate([(kc, (H_C, 2 * DC)), (vc, (H_C, 2 * DC)), (kd, (DH_D,)), (vd, (DH_D,)), (ki, (D_IDX,))]):
                att_s[j].append(a.reshape((bs, TS) + shp)[:, front_s:])
        w_up, w_down = w_ffn_up[l].astype(BF16), w_ffn_down[l].astype(BF16)
        cw, cb = ffn_conv_w[l].astype(F32), ffn_conv_b[l].astype(F32)
        xp, tail = ffn_fused(xp, g[2], w_up, cw, cb, w_down, g[3], tm=TM_FFN, t_pad=tp, front=front_p)
        conv_p.append(tail[:, SUB - (CONV_W - 1):])
        up = norm_matmul(xs, g[2], w_up, tm=tmp_s, tn=512)
        conv_s.append(up.reshape(bs, TS, 2 * D_FF)[:, TS - (CONV_W - 1):, :D_FF])
        buf = jnp.pad(state_ffn_conv[l].astype(F32), ((0, 0), (front_s - (CONV_W - 1), TS - front_s), (0, 0)))
        xs = ffn_down(up, buf.reshape(ms, D_FF), cw, cb, w_down, g[3], xs, tm=tmr_s, t_pad=TS, front=front_s)

    y_p = xp.reshape(bp, tp, d)[:, front_p + N_META:]
    y_s = xs.reshape(bs, TS, d)[:, front_s:]
    stack = lambda xs: xs[0][None] if len(xs) == 1 else jnp.stack(xs)
    return (y_p, y_s,
            stack(rec_p[0]), stack(rec_s[0]), stack(rec_p[1]), stack(rec_s[1]),
            stack(rec_p[2]), stack(rec_s[2]), stack(rec_p[3]), stack(rec_s[3]),
            stack(conv_p), stack(conv_s),
            stack(att_p[0]), stack(att_s[0]), stack(att_p[1]), stack(att_s[1]),
            stack(att_p[2]), stack(att_s[2]), stack(att_p[3]), stack(att_s[3]),
            stack(att_p[4]), stack(att_s[4]))
```

```python
import functools
import math

import jax
import jax.numpy as jnp
import numpy as np
from jax import lax
from jax.experimental import pallas as pl
from jax.experimental.pallas import tpu as pltpu

F32 = jnp.float32
BF16 = jnp.bfloat16

D_MODEL = 1024
N_META = 16
H_A, DK_A, DV_A = 4, 128, 128
H_B, DK_B, DV_B = 4, 128, 128
H_C, DC = 4, 64
H_D, DH_D = 4, 128
H_I, D_IDX = 4, 64
TOPK_MAX = 256
D_FF = 2816
CONV_W = 3
ROPE_THETA = 500000.0
ROT_FRAC = 4
EPS = 1e-6
PAGE_SIZE = 128
LANES = 128
NEG = -1e30

REC_N = 8 * 512 + LANES
ATT_N = 2560 + LANES
INT_MIN = -2 ** 31
VMEM_LIMIT = 56 * 1024 * 1024


def _cparams(sem):
    return pltpu.CompilerParams(dimension_semantics=sem, vmem_limit_bytes=VMEM_LIMIT)


def _rms(x, g):
    return x * lax.rsqrt(jnp.mean(x * x, axis=-1, keepdims=True) + EPS) * g


def _dot(a, b):
    return jnp.dot(a, b, preferred_element_type=F32)


def _dot_nt(a, b):
    return lax.dot_general(a, b, (((1,), (1,)), ((), ())), preferred_element_type=F32)


def _dot_tn(a, b):
    return lax.dot_general(a, b, (((0,), (0,)), ((), ())), preferred_element_type=F32)


def _dot_exact_lhs(tri, x):
    hi = x.astype(BF16)
    r1 = x - hi.astype(F32)
    mid = r1.astype(BF16)
    lo = (r1 - mid.astype(F32)).astype(BF16)
    return _dot(tri, hi) + _dot(tri, mid) + _dot(tri, lo)


def _row_valid(i, tm, t_pad, front):
    r = lax.broadcasted_iota(jnp.int32, (tm, 1), 0)
    if t_pad % tm == 0:
        t = (i % (t_pad // tm)) * tm + r
    else:
        assert tm % t_pad == 0 and (t_pad & (t_pad - 1)) == 0
        t = r & (t_pad - 1)
    return t >= front


def _norm_matmul_kernel(x_ref, g_ref, w_ref, o_ref, h_ref):
    @pl.when(pl.program_id(1) == 0)
    def _():
        h_ref[...] = _rms(x_ref[...], g_ref[...]).astype(BF16)

    o_ref[...] = _dot(h_ref[...], w_ref[...])


def norm_matmul(x, g, w, *, tm, tn):
    m, d = x.shape
    n = w.shape[1]
    assert m % tm == 0 and n % tn == 0
    return pl.pallas_call(
        _norm_matmul_kernel,
        grid=(m // tm, n // tn),
        in_specs=[pl.BlockSpec((tm, d), lambda i, j: (i, 0)),
                  pl.BlockSpec((1, d), lambda i, j: (0, 0)),
                  pl.BlockSpec((d, tn), lambda i, j: (0, j))],
        out_specs=pl.BlockSpec((tm, tn), lambda i, j: (i, j)),
        out_shape=jax.ShapeDtypeStruct((m, n), F32),
        scratch_shapes=[pltpu.VMEM((tm, d), BF16)],
        compiler_params=_cparams(("parallel", "arbitrary")),
        name="norm_matmul",
    )(x, g.reshape(1, d), w)


def _matmul_norm_res_kernel(*refs, n_in, tm, t_pad, front):
    a_refs, w_refs = refs[:n_in], refs[n_in:2 * n_in]
    g_ref, x_ref, o_ref = refs[2 * n_in:]
    acc = _dot(a_refs[0][...], w_refs[0][...])
    for a, w in zip(a_refs[1:], w_refs[1:]):
        acc = acc + _dot(a[...], w[...])
    out = x_ref[...] + _rms(acc, g_ref[...])
    o_ref[...] = jnp.where(_row_valid(pl.program_id(0), tm, t_pad, front), out, 0.0)


def matmul_norm_res(a_list, w_list, g, x, *, tm, t_pad, front):
    m, d = x.shape
    assert m % tm == 0
    n_in = len(a_list)
    in_specs = ([pl.BlockSpec((tm, a.shape[1]), lambda i: (i, 0)) for a in a_list]
                + [pl.BlockSpec(w.shape, lambda i: (0, 0)) for w in w_list]
                + [pl.BlockSpec((1, d), lambda i: (0, 0)), pl.BlockSpec((tm, d), lambda i: (i, 0))])
    return pl.pallas_call(
        functools.partial(_matmul_norm_res_kernel, n_in=n_in, tm=tm, t_pad=t_pad, front=front),
        grid=(m // tm,),
        in_specs=in_specs,
        out_specs=pl.BlockSpec((tm, d), lambda i: (i, 0)),
        out_shape=jax.ShapeDtypeStruct((m, d), F32),
        compiler_params=_cparams(("parallel",)),
        name="matmul_norm_res",
    )(*a_list, *w_list, g.reshape(1, d), x)


def _ffn_down_kernel(ug_ref, uv_ref, buf_ref, cw_ref, cb_ref, w_ref, g_ref, x_ref, o_ref, *, tm, t_pad, front):
    t = lax.broadcasted_iota(jnp.int32, (tm, 1), 0) & (t_pad - 1)
    ug = jnp.where((t >= front - (CONV_W - 1)) & (t < front), buf_ref[...], ug_ref[...])
    conv = (cb_ref[...] + cw_ref[0:1, :] * pltpu.roll(ug, 2, axis=0) + cw_ref[1:2, :] * pltpu.roll(ug, 1, axis=0)
            + cw_ref[2:3, :] * ug)
    act = (conv * jax.nn.sigmoid(conv) * uv_ref[...]).astype(BF16)
    out = x_ref[...] + _rms(_dot(act, w_ref[...]), g_ref[...])
    o_ref[...] = jnp.where(t >= front, out, 0.0)


def ffn_down(up, buf, cw, cb, w_down, g, x, *, tm, t_pad, front):
    m, d = x.shape
    f = w_down.shape[0]
    assert m % tm == 0 and tm % t_pad == 0 and (t_pad & (t_pad - 1)) == 0 and front >= CONV_W - 1
    assert up.shape == (m, 2 * f) and buf.shape == (m, f)
    return pl.pallas_call(
        functools.partial(_ffn_down_kernel, tm=tm, t_pad=t_pad, front=front),
        grid=(m // tm,),
        in_specs=[pl.BlockSpec((tm, f), lambda i: (i, 0)),
                  pl.BlockSpec((tm, f), lambda i: (i, 1)),
                  pl.BlockSpec((tm, f), lambda i: (i, 0)),
                  pl.BlockSpec((CONV_W, f), lambda i: (0, 0)),
                  pl.BlockSpec((1, f), lambda i: (0, 0)),
                  pl.BlockSpec((f, d), lambda i: (0, 0)),
                  pl.BlockSpec((1, d), lambda i: (0, 0)),
                  pl.BlockSpec((tm, d), lambda i: (i, 0))],
        out_specs=pl.BlockSpec((tm, d), lambda i: (i, 0)),
        out_shape=jax.ShapeDtypeStruct((m, d), F32),
        compiler_params=_cparams(("parallel",)),
        name="ffn_down",
    )(up, up, buf, cw, cb.reshape(1, f), w_down, g.reshape(1, d), x)


FF_CW = 256
FF_HALO = 16


def _ffn_fused_kernel(x_ref, halo_ref, gin_ref, wup_ref, cw_ref, cb_ref, wdn_ref, gout_ref, o_ref, conv_ref, acc_ref,
                      *, tm, t_pad, front):
    i = pl.program_id(0)
    x = x_ref[...]
    h = jnp.concatenate([_rms(halo_ref[...], gin_ref[...]), _rms(x, gin_ref[...])], axis=0).astype(BF16)
    for c in range(D_FF // FF_CW):
        cs = slice(c * FF_CW, (c + 1) * FF_CW)
        ug = _dot(h, wup_ref[:, cs])
        uv = _dot(h[FF_HALO:], wup_ref[:, D_FF + c * FF_CW:D_FF + (c + 1) * FF_CW])
        prev1 = pltpu.roll(ug, 1, axis=0)[FF_HALO:]
        prev2 = pltpu.roll(ug, 2, axis=0)[FF_HALO:]
        conv = cb_ref[:, cs] + cw_ref[0:1, cs] * prev2 + cw_ref[1:2, cs] * prev1 + cw_ref[2:3, cs] * ug[FF_HALO:]
        act = (conv * jax.nn.sigmoid(conv) * uv).astype(BF16)
        part = _dot(act, wdn_ref[cs, :])
        if c == 0:
            acc_ref[...] = part
        else:
            acc_ref[...] += part
        conv_ref[0, :, cs] = ug[FF_HALO + tm - SUB:, :]
    out = x + _rms(acc_ref[...], gout_ref[...])
    o_ref[...] = jnp.where(_row_valid(i, tm, t_pad, front), out, 0.0)


def ffn_fused(x, g_in, w_up, cw, cb, w_down, g_out, *, tm, t_pad, front):
    m, d = x.shape
    assert m % tm == 0 and t_pad % tm == 0 and tm % FF_HALO == 0 and D_FF % FF_CW == 0
    per_seq = t_pad // tm
    hb = tm // FF_HALO
    const = lambda shape: pl.BlockSpec(shape, lambda i: (0,) * len(shape), pipeline_mode=pl.Buffered(1))
    return pl.pallas_call(
        functools.partial(_ffn_fused_kernel, tm=tm, t_pad=t_pad, front=front),
        grid=(m // tm,),
        in_specs=[pl.BlockSpec((tm, d), lambda i: (i, 0)),
                  pl.BlockSpec((FF_HALO, d), lambda i: (jnp.maximum(i * hb - 1, 0), 0)),
                  const((1, d)), const((d, 2 * D_FF)), const((CONV_W, D_FF)), const((1, D_FF)),
                  const((D_FF, d)), const((1, d))],
        out_specs=[pl.BlockSpec((tm, d), lambda i: (i, 0)),
                   pl.BlockSpec((1, SUB, D_FF), lambda i: (i // per_seq, 0, 0))],
        out_shape=[jax.ShapeDtypeStruct((m, d), F32), jax.ShapeDtypeStruct((m // t_pad, SUB, D_FF), F32)],
        scratch_shapes=[pltpu.VMEM((tm, d), F32)],
        compiler_params=_cparams(("arbitrary",)),
        name="ffn_fused",
    )(x, x, g_in.reshape(1, d), w_up, cw, cb.reshape(1, D_FF), w_down, g_out.reshape(1, d))


def _log_sigmoid(x):
    return jnp.minimum(x, 0.0) - jnp.log1p(jnp.exp(-jnp.abs(x)))


def _rec_kernel(proj_ref, lb_ref, bg_ref, ga_ref, gb_ref, s0_ref, c0_ref, n0_ref, m0_ref,
                y_ref, s_ref, c_ref, n_ref, m_ref, st_ref, *, C, W, front):
    ci = pl.program_id(1)
    nci = pl.num_programs(1)

    @pl.when(ci == 0)
    def _():
        for h in range(H_A):
            st_ref[h] = s0_ref[0, h].T
        c_ref[...] = c0_ref[...]
        n_ref[...] = n0_ref[...]
        m_ref[...] = m0_ref[...]

    row = ci * C + lax.broadcasted_iota(jnp.int32, (C, 1), 0)
    valid = row >= front
    r_i = lax.broadcasted_iota(jnp.int32, (C, C), 0)
    c_i = lax.broadcasted_iota(jnp.int32, (C, C), 1)
    causal = r_i >= c_i
    tri = jnp.where(causal, 1.0, 0.0).astype(BF16)

    gates = proj_ref[0, :, 8 * 512:8 * 512 + LANES] + bg_ref[...]
    lf_all = jnp.where(valid, _log_sigmoid(gates), 0.0)
    ig_all = jnp.where(valid, gates, NEG)
    b_all = _dot_exact_lhs(tri, lf_all)
    b_all_t = b_all.T
    ig_all_t = ig_all.T

    qk_dots, qc_dots = [], []
    for h in range(H_B):
        qb = proj_ref[0, :, 2048 + h * 128:2048 + (h + 1) * 128].astype(BF16)
        kb = (proj_ref[0, :, 2560 + h * 128:2560 + (h + 1) * 128] * (DK_B ** -0.5)).astype(BF16)
        qk_dots.append(_dot_nt(qb, kb))
        qc_dots.append(_dot(qb, c_ref[0, h].astype(BF16)))

    for h in range(H_B):
        q = proj_ref[0, :, 2048 + h * 128:2048 + (h + 1) * 128]
        k = proj_ref[0, :, 2560 + h * 128:2560 + (h + 1) * 128] * (DK_B ** -0.5)
        v = proj_ref[0, :, 3072 + h * 128:3072 + (h + 1) * 128]
        og = proj_ref[0, :, 3584 + h * 128:3584 + (h + 1) * 128]
        vb = v.astype(BF16)
        b_col = b_all[:, H_B + h:H_B + h + 1]
        b_row = b_all_t[H_B + h:H_B + h + 1, :]
        i_col = ig_all[:, h:h + 1]
        i_row = ig_all_t[h:h + 1, :]
        m_prev = m_ref[0, h:h + 1, 0:1]
        dmat = jnp.where(causal, b_col - b_row + i_row, NEG)
        inter = b_col + m_prev
        mt = jnp.maximum(inter, jnp.max(dmat, axis=1, keepdims=True))
        w = jnp.exp(dmat - mt) * qk_dots[h]
        wi = jnp.exp(inter - mt)
        c_st = c_ref[0, h]
        n_st = n_ref[0, h:h + 1, :]
        num = wi * qc_dots[h] + _dot(w.astype(BF16), vb)
        den = wi * jnp.sum(q * n_st, axis=1, keepdims=True) + jnp.sum(w, axis=1, keepdims=True)
        hc = num / jnp.maximum(jnp.abs(den), jnp.exp(-mt))
        m_new = mt[C - 1:C, :]
        b_last = b_col[C - 1:C, :]
        decay = jnp.exp(b_last + m_prev - m_new)
        kw = k * jnp.exp(b_last - b_col + i_col - m_new)
        c_ref[0, h] = decay * c_st + _dot_tn(kw.astype(BF16), vb)
        n_ref[0, h:h + 1, :] = decay * n_st + jnp.sum(kw, axis=0, keepdims=True)
        m_ref[0, h:h + 1, :] = jnp.broadcast_to(m_new, (1, LANES))
        yb = _rms(hc, gb_ref[...]) * jax.nn.sigmoid(og)
        y_ref[0, :, 512 + h * 128:512 + (h + 1) * 128] = jnp.where(valid, yb, 0.0).astype(y_ref.dtype)

    nj = C // W
    rw = lax.broadcasted_iota(jnp.int32, (W, 1), 0)
    ones_sq = jnp.ones((DK_A, LANES), BF16)
    lbv = lb_ref[...]
    f = lbv + (1.0 - lbv) * jax.nn.sigmoid(proj_ref[0, :, 512:1024])
    logf = jnp.where(valid, jnp.log(f), 0.0)
    kk = jnp.where(valid, 1.0 - f, 0.0)
    qq = proj_ref[0, :, 0:512] * (DK_A ** -0.5)
    vv = proj_ref[0, :, 1024:1536]
    bfull = _dot_exact_lhs(tri, logf)
    bcs = []
    for j in range(nj):
        rows = slice(j * W, (j + 1) * W)
        bcs.append(bfull[rows] if j == 0 else bfull[rows] - bfull[j * W - 1:j * W])

    heads = [slice(h * 128, (h + 1) * 128) for h in range(H_A)]
    row_sums, outer, decay_last = {}, {}, []
    for j in range(nj):
        rows = slice(j * W, (j + 1) * W)
        bc = bcs[j]
        last = bc[W - 1:W, :]
        decay_last.append(jnp.exp(last))
        kdec = (kk[rows] * jnp.exp(last - bc)).astype(BF16)
        gs = []
        for s in range(W):
            e = jnp.exp(jnp.where(rw >= s, bc - bc[s:s + 1, :], NEG))
            gs.append(qq[rows] * kk[j * W + s:j * W + s + 1, :] * e)
        g = jnp.concatenate(gs, axis=0).astype(BF16)
        for h, cs in enumerate(heads):
            row_sums[h, j] = _dot(g[:, cs], ones_sq)
            outer[h, j] = _dot_tn(vv[rows, cs].astype(BF16), kdec[:, cs])

    from_state = {}
    for h, cs in enumerate(heads):
        st = st_ref[h]
        for j in range(nj):
            rows = slice(j * W, (j + 1) * W)
            from_state[h, j] = _dot_nt((qq[rows, cs] * jnp.exp(bcs[j][:, cs])).astype(BF16), st.astype(BF16))
            st = decay_last[j][:, cs] * st + outer[h, j]
        st_ref[h] = st

    for h, cs in enumerate(heads):
        for j in range(nj):
            rows = slice(j * W, (j + 1) * W)
            o = from_state[h, j]
            for s in range(W):
                o = o + row_sums[h, j][s * W:(s + 1) * W] * vv[j * W + s:j * W + s + 1, cs]
            ga = proj_ref[0, rows, 1536 + h * 128:1536 + (h + 1) * 128]
            ya = _rms(o, ga_ref[...]) * (ga * jax.nn.sigmoid(ga))
            vld = (ci * C + j * W + rw) >= front
            y_ref[0, rows, cs] = jnp.where(vld, ya, 0.0).astype(y_ref.dtype)

    @pl.when(ci == nci - 1)
    def _():
        for h in range(H_A):
            s_ref[0, h] = st_ref[h].T


def rec_mixer(proj, lb, bg, g_a, g_b, s0, c0, n0, m0, *, C, W, front):
    b, t, _ = proj.shape
    assert t % C == 0 and C % W == 0
    m0b = jnp.broadcast_to(m0[:, :, None], (b, H_B, LANES))
    bgp = jnp.zeros((1, LANES), F32).at[0, :2 * H_B].set(bg.reshape(-1))
    st_spec = pl.BlockSpec((1, 4, 128, 128), lambda i, c: (i, 0, 0, 0))
    v_spec = pl.BlockSpec((1, 4, LANES), lambda i, c: (i, 0, 0))
    row_spec = lambda n: pl.BlockSpec((1, n), lambda i, c: (0, 0))
    y, s, cc, n, m = pl.pallas_call(
        functools.partial(_rec_kernel, C=C, W=W, front=front),
        grid=(b, t // C),
        in_specs=[pl.BlockSpec((1, C, REC_N), lambda i, c: (i, c, 0)),
                  row_spec(512), row_spec(LANES), row_spec(128), row_spec(128),
                  st_spec, st_spec, v_spec, v_spec],
        out_specs=[pl.BlockSpec((1, C, 1024), lambda i, c: (i, c, 0)), st_spec, st_spec, v_spec, v_spec],
        out_shape=[jax.ShapeDtypeStruct((b, t, 1024), BF16),
                   jax.ShapeDtypeStruct((b, 4, 128, 128), F32),
                   jax.ShapeDtypeStruct((b, 4, 128, 128), F32),
                   jax.ShapeDtypeStruct((b, 4, LANES), F32),
                   jax.ShapeDtypeStruct((b, 4, LANES), F32)],
        scratch_shapes=[pltpu.VMEM((4, 128, 128), F32)],
        compiler_params=_cparams(("parallel", "arbitrary")),
        name="rec_mixer",
    )(proj, lb.reshape(1, 512), bgp, g_a.reshape(1, 128), g_b.reshape(1, 128), s0, c0, n0, m0b)
    return y, s, cc, n, m[:, :, 0]


def rope_tables(pos, period, half):
    r = 2 * half
    inv = ROPE_THETA ** (-jnp.arange(half, dtype=F32) * 2.0 / r)
    ang = pos.astype(F32)[:, None] * inv[None, :]
    cos, sin = jnp.cos(ang), jnp.sin(ang)
    lane = np.arange(LANES) % period
    idx = np.where(lane < half, lane, np.where(lane < r, lane - half, 0))
    first, second = jnp.asarray(lane < half), jnp.asarray((lane >= half) & (lane < r))
    c = jnp.where(first | second, cos[:, idx], 1.0)
    sa = jnp.where(first, -sin[:, idx], 0.0)
    sb = jnp.where(second, sin[:, idx], 0.0)
    return c, sa, sb


def _att_prep_kernel(x_ref, g_ref, w_ref, c64, a64, b64, c128, a128, b128,
                     qc_o, kc_o, kcb_o, vc_o, vcb_o, qd_o, kd_o, kdb_o, vd_o, vdb_o, qi_o, ki_o, kib_o, wi_o,
                     vct_o, vdt_o, p_ref):
    def rot(x, c, sa, sb, half):
        return x * c[...] + pltpu.roll(x, LANES - half, axis=1) * sa[...] + pltpu.roll(x, half, axis=1) * sb[...]

    p_ref[...] = _dot(_rms(x_ref[...], g_ref[...]).astype(BF16), w_ref[...])
    tm = p_ref.shape[0]
    h64 = D_IDX // ROT_FRAC // 2
    h128 = DH_D // ROT_FRAC // 2
    for t in range(4):
        sl = slice(t * LANES, (t + 1) * LANES)
        qc_o[:, sl] = (rot(p_ref[:, sl], c64, a64, b64, h64) * (DC ** -0.5)).astype(BF16)
        kc = rot(p_ref[:, 512 + t * LANES:512 + (t + 1) * LANES], c64, a64, b64, h64)
        kc_o[pl.ds(t, tm, stride=H_C), :] = kc
        kcb_o[:, sl] = kc.astype(BF16)
        vc = p_ref[:, 1024 + t * LANES:1024 + (t + 1) * LANES]
        vc_o[pl.ds(t, tm, stride=H_C), :] = vc
        vcb_o[:, sl] = vc.astype(BF16)
        vct_o[0, sl, :] = vc.T.astype(BF16)
        qd_o[:, sl] = rot(p_ref[:, 1536 + t * LANES:1536 + (t + 1) * LANES], c128, a128, b128, h128).astype(BF16)
    kd = rot(p_ref[:, 2048:2176], c128, a128, b128, h128)
    kd_o[...] = kd
    kdb_o[...] = kd.astype(BF16)
    vd = p_ref[:, 2176:2304]
    vd_o[...] = vd
    vdb_o[...] = vd.astype(BF16)
    vdt_o[0] = vd.T.astype(BF16)
    for t in range(2):
        qi = rot(p_ref[:, 2304 + t * LANES:2304 + (t + 1) * LANES], c64, a64, b64, h64) * (D_IDX ** -0.5)
        qi_o[:, (2 * t) * LANES:(2 * t + 1) * LANES] = qi.astype(BF16)
        qi_o[:, (2 * t + 1) * LANES:(2 * t + 2) * LANES] = pltpu.roll(qi, D_IDX, axis=1).astype(BF16)
    last = p_ref[:, 2560:2688]
    ki = rot(last, c64, a64, b64, h64)[:, :D_IDX]
    ki_o[...] = ki
    kib_o[...] = ki.astype(BF16)
    wi_o[...] = pltpu.roll(last, D_IDX, axis=1) * (H_I ** -0.5)


def att_prep(x, g, w, tabs64, tabs128, *, tm):
    m, d = x.shape
    p = tabs64[0].shape[0]
    assert m % tm == 0 and p % tm == 0
    nper = p // tm
    tab_spec = pl.BlockSpec((tm, LANES), lambda i: (i % nper, 0))
    outs = [(512, BF16, 1), (2 * DC, F32, H_C), (512, BF16, 1), (2 * DC, F32, H_C), (512, BF16, 1), (512, BF16, 1),
            (128, F32, 1), (128, BF16, 1), (128, F32, 1), (128, BF16, 1), (512, BF16, 1), (D_IDX, F32, 1),
            (D_IDX, BF16, 1), (128, F32, 1)]
    outs_t = [512, 128]
    return pl.pallas_call(
        _att_prep_kernel,
        grid=(m // tm,),
        in_specs=[pl.BlockSpec((tm, d), lambda i: (i, 0)),
                  pl.BlockSpec((1, d), lambda i: (0, 0)),
                  pl.BlockSpec((d, ATT_N), lambda i: (0, 0), pipeline_mode=pl.Buffered(1))] + [tab_spec] * 6,
        out_specs=([pl.BlockSpec((tm * r, w), lambda i: (i, 0)) for w, _, r in outs]
                   + [pl.BlockSpec((1, w, tm), lambda i: (i, 0, 0)) for w in outs_t]),
        out_shape=([jax.ShapeDtypeStruct((m * r, w), dt) for w, dt, r in outs]
                   + [jax.ShapeDtypeStruct((m // tm, w, tm), BF16) for w in outs_t]),
        scratch_shapes=[pltpu.VMEM((tm, ATT_N), F32)],
        compiler_params=_cparams(("parallel",)),
        name="att_prep",
    )(x, g.reshape(1, d), w, *tabs64, *tabs128)


QB = 128


def _diff_lambda(lam_ref, lam_init):
    dl = lam_ref[...]
    s1 = jnp.sum(dl[0:1, :] * dl[1:2, :], axis=1, keepdims=True)
    s2 = jnp.sum(dl[2:3, :] * dl[3:4, :], axis=1, keepdims=True)
    return jnp.exp(s1) - jnp.exp(s2) + lam_init


KB = 384
SUB = 8


def _group_max(x):
    return jnp.max(x.reshape(x.shape[0] // SUB, SUB, x.shape[1]), axis=0)


def _group_sum(x):
    return jnp.sum(x.reshape(x.shape[0] // SUB, SUB, x.shape[1]), axis=0)


def _key_visible(i, off, n_rep, front):
    krow = lax.broadcasted_iota(jnp.int32, (KB, 1), 0)
    lane = lax.broadcasted_iota(jnp.int32, (1, n_rep * KB), 1)
    q = lane
    for r in range(1, n_rep):
        q = jnp.where(lane >= r * KB, lane - r * KB, q)
    return ((i * KB + q - krow) >= off) & (krow >= front - off)


def _fori_pairs(lo, hi, one, two, init):
    n = jnp.maximum(hi - lo, 0)
    odd = n % 2
    carry = lax.fori_loop(0, odd, lambda t, c: one(lo, c), init)
    return lax.fori_loop(0, n // 2, lambda t, c: two(lo + odd + 2 * t, c), carry)


def _edge_then_middle(i, edge_body, middle_body, init):
    carry = edge_body(i, edge_body(0, init))
    return _fori_pairs(1, i, middle_body, lambda kb, c: middle_body(kb + 1, middle_body(kb, c)), carry)


def _diff_prompt_kernel(q_ref, k_ref, vt_ref, lam_ref, g_ref, o_ref, s_ref, *, front, lam_init):
    i = pl.program_id(1)
    lam = _diff_lambda(lam_ref, lam_init)
    lane = lax.broadcasted_iota(jnp.int32, (KB, LANES), 1)
    for h in range(H_C):
        cs = slice(h * LANES, (h + 1) * LANES)
        qh = q_ref[:, cs]
        qstack = jnp.concatenate([jnp.where(lane < DC, qh, jnp.zeros_like(qh)),
                                  jnp.where(lane >= DC, qh, jnp.zeros_like(qh))], axis=0)

        def scores(kb, masked):
            off = pl.multiple_of(kb * KB, KB)
            st = _dot_nt(k_ref[pl.ds(off, KB), cs], qstack)
            return jnp.where(_key_visible(i, off, 2, front), st, NEG) if masked else st

        def pass_a(kb, mx, masked):
            st = scores(kb, masked)
            s_ref[kb] = st
            return jnp.maximum(mx, _group_max(st))

        mx = _edge_then_middle(i, functools.partial(pass_a, masked=True), functools.partial(pass_a, masked=False),
                               jnp.full((SUB, 2 * KB), NEG, F32))
        m = jnp.max(mx, axis=0, keepdims=True)

        def pass_b(kb, carry):
            l8, acc = carry
            p = jnp.exp(s_ref[kb] - m)
            return l8 + _group_sum(p), acc + _dot(vt_ref[kb, cs, :], p.astype(BF16))

        def pass_b2(kb, carry):
            l8, acc = carry
            p = jnp.exp(jnp.concatenate([s_ref[kb], s_ref[kb + 1]], axis=0) - m)
            vt2 = jnp.concatenate([vt_ref[kb, cs, :], vt_ref[kb + 1, cs, :]], axis=1)
            return l8 + _group_sum(p), acc + _dot(vt2, p.astype(BF16))

        l8, acc = _fori_pairs(0, i + 1, pass_b, pass_b2,
                              (jnp.zeros((SUB, 2 * KB), F32), jnp.zeros((LANES, 2 * KB), F32)))
        a = acc / jnp.sum(l8, axis=0, keepdims=True)
        ot = a[:, :KB] - lam * a[:, KB:]
        ot = ot * lax.rsqrt(jnp.mean(ot * ot, axis=0, keepdims=True) + EPS) * g_ref[...] * (1.0 - lam_init)
        o_ref[:, cs] = ot.T.astype(o_ref.dtype)


def diff_prompt(q, k, vt, lam_p, g_c, *, b, t_pad, front, lam_init):
    nkb = t_pad // KB
    return pl.pallas_call(
        functools.partial(_diff_prompt_kernel, front=front, lam_init=lam_init),
        grid=(b, nkb),
        in_specs=[pl.BlockSpec((KB, 512), lambda bb, i: (bb * nkb + i, 0)),
                  pl.BlockSpec((t_pad, 512), lambda bb, i: (bb, 0)),
                  pl.BlockSpec((nkb, 512, KB), lambda bb, i: (bb, 0, 0)),
                  pl.BlockSpec((4, DC), lambda bb, i: (0, 0)),
                  pl.BlockSpec((2 * DC, 1), lambda bb, i: (0, 0))],
        out_specs=pl.BlockSpec((KB, 512), lambda bb, i: (bb * nkb + i, 0)),
        out_shape=jax.ShapeDtypeStruct((b * t_pad, 512), BF16),
        scratch_shapes=[pltpu.VMEM((nkb, KB, 2 * KB), F32)],
        compiler_params=_cparams(("parallel", "arbitrary")),
        name="diff_prompt",
    )(q, k, vt, lam_p, g_c.reshape(2 * DC, 1))


NINF = float("-inf")


def _kth_threshold(count_ge, shape, n_sel):
    zero_i = jnp.zeros(shape, jnp.int32)
    neg = jnp.where(count_ge(jnp.zeros(shape, F32)) < n_sel, 1, 0)
    sign = jnp.where(neg == 1, jnp.int32(INT_MIN), 0)

    def bit_body(t, mag):
        cand = mag | lax.shift_left(jnp.int32(1), 30 - t)
        enough = jnp.where(count_ge(pltpu.bitcast(cand | sign, F32)) >= n_sel, 1, 0)
        return jnp.where(enough + neg == 1, cand, mag)

    mag = lax.fori_loop(0, 31, bit_body, zero_i)
    tau = pltpu.bitcast(jnp.where(neg == 1, (mag + 1) | sign, mag), F32)
    ninf = jnp.full(shape, NINF, F32)
    return jnp.where(count_ge(ninf) >= n_sel, tau, ninf)


def _raise_to_kth_score(tau, count, min_where, n_sel):
    def cond(c):
        return jnp.max(c[1]) >= n_sel

    def body(c):
        tau, above = c
        tau = jnp.where(above >= n_sel, min_where(lambda sc: sc > tau), tau)
        return tau, count(lambda sc: sc > tau)

    return lax.while_loop(cond, body, (tau, count(lambda sc: sc > tau)))


def _dsa_prompt_kernel(qi_ref, wi_ref, qd_ref, ki_ref, kd_ref, vdt_ref, o_ref, sc_ref, sel_ref, s_ref, *, front, n_sel):
    i = pl.program_id(1)
    nkb = i + 1
    qi_all = jnp.concatenate([qi_ref[:, h * LANES:h * LANES + D_IDX] for h in range(H_I)], axis=0)
    wt = wi_ref[...].T
    w_row = jnp.concatenate([wt[h:h + 1, :] for h in range(H_I)], axis=1)

    def stage1(kb, c, masked):
        off = pl.multiple_of(kb * KB, KB)
        sct = jnp.maximum(_dot_nt(ki_ref[pl.ds(off, KB), :], qi_all), 0.0) * w_row
        score = sct[:, 0:KB] + sct[:, KB:2 * KB] + sct[:, 2 * KB:3 * KB] + sct[:, 3 * KB:4 * KB]
        sc_ref[kb] = jnp.where(_key_visible(i, off, 1, front), score, NINF) if masked else score
        return c

    _edge_then_middle(i, functools.partial(stage1, masked=True), functools.partial(stage1, masked=False), 0)

    def count(pred):
        def body(kb, acc):
            return acc + _group_sum(jnp.where(pred(sc_ref[kb]), 1, 0))
        return jnp.sum(lax.fori_loop(0, nkb, body, jnp.zeros((SUB, KB), jnp.int32)), axis=0, keepdims=True)

    def min_where(pred):
        def body(kb, acc):
            sc = sc_ref[kb]
            return jnp.minimum(acc, -_group_max(jnp.where(pred(sc), -sc, NINF)))
        return jnp.min(lax.fori_loop(0, nkb, body, jnp.full((SUB, KB), -NINF, F32)), axis=0, keepdims=True)

    tau = _kth_threshold(lambda cand: count(lambda sc: sc >= cand), (1, KB), n_sel)
    tau, above = _raise_to_kth_score(tau, count, min_where, n_sel)
    need = (n_sel - above).astype(F32)

    strict_lower = jnp.where(lax.broadcasted_iota(jnp.int32, (KB, KB), 1) < lax.broadcasted_iota(jnp.int32, (KB, KB), 0),
                             1.0, 0.0).astype(BF16)

    def select(kb, before):
        sc = sc_ref[kb]
        eq = sc == tau
        eqf = jnp.where(eq, 1.0, 0.0)
        rank = _dot(strict_lower, eqf.astype(BF16)) + before
        sel_ref[kb] = jnp.where(((sc > tau) | (eq & (rank < need))) & (sc > NINF), 0.0, NEG)
        return before + jnp.sum(eqf, axis=0, keepdims=True)

    lax.fori_loop(0, nkb, select, jnp.zeros((1, KB), F32))

    for h in range(H_D):
        cs = slice(h * LANES, (h + 1) * LANES)
        qd = qd_ref[:, cs]

        def stage3(kb, mx):
            off = pl.multiple_of(kb * KB, KB)
            sdt = _dot_nt(kd_ref[pl.ds(off, KB), :], qd) * (DH_D ** -0.5) + sel_ref[kb]
            s_ref[kb] = sdt
            return jnp.maximum(mx, _group_max(sdt))

        m = jnp.max(_fori_pairs(0, nkb, stage3, lambda kb, c: stage3(kb + 1, stage3(kb, c)),
                                jnp.full((SUB, KB), NEG, F32)), axis=0, keepdims=True)

        def stage4(kb, carry):
            l8, acc = carry
            p = jnp.exp(s_ref[kb] - m)
            return l8 + _group_sum(p), acc + _dot(vdt_ref[kb], p.astype(BF16))

        def stage4_pair(kb, carry):
            l8, acc = carry
            p = jnp.exp(jnp.concatenate([s_ref[kb], s_ref[kb + 1]], axis=0) - m)
            vt2 = jnp.concatenate([vdt_ref[kb], vdt_ref[kb + 1]], axis=1)
            return l8 + _group_sum(p), acc + _dot(vt2, p.astype(BF16))

        l8, acc = _fori_pairs(0, nkb, stage4, stage4_pair, (jnp.zeros((SUB, KB), F32), jnp.zeros((DH_D, KB), F32)))
        o_ref[:, cs] = (acc / jnp.sum(l8, axis=0, keepdims=True)).T.astype(o_ref.dtype)


def dsa_prompt(qi, wi, qd, ki, kd, vdt, *, b, t_pad, front, n_sel):
    nkb = t_pad // KB
    qspec = lambda w: pl.BlockSpec((KB, w), lambda bb, i: (bb * nkb + i, 0))
    kspec = lambda w: pl.BlockSpec((t_pad, w), lambda bb, i: (bb, 0))
    return pl.pallas_call(
        functools.partial(_dsa_prompt_kernel, front=front, n_sel=n_sel),
        grid=(b, nkb),
        in_specs=[qspec(512), qspec(128), qspec(512), kspec(D_IDX), kspec(DH_D),
                  pl.BlockSpec((nkb, DH_D, KB), lambda bb, i: (bb, 0, 0))],
        out_specs=qspec(512),
        out_shape=jax.ShapeDtypeStruct((b * t_pad, 512), BF16),
        scratch_shapes=[pltpu.VMEM((nkb, KB, KB), F32)] * 3,
        compiler_params=_cparams(("parallel", "arbitrary")),
        name="dsa_prompt",
    )(qi, wi, qd, ki, kd, vdt)


TS = 16


PG = 16
PAGE_BUFFERS = 2


def _page_map(g, n_pages, nd):
    def index(bb, j, pt):
        return (pt[bb * n_pages + jnp.minimum(j * PG + g, n_pages - 1)],) + (0,) * nd
    return index


def _online_softmax_update(s, vs, m_ref, l_ref, acc_ref):
    m_old = m_ref[...]
    m_new = jnp.maximum(m_old, jnp.max(s, axis=1, keepdims=True))
    alpha = jnp.exp(m_old - m_new)
    p = jnp.where(s > 0.5 * NEG, jnp.exp(s - m_new), 0.0)
    l_ref[...] = alpha * l_ref[...] + jnp.sum(p, axis=1, keepdims=True)
    r = s.shape[0] // len(vs)
    pv = jnp.concatenate([_dot(p[g * r:(g + 1) * r].astype(BF16), v) for g, v in enumerate(vs)], axis=0)
    acc_ref[...] = alpha * acc_ref[...] + pv
    m_ref[...] = m_new


def _sample_a_kernel(pt_ref, qc_ref, qi_ref, wi_ref, *refs, n_steps, front, lam_init):
    ck_refs, cv_refs, cik_refs = refs[:PG], refs[PG:2 * PG], refs[2 * PG:3 * PG]
    kn_ref, vn_ref, kin_ref, lam_ref, g_ref, o_ref, keys_ref, m_ref, l_ref, acc_ref = refs[3 * PG:]
    j = pl.program_id(1)
    lane = lax.broadcasted_iota(jnp.int32, (TS, LANES), 1)

    @pl.when(j == 0)
    def _():
        m_ref[...] = jnp.full(m_ref.shape, NEG, F32)
        l_ref[...] = jnp.zeros(l_ref.shape, F32)
        acc_ref[...] = jnp.zeros(acc_ref.shape, F32)

    def qstack(h):
        qh = qc_ref[:, h * LANES:(h + 1) * LANES]
        return jnp.concatenate([jnp.where(lane < DC, qh, jnp.zeros_like(qh)),
                                jnp.where(lane >= DC, qh, jnp.zeros_like(qh))], axis=0)

    def head_rows(h):
        return slice(h * 2 * TS, (h + 1) * 2 * TS)

    qi_all = jnp.concatenate([qi_ref[:, h * LANES:h * LANES + D_IDX] for h in range(H_I)], axis=0)
    w_col = jnp.concatenate([wi_ref[:, h:h + 1] for h in range(H_I)], axis=0)

    def idx_scores(kipt):
        sc = jnp.maximum(_dot(qi_all, kipt), 0.0) * w_col
        return sc[0:TS] + sc[TS:2 * TS] + sc[2 * TS:3 * TS] + sc[3 * TS:4 * TS]

    @pl.when(j < n_steps)
    def _():
        ss, vs = [], []
        for h in range(H_C):
            rows_h = pl.ds(h, PAGE_SIZE, stride=H_C)
            k = jnp.concatenate([r[0, rows_h, :].astype(BF16) for r in ck_refs], axis=0)
            vs.append(jnp.concatenate([r[0, rows_h, :].astype(BF16) for r in cv_refs], axis=0))
            ss.append(_dot_nt(qstack(h), k))
        _online_softmax_update(jnp.concatenate(ss, axis=0), vs, m_ref, l_ref, acc_ref)
        keys_ref[0] = idx_scores(jnp.concatenate([r[0].astype(BF16) for r in cik_refs], axis=1))

    @pl.when(j == n_steps)
    def _():
        kr = lax.broadcasted_iota(jnp.int32, (1, TS), 1)
        q_of_row = lax.broadcasted_iota(jnp.int32, (2 * TS, 1), 0) & (TS - 1)
        ok = (kr >= front) & (kr <= q_of_row)
        ss = [jnp.where(ok, _dot_nt(qstack(h), kn_ref[:, h * LANES:(h + 1) * LANES]), NEG) for h in range(H_C)]
        vs = [vn_ref[:, h * LANES:(h + 1) * LANES] for h in range(H_C)]
        _online_softmax_update(jnp.concatenate(ss, axis=0), vs, m_ref, l_ref, acc_ref)
        krp = lax.broadcasted_iota(jnp.int32, (1, PAGE_SIZE), 1)
        okq = (krp >= front) & (krp < TS) & (krp <= lax.broadcasted_iota(jnp.int32, (TS, 1), 0))
        keys_ref[0] = jnp.concatenate([jnp.where(okq, idx_scores(kin_ref[0]), NINF),
                                       jnp.full((TS, (PG - 1) * PAGE_SIZE), NINF, F32)], axis=1)
        lam = _diff_lambda(lam_ref, lam_init)
        for h in range(H_C):
            a = acc_ref[head_rows(h), :] / l_ref[head_rows(h), :]
            o = a[0:TS] - lam * a[TS:2 * TS]
            o_ref[:, h * LANES:(h + 1) * LANES] = (_rms(o, g_ref[...]) * (1.0 - lam_init)).astype(o_ref.dtype)


def sample_diff_idx(pt, qc, qi, wi, ck, cv, cik, kn, vn, kin, lam_p, g_c, *, b, n_pages, front, lam_init):
    assert n_pages % PG == 0
    n_steps = n_pages // PG
    rows = H_C * 2 * TS
    qspec = lambda w: pl.BlockSpec((TS, w), lambda bb, j, pt: (bb, 0))
    deep = pl.Buffered(PAGE_BUFFERS)
    pages4 = [pl.BlockSpec((1, PAGE_SIZE * H_C, 2 * DC), _page_map(g, n_pages, 2), pipeline_mode=deep) for g in range(PG)]
    pages_i = [pl.BlockSpec((1, D_IDX, PAGE_SIZE), _page_map(g, n_pages, 2), pipeline_mode=deep) for g in range(PG)]
    return pl.pallas_call(
        functools.partial(_sample_a_kernel, n_steps=n_steps, front=front, lam_init=lam_init),
        grid_spec=pltpu.PrefetchScalarGridSpec(
            num_scalar_prefetch=1,
            grid=(b, n_steps + 1),
            in_specs=[qspec(512), qspec(512), qspec(128)] + pages4 + pages4 + pages_i
                     + [qspec(512), qspec(512), pl.BlockSpec((1, D_IDX, PAGE_SIZE), lambda bb, j, pt: (bb, 0, 0)),
                        pl.BlockSpec((4, DC), lambda bb, j, pt: (0, 0)),
                        pl.BlockSpec((1, 2 * DC), lambda bb, j, pt: (0, 0))],
            out_specs=[qspec(512), pl.BlockSpec((1, TS, PG * PAGE_SIZE), lambda bb, j, pt: (bb, 0, j))],
            scratch_shapes=[pltpu.VMEM((rows, 1), F32), pltpu.VMEM((rows, 1), F32), pltpu.VMEM((rows, 2 * DC), F32)]),
        out_shape=[jax.ShapeDtypeStruct((b * TS, 512), BF16),
                   jax.ShapeDtypeStruct((b, TS, (n_steps + 1) * PG * PAGE_SIZE), F32)],
        compiler_params=_cparams(("parallel", "arbitrary")),
        name="sample_diff_idx",
    )(pt, qc, qi, wi, *([ck] * PG), *([cv] * PG), *([cik] * PG), kn, vn, kin, lam_p, g_c.reshape(1, 2 * DC))


def _sample_threshold_kernel(keys_ref, tau_ref, need_ref, *, n_sel):
    rows = keys_ref.shape[0]

    def count(pred):
        return jnp.sum(jnp.where(pred(keys_ref[...]), 1, 0), axis=1, keepdims=True)

    def min_where(pred):
        sc = keys_ref[...]
        return jnp.min(jnp.where(pred(sc), sc, -NINF), axis=1, keepdims=True)

    tau = _kth_threshold(lambda cand: count(lambda sc: sc >= cand), (rows, 1), n_sel)
    tau, above = _raise_to_kth_score(tau, count, min_where, n_sel)
    tau_ref[...] = tau
    need_ref[...] = (n_sel - above).astype(F32)


def sample_threshold(keys, *, n_sel):
    rows, nk = keys.shape
    return pl.pallas_call(
        functools.partial(_sample_threshold_kernel, n_sel=n_sel),
        grid=(1,),
        in_specs=[pl.BlockSpec((rows, nk), lambda i: (0, 0), pipeline_mode=pl.Buffered(1))],
        out_specs=[pl.BlockSpec((rows, 1), lambda i: (0, 0))] * 2,
        out_shape=[jax.ShapeDtypeStruct((rows, 1), F32)] * 2,
        compiler_params=_cparams(("arbitrary",)),
        name="sample_threshold",
    )(keys)


def _sample_b_kernel(pt_ref, tau_ref, need_ref, keys_ref, qd_ref, *refs, n_steps):
    ck_refs, cv_refs = refs[:PG], refs[PG:2 * PG]
    kn_ref, vn_ref, o_ref, before_ref, m_ref, l_ref, acc_ref = refs[2 * PG:]
    j = pl.program_id(1)

    @pl.when(j == 0)
    def _():
        before_ref[...] = jnp.zeros(before_ref.shape, F32)
        m_ref[...] = jnp.full(m_ref.shape, NEG, F32)
        l_ref[...] = jnp.zeros(l_ref.shape, F32)
        acc_ref[...] = jnp.zeros(acc_ref.shape, F32)

    strict_upper = jnp.where(lax.broadcasted_iota(jnp.int32, (PAGE_SIZE, PAGE_SIZE), 0)
                             < lax.broadcasted_iota(jnp.int32, (PAGE_SIZE, PAGE_SIZE), 1), 1.0, 0.0).astype(BF16)
    qd_all = jnp.concatenate([qd_ref[:, h * LANES:(h + 1) * LANES] for h in range(H_D)], axis=0)

    def select(key, before):
        tau = tau_ref[...]
        eq = key == tau
        eqf = jnp.where(eq, 1.0, 0.0)
        rank = _dot(eqf.astype(BF16), strict_upper) + before
        sel = ((key > tau) | (eq & (rank < need_ref[...]))) & (key > NINF)
        return sel, before + jnp.sum(eqf, axis=1, keepdims=True)

    def attend(s, sel, v):
        s = jnp.where(jnp.concatenate([sel] * H_D, axis=0), s * (DH_D ** -0.5), NEG)
        _online_softmax_update(s, [v], m_ref, l_ref, acc_ref)

    @pl.when(j < n_steps)
    def _():
        key = keys_ref[0]
        before = before_ref[...]
        sels = []
        for g in range(PG):
            sel, before = select(key[:, g * PAGE_SIZE:(g + 1) * PAGE_SIZE], before)
            sels.append(sel)
        before_ref[...] = before
        k = jnp.concatenate([r[0].astype(BF16) for r in ck_refs], axis=0)
        v = jnp.concatenate([r[0].astype(BF16) for r in cv_refs], axis=0)
        attend(_dot_nt(qd_all, k), jnp.concatenate(sels, axis=1), v)

    @pl.when(j == n_steps)
    def _():
        sel, _ = select(keys_ref[0][:, :PAGE_SIZE], before_ref[...])
        attend(_dot_nt(qd_all, kn_ref[0]), sel, vn_ref[0])
        o = acc_ref[...] / l_ref[...]
        for h in range(H_D):
            o_ref[:, h * LANES:(h + 1) * LANES] = o[h * TS:(h + 1) * TS].astype(o_ref.dtype)


def sample_dsa(pt, keys, qd, ck, cv, kn, vn, *, b, n_pages, n_sel):
    assert n_pages % PG == 0
    n_steps = n_pages // PG
    rows = H_D * TS
    tau, need = sample_threshold(keys.reshape(b * TS, keys.shape[2]), n_sel=n_sel)
    pages = [pl.BlockSpec((1, PAGE_SIZE, DH_D), _page_map(g, n_pages, 2), pipeline_mode=pl.Buffered(PAGE_BUFFERS))
             for g in range(PG)]
    nspec = pl.BlockSpec((1, PAGE_SIZE, DH_D), lambda bb, j, pt: (bb, 0, 0))
    col = pl.BlockSpec((TS, 1), lambda bb, j, pt: (bb, 0))
    return pl.pallas_call(
        functools.partial(_sample_b_kernel, n_steps=n_steps),
        grid_spec=pltpu.PrefetchScalarGridSpec(
            num_scalar_prefetch=1,
            grid=(b, n_steps + 1),
            in_specs=[col, col,
                      pl.BlockSpec((1, TS, PG * PAGE_SIZE), lambda bb, j, pt: (bb, 0, j)),
                      pl.BlockSpec((TS, 512), lambda bb, j, pt: (bb, 0))] + pages + pages + [nspec, nspec],
            out_specs=pl.BlockSpec((TS, 512), lambda bb, j, pt: (bb, 0)),
            scratch_shapes=[pltpu.VMEM((TS, 1), F32),
                            pltpu.VMEM((rows, 1), F32), pltpu.VMEM((rows, 1), F32), pltpu.VMEM((rows, DH_D), F32)]),
        out_shape=jax.ShapeDtypeStruct((b * TS, 512), BF16),
        compiler_params=_cparams(("parallel", "arbitrary")),
        name="sample_dsa",
    )(pt, tau, need, keys, qd, *([ck] * PG), *([cv] * PG), kn, vn)


REC_CHUNK = 64
REC_SUB = 16
TM_PROJ = 512
TM_REC, TN_REC = 1408, 1408
TM_ROWS = 704
TM_FFN = 528


def _pad_cols(w, n):
    return jnp.pad(w, ((0, 0), (0, n - w.shape[1])))


def _tile_rows(m, pref):
    return pref if m % pref == 0 else m


def kernel(x_prompt, x_sample, state_hgrn, state_mlstm_C, state_mlstm_n, state_mlstm_m, state_ffn_conv, cache_diff_k, cache_diff_v, cache_dsa_k, cache_dsa_v, cache_idx_k, page_table, meta_tokens, norm_gains, w_in_rec, b_gates_rec, lb_logits, g_norm_hgrn, g_norm_mlstm, w_out_rec, w_in_att, diff_lambda, g_norm_diff, w_out_att, w_ffn_up, ffn_conv_w, ffn_conv_b, w_ffn_down):
    bp, t_in, d = x_prompt.shape
    bs, t_s, _ = x_sample.shape
    depth = norm_gains.shape[0]
    n_pages = page_table.shape[1]
    past_len = n_pages * PAGE_SIZE
    real_p = N_META + t_in
    tp = -(-real_p // QB) * QB
    front_p = tp - real_p
    front_s = TS - t_s
    assert tp % REC_CHUNK == 0 and tp % TM_ROWS == 0 and front_p >= CONV_W - 1 and front_s >= CONV_W - 1
    mp, ms = bp * tp, bs * TS

    meta = jnp.broadcast_to(meta_tokens.astype(x_prompt.dtype)[None], (bp, N_META, d))
    xp = jnp.concatenate([jnp.zeros((bp, front_p, d), x_prompt.dtype), meta, x_prompt], axis=1).reshape(mp, d)
    xs = jnp.concatenate([jnp.zeros((bs, front_s, d), x_sample.dtype), x_sample], axis=1).reshape(ms, d)
    lb_all = jnp.cumsum(jax.nn.softmax(lb_logits.astype(F32), axis=0), axis=0)
    pt_flat = page_table.reshape(-1).astype(jnp.int32)
    sel_p = min(TOPK_MAX, t_in // 4)
    sel_s = min(TOPK_MAX, (past_len + t_s) // 4)
    tmp_s = _tile_rows(ms, TM_PROJ)
    tmr_s = _tile_rows(ms, TM_ROWS)

    pos_p = jnp.arange(tp, dtype=jnp.int32) - front_p
    pos_s = jnp.tile(past_len + jnp.arange(TS, dtype=jnp.int32) - front_s, ms // TS)
    tabs_p = (rope_tables(pos_p, DC, DC // ROT_FRAC // 2), rope_tables(pos_p, DH_D, DH_D // ROT_FRAC // 2))
    tabs_s = (rope_tables(pos_s, DC, DC // ROT_FRAC // 2), rope_tables(pos_s, DH_D, DH_D // ROT_FRAC // 2))

    rec_p, rec_s = [[], [], [], []], [[], [], [], []]
    att_p, att_s = [[], [], [], [], []], [[], [], [], [], []]
    conv_p, conv_s = [], []
    for l in range(depth):
        p = l // 2
        g = norm_gains[l].astype(F32)
        if l % 2 == 0:
            w_in = _pad_cols(w_in_rec[p], REC_N).astype(BF16)
            w_out = w_out_rec[p].astype(BF16)
            prm = (lb_all[p], b_gates_rec[p].astype(F32), g_norm_hgrn[p].astype(F32), g_norm_mlstm[p].astype(F32))
            proj = norm_matmul(xp, g[0], w_in, tm=_tile_rows(mp, TM_REC), tn=TN_REC).reshape(bp, tp, REC_N)
            zs = jnp.zeros((bp, 4, 128, 128), F32)
            y, *st = rec_mixer(proj, *prm, zs, zs, jnp.zeros((bp, 4, 128), F32), jnp.zeros((bp, 4), F32),
                               C=REC_CHUNK, W=REC_SUB, front=front_p)
            xp = matmul_norm_res([y.reshape(mp, -1)], [w_out], g[1], xp, tm=TM_ROWS, t_pad=tp, front=front_p)
            for j in range(4):
                rec_p[j].append(st[j])
            proj = norm_matmul(xs, g[0], w_in, tm=tmp_s, tn=384).reshape(bs, TS, REC_N)
            y, *st = rec_mixer(proj, *prm, state_hgrn[p].astype(F32), state_mlstm_C[p].astype(F32),
                               state_mlstm_n[p].astype(F32), state_mlstm_m[p].astype(F32), C=TS, W=TS, front=front_s)
            xs = matmul_norm_res([y.reshape(ms, -1)], [w_out], g[1], xs, tm=tmr_s, t_pad=TS, front=front_s)
            for j in range(4):
                rec_s[j].append(st[j])
        else:
            lam_init = 0.8 - 0.6 * math.exp(-0.3 * l)
            w_in = _pad_cols(w_in_att[p], ATT_N).astype(BF16)
            w_out = w_out_att[p].astype(BF16)
            dl, gc = diff_lambda[p].astype(F32), g_norm_diff[p].astype(F32)
            (qc, kc, kcb, vc, vcb, qd, kd, kdb, vd, vdb, qi, ki, kib, wi, vct, vdt) = att_prep(xp, g[0], w_in, *tabs_p, tm=KB)
            oc = diff_prompt(qc, kcb, vct, dl, gc, b=bp, t_pad=tp, front=front_p, lam_init=lam_init)
            od = dsa_prompt(qi, wi, qd, kib, kdb, vdt, b=bp, t_pad=tp, front=front_p, n_sel=sel_p)
            xp = matmul_norm_res([oc, od], [w_out[:512], w_out[512:]], g[1], xp, tm=TM_ROWS, t_pad=tp, front=front_p)
            for j, (a, shp) in enumerate([(kc, (H_C, 2 * DC)), (vc, (H_C, 2 * DC)), (kd, (DH_D,)), (vd, (DH_D,)), (ki, (D_IDX,))]):
                att_p[j].append(a.reshape((bp, tp) + shp)[:, front_p:])
            (qc, kc, kcb, vc, vcb, qd, kd, kdb, vd, vdb, qi, ki, kib, wi, _, _) = att_prep(xs, g[0], w_in, *tabs_s, tm=ms)
            as_page = lambda a: jnp.pad(a.reshape(bs, TS, -1), ((0, 0), (0, PAGE_SIZE - TS), (0, 0)))
            rows_kh = lambda c: c.reshape(c.shape[0], PAGE_SIZE * H_C, 2 * DC)
            oc, keys = sample_diff_idx(pt_flat, qc, qi, wi, rows_kh(cache_diff_k[p]), rows_kh(cache_diff_v[p]),
                                       jnp.swapaxes(cache_idx_k[p], 1, 2), kcb, vcb, jnp.swapaxes(as_page(kib), 1, 2), dl, gc,
                                       b=bs, n_pages=n_pages, front=front_s, lam_init=lam_init)
            od = sample_dsa(pt_flat, keys, qd, cache_dsa_k[p], cache_dsa_v[p], as_page(kdb), as_page(vdb),
                            b=bs, n_pages=n_pages, n_sel=sel_s)
            xs = matmul_norm_res([oc, od], [w_out[:512], w_out[512:]], g[1], xs, tm=tmr_s, t_pad=TS, front=front_s)
            for j, (a, shp) in enumerate([(kc, (H_C, 2 * DC)), (vc, (H_C, 2 * DC)), (kd, (DH_D,)), (vd, (DH_D,)), (ki, (D_IDX,))]):
                att_s[j].append(a.reshape((bs, TS) + shp)[:, front_s:])
        w_up, w_down = w_ffn_up[l].astype(BF16), w_ffn_down[l].astype(BF16)
        cw, cb = ffn_conv_w[l].astype(F32), ffn_conv_b[l].astype(F32)
        xp, tail = ffn_fused(xp, g[2], w_up, cw, cb, w_down, g[3], tm=TM_FFN, t_pad=tp, front=front_p)
        conv_p.append(tail[:, SUB - (CONV_W - 1):])
        up = norm_matmul(xs, g[2], w_up, tm=tmp_s, tn=512)
        conv_s.append(up.reshape(bs, TS, 2 * D_FF)[:, TS - (CONV_W - 1):, :D_FF])
        buf = jnp.pad(state_ffn_conv[l].astype(F32), ((0, 0), (front_s - (CONV_W - 1), TS - front_s), (0, 0)))
        xs = ffn_down(up, buf.reshape(ms, D_FF), cw, cb, w_down, g[3], xs, tm=tmr_s, t_pad=TS, front=front_s)

    y_p = xp.reshape(bp, tp, d)[:, front_p + N_META:]
    y_s = xs.reshape(bs, TS, d)[:, front_s:]
    stack = lambda xs: xs[0][None] if len(xs) == 1 else jnp.stack(xs)
    return (y_p, y_s,
            stack(rec_p[0]), stack(rec_s[0]), stack(rec_p[1]), stack(rec_s[1]),
            stack(rec_p[2]), stack(rec_s[2]), stack(rec_p[3]), stack(rec_s[3]),
            stack(conv_p), stack(conv_s),
            stack(att_p[0]), stack(att_s[0]), stack(att_p[1]), stack(att_s[1]),
            stack(att_p[2]), stack(att_s[2]), stack(att_p[3]), stack(att_s[3]),
            stack(att_p[4]), stack(att_s[4]))
```

```python
import functools
import math

import jax
import jax.numpy as jnp
import numpy as np
from jax import lax
from jax.experimental import pallas as pl
from jax.experimental.pallas import tpu as pltpu

F32 = jnp.float32
BF16 = jnp.bfloat16

D_MODEL = 1024
N_META = 16
H_A, DK_A, DV_A = 4, 128, 128
H_B, DK_B, DV_B = 4, 128, 128
H_C, DC = 4, 64
H_D, DH_D = 4, 128
H_I, D_IDX = 4, 64
TOPK_MAX = 256
D_FF = 2816
CONV_W = 3
ROPE_THETA = 500000.0
ROT_FRAC = 4
EPS = 1e-6
PAGE_SIZE = 128
LANES = 128
NEG = -1e30

REC_N = 8 * 512 + LANES
ATT_N = 2560 + LANES
INT_MIN = -2 ** 31
VMEM_LIMIT = 56 * 1024 * 1024


def _cparams(sem):
    return pltpu.CompilerParams(dimension_semantics=sem, vmem_limit_bytes=VMEM_LIMIT)


def _rms(x, g):
    return x * lax.rsqrt(jnp.mean(x * x, axis=-1, keepdims=True) + EPS) * g


def _dot(a, b):
    return jnp.dot(a, b, preferred_element_type=F32)


def _dot_nt(a, b):
    return lax.dot_general(a, b, (((1,), (1,)), ((), ())), preferred_element_type=F32)


def _dot_tn(a, b):
    return lax.dot_general(a, b, (((0,), (0,)), ((), ())), preferred_element_type=F32)


def _dot_exact_lhs(tri, x):
    hi = x.astype(BF16)
    r1 = x - hi.astype(F32)
    mid = r1.astype(BF16)
    lo = (r1 - mid.astype(F32)).astype(BF16)
    return _dot(tri, hi) + _dot(tri, mid) + _dot(tri, lo)


def _row_valid(i, tm, t_pad, front):
    r = lax.broadcasted_iota(jnp.int32, (tm, 1), 0)
    if t_pad % tm == 0:
        t = (i % (t_pad // tm)) * tm + r
    else:
        assert tm % t_pad == 0 and (t_pad & (t_pad - 1)) == 0
        t = r & (t_pad - 1)
    return t >= front


def _norm_matmul_kernel(x_ref, g_ref, w_ref, o_ref, h_ref):
    @pl.when(pl.program_id(1) == 0)
    def _():
        h_ref[...] = _rms(x_ref[...], g_ref[...]).astype(BF16)

    o_ref[...] = _dot(h_ref[...], w_ref[...])


def norm_matmul(x, g, w, *, tm, tn):
    m, d = x.shape
    n = w.shape[1]
    assert m % tm == 0 and n % tn == 0
    return pl.pallas_call(
        _norm_matmul_kernel,
        grid=(m // tm, n // tn),
        in_specs=[pl.BlockSpec((tm, d), lambda i, j: (i, 0)),
                  pl.BlockSpec((1, d), lambda i, j: (0, 0)),
                  pl.BlockSpec((d, tn), lambda i, j: (0, j))],
        out_specs=pl.BlockSpec((tm, tn), lambda i, j: (i, j)),
        out_shape=jax.ShapeDtypeStruct((m, n), F32),
        scratch_shapes=[pltpu.VMEM((tm, d), BF16)],
        compiler_params=_cparams(("parallel", "arbitrary")),
        name="norm_matmul",
    )(x, g.reshape(1, d), w)


def _matmul_norm_res_kernel(*refs, n_in, tm, t_pad, front):
    a_refs, w_refs = refs[:n_in], refs[n_in:2 * n_in]
    g_ref, x_ref, o_ref = refs[2 * n_in:]
    acc = _dot(a_refs[0][...], w_refs[0][...])
    for a, w in zip(a_refs[1:], w_refs[1:]):
        acc = acc + _dot(a[...], w[...])
    out = x_ref[...] + _rms(acc, g_ref[...])
    o_ref[...] = jnp.where(_row_valid(pl.program_id(0), tm, t_pad, front), out, 0.0)


def matmul_norm_res(a_list, w_list, g, x, *, tm, t_pad, front):
    m, d = x.shape
    assert m % tm == 0
    n_in = len(a_list)
    in_specs = ([pl.BlockSpec((tm, a.shape[1]), lambda i: (i, 0)) for a in a_list]
                + [pl.BlockSpec(w.shape, lambda i: (0, 0)) for w in w_list]
                + [pl.BlockSpec((1, d), lambda i: (0, 0)), pl.BlockSpec((tm, d), lambda i: (i, 0))])
    return pl.pallas_call(
        functools.partial(_matmul_norm_res_kernel, n_in=n_in, tm=tm, t_pad=t_pad, front=front),
        grid=(m // tm,),
        in_specs=in_specs,
        out_specs=pl.BlockSpec((tm, d), lambda i: (i, 0)),
        out_shape=jax.ShapeDtypeStruct((m, d), F32),
        compiler_params=_cparams(("parallel",)),
        name="matmul_norm_res",
    )(*a_list, *w_list, g.reshape(1, d), x)


def _ffn_down_kernel(ug_ref, uv_ref, buf_ref, cw_ref, cb_ref, w_ref, g_ref, x_ref, o_ref, *, tm, t_pad, front):
    t = lax.broadcasted_iota(jnp.int32, (tm, 1), 0) & (t_pad - 1)
    ug = jnp.where((t >= front - (CONV_W - 1)) & (t < front), buf_ref[...], ug_ref[...])
    conv = (cb_ref[...] + cw_ref[0:1, :] * pltpu.roll(ug, 2, axis=0) + cw_ref[1:2, :] * pltpu.roll(ug, 1, axis=0)
            + cw_ref[2:3, :] * ug)
    act = (conv * jax.nn.sigmoid(conv) * uv_ref[...]).astype(BF16)
    out = x_ref[...] + _rms(_dot(act, w_ref[...]), g_ref[...])
    o_ref[...] = jnp.where(t >= front, out, 0.0)


def ffn_down(up, buf, cw, cb, w_down, g, x, *, tm, t_pad, front):
    m, d = x.shape
    f = w_down.shape[0]
    assert m % tm == 0 and tm % t_pad == 0 and (t_pad & (t_pad - 1)) == 0 and front >= CONV_W - 1
    assert up.shape == (m, 2 * f) and buf.shape == (m, f)
    return pl.pallas_call(
        functools.partial(_ffn_down_kernel, tm=tm, t_pad=t_pad, front=front),
        grid=(m // tm,),
        in_specs=[pl.BlockSpec((tm, f), lambda i: (i, 0)),
                  pl.BlockSpec((tm, f), lambda i: (i, 1)),
                  pl.BlockSpec((tm, f), lambda i: (i, 0)),
                  pl.BlockSpec((CONV_W, f), lambda i: (0, 0)),
                  pl.BlockSpec((1, f), lambda i: (0, 0)),
                  pl.BlockSpec((f, d), lambda i: (0, 0)),
                  pl.BlockSpec((1, d), lambda i: (0, 0)),
                  pl.BlockSpec((tm, d), lambda i: (i, 0))],
        out_specs=pl.BlockSpec((tm, d), lambda i: (i, 0)),
        out_shape=jax.ShapeDtypeStruct((m, d), F32),
        compiler_params=_cparams(("parallel",)),
        name="ffn_down",
    )(up, up, buf, cw, cb.reshape(1, f), w_down, g.reshape(1, d), x)


FF_CW = 256
FF_HALO = 16


def _ffn_fused_kernel(x_ref, halo_ref, gin_ref, wup_ref, cw_ref, cb_ref, wdn_ref, gout_ref, o_ref, conv_ref, acc_ref,
                      *, tm, t_pad, front):
    i = pl.program_id(0)
    x = x_ref[...]
    h = jnp.concatenate([_rms(halo_ref[...], gin_ref[...]), _rms(x, gin_ref[...])], axis=0).astype(BF16)
    for c in range(D_FF // FF_CW):
        cs = slice(c * FF_CW, (c + 1) * FF_CW)
        ug = _dot(h, wup_ref[:, cs])
        uv = _dot(h[FF_HALO:], wup_ref[:, D_FF + c * FF_CW:D_FF + (c + 1) * FF_CW])
        prev1 = pltpu.roll(ug, 1, axis=0)[FF_HALO:]
        prev2 = pltpu.roll(ug, 2, axis=0)[FF_HALO:]
        conv = cb_ref[:, cs] + cw_ref[0:1, cs] * prev2 + cw_ref[1:2, cs] * prev1 + cw_ref[2:3, cs] * ug[FF_HALO:]
        act = (conv * jax.nn.sigmoid(conv) * uv).astype(BF16)
        part = _dot(act, wdn_ref[cs, :])
        if c == 0:
            acc_ref[...] = part
        else:
            acc_ref[...] += part
        conv_ref[0, :, cs] = ug[FF_HALO + tm - SUB:, :]
    out = x + _rms(acc_ref[...], gout_ref[...])
    o_ref[...] = jnp.where(_row_valid(i, tm, t_pad, front), out, 0.0)


def ffn_fused(x, g_in, w_up, cw, cb, w_down, g_out, *, tm, t_pad, front):
    m, d = x.shape
    assert m % tm == 0 and t_pad % tm == 0 and tm % FF_HALO == 0 and D_FF % FF_CW == 0
    per_seq = t_pad // tm
    hb = tm // FF_HALO
    const = lambda shape: pl.BlockSpec(shape, lambda i: (0,) * len(shape), pipeline_mode=pl.Buffered(1))
    return pl.pallas_call(
        functools.partial(_ffn_fused_kernel, tm=tm, t_pad=t_pad, front=front),
        grid=(m // tm,),
        in_specs=[pl.BlockSpec((tm, d), lambda i: (i, 0)),
                  pl.BlockSpec((FF_HALO, d), lambda i: (jnp.maximum(i * hb - 1, 0), 0)),
                  const((1, d)), const((d, 2 * D_FF)), const((CONV_W, D_FF)), const((1, D_FF)),
                  const((D_FF, d)), const((1, d))],
        out_specs=[pl.BlockSpec((tm, d), lambda i: (i, 0)),
                   pl.BlockSpec((1, SUB, D_FF), lambda i: (i // per_seq, 0, 0))],
        out_shape=[jax.ShapeDtypeStruct((m, d), F32), jax.ShapeDtypeStruct((m // t_pad, SUB, D_FF), F32)],
        scratch_shapes=[pltpu.VMEM((tm, d), F32)],
        compiler_params=_cparams(("arbitrary",)),
        name="ffn_fused",
    )(x, x, g_in.reshape(1, d), w_up, cw, cb.reshape(1, D_FF), w_down, g_out.reshape(1, d))


def _log_sigmoid(x):
    return jnp.minimum(x, 0.0) - jnp.log1p(jnp.exp(-jnp.abs(x)))


def _rec_kernel(proj_ref, lb_ref, bg_ref, ga_ref, gb_ref, s0_ref, c0_ref, n0_ref, m0_ref,
                y_ref, s_ref, c_ref, n_ref, m_ref, st_ref, *, C, W, front):
    ci = pl.program_id(1)
    nci = pl.num_programs(1)

    @pl.when(ci == 0)
    def _():
        for h in range(H_A):
            st_ref[h] = s0_ref[0, h].T
        c_ref[...] = c0_ref[...]
        n_ref[...] = n0_ref[...]
        m_ref[...] = m0_ref[...]

    row = ci * C + lax.broadcasted_iota(jnp.int32, (C, 1), 0)
    valid = row >= front
    r_i = lax.broadcasted_iota(jnp.int32, (C, C), 0)
    c_i = lax.broadcasted_iota(jnp.int32, (C, C), 1)
    causal = r_i >= c_i
    tri = jnp.where(causal, 1.0, 0.0).astype(BF16)

    gates = proj_ref[0, :, 8 * 512:8 * 512 + LANES] + bg_ref[...]
    lf_all = jnp.where(valid, _log_sigmoid(gates), 0.0)
    ig_all = jnp.where(valid, gates, NEG)
    b_all = _dot_exact_lhs(tri, lf_all)
    b_all_t = b_all.T
    ig_all_t = ig_all.T

    qk_dots, qc_dots = [], []
    for h in range(H_B):
        qb = proj_ref[0, :, 2048 + h * 128:2048 + (h + 1) * 128].astype(BF16)
        kb = (proj_ref[0, :, 2560 + h * 128:2560 + (h + 1) * 128] * (DK_B ** -0.5)).astype(BF16)
        qk_dots.append(_dot_nt(qb, kb))
        qc_dots.append(_dot(qb, c_ref[0, h].astype(BF16)))

    for h in range(H_B):
        q = proj_ref[0, :, 2048 + h * 128:2048 + (h + 1) * 128]
        k = proj_ref[0, :, 2560 + h * 128:2560 + (h + 1) * 128] * (DK_B ** -0.5)
        v = proj_ref[0, :, 3072 + h * 128:3072 + (h + 1) * 128]
        og = proj_ref[0, :, 3584 + h * 128:3584 + (h + 1) * 128]
        vb = v.astype(BF16)
        b_col = b_all[:, H_B + h:H_B + h + 1]
        b_row = b_all_t[H_B + h:H_B + h + 1, :]
        i_col = ig_all[:, h:h + 1]
        i_row = ig_all_t[h:h + 1, :]
        m_prev = m_ref[0, h:h + 1, 0:1]
        dmat = jnp.where(causal, b_col - b_row + i_row, NEG)
        inter = b_col + m_prev
        mt = jnp.maximum(inter, jnp.max(dmat, axis=1, keepdims=True))
        w = jnp.exp(dmat - mt) * qk_dots[h]
        wi = jnp.exp(inter - mt)
        c_st = c_ref[0, h]
        n_st = n_ref[0, h:h + 1, :]
        num = wi * qc_dots[h] + _dot(w.astype(BF16), vb)
        den = wi * jnp.sum(q * n_st, axis=1, keepdims=True) + jnp.sum(w, axis=1, keepdims=True)
        hc = num / jnp.maximum(jnp.abs(den), jnp.exp(-mt))
        m_new = mt[C - 1:C, :]
        b_last = b_col[C - 1:C, :]
        decay = jnp.exp(b_last + m_prev - m_new)
        kw = k * jnp.exp(b_last - b_col + i_col - m_new)
        c_ref[0, h] = decay * c_st + _dot_tn(kw.astype(BF16), vb)
        n_ref[0, h:h + 1, :] = decay * n_st + jnp.sum(kw, axis=0, keepdims=True)
        m_ref[0, h:h + 1, :] = jnp.broadcast_to(m_new, (1, LANES))
        yb = _rms(hc, gb_ref[...]) * jax.nn.sigmoid(og)
        y_ref[0, :, 512 + h * 128:512 + (h + 1) * 128] = jnp.where(valid, yb, 0.0).astype(y_ref.dtype)

    nj = C // W
    rw = lax.broadcasted_iota(jnp.int32, (W, 1), 0)
    ones_sq = jnp.ones((DK_A, LANES), BF16)
    lbv = lb_ref[...]
    f = lbv + (1.0 - lbv) * jax.nn.sigmoid(proj_ref[0, :, 512:1024])
    logf = jnp.where(valid, jnp.log(f), 0.0)
    kk = jnp.where(valid, 1.0 - f, 0.0)
    qq = proj_ref[0, :, 0:512] * (DK_A ** -0.5)
    vv = proj_ref[0, :, 1024:1536]
    bfull = _dot_exact_lhs(tri, logf)
    bcs = []
    for j in range(nj):
        rows = slice(j * W, (j + 1) * W)
        bcs.append(bfull[rows] if j == 0 else bfull[rows] - bfull[j * W - 1:j * W])

    heads = [slice(h * 128, (h + 1) * 128) for h in range(H_A)]
    row_sums, outer, decay_last = {}, {}, []
    for j in range(nj):
        rows = slice(j * W, (j + 1) * W)
        bc = bcs[j]
        last = bc[W - 1:W, :]
        decay_last.append(jnp.exp(last))
        kdec = (kk[rows] * jnp.exp(last - bc)).astype(BF16)
        gs = []
        for s in range(W):
            e = jnp.exp(jnp.where(rw >= s, bc - bc[s:s + 1, :], NEG))
            gs.append(qq[rows] * kk[j * W + s:j * W + s + 1, :] * e)
        g = jnp.concatenate(gs, axis=0).astype(BF16)
        for h, cs in enumerate(heads):
            row_sums[h, j] = _dot(g[:, cs], ones_sq)
            outer[h, j] = _dot_tn(vv[rows, cs].astype(BF16), kdec[:, cs])

    from_state = {}
    for h, cs in enumerate(heads):
        st = st_ref[h]
        for j in range(nj):
            rows = slice(j * W, (j + 1) * W)
            from_state[h, j] = _dot_nt((qq[rows, cs] * jnp.exp(bcs[j][:, cs])).astype(BF16), st.astype(BF16))
            st = decay_last[j][:, cs] * st + outer[h, j]
        st_ref[h] = st

    for h, cs in enumerate(heads):
        for j in range(nj):
            rows = slice(j * W, (j + 1) * W)
            o = from_state[h, j]
            for s in range(W):
                o = o + row_sums[h, j][s * W:(s + 1) * W] * vv[j * W + s:j * W + s + 1, cs]
            ga = proj_ref[0, rows, 1536 + h * 128:1536 + (h + 1) * 128]
            ya = _rms(o, ga_ref[...]) * (ga * jax.nn.sigmoid(ga))
            vld = (ci * C + j * W + rw) >= front
            y_ref[0, rows, cs] = jnp.where(vld, ya, 0.0).astype(y_ref.dtype)

    @pl.when(ci == nci - 1)
    def _():
        for h in range(H_A):
            s_ref[0, h] = st_ref[h].T


def rec_mixer(proj, lb, bg, g_a, g_b, s0, c0, n0, m0, *, C, W, front):
    b, t, _ = proj.shape
    assert t % C == 0 and C % W == 0
    m0b = jnp.broadcast_to(m0[:, :, None], (b, H_B, LANES))
    bgp = jnp.zeros((1, LANES), F32).at[0, :2 * H_B].set(bg.reshape(-1))
    st_spec = pl.BlockSpec((1, 4, 128, 128), lambda i, c: (i, 0, 0, 0))
    v_spec = pl.BlockSpec((1, 4, LANES), lambda i, c: (i, 0, 0))
    row_spec = lambda n: pl.BlockSpec((1, n), lambda i, c: (0, 0))
    y, s, cc, n, m = pl.pallas_call(
        functools.partial(_rec_kernel, C=C, W=W, front=front),
        grid=(b, t // C),
        in_specs=[pl.BlockSpec((1, C, REC_N), lambda i, c: (i, c, 0)),
                  row_spec(512), row_spec(LANES), row_spec(128), row_spec(128),
                  st_spec, st_spec, v_spec, v_spec],
        out_specs=[pl.BlockSpec((1, C, 1024), lambda i, c: (i, c, 0)), st_spec, st_spec, v_spec, v_spec],
        out_shape=[jax.ShapeDtypeStruct((b, t, 1024), BF16),
                   jax.ShapeDtypeStruct((b, 4, 128, 128), F32),
                   jax.ShapeDtypeStruct((b, 4, 128, 128), F32),
                   jax.ShapeDtypeStruct((b, 4, LANES), F32),
                   jax.ShapeDtypeStruct((b, 4, LANES), F32)],
        scratch_shapes=[pltpu.VMEM((4, 128, 128), F32)],
        compiler_params=_cparams(("parallel", "arbitrary")),
        name="rec_mixer",
    )(proj, lb.reshape(1, 512), bgp, g_a.reshape(1, 128), g_b.reshape(1, 128), s0, c0, n0, m0b)
    return y, s, cc, n, m[:, :, 0]


def rope_tables(pos, period, half):
    r = 2 * half
    inv = ROPE_THETA ** (-jnp.arange(half, dtype=F32) * 2.0 / r)
    ang = pos.astype(F32)[:, None] * inv[None, :]
    cos, sin = jnp.cos(ang), jnp.sin(ang)
    lane = np.arange(LANES) % period
    idx = np.where(lane < half, lane, np.where(lane < r, lane - half, 0))
    first, second = jnp.asarray(lane < half), jnp.asarray((lane >= half) & (lane < r))
    c = jnp.where(first | second, cos[:, idx], 1.0)
    sa = jnp.where(first, -sin[:, idx], 0.0)
    sb = jnp.where(second, sin[:, idx], 0.0)
    return c, sa, sb


def _att_prep_kernel(x_ref, g_ref, w_ref, c64, a64, b64, c128, a128, b128,
                     qc_o, kc_o, kcb_o, vc_o, vcb_o, qd_o, kd_o, kdb_o, vd_o, vdb_o, qi_o, ki_o, kib_o, wi_o,
                     vct_o, vdt_o, p_ref):
    def rot(x, c, sa, sb, half):
        return x * c[...] + pltpu.roll(x, LANES - half, axis=1) * sa[...] + pltpu.roll(x, half, axis=1) * sb[...]

    p_ref[...] = _dot(_rms(x_ref[...], g_ref[...]).astype(BF16), w_ref[...])
    tm = p_ref.shape[0]
    h64 = D_IDX // ROT_FRAC // 2
    h128 = DH_D // ROT_FRAC // 2
    for t in range(4):
        sl = slice(t * LANES, (t + 1) * LANES)
        qc_o[:, sl] = (rot(p_ref[:, sl], c64, a64, b64, h64) * (DC ** -0.5)).astype(BF16)
        kc = rot(p_ref[:, 512 + t * LANES:512 + (t + 1) * LANES], c64, a64, b64, h64)
        kc_o[pl.ds(t, tm, stride=H_C), :] = kc
        kcb_o[:, sl] = kc.astype(BF16)
        vc = p_ref[:, 1024 + t * LANES:1024 + (t + 1) * LANES]
        vc_o[pl.ds(t, tm, stride=H_C), :] = vc
        vcb_o[:, sl] = vc.astype(BF16)
        vct_o[0, sl, :] = vc.T.astype(BF16)
        qd_o[:, sl] = rot(p_ref[:, 1536 + t * LANES:1536 + (t + 1) * LANES], c128, a128, b128, h128).astype(BF16)
    kd = rot(p_ref[:, 2048:2176], c128, a128, b128, h128)
    kd_o[...] = kd
    kdb_o[...] = kd.astype(BF16)
    vd = p_ref[:, 2176:2304]
    vd_o[...] = vd
    vdb_o[...] = vd.astype(BF16)
    vdt_o[0] = vd.T.astype(BF16)
    for t in range(2):
        qi = rot(p_ref[:, 2304 + t * LANES:2304 + (t + 1) * LANES], c64, a64, b64, h64) * (D_IDX ** -0.5)
        qi_o[:, (2 * t) * LANES:(2 * t + 1) * LANES] = qi.astype(BF16)
        qi_o[:, (2 * t + 1) * LANES:(2 * t + 2) * LANES] = pltpu.roll(qi, D_IDX, axis=1).astype(BF16)
    last = p_ref[:, 2560:2688]
    ki = rot(last, c64, a64, b64, h64)[:, :D_IDX]
    ki_o[...] = ki
    kib_o[...] = ki.astype(BF16)
    wi_o[...] = pltpu.roll(last, D_IDX, axis=1) * (H_I ** -0.5)


def att_prep(x, g, w, tabs64, tabs128, *, tm):
    m, d = x.shape
    p = tabs64[0].shape[0]
    assert m % tm == 0 and p % tm == 0
    nper = p // tm
    tab_spec = pl.BlockSpec((tm, LANES), lambda i: (i % nper, 0))
    outs = [(512, BF16, 1), (2 * DC, F32, H_C), (512, BF16, 1), (2 * DC, F32, H_C), (512, BF16, 1), (512, BF16, 1),
            (128, F32, 1), (128, BF16, 1), (128, F32, 1), (128, BF16, 1), (512, BF16, 1), (D_IDX, F32, 1),
            (D_IDX, BF16, 1), (128, F32, 1)]
    outs_t = [512, 128]
    return pl.pallas_call(
        _att_prep_kernel,
        grid=(m // tm,),
        in_specs=[pl.BlockSpec((tm, d), lambda i: (i, 0)),
                  pl.BlockSpec((1, d), lambda i: (0, 0)),
                  pl.BlockSpec((d, ATT_N), lambda i: (0, 0), pipeline_mode=pl.Buffered(1))] + [tab_spec] * 6,
        out_specs=([pl.BlockSpec((tm * r, w), lambda i: (i, 0)) for w, _, r in outs]
                   + [pl.BlockSpec((1, w, tm), lambda i: (i, 0, 0)) for w in outs_t]),
        out_shape=([jax.ShapeDtypeStruct((m * r, w), dt) for w, dt, r in outs]
                   + [jax.ShapeDtypeStruct((m // tm, w, tm), BF16) for w in outs_t]),
        scratch_shapes=[pltpu.VMEM((tm, ATT_N), F32)],
        compiler_params=_cparams(("parallel",)),
        name="att_prep",
    )(x, g.reshape(1, d), w, *tabs64, *tabs128)


QB = 128


def _diff_lambda(lam_ref, lam_init):
    dl = lam_ref[...]
    s1 = jnp.sum(dl[0:1, :] * dl[1:2, :], axis=1, keepdims=True)
    s2 = jnp.sum(dl[2:3, :] * dl[3:4, :], axis=1, keepdims=True)
    return jnp.exp(s1) - jnp.exp(s2) + lam_init


KB = 384
HEADS_PER_TRIP = 2
SUB = 8


def _group_max(x):
    return jnp.max(x.reshape(x.shape[0] // SUB, SUB, x.shape[1]), axis=0)


def _group_sum(x):
    return jnp.sum(x.reshape(x.shape[0] // SUB, SUB, x.shape[1]), axis=0)


def _key_visible(i, off, n_rep, front):
    krow = lax.broadcasted_iota(jnp.int32, (KB, 1), 0)
    lane = lax.broadcasted_iota(jnp.int32, (1, n_rep * KB), 1)
    q = lane
    for r in range(1, n_rep):
        q = jnp.where(lane >= r * KB, lane - r * KB, q)
    return ((i * KB + q - krow) >= off) & (krow >= front - off)


def _fori_pairs(lo, hi, one, two, init):
    n = jnp.maximum(hi - lo, 0)
    odd = n % 2
    carry = lax.fori_loop(0, odd, lambda t, c: one(lo, c), init)
    return lax.fori_loop(0, n // 2, lambda t, c: two(lo + odd + 2 * t, c), carry)


def _edge_then_middle(i, edge_body, middle_body, init):
    carry = edge_body(i, edge_body(0, init))
    return _fori_pairs(1, i, middle_body, lambda kb, c: middle_body(kb + 1, middle_body(kb, c)), carry)


def _diff_prompt_kernel(q_ref, k_ref, vt_ref, lam_ref, g_ref, o_ref, s_ref, *, front, lam_init):
    i = pl.program_id(1)
    lam = _diff_lambda(lam_ref, lam_init)
    lane = lax.broadcasted_iota(jnp.int32, (KB, LANES), 1)
    for h0 in range(0, H_C, HEADS_PER_TRIP):
        cols = [slice(h * LANES, (h + 1) * LANES) for h in range(h0, h0 + HEADS_PER_TRIP)]
        qstacks = []
        for cs in cols:
            qh = q_ref[:, cs]
            qstacks.append(jnp.concatenate([jnp.where(lane < DC, qh, jnp.zeros_like(qh)),
                                            jnp.where(lane >= DC, qh, jnp.zeros_like(qh))], axis=0))

        def pass_a(kb, mxs, masked):
            off = pl.multiple_of(kb * KB, KB)
            out = []
            for g, (cs, qstack) in enumerate(zip(cols, qstacks)):
                st = _dot_nt(k_ref[pl.ds(off, KB), cs], qstack)
                if masked:
                    st = jnp.where(_key_visible(i, off, 2, front), st, NEG)
                s_ref[g, kb] = st
                out.append(jnp.maximum(mxs[g], _group_max(st)))
            return tuple(out)

        mxs = _edge_then_middle(i, functools.partial(pass_a, masked=True), functools.partial(pass_a, masked=False),
                                (jnp.full((SUB, 2 * KB), NEG, F32),) * HEADS_PER_TRIP)
        ms = [jnp.max(mx, axis=0, keepdims=True) for mx in mxs]

        def pass_b(kb, carry):
            out = []
            for g, cs in enumerate(cols):
                p = jnp.exp(s_ref[g, kb] - ms[g])
                out += [carry[2 * g] + _group_sum(p), carry[2 * g + 1] + _dot(vt_ref[kb, cs, :], p.astype(BF16))]
            return tuple(out)

        def pass_b2(kb, carry):
            out = []
            for g, cs in enumerate(cols):
                p = jnp.exp(jnp.concatenate([s_ref[g, kb], s_ref[g, kb + 1]], axis=0) - ms[g])
                vt2 = jnp.concatenate([vt_ref[kb, cs, :], vt_ref[kb + 1, cs, :]], axis=1)
                out += [carry[2 * g] + _group_sum(p), carry[2 * g + 1] + _dot(vt2, p.astype(BF16))]
            return tuple(out)

        res = _fori_pairs(0, i + 1, pass_b, pass_b2,
                          (jnp.zeros((SUB, 2 * KB), F32), jnp.zeros((LANES, 2 * KB), F32)) * HEADS_PER_TRIP)
        for g, cs in enumerate(cols):
            a = res[2 * g + 1] / jnp.sum(res[2 * g], axis=0, keepdims=True)
            ot = a[:, :KB] - lam * a[:, KB:]
            ot = ot * lax.rsqrt(jnp.mean(ot * ot, axis=0, keepdims=True) + EPS) * g_ref[...] * (1.0 - lam_init)
            o_ref[:, cs] = ot.T.astype(o_ref.dtype)


def diff_prompt(q, k, vt, lam_p, g_c, *, b, t_pad, front, lam_init):
    nkb = t_pad // KB
    return pl.pallas_call(
        functools.partial(_diff_prompt_kernel, front=front, lam_init=lam_init),
        grid=(b, nkb),
        in_specs=[pl.BlockSpec((KB, 512), lambda bb, i: (bb * nkb + i, 0)),
                  pl.BlockSpec((t_pad, 512), lambda bb, i: (bb, 0)),
                  pl.BlockSpec((nkb, 512, KB), lambda bb, i: (bb, 0, 0)),
                  pl.BlockSpec((4, DC), lambda bb, i: (0, 0)),
                  pl.BlockSpec((2 * DC, 1), lambda bb, i: (0, 0))],
        out_specs=pl.BlockSpec((KB, 512), lambda bb, i: (bb * nkb + i, 0)),
        out_shape=jax.ShapeDtypeStruct((b * t_pad, 512), BF16),
        scratch_shapes=[pltpu.VMEM((HEADS_PER_TRIP, nkb, KB, 2 * KB), F32)],
        compiler_params=_cparams(("parallel", "arbitrary")),
        name="diff_prompt",
    )(q, k, vt, lam_p, g_c.reshape(2 * DC, 1))


NINF = float("-inf")


def _kth_threshold(count_ge, shape, n_sel):
    zero_i = jnp.zeros(shape, jnp.int32)
    neg = jnp.where(count_ge(jnp.zeros(shape, F32)) < n_sel, 1, 0)
    sign = jnp.where(neg == 1, jnp.int32(INT_MIN), 0)

    def bit_body(t, mag):
        cand = mag | lax.shift_left(jnp.int32(1), 30 - t)
        enough = jnp.where(count_ge(pltpu.bitcast(cand | sign, F32)) >= n_sel, 1, 0)
        return jnp.where(enough + neg == 1, cand, mag)

    mag = lax.fori_loop(0, 31, bit_body, zero_i)
    tau = pltpu.bitcast(jnp.where(neg == 1, (mag + 1) | sign, mag), F32)
    ninf = jnp.full(shape, NINF, F32)
    return jnp.where(count_ge(ninf) >= n_sel, tau, ninf)


def _raise_to_kth_score(tau, count, min_where, n_sel):
    def cond(c):
        return jnp.max(c[1]) >= n_sel

    def body(c):
        tau, above = c
        tau = jnp.where(above >= n_sel, min_where(lambda sc: sc > tau), tau)
        return tau, count(lambda sc: sc > tau)

    return lax.while_loop(cond, body, (tau, count(lambda sc: sc > tau)))


def _dsa_prompt_kernel(qi_ref, wi_ref, qd_ref, ki_ref, kd_ref, vdt_ref, o_ref, sc_ref, sel_ref, s_ref, *, front, n_sel):
    i = pl.program_id(1)
    nkb = i + 1
    qi_all = jnp.concatenate([qi_ref[:, h * LANES:h * LANES + D_IDX] for h in range(H_I)], axis=0)
    wt = wi_ref[...].T
    w_row = jnp.concatenate([wt[h:h + 1, :] for h in range(H_I)], axis=1)

    def stage1(kb, c, masked):
        off = pl.multiple_of(kb * KB, KB)
        sct = jnp.maximum(_dot_nt(ki_ref[pl.ds(off, KB), :], qi_all), 0.0) * w_row
        score = sct[:, 0:KB] + sct[:, KB:2 * KB] + sct[:, 2 * KB:3 * KB] + sct[:, 3 * KB:4 * KB]
        sc_ref[kb] = jnp.where(_key_visible(i, off, 1, front), score, NINF) if masked else score
        return c

    _edge_then_middle(i, functools.partial(stage1, masked=True), functools.partial(stage1, masked=False), 0)

    def count(pred):
        def body(kb, acc):
            return acc + _group_sum(jnp.where(pred(sc_ref[kb]), 1, 0))
        return jnp.sum(lax.fori_loop(0, nkb, body, jnp.zeros((SUB, KB), jnp.int32)), axis=0, keepdims=True)

    def min_where(pred):
        def body(kb, acc):
            sc = sc_ref[kb]
            return jnp.minimum(acc, -_group_max(jnp.where(pred(sc), -sc, NINF)))
        return jnp.min(lax.fori_loop(0, nkb, body, jnp.full((SUB, KB), -NINF, F32)), axis=0, keepdims=True)

    tau = _kth_threshold(lambda cand: count(lambda sc: sc >= cand), (1, KB), n_sel)
    tau, above = _raise_to_kth_score(tau, count, min_where, n_sel)
    need = (n_sel - above).astype(F32)

    strict_lower = jnp.where(lax.broadcasted_iota(jnp.int32, (KB, KB), 1) < lax.broadcasted_iota(jnp.int32, (KB, KB), 0),
                             1.0, 0.0).astype(BF16)

    def select(kb, before):
        sc = sc_ref[kb]
        eq = sc == tau
        eqf = jnp.where(eq, 1.0, 0.0)
        rank = _dot(strict_lower, eqf.astype(BF16)) + before
        sel_ref[kb] = jnp.where(((sc > tau) | (eq & (rank < need))) & (sc > NINF), 0.0, NEG)
        return before + jnp.sum(eqf, axis=0, keepdims=True)

    lax.fori_loop(0, nkb, select, jnp.zeros((1, KB), F32))

    nq = HEADS_PER_TRIP * KB
    for h0 in range(0, H_D, HEADS_PER_TRIP):
        cols = [slice(h * LANES, (h + 1) * LANES) for h in range(h0, h0 + HEADS_PER_TRIP)]
        qd = jnp.concatenate([qd_ref[:, cs] for cs in cols], axis=0)

        def stage3(kb, mx):
            off = pl.multiple_of(kb * KB, KB)
            sel = jnp.concatenate([sel_ref[kb]] * HEADS_PER_TRIP, axis=1)
            sdt = _dot_nt(kd_ref[pl.ds(off, KB), :], qd) * (DH_D ** -0.5) + sel
            s_ref[kb] = sdt
            return jnp.maximum(mx, _group_max(sdt))

        m = jnp.max(_fori_pairs(0, nkb, stage3, lambda kb, c: stage3(kb + 1, stage3(kb, c)),
                                jnp.full((SUB, nq), NEG, F32)), axis=0, keepdims=True)

        def stage4(kb, carry):
            l8, acc = carry
            p = jnp.exp(s_ref[kb] - m)
            return l8 + _group_sum(p), acc + _dot(vdt_ref[kb], p.astype(BF16))

        def stage4_pair(kb, carry):
            l8, acc = carry
            p = jnp.exp(jnp.concatenate([s_ref[kb], s_ref[kb + 1]], axis=0) - m)
            vt2 = jnp.concatenate([vdt_ref[kb], vdt_ref[kb + 1]], axis=1)
            return l8 + _group_sum(p), acc + _dot(vt2, p.astype(BF16))

        l8, acc = _fori_pairs(0, nkb, stage4, stage4_pair, (jnp.zeros((SUB, nq), F32), jnp.zeros((DH_D, nq), F32)))
        ot = acc / jnp.sum(l8, axis=0, keepdims=True)
        for g, cs in enumerate(cols):
            o_ref[:, cs] = ot[:, g * KB:(g + 1) * KB].T.astype(o_ref.dtype)


def dsa_prompt(qi, wi, qd, ki, kd, vdt, *, b, t_pad, front, n_sel):
    nkb = t_pad // KB
    qspec = lambda w: pl.BlockSpec((KB, w), lambda bb, i: (bb * nkb + i, 0))
    kspec = lambda w: pl.BlockSpec((t_pad, w), lambda bb, i: (bb, 0))
    return pl.pallas_call(
        functools.partial(_dsa_prompt_kernel, front=front, n_sel=n_sel),
        grid=(b, nkb),
        in_specs=[qspec(512), qspec(128), qspec(512), kspec(D_IDX), kspec(DH_D),
                  pl.BlockSpec((nkb, DH_D, KB), lambda bb, i: (bb, 0, 0))],
        out_specs=qspec(512),
        out_shape=jax.ShapeDtypeStruct((b * t_pad, 512), BF16),
        scratch_shapes=[pltpu.VMEM((nkb, KB, KB), F32)] * 2 + [pltpu.VMEM((nkb, KB, HEADS_PER_TRIP * KB), F32)],
        compiler_params=_cparams(("parallel", "arbitrary")),
        name="dsa_prompt",
    )(qi, wi, qd, ki, kd, vdt)


TS = 16


PG = 16
PAGE_BUFFERS = 2


def _page_map(g, n_pages, nd):
    def index(bb, j, pt):
        return (pt[bb * n_pages + jnp.minimum(j * PG + g, n_pages - 1)],) + (0,) * nd
    return index


def _online_softmax_update(s, vs, m_ref, l_ref, acc_ref):
    m_old = m_ref[...]
    m_new = jnp.maximum(m_old, jnp.max(s, axis=1, keepdims=True))
    alpha = jnp.exp(m_old - m_new)
    p = jnp.where(s > 0.5 * NEG, jnp.exp(s - m_new), 0.0)
    l_ref[...] = alpha * l_ref[...] + jnp.sum(p, axis=1, keepdims=True)
    r = s.shape[0] // len(vs)
    pv = jnp.concatenate([_dot(p[g * r:(g + 1) * r].astype(BF16), v) for g, v in enumerate(vs)], axis=0)
    acc_ref[...] = alpha * acc_ref[...] + pv
    m_ref[...] = m_new


def _sample_a_kernel(pt_ref, qc_ref, qi_ref, wi_ref, *refs, n_steps, front, lam_init):
    ck_refs, cv_refs, cik_refs = refs[:PG], refs[PG:2 * PG], refs[2 * PG:3 * PG]
    kn_ref, vn_ref, kin_ref, lam_ref, g_ref, o_ref, keys_ref, m_ref, l_ref, acc_ref = refs[3 * PG:]
    j = pl.program_id(1)
    lane = lax.broadcasted_iota(jnp.int32, (TS, LANES), 1)

    @pl.when(j == 0)
    def _():
        m_ref[...] = jnp.full(m_ref.shape, NEG, F32)
        l_ref[...] = jnp.zeros(l_ref.shape, F32)
        acc_ref[...] = jnp.zeros(acc_ref.shape, F32)

    def qstack(h):
        qh = qc_ref[:, h * LANES:(h + 1) * LANES]
        return jnp.concatenate([jnp.where(lane < DC, qh, jnp.zeros_like(qh)),
                                jnp.where(lane >= DC, qh, jnp.zeros_like(qh))], axis=0)

    def head_rows(h):
        return slice(h * 2 * TS, (h + 1) * 2 * TS)

    qi_all = jnp.concatenate([qi_ref[:, h * LANES:h * LANES + D_IDX] for h in range(H_I)], axis=0)
    w_col = jnp.concatenate([wi_ref[:, h:h + 1] for h in range(H_I)], axis=0)

    def idx_scores(kipt):
        sc = jnp.maximum(_dot(qi_all, kipt), 0.0) * w_col
        return sc[0:TS] + sc[TS:2 * TS] + sc[2 * TS:3 * TS] + sc[3 * TS:4 * TS]

    @pl.when(j < n_steps)
    def _():
        ss, vs = [], []
        for h in range(H_C):
            rows_h = pl.ds(h, PAGE_SIZE, stride=H_C)
            k = jnp.concatenate([r[0, rows_h, :].astype(BF16) for r in ck_refs], axis=0)
            vs.append(jnp.concatenate([r[0, rows_h, :].astype(BF16) for r in cv_refs], axis=0))
            ss.append(_dot_nt(qstack(h), k))
        _online_softmax_update(jnp.concatenate(ss, axis=0), vs, m_ref, l_ref, acc_ref)
        keys_ref[0] = idx_scores(jnp.concatenate([r[0].astype(BF16) for r in cik_refs], axis=1))

    @pl.when(j == n_steps)
    def _():
        kr = lax.broadcasted_iota(jnp.int32, (1, TS), 1)
        q_of_row = lax.broadcasted_iota(jnp.int32, (2 * TS, 1), 0) & (TS - 1)
        ok = (kr >= front) & (kr <= q_of_row)
        ss = [jnp.where(ok, _dot_nt(qstack(h), kn_ref[:, h * LANES:(h + 1) * LANES]), NEG) for h in range(H_C)]
        vs = [vn_ref[:, h * LANES:(h + 1) * LANES] for h in range(H_C)]
        _online_softmax_update(jnp.concatenate(ss, axis=0), vs, m_ref, l_ref, acc_ref)
        krp = lax.broadcasted_iota(jnp.int32, (1, PAGE_SIZE), 1)
        okq = (krp >= front) & (krp < TS) & (krp <= lax.broadcasted_iota(jnp.int32, (TS, 1), 0))
        keys_ref[0] = jnp.concatenate([jnp.where(okq, idx_scores(kin_ref[0]), NINF),
                                       jnp.full((TS, (PG - 1) * PAGE_SIZE), NINF, F32)], axis=1)
        lam = _diff_lambda(lam_ref, lam_init)
        for h in range(H_C):
            a = acc_ref[head_rows(h), :] / l_ref[head_rows(h), :]
            o = a[0:TS] - lam * a[TS:2 * TS]
            o_ref[:, h * LANES:(h + 1) * LANES] = (_rms(o, g_ref[...]) * (1.0 - lam_init)).astype(o_ref.dtype)


def sample_diff_idx(pt, qc, qi, wi, ck, cv, cik, kn, vn, kin, lam_p, g_c, *, b, n_pages, front, lam_init):
    assert n_pages % PG == 0
    n_steps = n_pages // PG
    rows = H_C * 2 * TS
    qspec = lambda w: pl.BlockSpec((TS, w), lambda bb, j, pt: (bb, 0))
    deep = pl.Buffered(PAGE_BUFFERS)
    pages4 = [pl.BlockSpec((1, PAGE_SIZE * H_C, 2 * DC), _page_map(g, n_pages, 2), pipeline_mode=deep) for g in range(PG)]
    pages_i = [pl.BlockSpec((1, D_IDX, PAGE_SIZE), _page_map(g, n_pages, 2), pipeline_mode=deep) for g in range(PG)]
    return pl.pallas_call(
        functools.partial(_sample_a_kernel, n_steps=n_steps, front=front, lam_init=lam_init),
        grid_spec=pltpu.PrefetchScalarGridSpec(
            num_scalar_prefetch=1,
            grid=(b, n_steps + 1),
            in_specs=[qspec(512), qspec(512), qspec(128)] + pages4 + pages4 + pages_i
                     + [qspec(512), qspec(512), pl.BlockSpec((1, D_IDX, PAGE_SIZE), lambda bb, j, pt: (bb, 0, 0)),
                        pl.BlockSpec((4, DC), lambda bb, j, pt: (0, 0)),
                        pl.BlockSpec((1, 2 * DC), lambda bb, j, pt: (0, 0))],
            out_specs=[qspec(512), pl.BlockSpec((1, TS, PG * PAGE_SIZE), lambda bb, j, pt: (bb, 0, j))],
            scratch_shapes=[pltpu.VMEM((rows, 1), F32), pltpu.VMEM((rows, 1), F32), pltpu.VMEM((rows, 2 * DC), F32)]),
        out_shape=[jax.ShapeDtypeStruct((b * TS, 512), BF16),
                   jax.ShapeDtypeStruct((b, TS, (n_steps + 1) * PG * PAGE_SIZE), F32)],
        compiler_params=_cparams(("parallel", "arbitrary")),
        name="sample_diff_idx",
    )(pt, qc, qi, wi, *([ck] * PG), *([cv] * PG), *([cik] * PG), kn, vn, kin, lam_p, g_c.reshape(1, 2 * DC))


def _sample_threshold_kernel(keys_ref, tau_ref, need_ref, *, n_sel):
    rows = keys_ref.shape[0]

    def count(pred):
        return jnp.sum(jnp.where(pred(keys_ref[...]), 1, 0), axis=1, keepdims=True)

    def min_where(pred):
        sc = keys_ref[...]
        return jnp.min(jnp.where(pred(sc), sc, -NINF), axis=1, keepdims=True)

    tau = _kth_threshold(lambda cand: count(lambda sc: sc >= cand), (rows, 1), n_sel)
    tau, above = _raise_to_kth_score(tau, count, min_where, n_sel)
    tau_ref[...] = tau
    need_ref[...] = (n_sel - above).astype(F32)


def sample_threshold(keys, *, n_sel):
    rows, nk = keys.shape
    return pl.pallas_call(
        functools.partial(_sample_threshold_kernel, n_sel=n_sel),
        grid=(1,),
        in_specs=[pl.BlockSpec((rows, nk), lambda i: (0, 0), pipeline_mode=pl.Buffered(1))],
        out_specs=[pl.BlockSpec((rows, 1), lambda i: (0, 0))] * 2,
        out_shape=[jax.ShapeDtypeStruct((rows, 1), F32)] * 2,
        compiler_params=_cparams(("arbitrary",)),
        name="sample_threshold",
    )(keys)


def _sample_b_kernel(pt_ref, tau_ref, need_ref, keys_ref, qd_ref, *refs, n_steps):
    ck_refs, cv_refs = refs[:PG], refs[PG:2 * PG]
    kn_ref, vn_ref, o_ref, before_ref, m_ref, l_ref, acc_ref = refs[2 * PG:]
    j = pl.program_id(1)

    @pl.when(j == 0)
    def _():
        before_ref[...] = jnp.zeros(before_ref.shape, F32)
        m_ref[...] = jnp.full(m_ref.shape, NEG, F32)
        l_ref[...] = jnp.zeros(l_ref.shape, F32)
        acc_ref[...] = jnp.zeros(acc_ref.shape, F32)

    strict_upper = jnp.where(lax.broadcasted_iota(jnp.int32, (PAGE_SIZE, PAGE_SIZE), 0)
                             < lax.broadcasted_iota(jnp.int32, (PAGE_SIZE, PAGE_SIZE), 1), 1.0, 0.0).astype(BF16)
    qd_all = jnp.concatenate([qd_ref[:, h * LANES:(h + 1) * LANES] for h in range(H_D)], axis=0)

    def select(key, before):
        tau = tau_ref[...]
        eq = key == tau
        eqf = jnp.where(eq, 1.0, 0.0)
        rank = _dot(eqf.astype(BF16), strict_upper) + before
        sel = ((key > tau) | (eq & (rank < need_ref[...]))) & (key > NINF)
        return sel, before + jnp.sum(eqf, axis=1, keepdims=True)

    def attend(s, sel, v):
        s = jnp.where(jnp.concatenate([sel] * H_D, axis=0), s * (DH_D ** -0.5), NEG)
        _online_softmax_update(s, [v], m_ref, l_ref, acc_ref)

    @pl.when(j < n_steps)
    def _():
        key = keys_ref[0]
        before = before_ref[...]
        sels = []
        for g in range(PG):
            sel, before = select(key[:, g * PAGE_SIZE:(g + 1) * PAGE_SIZE], before)
            sels.append(sel)
        before_ref[...] = before
        k = jnp.concatenate([r[0].astype(BF16) for r in ck_refs], axis=0)
        v = jnp.concatenate([r[0].astype(BF16) for r in cv_refs], axis=0)
        attend(_dot_nt(qd_all, k), jnp.concatenate(sels, axis=1), v)

    @pl.when(j == n_steps)
    def _():
        sel, _ = select(keys_ref[0][:, :PAGE_SIZE], before_ref[...])
        attend(_dot_nt(qd_all, kn_ref[0]), sel, vn_ref[0])
        o = acc_ref[...] / l_ref[...]
        for h in range(H_D):
            o_ref[:, h * LANES:(h + 1) * LANES] = o[h * TS:(h + 1) * TS].astype(o_ref.dtype)


def sample_dsa(pt, keys, qd, ck, cv, kn, vn, *, b, n_pages, n_sel):
    assert n_pages % PG == 0
    n_steps = n_pages // PG
    rows = H_D * TS
    tau, need = sample_threshold(keys.reshape(b * TS, keys.shape[2]), n_sel=n_sel)
    pages = [pl.BlockSpec((1, PAGE_SIZE, DH_D), _page_map(g, n_pages, 2), pipeline_mode=pl.Buffered(PAGE_BUFFERS))
             for g in range(PG)]
    nspec = pl.BlockSpec((1, PAGE_SIZE, DH_D), lambda bb, j, pt: (bb, 0, 0))
    col = pl.BlockSpec((TS, 1), lambda bb, j, pt: (bb, 0))
    return pl.pallas_call(
        functools.partial(_sample_b_kernel, n_steps=n_steps),
        grid_spec=pltpu.PrefetchScalarGridSpec(
            num_scalar_prefetch=1,
            grid=(b, n_steps + 1),
            in_specs=[col, col,
                      pl.BlockSpec((1, TS, PG * PAGE_SIZE), lambda bb, j, pt: (bb, 0, j)),
                      pl.BlockSpec((TS, 512), lambda bb, j, pt: (bb, 0))] + pages + pages + [nspec, nspec],
            out_specs=pl.BlockSpec((TS, 512), lambda bb, j, pt: (bb, 0)),
            scratch_shapes=[pltpu.VMEM((TS, 1), F32),
                            pltpu.VMEM((rows, 1), F32), pltpu.VMEM((rows, 1), F32), pltpu.VMEM((rows, DH_D), F32)]),
        out_shape=jax.ShapeDtypeStruct((b * TS, 512), BF16),
        compiler_params=_cparams(("parallel", "arbitrary")),
        name="sample_dsa",
    )(pt, tau, need, keys, qd, *([ck] * PG), *([cv] * PG), kn, vn)


REC_CHUNK = 64
REC_SUB = 16
TM_PROJ = 512
TM_REC, TN_REC = 1408, 1408
TM_ROWS = 704
TM_FFN = 528


def _pad_cols(w, n):
    return jnp.pad(w, ((0, 0), (0, n - w.shape[1])))


def _tile_rows(m, pref):
    return pref if m % pref == 0 else m


def kernel(x_prompt, x_sample, state_hgrn, state_mlstm_C, state_mlstm_n, state_mlstm_m, state_ffn_conv, cache_diff_k, cache_diff_v, cache_dsa_k, cache_dsa_v, cache_idx_k, page_table, meta_tokens, norm_gains, w_in_rec, b_gates_rec, lb_logits, g_norm_hgrn, g_norm_mlstm, w_out_rec, w_in_att, diff_lambda, g_norm_diff, w_out_att, w_ffn_up, ffn_conv_w, ffn_conv_b, w_ffn_down):
    bp, t_in, d = x_prompt.shape
    bs, t_s, _ = x_sample.shape
    depth = norm_gains.shape[0]
    n_pages = page_table.shape[1]
    past_len = n_pages * PAGE_SIZE
    real_p = N_META + t_in
    tp = -(-real_p // QB) * QB
    front_p = tp - real_p
    front_s = TS - t_s
    assert tp % REC_CHUNK == 0 and tp % TM_ROWS == 0 and front_p >= CONV_W - 1 and front_s >= CONV_W - 1
    mp, ms = bp * tp, bs * TS

    meta = jnp.broadcast_to(meta_tokens.astype(x_prompt.dtype)[None], (bp, N_META, d))
    xp = jnp.concatenate([jnp.zeros((bp, front_p, d), x_prompt.dtype), meta, x_prompt], axis=1).reshape(mp, d)
    xs = jnp.concatenate([jnp.zeros((bs, front_s, d), x_sample.dtype), x_sample], axis=1).reshape(ms, d)
    lb_all = jnp.cumsum(jax.nn.softmax(lb_logits.astype(F32), axis=0), axis=0)
    pt_flat = page_table.reshape(-1).astype(jnp.int32)
    sel_p = min(TOPK_MAX, t_in // 4)
    sel_s = min(TOPK_MAX, (past_len + t_s) // 4)
    tmp_s = _tile_rows(ms, TM_PROJ)
    tmr_s = _tile_rows(ms, TM_ROWS)

    pos_p = jnp.arange(tp, dtype=jnp.int32) - front_p
    pos_s = jnp.tile(past_len + jnp.arange(TS, dtype=jnp.int32) - front_s, ms // TS)
    tabs_p = (rope_tables(pos_p, DC, DC // ROT_FRAC // 2), rope_tables(pos_p, DH_D, DH_D // ROT_FRAC // 2))
    tabs_s = (rope_tables(pos_s, DC, DC // ROT_FRAC // 2), rope_tables(pos_s, DH_D, DH_D // ROT_FRAC // 2))

    rec_p, rec_s = [[], [], [], []], [[], [], [], []]
    att_p, att_s = [[], [], [], [], []], [[], [], [], [], []]
    conv_p, conv_s = [], []
    for l in range(depth):
        p = l // 2
        g = norm_gains[l].astype(F32)
        if l % 2 == 0:
            w_in = _pad_cols(w_in_rec[p], REC_N).astype(BF16)
            w_out = w_out_rec[p].astype(BF16)
            prm = (lb_all[p], b_gates_rec[p].astype(F32), g_norm_hgrn[p].astype(F32), g_norm_mlstm[p].astype(F32))
            proj = norm_matmul(xp, g[0], w_in, tm=_tile_rows(mp, TM_REC), tn=TN_REC).reshape(bp, tp, REC_N)
            zs = jnp.zeros((bp, 4, 128, 128), F32)
            y, *st = rec_mixer(proj, *prm, zs, zs, jnp.zeros((bp, 4, 128), F32), jnp.zeros((bp, 4), F32),
                               C=REC_CHUNK, W=REC_SUB, front=front_p)
            xp = matmul_norm_res([y.reshape(mp, -1)], [w_out], g[1], xp, tm=TM_ROWS, t_pad=tp, front=front_p)
            for j in range(4):
                rec_p[j].append(st[j])
            proj = norm_matmul(xs, g[0], w_in, tm=tmp_s, tn=384).reshape(bs, TS, REC_N)
            y, *st = rec_mixer(proj, *prm, state_hgrn[p].astype(F32), state_mlstm_C[p].astype(F32),
                               state_mlstm_n[p].astype(F32), state_mlstm_m[p].astype(F32), C=TS, W=TS, front=front_s)
            xs = matmul_norm_res([y.reshape(ms, -1)], [w_out], g[1], xs, tm=tmr_s, t_pad=TS, front=front_s)
            for j in range(4):
                rec_s[j].append(st[j])
        else:
            lam_init = 0.8 - 0.6 * math.exp(-0.3 * l)
            w_in = _pad_cols(w_in_att[p], ATT_N).astype(BF16)
            w_out = w_out_att[p].astype(BF16)
            dl, gc = diff_lambda[p].astype(F32), g_norm_diff[p].astype(F32)
            (qc, kc, kcb, vc, vcb, qd, kd, kdb, vd, vdb, qi, ki, kib, wi, vct, vdt) = att_prep(xp, g[0], w_in, *tabs_p, tm=KB)
            oc = diff_prompt(qc, kcb, vct, dl, gc, b=bp, t_pad=tp, front=front_p, lam_init=lam_init)
            od = dsa_prompt(qi, wi, qd, kib, kdb, vdt, b=bp, t_pad=tp, front=front_p, n_sel=sel_p)
            xp = matmul_norm_res([oc, od], [w_out[:512], w_out[512:]], g[1], xp, tm=TM_ROWS, t_pad=tp, front=front_p)
            for j, (a, shp) in enumerate([(kc, (H_C, 2 * DC)), (vc, (H_C, 2 * DC)), (kd, (DH_D,)), (vd, (DH_D,)), (ki, (D_IDX,))]):
                att_p[j].append(a.reshape((bp, tp) + shp)[:, front_p:])
            (qc, kc, kcb, vc, vcb, qd, kd, kdb, vd, vdb, qi, ki, kib, wi, _, _) = att_prep(xs, g[0], w_in, *tabs_s, tm=ms)
            as_page = lambda a: jnp.pad(a.reshape(bs, TS, -1), ((0, 0), (0, PAGE_SIZE - TS), (0, 0)))
            rows_kh = lambda c: c.reshape(c.shape[0], PAGE_SIZE * H_C, 2 * DC)
            oc, keys = sample_diff_idx(pt_flat, qc, qi, wi, rows_kh(cache_diff_k[p]), rows_kh(cache_diff_v[p]),
                                       jnp.swapaxes(cache_idx_k[p], 1, 2), kcb, vcb, jnp.swapaxes(as_page(kib), 1, 2), dl, gc,
                                       b=bs, n_pages=n_pages, front=front_s, lam_init=lam_init)
            od = sample_dsa(pt_flat, keys, qd, cache_dsa_k[p], cache_dsa_v[p], as_page(kdb), as_page(vdb),
                            b=bs, n_pages=n_pages, n_sel=sel_s)
            xs = matmul_norm_res([oc, od], [w_out[:512], w_out[512:]], g[1], xs, tm=tmr_s, t_pad=TS, front=front_s)
            for j, (a, shp) in enumerate([(kc, (H_C, 2 * DC)), (vc, (H_C, 2 * DC)), (kd, (DH_D,)), (vd, (DH_D,)), (ki, (D_IDX,))]):
                att_s[j].append(a.reshape((bs, TS) + shp)[:, front_s:])
        w_up, w_down = w_ffn_up[l].astype(BF16), w_ffn_down[l].astype(BF16)
        cw, cb = ffn_conv_w[l].astype(F32), ffn_conv_b[l].astype(F32)
        xp, tail = ffn_fused(xp, g[2], w_up, cw, cb, w_down, g[3], tm=TM_FFN, t_pad=tp, front=front_p)
        conv_p.append(tail[:, SUB - (CONV_W - 1):])
        up = norm_matmul(xs, g[2], w_up, tm=tmp_s, tn=512)
        conv_s.append(up.reshape(bs, TS, 2 * D_FF)[:, TS - (CONV_W - 1):, :D_FF])
        buf = jnp.pad(state_ffn_conv[l].astype(F32), ((0, 0), (front_s - (CONV_W - 1), TS - front_s), (0, 0)))
        xs = ffn_down(up, buf.reshape(ms, D_FF), cw, cb, w_down, g[3], xs, tm=tmr_s, t_pad=TS, front=front_s)

    y_p = xp.reshape(bp, tp, d)[:, front_p + N_META:]
    y_s = xs.reshape(bs, TS, d)[:, front_s:]
    stack = lambda xs: xs[0][None] if len(xs) == 1 else jnp.stack(xs)
    return (y_p, y_s,
            stack(rec_p[0]), stack(rec_s[0]), stack(rec_p[1]), stack(rec_s[1]),
            stack(rec_p[2]), stack(rec_s[2]), stack(rec_p[3]), stack(rec_s[3]),
            stack(conv_p), stack(conv_s),
            stack(att_p[0]), stack(att_s[0]), stack(att_p[1]), stack(att_s[1]),
            stack(att_p[2]), stack(att_s[2]), stack(att_p[3]), stack(att_s[3]),
            stack(att_p[4]), stack(att_s[4]))
```

```python
import functools
import math

import jax
import jax.numpy as jnp
import numpy as np
from jax import lax
from jax.experimental import pallas as pl
from jax.experimental.pallas import tpu as pltpu

F32 = jnp.float32
BF16 = jnp.bfloat16

D_MODEL = 1024
N_META = 16
H_A, DK_A, DV_A = 4, 128, 128
H_B, DK_B, DV_B = 4, 128, 128
H_C, DC = 4, 64
H_D, DH_D = 4, 128
H_I, D_IDX = 4, 64
TOPK_MAX = 256
D_FF = 2816
CONV_W = 3
ROPE_THETA = 500000.0
ROT_FRAC = 4
EPS = 1e-6
PAGE_SIZE = 128
LANES = 128
NEG = -1e30

REC_N = 8 * 512 + LANES
ATT_N = 2560 + LANES
INT_MIN = -2 ** 31
VMEM_LIMIT = 56 * 1024 * 1024


def _cparams(sem):
    return pltpu.CompilerParams(dimension_semantics=sem, vmem_limit_bytes=VMEM_LIMIT)


def _rms(x, g):
    return x * lax.rsqrt(jnp.mean(x * x, axis=-1, keepdims=True) + EPS) * g


def _dot(a, b):
    return jnp.dot(a, b, preferred_element_type=F32)


def _dot_nt(a, b):
    return lax.dot_general(a, b, (((1,), (1,)), ((), ())), preferred_element_type=F32)


def _dot_tn(a, b):
    return lax.dot_general(a, b, (((0,), (0,)), ((), ())), preferred_element_type=F32)


def _dot_exact_lhs(tri, x):
    hi = x.astype(BF16)
    r1 = x - hi.astype(F32)
    mid = r1.astype(BF16)
    lo = (r1 - mid.astype(F32)).astype(BF16)
    return _dot(tri, hi) + _dot(tri, mid) + _dot(tri, lo)


def _row_valid(i, tm, t_pad, front):
    r = lax.broadcasted_iota(jnp.int32, (tm, 1), 0)
    if t_pad % tm == 0:
        t = (i % (t_pad // tm)) * tm + r
    else:
        assert tm % t_pad == 0 and (t_pad & (t_pad - 1)) == 0
        t = r & (t_pad - 1)
    return t >= front


def _norm_matmul_kernel(x_ref, g_ref, w_ref, o_ref, h_ref):
    @pl.when(pl.program_id(1) == 0)
    def _():
        h_ref[...] = _rms(x_ref[...], g_ref[...]).astype(BF16)

    o_ref[...] = _dot(h_ref[...], w_ref[...])


def norm_matmul(x, g, w, *, tm, tn):
    m, d = x.shape
    n = w.shape[1]
    assert m % tm == 0 and n % tn == 0
    return pl.pallas_call(
        _norm_matmul_kernel,
        grid=(m // tm, n // tn),
        in_specs=[pl.BlockSpec((tm, d), lambda i, j: (i, 0)),
                  pl.BlockSpec((1, d), lambda i, j: (0, 0)),
                  pl.BlockSpec((d, tn), lambda i, j: (0, j))],
        out_specs=pl.BlockSpec((tm, tn), lambda i, j: (i, j)),
        out_shape=jax.ShapeDtypeStruct((m, n), F32),
        scratch_shapes=[pltpu.VMEM((tm, d), BF16)],
        compiler_params=_cparams(("parallel", "arbitrary")),
        name="norm_matmul",
    )(x, g.reshape(1, d), w)


def _matmul_norm_res_kernel(*refs, n_in, tm, t_pad, front):
    a_refs, w_refs = refs[:n_in], refs[n_in:2 * n_in]
    g_ref, x_ref, o_ref = refs[2 * n_in:]
    acc = _dot(a_refs[0][...], w_refs[0][...])
    for a, w in zip(a_refs[1:], w_refs[1:]):
        acc = acc + _dot(a[...], w[...])
    out = x_ref[...] + _rms(acc, g_ref[...])
    o_ref[...] = jnp.where(_row_valid(pl.program_id(0), tm, t_pad, front), out, 0.0)


def matmul_norm_res(a_list, w_list, g, x, *, tm, t_pad, front):
    m, d = x.shape
    assert m % tm == 0
    n_in = len(a_list)
    in_specs = ([pl.BlockSpec((tm, a.shape[1]), lambda i: (i, 0)) for a in a_list]
                + [pl.BlockSpec(w.shape, lambda i: (0, 0)) for w in w_list]
                + [pl.BlockSpec((1, d), lambda i: (0, 0)), pl.BlockSpec((tm, d), lambda i: (i, 0))])
    return pl.pallas_call(
        functools.partial(_matmul_norm_res_kernel, n_in=n_in, tm=tm, t_pad=t_pad, front=front),
        grid=(m // tm,),
        in_specs=in_specs,
        out_specs=pl.BlockSpec((tm, d), lambda i: (i, 0)),
        out_shape=jax.ShapeDtypeStruct((m, d), F32),
        compiler_params=_cparams(("parallel",)),
        name="matmul_norm_res",
    )(*a_list, *w_list, g.reshape(1, d), x)


def _ffn_down_kernel(ug_ref, uv_ref, buf_ref, cw_ref, cb_ref, w_ref, g_ref, x_ref, o_ref, *, tm, t_pad, front):
    t = lax.broadcasted_iota(jnp.int32, (tm, 1), 0) & (t_pad - 1)
    ug = jnp.where((t >= front - (CONV_W - 1)) & (t < front), buf_ref[...], ug_ref[...])
    conv = (cb_ref[...] + cw_ref[0:1, :] * pltpu.roll(ug, 2, axis=0) + cw_ref[1:2, :] * pltpu.roll(ug, 1, axis=0)
            + cw_ref[2:3, :] * ug)
    act = (conv * jax.nn.sigmoid(conv) * uv_ref[...]).astype(BF16)
    out = x_ref[...] + _rms(_dot(act, w_ref[...]), g_ref[...])
    o_ref[...] = jnp.where(t >= front, out, 0.0)


def ffn_down(up, buf, cw, cb, w_down, g, x, *, tm, t_pad, front):
    m, d = x.shape
    f = w_down.shape[0]
    assert m % tm == 0 and tm % t_pad == 0 and (t_pad & (t_pad - 1)) == 0 and front >= CONV_W - 1
    assert up.shape == (m, 2 * f) and buf.shape == (m, f)
    return pl.pallas_call(
        functools.partial(_ffn_down_kernel, tm=tm, t_pad=t_pad, front=front),
        grid=(m // tm,),
        in_specs=[pl.BlockSpec((tm, f), lambda i: (i, 0)),
                  pl.BlockSpec((tm, f), lambda i: (i, 1)),
                  pl.BlockSpec((tm, f), lambda i: (i, 0)),
                  pl.BlockSpec((CONV_W, f), lambda i: (0, 0)),
                  pl.BlockSpec((1, f), lambda i: (0, 0)),
                  pl.BlockSpec((f, d), lambda i: (0, 0)),
                  pl.BlockSpec((1, d), lambda i: (0, 0)),
                  pl.BlockSpec((tm, d), lambda i: (i, 0))],
        out_specs=pl.BlockSpec((tm, d), lambda i: (i, 0)),
        out_shape=jax.ShapeDtypeStruct((m, d), F32),
        compiler_params=_cparams(("parallel",)),
        name="ffn_down",
    )(up, up, buf, cw, cb.reshape(1, f), w_down, g.reshape(1, d), x)


FF_CW = 256
FF_HALO = 16


def _ffn_fused_kernel(x_ref, halo_ref, gin_ref, wup_ref, cw_ref, cb_ref, wdn_ref, gout_ref, o_ref, conv_ref, acc_ref,
                      *, tm, t_pad, front):
    i = pl.program_id(0)
    x = x_ref[...]
    h = jnp.concatenate([_rms(halo_ref[...], gin_ref[...]), _rms(x, gin_ref[...])], axis=0).astype(BF16)
    for c in range(D_FF // FF_CW):
        cs = slice(c * FF_CW, (c + 1) * FF_CW)
        ug = _dot(h, wup_ref[:, cs])
        uv = _dot(h[FF_HALO:], wup_ref[:, D_FF + c * FF_CW:D_FF + (c + 1) * FF_CW])
        prev1 = pltpu.roll(ug, 1, axis=0)[FF_HALO:]
        prev2 = pltpu.roll(ug, 2, axis=0)[FF_HALO:]
        conv = cb_ref[:, cs] + cw_ref[0:1, cs] * prev2 + cw_ref[1:2, cs] * prev1 + cw_ref[2:3, cs] * ug[FF_HALO:]
        act = (conv * jax.nn.sigmoid(conv) * uv).astype(BF16)
        part = _dot(act, wdn_ref[cs, :])
        if c == 0:
            acc_ref[...] = part
        else:
            acc_ref[...] += part
        conv_ref[0, :, cs] = ug[FF_HALO + tm - SUB:, :]
    out = x + _rms(acc_ref[...], gout_ref[...])
    o_ref[...] = jnp.where(_row_valid(i, tm, t_pad, front), out, 0.0)


def ffn_fused(x, g_in, w_up, cw, cb, w_down, g_out, *, tm, t_pad, front):
    m, d = x.shape
    assert m % tm == 0 and t_pad % tm == 0 and tm % FF_HALO == 0 and D_FF % FF_CW == 0
    per_seq = t_pad // tm
    hb = tm // FF_HALO
    const = lambda shape: pl.BlockSpec(shape, lambda i: (0,) * len(shape), pipeline_mode=pl.Buffered(1))
    return pl.pallas_call(
        functools.partial(_ffn_fused_kernel, tm=tm, t_pad=t_pad, front=front),
        grid=(m // tm,),
        in_specs=[pl.BlockSpec((tm, d), lambda i: (i, 0)),
                  pl.BlockSpec((FF_HALO, d), lambda i: (jnp.maximum(i * hb - 1, 0), 0)),
                  const((1, d)), const((d, 2 * D_FF)), const((CONV_W, D_FF)), const((1, D_FF)),
                  const((D_FF, d)), const((1, d))],
        out_specs=[pl.BlockSpec((tm, d), lambda i: (i, 0)),
                   pl.BlockSpec((1, SUB, D_FF), lambda i: (i // per_seq, 0, 0))],
        out_shape=[jax.ShapeDtypeStruct((m, d), F32), jax.ShapeDtypeStruct((m // t_pad, SUB, D_FF), F32)],
        scratch_shapes=[pltpu.VMEM((tm, d), F32)],
        compiler_params=_cparams(("arbitrary",)),
        name="ffn_fused",
    )(x, x, g_in.reshape(1, d), w_up, cw, cb.reshape(1, D_FF), w_down, g_out.reshape(1, d))


def _log_sigmoid(x):
    return jnp.minimum(x, 0.0) - jnp.log1p(jnp.exp(-jnp.abs(x)))


def _rec_kernel(proj_ref, lb_ref, bg_ref, ga_ref, gb_ref, s0_ref, c0_ref, n0_ref, m0_ref,
                y_ref, s_ref, c_ref, n_ref, m_ref, st_ref, *, C, W, front):
    ci = pl.program_id(1)
    nci = pl.num_programs(1)

    @pl.when(ci == 0)
    def _():
        for h in range(H_A):
            st_ref[h] = s0_ref[0, h].T
        c_ref[...] = c0_ref[...]
        n_ref[...] = n0_ref[...]
        m_ref[...] = m0_ref[...]

    row = ci * C + lax.broadcasted_iota(jnp.int32, (C, 1), 0)
    valid = row >= front
    r_i = lax.broadcasted_iota(jnp.int32, (C, C), 0)
    c_i = lax.broadcasted_iota(jnp.int32, (C, C), 1)
    causal = r_i >= c_i
    tri = jnp.where(causal, 1.0, 0.0).astype(BF16)

    gates = proj_ref[0, :, 8 * 512:8 * 512 + LANES] + bg_ref[...]
    lf_all = jnp.where(valid, _log_sigmoid(gates), 0.0)
    ig_all = jnp.where(valid, gates, NEG)
    b_all = _dot_exact_lhs(tri, lf_all)
    b_all_t = b_all.T
    ig_all_t = ig_all.T

    qk_dots, qc_dots = [], []
    for h in range(H_B):
        qb = proj_ref[0, :, 2048 + h * 128:2048 + (h + 1) * 128].astype(BF16)
        kb = (proj_ref[0, :, 2560 + h * 128:2560 + (h + 1) * 128] * (DK_B ** -0.5)).astype(BF16)
        qk_dots.append(_dot_nt(qb, kb))
        qc_dots.append(_dot(qb, c_ref[0, h].astype(BF16)))

    for h in range(H_B):
        q = proj_ref[0, :, 2048 + h * 128:2048 + (h + 1) * 128]
        k = proj_ref[0, :, 2560 + h * 128:2560 + (h + 1) * 128] * (DK_B ** -0.5)
        v = proj_ref[0, :, 3072 + h * 128:3072 + (h + 1) * 128]
        og = proj_ref[0, :, 3584 + h * 128:3584 + (h + 1) * 128]
        vb = v.astype(BF16)
        b_col = b_all[:, H_B + h:H_B + h + 1]
        b_row = b_all_t[H_B + h:H_B + h + 1, :]
        i_col = ig_all[:, h:h + 1]
        i_row = ig_all_t[h:h + 1, :]
        m_prev = m_ref[0, h:h + 1, 0:1]
        dmat = jnp.where(causal, b_col - b_row + i_row, NEG)
        inter = b_col + m_prev
        mt = jnp.maximum(inter, jnp.max(dmat, axis=1, keepdims=True))
        w = jnp.exp(dmat - mt) * qk_dots[h]
        wi = jnp.exp(inter - mt)
        c_st = c_ref[0, h]
        n_st = n_ref[0, h:h + 1, :]
        num = wi * qc_dots[h] + _dot(w.astype(BF16), vb)
        den = wi * jnp.sum(q * n_st, axis=1, keepdims=True) + jnp.sum(w, axis=1, keepdims=True)
        hc = num / jnp.maximum(jnp.abs(den), jnp.exp(-mt))
        m_new = mt[C - 1:C, :]
        b_last = b_col[C - 1:C, :]
        decay = jnp.exp(b_last + m_prev - m_new)
        kw = k * jnp.exp(b_last - b_col + i_col - m_new)
        c_ref[0, h] = decay * c_st + _dot_tn(kw.astype(BF16), vb)
        n_ref[0, h:h + 1, :] = decay * n_st + jnp.sum(kw, axis=0, keepdims=True)
        m_ref[0, h:h + 1, :] = jnp.broadcast_to(m_new, (1, LANES))
        yb = _rms(hc, gb_ref[...]) * jax.nn.sigmoid(og)
        y_ref[0, :, 512 + h * 128:512 + (h + 1) * 128] = jnp.where(valid, yb, 0.0).astype(y_ref.dtype)

    nj = C // W
    rw = lax.broadcasted_iota(jnp.int32, (W, 1), 0)
    ones_sq = jnp.ones((DK_A, LANES), BF16)
    lbv = lb_ref[...]
    f = lbv + (1.0 - lbv) * jax.nn.sigmoid(proj_ref[0, :, 512:1024])
    logf = jnp.where(valid, jnp.log(f), 0.0)
    kk = jnp.where(valid, 1.0 - f, 0.0)
    qq = proj_ref[0, :, 0:512] * (DK_A ** -0.5)
    vv = proj_ref[0, :, 1024:1536]
    bfull = _dot_exact_lhs(tri, logf)
    bcs = []
    for j in range(nj):
        rows = slice(j * W, (j + 1) * W)
        bcs.append(bfull[rows] if j == 0 else bfull[rows] - bfull[j * W - 1:j * W])

    heads = [slice(h * 128, (h + 1) * 128) for h in range(H_A)]
    row_sums, outer, decay_last = {}, {}, []
    for j in range(nj):
        rows = slice(j * W, (j + 1) * W)
        bc = bcs[j]
        last = bc[W - 1:W, :]
        decay_last.append(jnp.exp(last))
        kdec = (kk[rows] * jnp.exp(last - bc)).astype(BF16)
        gs = []
        for s in range(W):
            e = jnp.exp(jnp.where(rw >= s, bc - bc[s:s + 1, :], NEG))
            gs.append(qq[rows] * kk[j * W + s:j * W + s + 1, :] * e)
        g = jnp.concatenate(gs, axis=0).astype(BF16)
        for h, cs in enumerate(heads):
            row_sums[h, j] = _dot(g[:, cs], ones_sq)
            outer[h, j] = _dot_tn(vv[rows, cs].astype(BF16), kdec[:, cs])

    from_state = {}
    for h, cs in enumerate(heads):
        st = st_ref[h]
        for j in range(nj):
            rows = slice(j * W, (j + 1) * W)
            from_state[h, j] = _dot_nt((qq[rows, cs] * jnp.exp(bcs[j][:, cs])).astype(BF16), st.astype(BF16))
            st = decay_last[j][:, cs] * st + outer[h, j]
        st_ref[h] = st

    for h, cs in enumerate(heads):
        for j in range(nj):
            rows = slice(j * W, (j + 1) * W)
            o = from_state[h, j]
            for s in range(W):
                o = o + row_sums[h, j][s * W:(s + 1) * W] * vv[j * W + s:j * W + s + 1, cs]
            ga = proj_ref[0, rows, 1536 + h * 128:1536 + (h + 1) * 128]
            ya = _rms(o, ga_ref[...]) * (ga * jax.nn.sigmoid(ga))
            vld = (ci * C + j * W + rw) >= front
            y_ref[0, rows, cs] = jnp.where(vld, ya, 0.0).astype(y_ref.dtype)

    @pl.when(ci == nci - 1)
    def _():
        for h in range(H_A):
            s_ref[0, h] = st_ref[h].T


def rec_mixer(proj, lb, bg, g_a, g_b, s0, c0, n0, m0, *, C, W, front):
    b, t, _ = proj.shape
    assert t % C == 0 and C % W == 0
    m0b = jnp.broadcast_to(m0[:, :, None], (b, H_B, LANES))
    bgp = jnp.zeros((1, LANES), F32).at[0, :2 * H_B].set(bg.reshape(-1))
    st_spec = pl.BlockSpec((1, 4, 128, 128), lambda i, c: (i, 0, 0, 0))
    v_spec = pl.BlockSpec((1, 4, LANES), lambda i, c: (i, 0, 0))
    row_spec = lambda n: pl.BlockSpec((1, n), lambda i, c: (0, 0))
    y, s, cc, n, m = pl.pallas_call(
        functools.partial(_rec_kernel, C=C, W=W, front=front),
        grid=(b, t // C),
        in_specs=[pl.BlockSpec((1, C, REC_N), lambda i, c: (i, c, 0)),
                  row_spec(512), row_spec(LANES), row_spec(128), row_spec(128),
                  st_spec, st_spec, v_spec, v_spec],
        out_specs=[pl.BlockSpec((1, C, 1024), lambda i, c: (i, c, 0)), st_spec, st_spec, v_spec, v_spec],
        out_shape=[jax.ShapeDtypeStruct((b, t, 1024), BF16),
                   jax.ShapeDtypeStruct((b, 4, 128, 128), F32),
                   jax.ShapeDtypeStruct((b, 4, 128, 128), F32),
                   jax.ShapeDtypeStruct((b, 4, LANES), F32),
                   jax.ShapeDtypeStruct((b, 4, LANES), F32)],
        scratch_shapes=[pltpu.VMEM((4, 128, 128), F32)],
        compiler_params=_cparams(("parallel", "arbitrary")),
        name="rec_mixer",
    )(proj, lb.reshape(1, 512), bgp, g_a.reshape(1, 128), g_b.reshape(1, 128), s0, c0, n0, m0b)
    return y, s, cc, n, m[:, :, 0]


def rope_tables(pos, period, half):
    r = 2 * half
    inv = ROPE_THETA ** (-jnp.arange(half, dtype=F32) * 2.0 / r)
    ang = pos.astype(F32)[:, None] * inv[None, :]
    cos, sin = jnp.cos(ang), jnp.sin(ang)
    lane = np.arange(LANES) % period
    idx = np.where(lane < half, lane, np.where(lane < r, lane - half, 0))
    first, second = jnp.asarray(lane < half), jnp.asarray((lane >= half) & (lane < r))
    c = jnp.where(first | second, cos[:, idx], 1.0)
    sa = jnp.where(first, -sin[:, idx], 0.0)
    sb = jnp.where(second, sin[:, idx], 0.0)
    return c, sa, sb


def _att_prep_kernel(x_ref, g_ref, w_ref, c64, a64, b64, c128, a128, b128,
                     qc_o, kc_o, kcb_o, vc_o, vcb_o, qd_o, kd_o, kdb_o, vd_o, vdb_o, qi_o, ki_o, kib_o, wi_o,
                     vct_o, vdt_o, p_ref):
    def rot(x, c, sa, sb, half):
        return x * c[...] + pltpu.roll(x, LANES - half, axis=1) * sa[...] + pltpu.roll(x, half, axis=1) * sb[...]

    p_ref[...] = _dot(_rms(x_ref[...], g_ref[...]).astype(BF16), w_ref[...])
    tm = p_ref.shape[0]
    h64 = D_IDX // ROT_FRAC // 2
    h128 = DH_D // ROT_FRAC // 2
    for t in range(4):
        sl = slice(t * LANES, (t + 1) * LANES)
        qc_o[:, sl] = (rot(p_ref[:, sl], c64, a64, b64, h64) * (DC ** -0.5)).astype(BF16)
        kc = rot(p_ref[:, 512 + t * LANES:512 + (t + 1) * LANES], c64, a64, b64, h64)
        kc_o[pl.ds(t, tm, stride=H_C), :] = kc
        kcb_o[:, sl] = kc.astype(BF16)
        vc = p_ref[:, 1024 + t * LANES:1024 + (t + 1) * LANES]
        vc_o[pl.ds(t, tm, stride=H_C), :] = vc
        vcb_o[:, sl] = vc.astype(BF16)
        vct_o[0, sl, :] = vc.T.astype(BF16)
        qd_o[:, sl] = rot(p_ref[:, 1536 + t * LANES:1536 + (t + 1) * LANES], c128, a128, b128, h128).astype(BF16)
    kd = rot(p_ref[:, 2048:2176], c128, a128, b128, h128)
    kd_o[...] = kd
    kdb_o[...] = kd.astype(BF16)
    vd = p_ref[:, 2176:2304]
    vd_o[...] = vd
    vdb_o[...] = vd.astype(BF16)
    vdt_o[0] = vd.T.astype(BF16)
    for t in range(2):
        qi = rot(p_ref[:, 2304 + t * LANES:2304 + (t + 1) * LANES], c64, a64, b64, h64) * (D_IDX ** -0.5)
        qi_o[:, (2 * t) * LANES:(2 * t + 1) * LANES] = qi.astype(BF16)
        qi_o[:, (2 * t + 1) * LANES:(2 * t + 2) * LANES] = pltpu.roll(qi, D_IDX, axis=1).astype(BF16)
    last = p_ref[:, 2560:2688]
    ki = rot(last, c64, a64, b64, h64)[:, :D_IDX]
    ki_o[...] = ki
    kib_o[...] = ki.astype(BF16)
    wi_o[...] = pltpu.roll(last, D_IDX, axis=1) * (H_I ** -0.5)


def att_prep(x, g, w, tabs64, tabs128, *, tm):
    m, d = x.shape
    p = tabs64[0].shape[0]
    assert m % tm == 0 and p % tm == 0
    nper = p // tm
    tab_spec = pl.BlockSpec((tm, LANES), lambda i: (i % nper, 0))
    outs = [(512, BF16, 1), (2 * DC, F32, H_C), (512, BF16, 1), (2 * DC, F32, H_C), (512, BF16, 1), (512, BF16, 1),
            (128, F32, 1), (128, BF16, 1), (128, F32, 1), (128, BF16, 1), (512, BF16, 1), (D_IDX, F32, 1),
            (D_IDX, BF16, 1), (128, F32, 1)]
    outs_t = [512, 128]
    return pl.pallas_call(
        _att_prep_kernel,
        grid=(m // tm,),
        in_specs=[pl.BlockSpec((tm, d), lambda i: (i, 0)),
                  pl.BlockSpec((1, d), lambda i: (0, 0)),
                  pl.BlockSpec((d, ATT_N), lambda i: (0, 0), pipeline_mode=pl.Buffered(1))] + [tab_spec] * 6,
        out_specs=([pl.BlockSpec((tm * r, w), lambda i: (i, 0)) for w, _, r in outs]
                   + [pl.BlockSpec((1, w, tm), lambda i: (i, 0, 0)) for w in outs_t]),
        out_shape=([jax.ShapeDtypeStruct((m * r, w), dt) for w, dt, r in outs]
                   + [jax.ShapeDtypeStruct((m // tm, w, tm), BF16) for w in outs_t]),
        scratch_shapes=[pltpu.VMEM((tm, ATT_N), F32)],
        compiler_params=_cparams(("parallel",)),
        name="att_prep",
    )(x, g.reshape(1, d), w, *tabs64, *tabs128)


QB = 128


def _diff_lambda(lam_ref, lam_init):
    dl = lam_ref[...]
    s1 = jnp.sum(dl[0:1, :] * dl[1:2, :], axis=1, keepdims=True)
    s2 = jnp.sum(dl[2:3, :] * dl[3:4, :], axis=1, keepdims=True)
    return jnp.exp(s1) - jnp.exp(s2) + lam_init


KB = 384
HEADS_PER_TRIP = 2
DSA_HEADS_PER_TRIP = 4
SUB = 8


def _group_max(x):
    return jnp.max(x.reshape(x.shape[0] // SUB, SUB, x.shape[1]), axis=0)


def _group_sum(x):
    return jnp.sum(x.reshape(x.shape[0] // SUB, SUB, x.shape[1]), axis=0)


def _key_visible(i, off, n_rep, front):
    krow = lax.broadcasted_iota(jnp.int32, (KB, 1), 0)
    lane = lax.broadcasted_iota(jnp.int32, (1, n_rep * KB), 1)
    q = lane
    for r in range(1, n_rep):
        q = jnp.where(lane >= r * KB, lane - r * KB, q)
    return ((i * KB + q - krow) >= off) & (krow >= front - off)


def _fori_pairs(lo, hi, one, two, init):
    n = jnp.maximum(hi - lo, 0)
    odd = n % 2
    carry = lax.fori_loop(0, odd, lambda t, c: one(lo, c), init)
    return lax.fori_loop(0, n // 2, lambda t, c: two(lo + odd + 2 * t, c), carry)


def _edge_then_middle(i, edge_body, middle_body, init):
    carry = edge_body(i, edge_body(0, init))
    return _fori_pairs(1, i, middle_body, lambda kb, c: middle_body(kb + 1, middle_body(kb, c)), carry)


def _diff_prompt_kernel(q_ref, k_ref, vt_ref, lam_ref, g_ref, o_ref, s_ref, *, front, lam_init):
    i = pl.program_id(1)
    lam = _diff_lambda(lam_ref, lam_init)
    lane = lax.broadcasted_iota(jnp.int32, (KB, LANES), 1)
    for h0 in range(0, H_C, HEADS_PER_TRIP):
        cols = [slice(h * LANES, (h + 1) * LANES) for h in range(h0, h0 + HEADS_PER_TRIP)]
        qstacks = []
        for cs in cols:
            qh = q_ref[:, cs]
            qstacks.append(jnp.concatenate([jnp.where(lane < DC, qh, jnp.zeros_like(qh)),
                                            jnp.where(lane >= DC, qh, jnp.zeros_like(qh))], axis=0))

        def pass_a(kb, mxs, masked):
            off = pl.multiple_of(kb * KB, KB)
            out = []
            for g, (cs, qstack) in enumerate(zip(cols, qstacks)):
                st = _dot_nt(k_ref[pl.ds(off, KB), cs], qstack)
                if masked:
                    st = jnp.where(_key_visible(i, off, 2, front), st, NEG)
                s_ref[g, kb] = st
                out.append(jnp.maximum(mxs[g], _group_max(st)))
            return tuple(out)

        mxs = _edge_then_middle(i, functools.partial(pass_a, masked=True), functools.partial(pass_a, masked=False),
                                (jnp.full((SUB, 2 * KB), NEG, F32),) * HEADS_PER_TRIP)
        ms = [jnp.max(mx, axis=0, keepdims=True) for mx in mxs]

        def pass_b(kb, carry):
            out = []
            for g, cs in enumerate(cols):
                p = jnp.exp(s_ref[g, kb] - ms[g])
                out += [carry[2 * g] + _group_sum(p), carry[2 * g + 1] + _dot(vt_ref[kb, cs, :], p.astype(BF16))]
            return tuple(out)

        def pass_b2(kb, carry):
            out = []
            for g, cs in enumerate(cols):
                p = jnp.exp(jnp.concatenate([s_ref[g, kb], s_ref[g, kb + 1]], axis=0) - ms[g])
                vt2 = jnp.concatenate([vt_ref[kb, cs, :], vt_ref[kb + 1, cs, :]], axis=1)
                out += [carry[2 * g] + _group_sum(p), carry[2 * g + 1] + _dot(vt2, p.astype(BF16))]
            return tuple(out)

        res = _fori_pairs(0, i + 1, pass_b, pass_b2,
                          (jnp.zeros((SUB, 2 * KB), F32), jnp.zeros((LANES, 2 * KB), F32)) * HEADS_PER_TRIP)
        for g, cs in enumerate(cols):
            a = res[2 * g + 1] / jnp.sum(res[2 * g], axis=0, keepdims=True)
            ot = a[:, :KB] - lam * a[:, KB:]
            ot = ot * lax.rsqrt(jnp.mean(ot * ot, axis=0, keepdims=True) + EPS) * g_ref[...] * (1.0 - lam_init)
            o_ref[:, cs] = ot.T.astype(o_ref.dtype)


def diff_prompt(q, k, vt, lam_p, g_c, *, b, t_pad, front, lam_init):
    nkb = t_pad // KB
    return pl.pallas_call(
        functools.partial(_diff_prompt_kernel, front=front, lam_init=lam_init),
        grid=(b, nkb),
        in_specs=[pl.BlockSpec((KB, 512), lambda bb, i: (bb * nkb + i, 0)),
                  pl.BlockSpec((t_pad, 512), lambda bb, i: (bb, 0)),
                  pl.BlockSpec((nkb, 512, KB), lambda bb, i: (bb, 0, 0)),
                  pl.BlockSpec((4, DC), lambda bb, i: (0, 0)),
                  pl.BlockSpec((2 * DC, 1), lambda bb, i: (0, 0))],
        out_specs=pl.BlockSpec((KB, 512), lambda bb, i: (bb * nkb + i, 0)),
        out_shape=jax.ShapeDtypeStruct((b * t_pad, 512), BF16),
        scratch_shapes=[pltpu.VMEM((HEADS_PER_TRIP, nkb, KB, 2 * KB), F32)],
        compiler_params=_cparams(("parallel", "arbitrary")),
        name="diff_prompt",
    )(q, k, vt, lam_p, g_c.reshape(2 * DC, 1))


NINF = float("-inf")


def _kth_threshold(count_ge, shape, n_sel):
    zero_i = jnp.zeros(shape, jnp.int32)
    neg = jnp.where(count_ge(jnp.zeros(shape, F32)) < n_sel, 1, 0)
    sign = jnp.where(neg == 1, jnp.int32(INT_MIN), 0)

    def bit_body(t, mag):
        cand = mag | lax.shift_left(jnp.int32(1), 30 - t)
        enough = jnp.where(count_ge(pltpu.bitcast(cand | sign, F32)) >= n_sel, 1, 0)
        return jnp.where(enough + neg == 1, cand, mag)

    mag = lax.fori_loop(0, 31, bit_body, zero_i)
    tau = pltpu.bitcast(jnp.where(neg == 1, (mag + 1) | sign, mag), F32)
    ninf = jnp.full(shape, NINF, F32)
    return jnp.where(count_ge(ninf) >= n_sel, tau, ninf)


def _raise_to_kth_score(tau, count, min_where, n_sel):
    def cond(c):
        return jnp.max(c[1]) >= n_sel

    def body(c):
        tau, above = c
        tau = jnp.where(above >= n_sel, min_where(lambda sc: sc > tau), tau)
        return tau, count(lambda sc: sc > tau)

    return lax.while_loop(cond, body, (tau, count(lambda sc: sc > tau)))


def _dsa_prompt_kernel(qi_ref, wi_ref, qd_ref, ki_ref, kd_ref, vdt_ref, o_ref, sc_ref, sel_ref, s_ref, *, front, n_sel):
    i = pl.program_id(1)
    nkb = i + 1
    qi_all = jnp.concatenate([qi_ref[:, h * LANES:h * LANES + D_IDX] for h in range(H_I)], axis=0)
    wt = wi_ref[...].T
    w_row = jnp.concatenate([wt[h:h + 1, :] for h in range(H_I)], axis=1)

    def stage1(kb, c, masked):
        off = pl.multiple_of(kb * KB, KB)
        sct = jnp.maximum(_dot_nt(ki_ref[pl.ds(off, KB), :], qi_all), 0.0) * w_row
        score = sct[:, 0:KB] + sct[:, KB:2 * KB] + sct[:, 2 * KB:3 * KB] + sct[:, 3 * KB:4 * KB]
        sc_ref[kb] = jnp.where(_key_visible(i, off, 1, front), score, NINF) if masked else score
        return c

    _edge_then_middle(i, functools.partial(stage1, masked=True), functools.partial(stage1, masked=False), 0)

    def count(pred):
        def body(kb, acc):
            return acc + _group_sum(jnp.where(pred(sc_ref[kb]), 1, 0))
        return jnp.sum(lax.fori_loop(0, nkb, body, jnp.zeros((SUB, KB), jnp.int32)), axis=0, keepdims=True)

    def min_where(pred):
        def body(kb, acc):
            sc = sc_ref[kb]
            return jnp.minimum(acc, -_group_max(jnp.where(pred(sc), -sc, NINF)))
        return jnp.min(lax.fori_loop(0, nkb, body, jnp.full((SUB, KB), -NINF, F32)), axis=0, keepdims=True)

    tau = _kth_threshold(lambda cand: count(lambda sc: sc >= cand), (1, KB), n_sel)
    tau, above = _raise_to_kth_score(tau, count, min_where, n_sel)
    need = (n_sel - above).astype(F32)

    strict_lower = jnp.where(lax.broadcasted_iota(jnp.int32, (KB, KB), 1) < lax.broadcasted_iota(jnp.int32, (KB, KB), 0),
                             1.0, 0.0).astype(BF16)

    def select(kb, before):
        sc = sc_ref[kb]
        eq = sc == tau
        eqf = jnp.where(eq, 1.0, 0.0)
        rank = _dot(strict_lower, eqf.astype(BF16)) + before
        sel_ref[kb] = jnp.where(((sc > tau) | (eq & (rank < need))) & (sc > NINF), 0.0, NEG)
        return before + jnp.sum(eqf, axis=0, keepdims=True)

    _fori_pairs(0, nkb, select, lambda kb, c: select(kb + 1, select(kb, c)), jnp.zeros((1, KB), F32))

    nq = DSA_HEADS_PER_TRIP * KB
    for h0 in range(0, H_D, DSA_HEADS_PER_TRIP):
        cols = [slice(h * LANES, (h + 1) * LANES) for h in range(h0, h0 + DSA_HEADS_PER_TRIP)]
        qd = jnp.concatenate([qd_ref[:, cs] for cs in cols], axis=0)

        def stage3(kb, mx):
            off = pl.multiple_of(kb * KB, KB)
            sel = jnp.concatenate([sel_ref[kb]] * DSA_HEADS_PER_TRIP, axis=1)
            sdt = _dot_nt(kd_ref[pl.ds(off, KB), :], qd) * (DH_D ** -0.5) + sel
            s_ref[kb] = sdt
            return jnp.maximum(mx, _group_max(sdt))

        m = jnp.max(_fori_pairs(0, nkb, stage3, lambda kb, c: stage3(kb + 1, stage3(kb, c)),
                                jnp.full((SUB, nq), NEG, F32)), axis=0, keepdims=True)

        def stage4(kb, carry):
            l8, acc = carry
            p = jnp.exp(s_ref[kb] - m)
            return l8 + _group_sum(p), acc + _dot(vdt_ref[kb], p.astype(BF16))

        def stage4_pair(kb, carry):
            l8, acc = carry
            p = jnp.exp(jnp.concatenate([s_ref[kb], s_ref[kb + 1]], axis=0) - m)
            vt2 = jnp.concatenate([vdt_ref[kb], vdt_ref[kb + 1]], axis=1)
            return l8 + _group_sum(p), acc + _dot(vt2, p.astype(BF16))

        l8, acc = _fori_pairs(0, nkb, stage4, stage4_pair, (jnp.zeros((SUB, nq), F32), jnp.zeros((DH_D, nq), F32)))
        ot = acc / jnp.sum(l8, axis=0, keepdims=True)
        for g, cs in enumerate(cols):
            o_ref[:, cs] = ot[:, g * KB:(g + 1) * KB].T.astype(o_ref.dtype)


def dsa_prompt(qi, wi, qd, ki, kd, vdt, *, b, t_pad, front, n_sel):
    nkb = t_pad // KB
    qspec = lambda w: pl.BlockSpec((KB, w), lambda bb, i: (bb * nkb + i, 0))
    kspec = lambda w: pl.BlockSpec((t_pad, w), lambda bb, i: (bb, 0))
    return pl.pallas_call(
        functools.partial(_dsa_prompt_kernel, front=front, n_sel=n_sel),
        grid=(b, nkb),
        in_specs=[qspec(512), qspec(128), qspec(512), kspec(D_IDX), kspec(DH_D),
                  pl.BlockSpec((nkb, DH_D, KB), lambda bb, i: (bb, 0, 0))],
        out_specs=qspec(512),
        out_shape=jax.ShapeDtypeStruct((b * t_pad, 512), BF16),
        scratch_shapes=[pltpu.VMEM((nkb, KB, KB), F32)] * 2 + [pltpu.VMEM((nkb, KB, DSA_HEADS_PER_TRIP * KB), F32)],
        compiler_params=_cparams(("parallel", "arbitrary")),
        name="dsa_prompt",
    )(qi, wi, qd, ki, kd, vdt)


TS = 16


PG = 16
PAGE_BUFFERS = 2


def _page_map(g, n_pages, nd):
    def index(bb, j, pt):
        return (pt[bb * n_pages + jnp.minimum(j * PG + g, n_pages - 1)],) + (0,) * nd
    return index


def _online_softmax_update(s, vs, m_ref, l_ref, acc_ref):
    m_old = m_ref[...]
    m_new = jnp.maximum(m_old, jnp.max(s, axis=1, keepdims=True))
    alpha = jnp.exp(m_old - m_new)
    p = jnp.where(s > 0.5 * NEG, jnp.exp(s - m_new), 0.0)
    l_ref[...] = alpha * l_ref[...] + jnp.sum(p, axis=1, keepdims=True)
    r = s.shape[0] // len(vs)
    pv = jnp.concatenate([_dot(p[g * r:(g + 1) * r].astype(BF16), v) for g, v in enumerate(vs)], axis=0)
    acc_ref[...] = alpha * acc_ref[...] + pv
    m_ref[...] = m_new


def _sample_a_kernel(pt_ref, qc_ref, qi_ref, wi_ref, *refs, n_steps, front, lam_init):
    ck_refs, cv_refs, cik_refs = refs[:PG], refs[PG:2 * PG], refs[2 * PG:3 * PG]
    kn_ref, vn_ref, kin_ref, lam_ref, g_ref, o_ref, keys_ref, m_ref, l_ref, acc_ref = refs[3 * PG:]
    j = pl.program_id(1)
    lane = lax.broadcasted_iota(jnp.int32, (TS, LANES), 1)

    @pl.when(j == 0)
    def _():
        m_ref[...] = jnp.full(m_ref.shape, NEG, F32)
        l_ref[...] = jnp.zeros(l_ref.shape, F32)
        acc_ref[...] = jnp.zeros(acc_ref.shape, F32)

    def qstack(h):
        qh = qc_ref[:, h * LANES:(h + 1) * LANES]
        return jnp.concatenate([jnp.where(lane < DC, qh, jnp.zeros_like(qh)),
                                jnp.where(lane >= DC, qh, jnp.zeros_like(qh))], axis=0)

    def head_rows(h):
        return slice(h * 2 * TS, (h + 1) * 2 * TS)

    qi_all = jnp.concatenate([qi_ref[:, h * LANES:h * LANES + D_IDX] for h in range(H_I)], axis=0)
    w_col = jnp.concatenate([wi_ref[:, h:h + 1] for h in range(H_I)], axis=0)

    def idx_scores(kipt):
        sc = jnp.maximum(_dot(qi_all, kipt), 0.0) * w_col
        return sc[0:TS] + sc[TS:2 * TS] + sc[2 * TS:3 * TS] + sc[3 * TS:4 * TS]

    @pl.when(j < n_steps)
    def _():
        ss, vs = [], []
        for h in range(H_C):
            rows_h = pl.ds(h, PAGE_SIZE, stride=H_C)
            k = jnp.concatenate([r[0, rows_h, :].astype(BF16) for r in ck_refs], axis=0)
            vs.append(jnp.concatenate([r[0, rows_h, :].astype(BF16) for r in cv_refs], axis=0))
            ss.append(_dot_nt(qstack(h), k))
        _online_softmax_update(jnp.concatenate(ss, axis=0), vs, m_ref, l_ref, acc_ref)
        keys_ref[0] = idx_scores(jnp.concatenate([r[0].astype(BF16) for r in cik_refs], axis=1))

    @pl.when(j == n_steps)
    def _():
        kr = lax.broadcasted_iota(jnp.int32, (1, TS), 1)
        q_of_row = lax.broadcasted_iota(jnp.int32, (2 * TS, 1), 0) & (TS - 1)
        ok = (kr >= front) & (kr <= q_of_row)
        ss = [jnp.where(ok, _dot_nt(qstack(h), kn_ref[:, h * LANES:(h + 1) * LANES]), NEG) for h in range(H_C)]
        vs = [vn_ref[:, h * LANES:(h + 1) * LANES] for h in range(H_C)]
        _online_softmax_update(jnp.concatenate(ss, axis=0), vs, m_ref, l_ref, acc_ref)
        krp = lax.broadcasted_iota(jnp.int32, (1, PAGE_SIZE), 1)
        okq = (krp >= front) & (krp < TS) & (krp <= lax.broadcasted_iota(jnp.int32, (TS, 1), 0))
        keys_ref[0] = jnp.concatenate([jnp.where(okq, idx_scores(kin_ref[0]), NINF),
                                       jnp.full((TS, (PG - 1) * PAGE_SIZE), NINF, F32)], axis=1)
        lam = _diff_lambda(lam_ref, lam_init)
        for h in range(H_C):
            a = acc_ref[head_rows(h), :] / l_ref[head_rows(h), :]
            o = a[0:TS] - lam * a[TS:2 * TS]
            o_ref[:, h * LANES:(h + 1) * LANES] = (_rms(o, g_ref[...]) * (1.0 - lam_init)).astype(o_ref.dtype)


def sample_diff_idx(pt, qc, qi, wi, ck, cv, cik, kn, vn, kin, lam_p, g_c, *, b, n_pages, front, lam_init):
    assert n_pages % PG == 0
    n_steps = n_pages // PG
    rows = H_C * 2 * TS
    qspec = lambda w: pl.BlockSpec((TS, w), lambda bb, j, pt: (bb, 0))
    deep = pl.Buffered(PAGE_BUFFERS)
    pages4 = [pl.BlockSpec((1, PAGE_SIZE * H_C, 2 * DC), _page_map(g, n_pages, 2), pipeline_mode=deep) for g in range(PG)]
    pages_i = [pl.BlockSpec((1, D_IDX, PAGE_SIZE), _page_map(g, n_pages, 2), pipeline_mode=deep) for g in range(PG)]
    return pl.pallas_call(
        functools.partial(_sample_a_kernel, n_steps=n_steps, front=front, lam_init=lam_init),
        grid_spec=pltpu.PrefetchScalarGridSpec(
            num_scalar_prefetch=1,
            grid=(b, n_steps + 1),
            in_specs=[qspec(512), qspec(512), qspec(128)] + pages4 + pages4 + pages_i
                     + [qspec(512), qspec(512), pl.BlockSpec((1, D_IDX, PAGE_SIZE), lambda bb, j, pt: (bb, 0, 0)),
                        pl.BlockSpec((4, DC), lambda bb, j, pt: (0, 0)),
                        pl.BlockSpec((1, 2 * DC), lambda bb, j, pt: (0, 0))],
            out_specs=[qspec(512), pl.BlockSpec((1, TS, PG * PAGE_SIZE), lambda bb, j, pt: (bb, 0, j))],
            scratch_shapes=[pltpu.VMEM((rows, 1), F32), pltpu.VMEM((rows, 1), F32), pltpu.VMEM((rows, 2 * DC), F32)]),
        out_shape=[jax.ShapeDtypeStruct((b * TS, 512), BF16),
                   jax.ShapeDtypeStruct((b, TS, (n_steps + 1) * PG * PAGE_SIZE), F32)],
        compiler_params=_cparams(("parallel", "arbitrary")),
        name="sample_diff_idx",
    )(pt, qc, qi, wi, *([ck] * PG), *([cv] * PG), *([cik] * PG), kn, vn, kin, lam_p, g_c.reshape(1, 2 * DC))


def _sample_threshold_kernel(keys_ref, tau_ref, need_ref, *, n_sel):
    rows = keys_ref.shape[0]

    def count(pred):
        return jnp.sum(jnp.where(pred(keys_ref[...]), 1, 0), axis=1, keepdims=True)

    def min_where(pred):
        sc = keys_ref[...]
        return jnp.min(jnp.where(pred(sc), sc, -NINF), axis=1, keepdims=True)

    tau = _kth_threshold(lambda cand: count(lambda sc: sc >= cand), (rows, 1), n_sel)
    tau, above = _raise_to_kth_score(tau, count, min_where, n_sel)
    tau_ref[...] = tau
    need_ref[...] = (n_sel - above).astype(F32)


def sample_threshold(keys, *, n_sel):
    rows, nk = keys.shape
    return pl.pallas_call(
        functools.partial(_sample_threshold_kernel, n_sel=n_sel),
        grid=(1,),
        in_specs=[pl.BlockSpec((rows, nk), lambda i: (0, 0), pipeline_mode=pl.Buffered(1))],
        out_specs=[pl.BlockSpec((rows, 1), lambda i: (0, 0))] * 2,
        out_shape=[jax.ShapeDtypeStruct((rows, 1), F32)] * 2,
        compiler_params=_cparams(("arbitrary",)),
        name="sample_threshold",
    )(keys)


def _sample_b_kernel(pt_ref, tau_ref, need_ref, keys_ref, qd_ref, *refs, n_steps):
    ck_refs, cv_refs = refs[:PG], refs[PG:2 * PG]
    kn_ref, vn_ref, o_ref, before_ref, m_ref, l_ref, acc_ref = refs[2 * PG:]
    j = pl.program_id(1)

    @pl.when(j == 0)
    def _():
        before_ref[...] = jnp.zeros(before_ref.shape, F32)
        m_ref[...] = jnp.full(m_ref.shape, NEG, F32)
        l_ref[...] = jnp.zeros(l_ref.shape, F32)
        acc_ref[...] = jnp.zeros(acc_ref.shape, F32)

    strict_upper = jnp.where(lax.broadcasted_iota(jnp.int32, (PAGE_SIZE, PAGE_SIZE), 0)
                             < lax.broadcasted_iota(jnp.int32, (PAGE_SIZE, PAGE_SIZE), 1), 1.0, 0.0).astype(BF16)
    qd_all = jnp.concatenate([qd_ref[:, h * LANES:(h + 1) * LANES] for h in range(H_D)], axis=0)

    def select(key, before):
        tau = tau_ref[...]
        eq = key == tau
        eqf = jnp.where(eq, 1.0, 0.0)
        rank = _dot(eqf.astype(BF16), strict_upper) + before
        sel = ((key > tau) | (eq & (rank < need_ref[...]))) & (key > NINF)
        return sel, before + jnp.sum(eqf, axis=1, keepdims=True)

    def attend(s, sel, v):
        s = jnp.where(jnp.concatenate([sel] * H_D, axis=0), s * (DH_D ** -0.5), NEG)
        _online_softmax_update(s, [v], m_ref, l_ref, acc_ref)

    @pl.when(j < n_steps)
    def _():
        key = keys_ref[0]
        before = before_ref[...]
        sels = []
        for g in range(PG):
            sel, before = select(key[:, g * PAGE_SIZE:(g + 1) * PAGE_SIZE], before)
            sels.append(sel)
        before_ref[...] = before
        k = jnp.concatenate([r[0].astype(BF16) for r in ck_refs], axis=0)
        v = jnp.concatenate([r[0].astype(BF16) for r in cv_refs], axis=0)
        attend(_dot_nt(qd_all, k), jnp.concatenate(sels, axis=1), v)

    @pl.when(j == n_steps)
    def _():
        sel, _ = select(keys_ref[0][:, :PAGE_SIZE], before_ref[...])
        attend(_dot_nt(qd_all, kn_ref[0]), sel, vn_ref[0])
        o = acc_ref[...] / l_ref[...]
        for h in range(H_D):
            o_ref[:, h * LANES:(h + 1) * LANES] = o[h * TS:(h + 1) * TS].astype(o_ref.dtype)


def sample_dsa(pt, keys, qd, ck, cv, kn, vn, *, b, n_pages, n_sel):
    assert n_pages % PG == 0
    n_steps = n_pages // PG
    rows = H_D * TS
    tau, need = sample_threshold(keys.reshape(b * TS, keys.shape[2]), n_sel=n_sel)
    pages = [pl.BlockSpec((1, PAGE_SIZE, DH_D), _page_map(g, n_pages, 2), pipeline_mode=pl.Buffered(PAGE_BUFFERS))
             for g in range(PG)]
    nspec = pl.BlockSpec((1, PAGE_SIZE, DH_D), lambda bb, j, pt: (bb, 0, 0))
    col = pl.BlockSpec((TS, 1), lambda bb, j, pt: (bb, 0))
    return pl.pallas_call(
        functools.partial(_sample_b_kernel, n_steps=n_steps),
        grid_spec=pltpu.PrefetchScalarGridSpec(
            num_scalar_prefetch=1,
            grid=(b, n_steps + 1),
            in_specs=[col, col,
                      pl.BlockSpec((1, TS, PG * PAGE_SIZE), lambda bb, j, pt: (bb, 0, j)),
                      pl.BlockSpec((TS, 512), lambda bb, j, pt: (bb, 0))] + pages + pages + [nspec, nspec],
            out_specs=pl.BlockSpec((TS, 512), lambda bb, j, pt: (bb, 0)),
            scratch_shapes=[pltpu.VMEM((TS, 1), F32),
                            pltpu.VMEM((rows, 1), F32), pltpu.VMEM((rows, 1), F32), pltpu.VMEM((rows, DH_D), F32)]),
        out_shape=jax.ShapeDtypeStruct((b * TS, 512), BF16),
        compiler_params=_cparams(("parallel", "arbitrary")),
        name="sample_dsa",
    )(pt, tau, need, keys, qd, *([ck] * PG), *([cv] * PG), kn, vn)


REC_CHUNK = 64
REC_SUB = 16
TM_PROJ = 512
TM_REC, TN_REC = 1408, 1408
TM_ROWS = 704
TM_FFN = 528


def _pad_cols(w, n):
    return jnp.pad(w, ((0, 0), (0, n - w.shape[1])))


def _tile_rows(m, pref):
    return pref if m % pref == 0 else m


def kernel(x_prompt, x_sample, state_hgrn, state_mlstm_C, state_mlstm_n, state_mlstm_m, state_ffn_conv, cache_diff_k, cache_diff_v, cache_dsa_k, cache_dsa_v, cache_idx_k, page_table, meta_tokens, norm_gains, w_in_rec, b_gates_rec, lb_logits, g_norm_hgrn, g_norm_mlstm, w_out_rec, w_in_att, diff_lambda, g_norm_diff, w_out_att, w_ffn_up, ffn_conv_w, ffn_conv_b, w_ffn_down):
    bp, t_in, d = x_prompt.shape
    bs, t_s, _ = x_sample.shape
    depth = norm_gains.shape[0]
    n_pages = page_table.shape[1]
    past_len = n_pages * PAGE_SIZE
    real_p = N_META + t_in
    tp = -(-real_p // QB) * QB
    front_p = tp - real_p
    front_s = TS - t_s
    assert tp % REC_CHUNK == 0 and tp % TM_ROWS == 0 and front_p >= CONV_W - 1 and front_s >= CONV_W - 1
    mp, ms = bp * tp, bs * TS

    meta = jnp.broadcast_to(meta_tokens.astype(x_prompt.dtype)[None], (bp, N_META, d))
    xp = jnp.concatenate([jnp.zeros((bp, front_p, d), x_prompt.dtype), meta, x_prompt], axis=1).reshape(mp, d)
    xs = jnp.concatenate([jnp.zeros((bs, front_s, d), x_sample.dtype), x_sample], axis=1).reshape(ms, d)
    lb_all = jnp.cumsum(jax.nn.softmax(lb_logits.astype(F32), axis=0), axis=0)
    pt_flat = page_table.reshape(-1).astype(jnp.int32)
    sel_p = min(TOPK_MAX, t_in // 4)
    sel_s = min(TOPK_MAX, (past_len + t_s) // 4)
    tmp_s = _tile_rows(ms, TM_PROJ)
    tmr_s = _tile_rows(ms, TM_ROWS)

    pos_p = jnp.arange(tp, dtype=jnp.int32) - front_p
    pos_s = jnp.tile(past_len + jnp.arange(TS, dtype=jnp.int32) - front_s, ms // TS)
    tabs_p = (rope_tables(pos_p, DC, DC // ROT_FRAC // 2), rope_tables(pos_p, DH_D, DH_D // ROT_FRAC // 2))
    tabs_s = (rope_tables(pos_s, DC, DC // ROT_FRAC // 2), rope_tables(pos_s, DH_D, DH_D // ROT_FRAC // 2))

    rec_p, rec_s = [[], [], [], []], [[], [], [], []]
    att_p, att_s = [[], [], [], [], []], [[], [], [], [], []]
    conv_p, conv_s = [], []
    for l in range(depth):
        p = l // 2
        g = norm_gains[l].astype(F32)
        if l % 2 == 0:
            w_in = _pad_cols(w_in_rec[p], REC_N).astype(BF16)
            w_out = w_out_rec[p].astype(BF16)
            prm = (lb_all[p], b_gates_rec[p].astype(F32), g_norm_hgrn[p].astype(F32), g_norm_mlstm[p].astype(F32))
            proj = norm_matmul(xp, g[0], w_in, tm=_tile_rows(mp, TM_REC), tn=TN_REC).reshape(bp, tp, REC_N)
            zs = jnp.zeros((bp, 4, 128, 128), F32)
            y, *st = rec_mixer(proj, *prm, zs, zs, jnp.zeros((bp, 4, 128), F32), jnp.zeros((bp, 4), F32),
                               C=REC_CHUNK, W=REC_SUB, front=front_p)
            xp = matmul_norm_res([y.reshape(mp, -1)], [w_out], g[1], xp, tm=TM_ROWS, t_pad=tp, front=front_p)
            for j in range(4):
                rec_p[j].append(st[j])
            proj = norm_matmul(xs, g[0], w_in, tm=tmp_s, tn=384).reshape(bs, TS, REC_N)
            y, *st = rec_mixer(proj, *prm, state_hgrn[p].astype(F32), state_mlstm_C[p].astype(F32),
                               state_mlstm_n[p].astype(F32), state_mlstm_m[p].astype(F32), C=TS, W=TS, front=front_s)
            xs = matmul_norm_res([y.reshape(ms, -1)], [w_out], g[1], xs, tm=tmr_s, t_pad=TS, front=front_s)
            for j in range(4):
                rec_s[j].append(st[j])
        else:
            lam_init = 0.8 - 0.6 * math.exp(-0.3 * l)
            w_in = _pad_cols(w_in_att[p], ATT_N).astype(BF16)
            w_out = w_out_att[p].astype(BF16)
            dl, gc = diff_lambda[p].astype(F32), g_norm_diff[p].astype(F32)
            (qc, kc, kcb, vc, vcb, qd, kd, kdb, vd, vdb, qi, ki, kib, wi, vct, vdt) = att_prep(xp, g[0], w_in, *tabs_p, tm=KB)
            oc = diff_prompt(qc, kcb, vct, dl, gc, b=bp, t_pad=tp, front=front_p, lam_init=lam_init)
            od = dsa_prompt(qi, wi, qd, kib, kdb, vdt, b=bp, t_pad=tp, front=front_p, n_sel=sel_p)
            xp = matmul_norm_res([oc, od], [w_out[:512], w_out[512:]], g[1], xp, tm=TM_ROWS, t_pad=tp, front=front_p)
            for j, (a, shp) in enumerate([(kc, (H_C, 2 * DC)), (vc, (H_C, 2 * DC)), (kd, (DH_D,)), (vd, (DH_D,)), (ki, (D_IDX,))]):
                att_p[j].append(a.reshape((bp, tp) + shp)[:, front_p:])
            (qc, kc, kcb, vc, vcb, qd, kd, kdb, vd, vdb, qi, ki, kib, wi, _, _) = att_prep(xs, g[0], w_in, *tabs_s, tm=ms)
            as_page = lambda a: jnp.pad(a.reshape(bs, TS, -1), ((0, 0), (0, PAGE_SIZE - TS), (0, 0)))
            rows_kh = lambda c: c.reshape(c.shape[0], PAGE_SIZE * H_C, 2 * DC)
            oc, keys = sample_diff_idx(pt_flat, qc, qi, wi, rows_kh(cache_diff_k[p]), rows_kh(cache_diff_v[p]),
                                       jnp.swapaxes(cache_idx_k[p], 1, 2), kcb, vcb, jnp.swapaxes(as_page(kib), 1, 2), dl, gc,
                                       b=bs, n_pages=n_pages, front=front_s, lam_init=lam_init)
            od = sample_dsa(pt_flat, keys, qd, cache_dsa_k[p], cache_dsa_v[p], as_page(kdb), as_page(vdb),
                            b=bs, n_pages=n_pages, n_sel=sel_s)
            xs = matmul_norm_res([oc, od], [w_out[:512], w_out[512:]], g[1], xs, tm=tmr_s, t_pad=TS, front=front_s)
            for j, (a, shp) in enumerate([(kc, (H_C, 2 * DC)), (vc, (H_C, 2 * DC)), (kd, (DH_D,)), (vd, (DH_D,)), (ki, (D_IDX,))]):
                att_s[j].append(a.reshape((bs, TS) + shp)[:, front_s:])
        w_up, w_down = w_ffn_up[l].astype(BF16), w_ffn_down[l].astype(BF16)
        cw, cb = ffn_conv_w[l].astype(F32), ffn_conv_b[l].astype(F32)
        xp, tail = ffn_fused(xp, g[2], w_up, cw, cb, w_down, g[3], tm=TM_FFN, t_pad=tp, front=front_p)
        conv_p.append(tail[:, SUB - (CONV_W - 1):])
        up = norm_matmul(xs, g[2], w_up, tm=tmp_s, tn=512)
        conv_s.append(up.reshape(bs, TS, 2 * D_FF)[:, TS - (CONV_W - 1):, :D_FF])
        buf = jnp.pad(state_ffn_conv[l].astype(F32), ((0, 0), (front_s - (CONV_W - 1), TS - front_s), (0, 0)))
        xs = ffn_down(up, buf.reshape(ms, D_FF), cw, cb, w_down, g[3], xs, tm=tmr_s, t_pad=TS, front=front_s)

    y_p = xp.reshape(bp, tp, d)[:, front_p + N_META:]
    y_s = xs.reshape(bs, TS, d)[:, front_s:]
    stack = lambda xs: xs[0][None] if len(xs) == 1 else jnp.stack(xs)
    return (y_p, y_s,
            stack(rec_p[0]), stack(rec_s[0]), stack(rec_p[1]), stack(rec_s[1]),
            stack(rec_p[2]), stack(rec_s[2]), stack(rec_p[3]), stack(rec_s[3]),
            stack(conv_p), stack(conv_s),
            stack(att_p[0]), stack(att_s[0]), stack(att_p[1]), stack(att_s[1]),
            stack(att_p[2]), stack(att_s[2]), stack(att_p[3]), stack(att_s[3]),
            stack(att_p[4]), stack(att_s[4]))
```
